```python
import math
import jax, jax.numpy as jnp
from jax import lax
import numpy as np

D_MODEL = 1024
BATCH = 2
SEQ = 8192
DEPTH = 4
DEC_BATCH = 128
DEC_SEQ = 8
PAST_LEN = 8192
PAGE_SIZE = 128

N_EVEN = (DEPTH + 1) // 2
N_ODD = DEPTH // 2

ML_HEADS = 4
ML_DK = 128
ML_DV = 128
ML_WIDTH = ML_HEADS * ML_DV
ML_CHUNK = 64

SC_WIDTH = D_MODEL // 2
CONV_W = 3

ATT_HEADS = 16
KV_HEADS = 4
HEAD_DIM = 64
GROUP = ATT_HEADS // KV_HEADS
WINDOW = 128
WIN_BUF = min(WINDOW, PAST_LEN)
ROPE_THETA = 10000.0

D_FF = 2816

EPS = 1e-6
IN_A = 4 * ML_WIDTH + 2 * ML_HEADS + 3 * SC_WIDTH
QKV_W = (ATT_HEADS + 2 * KV_HEADS) * HEAD_DIM

kernel_name = 'hybrid_mlstm_shortconv_swa_decoder_step'


def rms_norm(x, g):
    xf = x.astype(jnp.float32)
    y = xf * lax.rsqrt(jnp.mean(xf * xf, axis=-1, keepdims=True) + EPS)
    return (y * g.astype(jnp.float32)).astype(x.dtype)


def rope(x, pos):
    half = HEAD_DIM // 2
    inv = ROPE_THETA ** (-jnp.arange(half, dtype=jnp.float32) / half)
    ang = pos.astype(jnp.float32)[:, None] * inv[None, :]
    cos = jnp.cos(ang)[:, None, :]
    sin = jnp.sin(ang)[:, None, :]
    xf = x.astype(jnp.float32)
    x1, x2 = xf[..., :half], xf[..., half:]
    return jnp.concatenate([x1 * cos - x2 * sin, x2 * cos + x1 * sin], axis=-1).astype(x.dtype)


def causal_conv(u, prev, w):
    S = u.shape[1]
    up = jnp.concatenate([prev.astype(u.dtype), u], axis=1)
    y = up[:, 0:S] * w[0]
    for j in range(1, CONV_W):
        y = y + up[:, j:j + S] * w[j]
    return y, up[:, S:]


def mlstm_chunk_step(carry, inp):
    C, n, m = carry
    q, k, v, ig, lf = inp
    L = q.shape[-2]
    causal = jnp.tril(jnp.ones((L, L), dtype=bool))
    b = jnp.cumsum(lf, axis=-1)
    dlog = jnp.where(causal, b[..., :, None] - b[..., None, :] + ig[..., None, :], -jnp.inf)
    g = b + m[..., None]
    m_t = jnp.maximum(jnp.max(dlog, axis=-1), g)
    w = jnp.exp(dlog - m_t[..., None])
    wg = jnp.exp(g - m_t)
    s = jnp.einsum('bhtd,bhsd->bhts', q, k) * w
    num = wg[..., None] * jnp.einsum('bhtd,bhde->bhte', q, C) + jnp.einsum('bhts,bhse->bhte', s, v)
    den = wg * jnp.einsum('bhtd,bhd->bht', q, n) + jnp.sum(s, axis=-1)
    h = num / jnp.maximum(jnp.abs(den), jnp.exp(-m_t))[..., None]
    bL = b[..., -1]
    a = bL[..., None] - b + ig
    m_new = jnp.maximum(bL + m, jnp.max(a, axis=-1))
    wc = jnp.exp(bL + m - m_new)
    ws = jnp.exp(a - m_new[..., None])
    C_new = wc[..., None, None] * C + jnp.einsum('bhs,bhsd,bhse->bhde', ws, k, v)
    n_new = wc[..., None] * n + jnp.einsum('bhs,bhsd->bhd', ws, k)
    return (C_new, n_new, m_new), h


def mlstm(q, k, v, ig, lf, C0, n0, m0):
    B, H, S, _ = q.shape
    L = math.gcd(S, ML_CHUNK)
    NC = S // L

    def chunks(a):
        return jnp.moveaxis(a.reshape((B, H, NC, L) + a.shape[3:]), 2, 0)

    (C, n, m), h = lax.scan(mlstm_chunk_step, (C0, n0, m0),
                            (chunks(q), chunks(k), chunks(v), chunks(ig), chunks(lf)))
    h = jnp.moveaxis(h, 0, 2).reshape(B, H, S, h.shape[-1])
    return h, C, n, m


def mixer_ab(h, C0, n0, m0, sc_prev, w_in, b_if, out_norm, conv_w, w_out):
    B, S, _ = h.shape
    f32 = jnp.float32
    z = h @ w_in
    splits = [int(i) for i in np.cumsum([ML_WIDTH] * 4 + [2 * ML_HEADS] + [SC_WIDTH] * 2)]
    zq, zk, zv, zo, zg, zb, zc, zx = jnp.split(z, splits, axis=-1)

    def heads(a):
        return a.reshape(B, S, ML_HEADS, -1).transpose(0, 2, 1, 3).astype(f32)

    gates = (zg.astype(f32) + b_if.astype(f32)).transpose(0, 2, 1)
    ig = gates[:, :ML_HEADS]
    lf = jax.nn.log_sigmoid(gates[:, ML_HEADS:])
    hm, C, n, m = mlstm(heads(zq), heads(zk) * (ML_DK ** -0.5), heads(zv), ig, lf,
                        C0.astype(f32), n0.astype(f32), m0.astype(f32))
    hm = rms_norm(hm.transpose(0, 2, 1, 3).astype(h.dtype), out_norm.reshape(ML_HEADS, ML_DV))
    hm = (hm * jax.nn.sigmoid(zo).reshape(B, S, ML_HEADS, ML_DV)).reshape(B, S, ML_WIDTH)
    u, sc_new = causal_conv(zc * zx, sc_prev, conv_w)
    y = jnp.concatenate([hm, zb * u], axis=-1) @ w_out
    return y, C, n, m, sc_new


def qkv_heads(h, w_qkv, q_norm, k_norm, pos):
    B, S, _ = h.shape
    z = h @ w_qkv
    q, k, v = jnp.split(z, [ATT_HEADS * HEAD_DIM, (ATT_HEADS + KV_HEADS) * HEAD_DIM], axis=-1)
    q = rope(rms_norm(q.reshape(B, S, ATT_HEADS, HEAD_DIM), q_norm), pos)
    k = rope(rms_norm(k.reshape(B, S, KV_HEADS, HEAD_DIM), k_norm), pos)
    v = v.reshape(B, S, KV_HEADS, HEAD_DIM)
    return q, k, v


def sink_attention(q, k, v, mask, sink):
    s = jnp.einsum('...tkgd,...skd->...kgts', q, k).astype(jnp.float32) * (HEAD_DIM ** -0.5)
    s = jnp.where(mask, s, -jnp.inf)
    sk = jnp.broadcast_to(sink.astype(jnp.float32).reshape(KV_HEADS, GROUP, 1, 1), s.shape[:-1] + (1,))
    p = jax.nn.softmax(jnp.concatenate([s, sk], axis=-1), axis=-1)[..., :-1]
    return jnp.einsum('...kgts,...skd->...tkgd', p.astype(v.dtype), v)


def attn_prompt(h, w_qkv, q_norm, k_norm, sink, w_out):
    B, S, _ = h.shape
    pos = jnp.arange(S, dtype=jnp.int32)
    q, k, v = qkv_heads(h, w_qkv, q_norm, k_norm, pos)
    NB = S // WINDOW
    qb = q.reshape(B, NB, WINDOW, KV_HEADS, GROUP, HEAD_DIM)

    def with_prev(a):
        ab = a.reshape(B, NB, WINDOW, KV_HEADS, HEAD_DIM)
        prev = jnp.concatenate([jnp.zeros_like(ab[:, :1]), ab[:, :-1]], axis=1)
        return jnp.concatenate([prev, ab], axis=2)

    blk = jnp.arange(NB)[:, None, None]
    qpos = blk * WINDOW + jnp.arange(WINDOW)[None, :, None]
    kpos = (blk - 1) * WINDOW + jnp.arange(2 * WINDOW)[None, None, :]
    d = qpos - kpos
    mask = (d >= 0) & (d < WINDOW) & (kpos >= 0)
    o = sink_attention(qb, with_prev(k), with_prev(v), mask[:, None, None], sink)
    y = o.reshape(B, S, ATT_HEADS * HEAD_DIM) @ w_out
    return y, k[:, S - WIN_BUF:], v[:, S - WIN_BUF:]


def attn_sample(h, buf_k, buf_v, w_qkv, q_norm, k_norm, sink, w_out):
    B, L, _ = h.shape
    WB = buf_k.shape[1]
    pos = PAST_LEN + jnp.arange(L, dtype=jnp.int32)
    q, k, v = qkv_heads(h, w_qkv, q_norm, k_norm, pos)
    kk = jnp.concatenate([buf_k.astype(k.dtype), k], axis=1)
    vv = jnp.concatenate([buf_v.astype(v.dtype), v], axis=1)
    kpos = PAST_LEN - WB + jnp.arange(WB + L, dtype=jnp.int32)
    d = pos[:, None] - kpos[None, :]
    mask = (d >= 0) & (d < WINDOW)
    o = sink_attention(q.reshape(B, L, KV_HEADS, GROUP, HEAD_DIM), kk, vv, mask, sink)
    y = o.reshape(B, L, ATT_HEADS * HEAD_DIM) @ w_out
    return y, kk[:, L:], vv[:, L:]


def conv_ffn(h, prev, w_up, conv_w, w_down):
    u = h @ w_up
    u, new_prev = causal_conv(u, prev, conv_w)
    g, a = jnp.split(u, 2, axis=-1)
    return (jax.nn.silu(g) * a) @ w_down, new_prev


def run_trunk(x, c, prompt, st_C, st_n, st_m, st_sc, st_wk, st_wv, st_ffn,
              norm1, norm2, w_ada, b_ada, a_w_in, a_b_if, a_out_norm, a_conv_w, a_w_out,
              c_w_qkv, c_q_norm, c_k_norm, c_sink, c_w_out, f_w_up, f_conv_w, f_w_down):
    Cs, ns, ms, scs, wks, wvs, ffs = [], [], [], [], [], [], []
    for l in range(DEPTH):
        mod = jax.nn.silu(c) @ w_ada[l] + b_ada[l]
        sh1, sc1, g1, sh2, sc2, g2 = jnp.split(mod[:, None, :], 6, axis=-1)
        h = rms_norm(x, norm1[l]) * (1 + sc1) + sh1
        if l % 2 == 0:
            i = l // 2
            y, C, n, m, sc_new = mixer_ab(h, st_C[i], st_n[i], st_m[i], st_sc[i], a_w_in[i], a_b_if[i],
                                          a_out_norm[i], a_conv_w[i], a_w_out[i])
            Cs.append(C)
            ns.append(n)
            ms.append(m)
            scs.append(sc_new)
        else:
            j = l // 2
            if prompt:
                y, wk, wv = attn_prompt(h, c_w_qkv[j], c_q_norm[j], c_k_norm[j], c_sink[j], c_w_out[j])
            else:
                y, wk, wv = attn_sample(h, st_wk[j], st_wv[j], c_w_qkv[j], c_q_norm[j], c_k_norm[j],
                                        c_sink[j], c_w_out[j])
            wks.append(wk)
            wvs.append(wv)
        x = x + g1 * y
        h = rms_norm(x, norm2[l]) * (1 + sc2) + sh2
        y, fp = conv_ffn(h, st_ffn[l], f_w_up[l], f_conv_w[l], f_w_down[l])
        ffs.append(fp)
        x = x + g2 * y
    return (x, jnp.stack(Cs), jnp.stack(ns), jnp.stack(ms), jnp.stack(scs),
            jnp.stack(wks), jnp.stack(wvs), jnp.stack(ffs))


def setup_inputs(seed: int = 0) -> dict:
    key = jax.random.key(seed)
    ks = iter(jax.random.split(key, 40))
    nrm = lambda shape, s=1.0: jax.random.normal(next(ks), shape, jnp.float32) * s
    D = D_MODEL
    b_if = jnp.concatenate([nrm((N_EVEN, ML_HEADS), 0.1),
                            3.0 + nrm((N_EVEN, ML_HEADS), 0.5)], axis=-1)
    return {
        'x_prompt': nrm((BATCH, SEQ, D)),
        'x_sample': nrm((DEC_BATCH, DEC_SEQ, D)),
        'c_prompt': nrm((BATCH, D)),
        'c_sample': nrm((DEC_BATCH, D)),
        'state_mlstm_C': nrm((N_EVEN, DEC_BATCH, ML_HEADS, ML_DK, ML_DV), 0.1),
        'state_mlstm_n': nrm((N_EVEN, DEC_BATCH, ML_HEADS, ML_DK), 0.3),
        'state_mlstm_m': nrm((N_EVEN, DEC_BATCH, ML_HEADS), 1.0),
        'state_sconv': nrm((N_EVEN, DEC_BATCH, CONV_W - 1, SC_WIDTH)),
        'cache_win_k': nrm((N_ODD, DEC_BATCH, WIN_BUF, KV_HEADS, HEAD_DIM)),
        'cache_win_v': nrm((N_ODD, DEC_BATCH, WIN_BUF, KV_HEADS, HEAD_DIM)),
        'state_ffn_conv': nrm((DEPTH, DEC_BATCH, CONV_W - 1, 2 * D_FF)),
        'norm1': 1.0 + nrm((DEPTH, D), 0.02),
        'norm2': 1.0 + nrm((DEPTH, D), 0.02),
        'w_ada': nrm((DEPTH, D, 6 * D), 0.5 * D ** -0.5),
        'b_ada': nrm((DEPTH, 6 * D), 0.02),
        'a_w_in': nrm((N_EVEN, D, IN_A), D ** -0.5),
        'a_b_if': b_if,
        'a_out_norm': 1.0 + nrm((N_EVEN, ML_WIDTH), 0.02),
        'a_conv_w': nrm((N_EVEN, CONV_W, SC_WIDTH), CONV_W ** -0.5),
        'a_w_out': nrm((N_EVEN, ML_WIDTH + SC_WIDTH, D), (ML_WIDTH + SC_WIDTH) ** -0.5),
        'c_w_qkv': nrm((N_ODD, D, QKV_W), D ** -0.5),
        'c_q_norm': 1.0 + nrm((N_ODD, HEAD_DIM), 0.02),
        'c_k_norm': 1.0 + nrm((N_ODD, HEAD_DIM), 0.02),
        'c_sink': nrm((N_ODD, ATT_HEADS), 0.5),
        'c_w_out': nrm((N_ODD, ATT_HEADS * HEAD_DIM, D), (ATT_HEADS * HEAD_DIM) ** -0.5),
        'f_w_up': nrm((DEPTH, D, 2 * D_FF), D ** -0.5),
        'f_conv_w': nrm((DEPTH, CONV_W, 2 * D_FF), CONV_W ** -0.5),
        'f_w_down': nrm((DEPTH, D_FF, D), D_FF ** -0.5),
    }


def reference(x_prompt, x_sample, c_prompt, c_sample, state_mlstm_C, state_mlstm_n, state_mlstm_m,
              state_sconv, cache_win_k, cache_win_v, state_ffn_conv,
              norm1, norm2, w_ada, b_ada, a_w_in, a_b_if, a_out_norm, a_conv_w, a_w_out,
              c_w_qkv, c_q_norm, c_k_norm, c_sink, c_w_out, f_w_up, f_conv_w, f_w_down):
    B = x_prompt.shape[0]
    f32 = jnp.float32
    z_C = jnp.zeros((N_EVEN, B, ML_HEADS, ML_DK, ML_DV), f32)
    z_n = jnp.zeros((N_EVEN, B, ML_HEADS, ML_DK), f32)
    z_m = jnp.zeros((N_EVEN, B, ML_HEADS), f32)
    z_sc = jnp.zeros((N_EVEN, B, CONV_W - 1, SC_WIDTH), x_prompt.dtype)
    z_ffn = jnp.zeros((DEPTH, B, CONV_W - 1, 2 * D_FF), x_prompt.dtype)

    y_prompt, p_C, p_n, p_m, p_sc, p_wk, p_wv, p_ffn = run_trunk(
        x_prompt, c_prompt, True, z_C, z_n, z_m, z_sc, None, None, z_ffn,
        norm1, norm2, w_ada, b_ada, a_w_in, a_b_if, a_out_norm, a_conv_w, a_w_out,
        c_w_qkv, c_q_norm, c_k_norm, c_sink, c_w_out, f_w_up, f_conv_w, f_w_down)

    y_sample, s_C, s_n, s_m, s_sc, s_wk, s_wv, s_ffn = run_trunk(
        x_sample, c_sample, False, state_mlstm_C, state_mlstm_n, state_mlstm_m, state_sconv,
        cache_win_k, cache_win_v, state_ffn_conv,
        norm1, norm2, w_ada, b_ada, a_w_in, a_b_if, a_out_norm, a_conv_w, a_w_out,
        c_w_qkv, c_q_norm, c_k_norm, c_sink, c_w_out, f_w_up, f_conv_w, f_w_down)

    return (y_prompt, y_sample, p_C, p_n, p_m, p_sc, p_wk, p_wv, p_ffn,
            s_C, s_n, s_m, s_sc, s_wk, s_wv, s_ffn)
```

```python
import functools

import jax
import jax.numpy as jnp
import numpy as np
from jax import lax
from jax.experimental import pallas as pl
from jax.experimental.pallas import tpu as pltpu

F32 = jnp.float32
BF16 = jnp.bfloat16

D_MODEL = 1024
DEPTH = 4
PAST_LEN = 8192
ML_HEADS = 4
ML_DK = 128
ML_DV = 128
ML_WIDTH = ML_HEADS * ML_DV
SC_WIDTH = D_MODEL // 2
ATT_HEADS = 16
KV_HEADS = 4
HEAD_DIM = 64
WINDOW = 128
ROPE_THETA = 10000.0
D_FF = 2816
EPS = 1e-6

LANES = 128
SUBLANES = 8
VMEM_LIMIT = 56 * 1024 * 1024

TM = 512
CHUNK = 128
SEQ_BLK = 16
FC = 256
NCH = D_FF // FC
GATE_COL = 4 * ML_WIDTH
B_COL = GATE_COL + LANES
C_COL = B_COL + SC_WIDTH
X_COL = C_COL + SC_WIDTH
IN_W = X_COL + SC_WIDTH
Q_W = ATT_HEADS * HEAD_DIM
KV_W = KV_HEADS * HEAD_DIM
HEAD_PERM = (0, 4, 1, 5, 2, 6, 3, 7, 8, 12, 9, 13, 10, 14, 11, 15)


def _cparams(n_axes):
    return pltpu.CompilerParams(dimension_semantics=("arbitrary",) * n_axes,
                                vmem_limit_bytes=VMEM_LIMIT)


def _dot(a, b):
    return jnp.dot(a, b, preferred_element_type=F32)


def _dot_nt(a, b):
    return lax.dot_general(a, b, (((1,), (1,)), ((), ())), preferred_element_type=F32)


def _dot_tn(a, b):
    return lax.dot_general(a, b, (((0,), (0,)), ((), ())), preferred_element_type=F32)


def _dot_exact01(m, a):
    a1 = a.astype(BF16)
    r1 = a - a1.astype(F32)
    a2 = r1.astype(BF16)
    a3 = (r1 - a2.astype(F32)).astype(BF16)
    return _dot(m, a1) + _dot(m, a2) + _dot(m, a3)


def _norm_mod(x, nw, sc, sh):
    ms = jnp.mean(x * x, axis=-1, keepdims=True)
    return (x * lax.rsqrt(ms + EPS) * nw) * (1.0 + sc) + sh


def _sigmoid(x):
    return 1.0 / (1.0 + jnp.exp(-x))


def _log_sigmoid(x):
    return jnp.minimum(x, 0.0) - jnp.log(1.0 + jnp.exp(-jnp.abs(x)))


ADA_TN = 1536


def _ada_kernel(c_ref, w_ref, b_ref, o_ref):
    c = c_ref[...]
    s = (c * _sigmoid(c)).astype(BF16)
    o_ref[0] = _dot(s, w_ref[0].astype(BF16)) + b_ref[0]


def _ada_call(c_all, w_ada, b_ada):
    rows = c_all.shape[0]
    n6 = w_ada.shape[-1]
    return pl.pallas_call(
        _ada_kernel,
        grid=(DEPTH, n6 // ADA_TN),
        in_specs=[
            pl.BlockSpec((rows, D_MODEL), lambda l, j: (0, 0)),
            pl.BlockSpec((1, D_MODEL, ADA_TN), lambda l, j: (l, 0, j)),
            pl.BlockSpec((1, 1, ADA_TN), lambda l, j: (l, 0, j)),
        ],
        out_specs=pl.BlockSpec((1, rows, ADA_TN), lambda l, j: (l, 0, j)),
        out_shape=jax.ShapeDtypeStruct((DEPTH, rows, n6), F32),
        compiler_params=_cparams(2),
        name="adaln_mod",
    )(c_all, w_ada, b_ada.reshape(DEPTH, 1, n6))


def _ffn_prompt_kernel(x_ref, mod_ref, nw_ref, wup_ref, cw_ref, wdn_ref,
                       xo_ref, st_ref, h_scr, acc_scr, carry_scr):
    t = pl.program_id(1)

    @pl.when(t == 0)
    def _():
        carry_scr[...] = jnp.zeros_like(carry_scr)

    x = x_ref[0]
    h_scr[...] = _norm_mod(x, nw_ref[...], mod_ref[0, 4:5, :], mod_ref[0, 3:4, :]).astype(BF16)
    acc_scr[...] = jnp.zeros_like(acc_scr)
    row = lax.broadcasted_iota(jnp.int32, (SUBLANES, 2 * FC), 0)

    def body(j, carry):
        u = _dot(h_scr[...], wup_ref[j])
        prev = carry_scr[j]
        cw = cw_ref[j]
        s1 = pltpu.roll(u, 1, 0)
        s2 = pltpu.roll(u, 2, 0)
        f1 = jnp.where(row < 1, pltpu.roll(prev, 1, 0), s1[0:SUBLANES])
        f2 = jnp.where(row < 2, pltpu.roll(prev, 2, 0), s2[0:SUBLANES])
        s1 = jnp.concatenate([f1, s1[SUBLANES:]], axis=0)
        s2 = jnp.concatenate([f2, s2[SUBLANES:]], axis=0)
        y = s2 * cw[0:1] + s1 * cw[1:2] + u * cw[2:3]
        carry_scr[j] = u[TM - SUBLANES:TM]
        g = y[:, :FC]
        act = (g * _sigmoid(g) * y[:, FC:]).astype(BF16)
        acc_scr[...] += _dot(act, wdn_ref[j])
        return carry

    lax.fori_loop(0, NCH, body, 0)
    xo_ref[0] = x + mod_ref[0, 5:6, :] * acc_scr[...]
    st_ref[0] = carry_scr[...]


def _ffn_prompt_call(x, mod, nw, wup, cw, wdn):
    B, S, _ = x.shape
    nt = S // TM
    return pl.pallas_call(
        _ffn_prompt_kernel,
        grid=(B, nt),
        in_specs=[
            pl.BlockSpec((1, TM, D_MODEL), lambda b, t: (b, t, 0)),
            pl.BlockSpec((1, SUBLANES, D_MODEL), lambda b, t: (b, 0, 0)),
            pl.BlockSpec((1, D_MODEL), lambda b, t: (0, 0)),
            pl.BlockSpec((NCH, D_MODEL, 2 * FC), lambda b, t: (0, 0, 0)),
            pl.BlockSpec((NCH, SUBLANES, 2 * FC), lambda b, t: (0, 0, 0)),
            pl.BlockSpec((NCH, FC, D_MODEL), lambda b, t: (0, 0, 0)),
        ],
        out_specs=[
            pl.BlockSpec((1, TM, D_MODEL), lambda b, t: (b, t, 0)),
            pl.BlockSpec((1, NCH, SUBLANES, 2 * FC), lambda b, t: (b, 0, 0, 0)),
        ],
        out_shape=[
            jax.ShapeDtypeStruct((B, S, D_MODEL), F32),
            jax.ShapeDtypeStruct((B, NCH, SUBLANES, 2 * FC), F32),
        ],
        scratch_shapes=[
            pltpu.VMEM((TM, D_MODEL), BF16),
            pltpu.VMEM((TM, D_MODEL), F32),
            pltpu.VMEM((NCH, SUBLANES, 2 * FC), F32),
        ],
        compiler_params=_cparams(2),
        name="ffn_prompt",
    )(x, mod, nw, wup, cw, wdn)


def _ffn_sample_kernel(x_ref, mod_ref, nw_ref, st_ref, wup_ref, cw_ref, wdn_ref,
                       xo_ref, sto_ref, h_scr, acc_scr):
    j = pl.program_id(0)
    n_t, n_seq, _ = x_ref.shape

    @pl.when(j == 0)
    def _():
        for t in range(n_t):
            h_scr[t * n_seq:(t + 1) * n_seq, :] = _norm_mod(
                x_ref[t], nw_ref[...], mod_ref[4], mod_ref[3]).astype(BF16)
        acc_scr[...] = jnp.zeros_like(acc_scr)

    u = _dot(h_scr[...], wup_ref[0])
    cw = cw_ref[0]
    slabs = [st_ref[0, 0], st_ref[0, 1]] + [u[t * n_seq:(t + 1) * n_seq] for t in range(n_t)]
    ys = [slabs[t] * cw[0:1] + slabs[t + 1] * cw[1:2] + slabs[t + 2] * cw[2:3] for t in range(n_t)]
    y = jnp.concatenate(ys, axis=0)
    g = y[:, :FC]
    act = (g * _sigmoid(g) * y[:, FC:]).astype(BF16)
    acc_scr[...] += _dot(act, wdn_ref[0])
    sto_ref[0, 0] = slabs[n_t]
    sto_ref[0, 1] = slabs[n_t + 1]

    @pl.when(j == NCH - 1)
    def _():
        for t in range(n_t):
            xo_ref[t] = x_ref[t] + mod_ref[5] * acc_scr[t * n_seq:(t + 1) * n_seq, :]


def _ffn_sample_call(x_tb, mod_kb, nw, st, wup, cw, wdn):
    n_t, n_seq, _ = x_tb.shape
    return pl.pallas_call(
        _ffn_sample_kernel,
        grid=(NCH,),
        in_specs=[
            pl.BlockSpec((n_t, n_seq, D_MODEL), lambda j: (0, 0, 0)),
            pl.BlockSpec((SUBLANES, n_seq, D_MODEL), lambda j: (0, 0, 0)),
            pl.BlockSpec((1, D_MODEL), lambda j: (0, 0)),
            pl.BlockSpec((1, 2, n_seq, 2 * FC), lambda j: (j, 0, 0, 0)),
            pl.BlockSpec((1, D_MODEL, 2 * FC), lambda j: (j, 0, 0)),
            pl.BlockSpec((1, SUBLANES, 2 * FC), lambda j: (j, 0, 0)),
            pl.BlockSpec((1, FC, D_MODEL), lambda j: (j, 0, 0)),
        ],
        out_specs=[
            pl.BlockSpec((n_t, n_seq, D_MODEL), lambda j: (0, 0, 0)),
            pl.BlockSpec((1, 2, n_seq, 2 * FC), lambda j: (j, 0, 0, 0)),
        ],
        out_shape=[
            jax.ShapeDtypeStruct((n_t, n_seq, D_MODEL), F32),
            jax.ShapeDtypeStruct((NCH, 2, n_seq, 2 * FC), F32),
        ],
        scratch_shapes=[
            pltpu.VMEM((n_t * n_seq, D_MODEL), BF16),
            pltpu.VMEM((n_t * n_seq, D_MODEL), F32),
        ],
        compiler_params=_cparams(1),
        name="ffn_sample",
    )(x_tb, mod_kb, nw, st, wup, cw, wdn)


def _chunk_masks(rows_per_seq):
    r = lax.broadcasted_iota(jnp.int32, (CHUNK, CHUNK), 0)
    c = lax.broadcasted_iota(jnp.int32, (CHUNK, CHUNK), 1)
    if rows_per_seq >= CHUNK:
        same = r >= 0
    else:
        same = (r // rows_per_seq) == (c // rows_per_seq)
    mask = same & (c <= r)
    return same, mask, mask.astype(BF16), same.astype(BF16)


def _gate_prep(gates, lmat, tot):
    lane = lax.broadcasted_iota(jnp.int32, (CHUNK, LANES), 1)
    a = jnp.where(lane < 4, gates, jnp.where(lane < 12, _log_sigmoid(gates), 0.0))
    bc = _dot_exact01(lmat, a)
    tc = _dot_exact01(tot, a)
    x = jnp.where(lane < 4, a, jnp.where(lane < 8, bc, tc))
    return a, bc, tc, x.T


def _head_chunk(hd, q, k, v, a, bc, tc, xt, mprev_c, qn, same, mask):
    neg = -jnp.inf
    b_c = bc[:, 4 + hd:5 + hd]
    ig_c = a[:, hd:hd + 1]
    bl_c = tc[:, 4 + hd:5 + hd]
    b_r = xt[4 + hd:5 + hd, :]
    ig_r = xt[hd:hd + 1, :]
    bl_r = xt[8 + hd:9 + hd, :]

    dl = jnp.where(mask, b_c - b_r + ig_r, neg)
    g_c = b_c + mprev_c
    m_t = jnp.maximum(jnp.max(dl, axis=-1, keepdims=True), g_c)
    w = jnp.exp(dl - m_t)
    wg = jnp.exp(g_c - m_t)
    s = _dot_nt(q.astype(BF16), k.astype(BF16)) * w

    a_r = jnp.where(same, bl_r - b_r + ig_r, neg)
    mnew_c = jnp.maximum(bl_c + mprev_c, jnp.max(a_r, axis=-1, keepdims=True))
    wc_c = jnp.exp(bl_c + mprev_c - mnew_c)
    ws_c = jnp.exp(bl_c - b_c + ig_c - mnew_c)
    kw = k * ws_c
    return s, wg, m_t, kw, wc_c, mnew_c


def _head_out(s, v, wg, m_t, inter, qn):
    num = wg * inter + _dot(s.astype(BF16), v.astype(BF16))
    den = wg * qn + jnp.sum(s, axis=-1, keepdims=True)
    return num * (1.0 / jnp.maximum(jnp.abs(den), jnp.exp(-m_t)))


def _conv3_rows(cx, prev8, cw):
    n = cx.shape[1]
    row = lax.broadcasted_iota(jnp.int32, (SUBLANES, n), 0)
    s1 = pltpu.roll(cx, 1, 0)
    s2 = pltpu.roll(cx, 2, 0)
    f1 = jnp.where(row < 1, pltpu.roll(prev8, 1, 0), s1[0:SUBLANES])
    f2 = jnp.where(row < 2, pltpu.roll(prev8, 2, 0), s2[0:SUBLANES])
    s1 = jnp.concatenate([f1, s1[SUBLANES:]], axis=0)
    s2 = jnp.concatenate([f2, s2[SUBLANES:]], axis=0)
    return s2 * cw[0:1] + s1 * cw[1:2] + cx * cw[2:3]


def _mlstm_out_norm(hm, zo, onw):
    ms = jnp.mean(hm * hm, axis=-1, keepdims=True)
    return hm * lax.rsqrt(ms + EPS) * onw * _sigmoid(zo)


def _mix_even_prompt_kernel(x_ref, mod_ref, nw_ref, win_ref, bif_ref, onw_ref, cw_ref, wout_ref,
                            xo_ref, co_ref, no_ref, mo_ref, sco_ref,
                            z_scr, hm_scr, cat_scr, c_scr, n_scr, m_scr, cc_scr):
    t = pl.program_id(1)

    @pl.when(t == 0)
    def _():
        c_scr[...] = jnp.zeros_like(c_scr)
        n_scr[...] = jnp.zeros_like(n_scr)
        m_scr[...] = jnp.zeros_like(m_scr)
        cc_scr[...] = jnp.zeros_like(cc_scr)

    x = x_ref[0]
    h = _norm_mod(x, nw_ref[...], mod_ref[0, 1:2, :], mod_ref[0, 0:1, :]).astype(BF16)
    z_scr[...] = _dot(h, win_ref[...])

    same, mask, lmat, tot = _chunk_masks(CHUNK)
    scale = ML_DK ** -0.5

    def chunk(c, carry):
        r0 = pl.multiple_of(c * CHUNK, CHUNK)
        rows = pl.ds(r0, CHUNK)
        gates = z_scr[rows, GATE_COL:GATE_COL + LANES] + bif_ref[...]
        a, bc, tc, xt = _gate_prep(gates, lmat, tot)
        for hd in range(ML_HEADS):
            col = slice(hd * ML_DK, (hd + 1) * ML_DK)
            q = z_scr[rows, col]
            k = z_scr[rows, ML_WIDTH + hd * ML_DK:ML_WIDTH + (hd + 1) * ML_DK] * scale
            v = z_scr[rows, 2 * ML_WIDTH + hd * ML_DV:2 * ML_WIDTH + (hd + 1) * ML_DV]
            n_prev = n_scr[hd, 0:1, :]
            m_prev = m_scr[hd, 0:1, :][:, 0:1]
            c_prev = c_scr[hd]
            qn = jnp.sum(q * n_prev, axis=-1, keepdims=True)
            s, wg, m_t, kw, wc_c, mnew_c = _head_chunk(hd, q, k, v, a, bc, tc, xt, m_prev, qn, same, mask)
            inter = _dot(q.astype(BF16), c_prev.astype(BF16))
            hm_scr[rows, hd * ML_DV:(hd + 1) * ML_DV] = _head_out(s, v, wg, m_t, inter, qn)
            wc = wc_c[0:1, 0:1]
            c_scr[hd] = wc * c_prev + _dot_tn(kw.astype(BF16), v.astype(BF16))
            n_new = wc * n_prev + jnp.sum(kw, axis=0, keepdims=True)
            n_scr[hd] = jnp.broadcast_to(n_new, (SUBLANES, ML_DK))
            m_scr[hd] = jnp.broadcast_to(mnew_c[0:1, 0:1], (SUBLANES, LANES))
        return carry

    lax.fori_loop(0, TM // CHUNK, chunk, 0)

    for hd in range(ML_HEADS):
        col = slice(hd * ML_DV, (hd + 1) * ML_DV)
        zo = z_scr[:, 3 * ML_WIDTH + hd * ML_DV:3 * ML_WIDTH + (hd + 1) * ML_DV]
        cat_scr[:, col] = _mlstm_out_norm(hm_scr[:, col], zo, onw_ref[:, col]).astype(BF16)

    cx = z_scr[:, C_COL:C_COL + SC_WIDTH] * z_scr[:, X_COL:X_COL + SC_WIDTH]
    u = _conv3_rows(cx, cc_scr[...], cw_ref[...])
    cc_scr[...] = cx[TM - SUBLANES:TM]
    cat_scr[:, ML_WIDTH:] = (z_scr[:, B_COL:B_COL + SC_WIDTH] * u).astype(BF16)

    y = _dot(cat_scr[...], wout_ref[...])
    xo_ref[0] = x + mod_ref[0, 2:3, :] * y
    co_ref[0] = c_scr[...]
    no_ref[0] = n_scr[...]
    mo_ref[0] = m_scr[...]
    sco_ref[0] = cc_scr[...]


def _mix_even_prompt_call(x, mod, nw, win, bif, onw, cw, wout):
    B, S, _ = x.shape
    nt = S // TM
    const2 = lambda b, t: (0, 0)
    return pl.pallas_call(
        _mix_even_prompt_kernel,
        grid=(B, nt),
        in_specs=[
            pl.BlockSpec((1, TM, D_MODEL), lambda b, t: (b, t, 0)),
            pl.BlockSpec((1, SUBLANES, D_MODEL), lambda b, t: (b, 0, 0)),
            pl.BlockSpec((1, D_MODEL), const2),
            pl.BlockSpec((D_MODEL, IN_W), const2),
            pl.BlockSpec((1, LANES), const2),
            pl.BlockSpec((1, ML_WIDTH), const2),
            pl.BlockSpec((SUBLANES, SC_WIDTH), const2),
            pl.BlockSpec((ML_WIDTH + SC_WIDTH, D_MODEL), const2),
        ],
        out_specs=[
            pl.BlockSpec((1, TM, D_MODEL), lambda b, t: (b, t, 0)),
            pl.BlockSpec((1, ML_HEADS, ML_DK, ML_DV), lambda b, t: (b, 0, 0, 0)),
            pl.BlockSpec((1, ML_HEADS, SUBLANES, ML_DK), lambda b, t: (b, 0, 0, 0)),
            pl.BlockSpec((1, ML_HEADS, SUBLANES, LANES), lambda b, t: (b, 0, 0, 0)),
            pl.BlockSpec((1, SUBLANES, SC_WIDTH), lambda b, t: (b, 0, 0)),
        ],
        out_shape=[
            jax.ShapeDtypeStruct((B, S, D_MODEL), F32),
            jax.ShapeDtypeStruct((B, ML_HEADS, ML_DK, ML_DV), F32),
            jax.ShapeDtypeStruct((B, ML_HEADS, SUBLANES, ML_DK), F32),
            jax.ShapeDtypeStruct((B, ML_HEADS, SUBLANES, LANES), F32),
            jax.ShapeDtypeStruct((B, SUBLANES, SC_WIDTH), F32),
        ],
        scratch_shapes=[
            pltpu.VMEM((TM, IN_W), F32),
            pltpu.VMEM((TM, ML_WIDTH), F32),
            pltpu.VMEM((TM, ML_WIDTH + SC_WIDTH), BF16),
            pltpu.VMEM((ML_HEADS, ML_DK, ML_DV), F32),
            pltpu.VMEM((ML_HEADS, SUBLANES, ML_DK), F32),
            pltpu.VMEM((ML_HEADS, SUBLANES, LANES), F32),
            pltpu.VMEM((SUBLANES, SC_WIDTH), F32),
        ],
        compiler_params=_cparams(2),
        name="mix_even_prompt",
    )(x, mod, nw, win, bif, onw, cw, wout)


def _mix_even_sample_kernel(x_ref, mod_ref, nw_ref, win_ref, bif_ref, onw_ref, cw_ref, wout_ref,
                            c_ref, nt_ref, mt_ref, sc_ref,
                            xo_ref, co_ref, no_ref, mo_ref, sco_ref,
                            h_scr, z_scr, hm_scr, cat_scr, g1_scr, inter_scr, kw_scr, wc_scr):
    n_tok = x_ref.shape[0]
    seq_len = n_tok // SEQ_BLK

    def modulate(b, carry):
        rows = pl.ds(pl.multiple_of(b * seq_len, seq_len), seq_len)
        m = mod_ref[b]
        h_scr[rows, :] = _norm_mod(x_ref[rows, :], nw_ref[...], m[1:2], m[0:1])
        g1_scr[rows, :] = jnp.broadcast_to(m[2:3], (seq_len, D_MODEL))
        return carry

    lax.fori_loop(0, SEQ_BLK, modulate, 0)
    z_scr[...] = _dot(h_scr[...].astype(BF16), win_ref[...])

    same, mask, lmat, tot = _chunk_masks(seq_len)
    scale = ML_DK ** -0.5
    gates = z_scr[:, GATE_COL:GATE_COL + LANES] + bif_ref[...]
    a, bc, tc, xt = _gate_prep(gates, lmat, tot)
    rowi = lax.broadcasted_iota(jnp.int32, (n_tok, LANES), 0)

    for hd in range(ML_HEADS):
        q = z_scr[:, hd * ML_DK:(hd + 1) * ML_DK]
        k = z_scr[:, ML_WIDTH + hd * ML_DK:ML_WIDTH + (hd + 1) * ML_DK] * scale
        v = z_scr[:, 2 * ML_WIDTH + hd * ML_DV:2 * ML_WIDTH + (hd + 1) * ML_DV]
        n_tok_prev = nt_ref[hd]
        m_prev_c = mt_ref[hd][:, 0:1]
        qn = jnp.sum(q * n_tok_prev, axis=-1, keepdims=True)
        s, wg, m_t, kw, wc_c, mnew_c = _head_chunk(hd, q, k, v, a, bc, tc, xt, m_prev_c, qn, same, mask)
        qb = q.astype(BF16)
        vb = v.astype(BF16)
        kw_scr[...] = kw
        wc_scr[...] = jnp.broadcast_to(wc_c, (n_tok, LANES))
        inter_scr[...] = jnp.zeros_like(inter_scr)

        def per_seq(b, carry):
            r0 = pl.multiple_of(b * seq_len, seq_len)
            sel = (rowi >= r0) & (rowi < r0 + seq_len)
            c_prev = c_ref[b, hd]
            inter_scr[...] += jnp.where(sel, _dot(qb, c_prev.astype(BF16)), 0.0)
            kw_b = jnp.where(sel, kw_scr[...], 0.0).astype(BF16)
            wc = wc_scr[pl.ds(r0, 1), :]
            co_ref[b, hd] = wc * c_prev + _dot_tn(kw_b, vb)
            return carry

        lax.fori_loop(0, SEQ_BLK, per_seq, 0)
        hm_scr[:, hd * ML_DV:(hd + 1) * ML_DV] = _head_out(s, v, wg, m_t, inter_scr[...], qn)
        no_ref[hd] = wc_c * n_tok_prev + _dot_exact01(tot, kw)
        mo_ref[hd] = jnp.broadcast_to(mnew_c, (n_tok, LANES))

    for hd in range(ML_HEADS):
        col = slice(hd * ML_DV, (hd + 1) * ML_DV)
        zo = z_scr[:, 3 * ML_WIDTH + hd * ML_DV:3 * ML_WIDTH + (hd + 1) * ML_DV]
        cat_scr[:, col] = _mlstm_out_norm(hm_scr[:, col], zo, onw_ref[:, col]).astype(BF16)

    cx = z_scr[:, C_COL:C_COL + SC_WIDTH] * z_scr[:, X_COL:X_COL + SC_WIDTH]
    sub = lax.broadcasted_iota(jnp.int32, (n_tok, SC_WIDTH), 0) % seq_len
    p1 = sc_ref[...]
    s1 = jnp.where(sub < 1, pltpu.roll(p1, n_tok - 1, 0), pltpu.roll(cx, 1, 0))
    s2 = jnp.where(sub < 2, p1, pltpu.roll(cx, 2, 0))
    cw = cw_ref[...]
    u = s2 * cw[0:1] + s1 * cw[1:2] + cx * cw[2:3]
    sco_ref[...] = pltpu.roll(cx, n_tok - (seq_len - 2), 0)
    cat_scr[:, ML_WIDTH:] = (z_scr[:, B_COL:B_COL + SC_WIDTH] * u).astype(BF16)

    y = _dot(cat_scr[...], wout_ref[...])
    xo_ref[...] = x_ref[...] + g1_scr[...] * y


def _mix_even_sample_call(x, mod, nw, win, bif, onw, cw, wout, c0, n_tok, m_tok, sc_pad):
    n_rows = x.shape[0]
    n_seq = c0.shape[0]
    seq_len = n_rows // n_seq
    blk = SEQ_BLK * seq_len
    nb = n_seq // SEQ_BLK
    const2 = lambda i: (0, 0)
    return pl.pallas_call(
        _mix_even_sample_kernel,
        grid=(nb,),
        in_specs=[
            pl.BlockSpec((blk, D_MODEL), lambda i: (i, 0)),
            pl.BlockSpec((SEQ_BLK, SUBLANES, D_MODEL), lambda i: (i, 0, 0)),
            pl.BlockSpec((1, D_MODEL), const2),
            pl.BlockSpec((D_MODEL, IN_W), const2),
            pl.BlockSpec((1, LANES), const2),
            pl.BlockSpec((1, ML_WIDTH), const2),
            pl.BlockSpec((SUBLANES, SC_WIDTH), const2),
            pl.BlockSpec((ML_WIDTH + SC_WIDTH, D_MODEL), const2),
            pl.BlockSpec((SEQ_BLK, ML_HEADS, ML_DK, ML_DV), lambda i: (i, 0, 0, 0)),
            pl.BlockSpec((ML_HEADS, blk, ML_DK), lambda i: (0, i, 0)),
            pl.BlockSpec((ML_HEADS, blk, LANES), lambda i: (0, i, 0)),
            pl.BlockSpec((blk, SC_WIDTH), lambda i: (i, 0)),
        ],
        out_specs=[
            pl.BlockSpec((blk, D_MODEL), lambda i: (i, 0)),
            pl.BlockSpec((SEQ_BLK, ML_HEADS, ML_DK, ML_DV), lambda i: (i, 0, 0, 0)),
            pl.BlockSpec((ML_HEADS, blk, ML_DK), lambda i: (0, i, 0)),
            pl.BlockSpec((ML_HEADS, blk, LANES), lambda i: (0, i, 0)),
            pl.BlockSpec((blk, SC_WIDTH), lambda i: (i, 0)),
        ],
        out_shape=[
            jax.ShapeDtypeStruct((n_rows, D_MODEL), F32),
            jax.ShapeDtypeStruct(c0.shape, F32),
            jax.ShapeDtypeStruct((ML_HEADS, n_rows, ML_DK), F32),
            jax.ShapeDtypeStruct((ML_HEADS, n_rows, LANES), F32),
            jax.ShapeDtypeStruct((n_rows, SC_WIDTH), F32),
        ],
        scratch_shapes=[
            pltpu.VMEM((blk, D_MODEL), F32),
            pltpu.VMEM((blk, IN_W), F32),
            pltpu.VMEM((blk, ML_WIDTH), F32),
            pltpu.VMEM((blk, ML_WIDTH + SC_WIDTH), BF16),
            pltpu.VMEM((blk, D_MODEL), F32),
            pltpu.VMEM((blk, ML_DV), F32),
            pltpu.VMEM((blk, ML_DK), F32),
            pltpu.VMEM((blk, LANES), F32),
        ],
        compiler_params=_cparams(1),
        name="mix_even_sample",
    )(x, mod, nw, win, bif, onw, cw, wout, c0, n_tok, m_tok, sc_pad)


def _qk_norm_rope(xb, gw, cos, sin):
    lane = lax.broadcasted_iota(jnp.int32, xb.shape, 1)
    half0 = lane < HEAD_DIM
    sq = xb * xb
    s0 = jnp.sum(jnp.where(half0, sq, 0.0), axis=-1, keepdims=True)
    s1 = jnp.sum(jnp.where(half0, 0.0, sq), axis=-1, keepdims=True)
    ms = jnp.where(half0, s0, s1) * (1.0 / HEAD_DIM)
    y = xb * lax.rsqrt(ms + EPS) * gw
    rot = jnp.where((lane & (HEAD_DIM - 1)) < HEAD_DIM // 2,
                    pltpu.roll(y, LANES - HEAD_DIM // 2, 1), pltpu.roll(y, HEAD_DIM // 2, 1))
    return y * cos + rot * sin


def _sink_col(sink8, reps):
    parts = [jnp.broadcast_to(sink8[r:r + 1, :], (reps, LANES)) for r in range(SUBLANES)]
    return jnp.concatenate(parts, axis=0)[:, 0:1]


def _attn_prompt_kernel(x_ref, mod_ref, nw_ref, wqkv_ref, qnw_ref, knw_ref, cos_ref, sin_ref,
                        sink_ref, wout_ref, xo_ref, ko_ref, vo_ref,
                        z_scr, qm_scr, k_scr, v_scr, o_scr):
    t = pl.program_id(1)
    n_qb = TM // WINDOW

    @pl.when(t == 0)
    def _():
        k_scr[0:WINDOW, :] = jnp.zeros((WINDOW, KV_W), BF16)
        v_scr[0:WINDOW, :] = jnp.zeros((WINDOW, KV_W), BF16)

    x = x_ref[0]
    h = _norm_mod(x, nw_ref[...], mod_ref[0, 1:2, :], mod_ref[0, 0:1, :]).astype(BF16)
    z_scr[...] = _dot(h, wqkv_ref[...])
    cos = cos_ref[...]
    sin = sin_ref[...]
    half0 = lax.broadcasted_iota(jnp.int32, (TM, LANES), 1) < HEAD_DIM
    qscale = HEAD_DIM ** -0.5
    for jb in range(Q_W // LANES):
        y = _qk_norm_rope(z_scr[:, jb * LANES:(jb + 1) * LANES], qnw_ref[...], cos, sin) * qscale
        qm_scr[2 * jb] = jnp.where(half0, y, 0.0).astype(BF16)
        qm_scr[2 * jb + 1] = jnp.where(half0, 0.0, y).astype(BF16)
    for p in range(KV_W // LANES):
        kf = _qk_norm_rope(z_scr[:, Q_W + p * LANES:Q_W + (p + 1) * LANES], knw_ref[...], cos, sin)
        ko_ref[0, :, p * LANES:(p + 1) * LANES] = kf[TM - WINDOW:TM]
        k_scr[WINDOW:WINDOW + TM, p * LANES:(p + 1) * LANES] = kf.astype(BF16)
    vf = z_scr[:, Q_W + KV_W:Q_W + 2 * KV_W]
    vo_ref[0] = vf[TM - WINDOW:TM]
    v_scr[WINDOW:WINDOW + TM, :] = vf.astype(BF16)

    r = lax.broadcasted_iota(jnp.int32, (8 * WINDOW, 2 * WINDOW), 0) % WINDOW
    c = lax.broadcasted_iota(jnp.int32, (8 * WINDOW, 2 * WINDOW), 1)
    valid = ((c < WINDOW) & (c > r)) | ((c >= WINDOW) & ((c - WINDOW) <= r))
    first_lim = jnp.where(t == 0, WINDOW, 0)
    half0q = lax.broadcasted_iota(jnp.int32, (4 * WINDOW, LANES), 1) < HEAD_DIM
    for qb in range(n_qb):
        rows = slice(qb * WINDOW, (qb + 1) * WINDOW)
        krows = slice(qb * WINDOW, (qb + 2) * WINDOW)
        vmask = (valid & (c >= first_lim)) if qb == 0 else valid
        for p in range(KV_W // LANES):
            kb = k_scr[krows, p * LANES:(p + 1) * LANES]
            vb = v_scr[krows, p * LANES:(p + 1) * LANES]
            qs = jnp.concatenate([qm_scr[2 * (4 * p + i) + e, rows, :] for e in range(2) for i in range(4)],
                                 axis=0)
            s = jnp.where(vmask, _dot_nt(qs, kb), -jnp.inf)
            sk = _sink_col(sink_ref[p], WINDOW)
            mx = jnp.maximum(jnp.max(s, axis=-1, keepdims=True), sk)
            pr = jnp.exp(s - mx)
            den = jnp.sum(pr, axis=-1, keepdims=True) + jnp.exp(sk - mx)
            o = _dot(pr.astype(BF16), vb) * (1.0 / den)
            merged = jnp.where(half0q, o[0:4 * WINDOW], o[4 * WINDOW:])
            for i in range(4):
                o_scr[rows, (4 * p + i) * LANES:(4 * p + i + 1) * LANES] = (
                    merged[i * WINDOW:(i + 1) * WINDOW].astype(BF16))

    y = _dot(o_scr[...], wout_ref[...])
    xo_ref[0] = x + mod_ref[0, 2:3, :] * y
    k_scr[0:WINDOW, :] = k_scr[TM:TM + WINDOW, :]
    v_scr[0:WINDOW, :] = v_scr[TM:TM + WINDOW, :]


def _attn_prompt_call(x, mod, nw, wqkv, qnw, knw, cos, sin, sink, wout):
    B, S, _ = x.shape
    nt = S // TM
    const2 = lambda b, t: (0, 0)
    return pl.pallas_call(
        _attn_prompt_kernel,
        grid=(B, nt),
        in_specs=[
            pl.BlockSpec((1, TM, D_MODEL), lambda b, t: (b, t, 0)),
            pl.BlockSpec((1, SUBLANES, D_MODEL), lambda b, t: (b, 0, 0)),
            pl.BlockSpec((1, D_MODEL), const2),
            pl.BlockSpec((D_MODEL, Q_W + 2 * KV_W), const2),
            pl.BlockSpec((1, LANES), const2),
            pl.BlockSpec((1, LANES), const2),
            pl.BlockSpec((TM, LANES), lambda b, t: (t, 0)),
            pl.BlockSpec((TM, LANES), lambda b, t: (t, 0)),
            pl.BlockSpec((2, SUBLANES, LANES), lambda b, t: (0, 0, 0)),
            pl.BlockSpec((Q_W, D_MODEL), const2),
        ],
        out_specs=[
            pl.BlockSpec((1, TM, D_MODEL), lambda b, t: (b, t, 0)),
            pl.BlockSpec((1, WINDOW, KV_W), lambda b, t: (b, 0, 0)),
            pl.BlockSpec((1, WINDOW, KV_W), lambda b, t: (b, 0, 0)),
        ],
        out_shape=[
            jax.ShapeDtypeStruct((B, S, D_MODEL), F32),
            jax.ShapeDtypeStruct((B, WINDOW, KV_W), F32),
            jax.ShapeDtypeStruct((B, WINDOW, KV_W), F32),
        ],
        scratch_shapes=[
            pltpu.VMEM((TM, Q_W + 2 * KV_W), F32),
            pltpu.VMEM((2 * Q_W // LANES, TM, LANES), BF16),
            pltpu.VMEM((TM + WINDOW, KV_W), BF16),
            pltpu.VMEM((TM + WINDOW, KV_W), BF16),
            pltpu.VMEM((TM, Q_W), BF16),
        ],
        compiler_params=_cparams(2),
        name="attn_prompt",
    )(x, mod, nw, wqkv, qnw, knw, cos, sin, sink, wout)


def _attn_sample_kernel(x_ref, mod_ref, nw_ref, wqkv_ref, qnw_ref, knw_ref, cos_ref, sin_ref,
                        sink_ref, wout_ref, kc_ref, vc_ref,
                        xo_ref, kco_ref, vco_ref,
                        h_scr, g1_scr, z_scr, qm_scr, kn_scr, o_scr):
    n_tok = x_ref.shape[0]
    seq_len = n_tok // SEQ_BLK
    win = kc_ref.shape[1]

    def modulate(b, carry):
        rows = pl.ds(pl.multiple_of(b * seq_len, seq_len), seq_len)
        m = mod_ref[b]
        h_scr[rows, :] = _norm_mod(x_ref[rows, :], nw_ref[...], m[1:2], m[0:1])
        g1_scr[rows, :] = jnp.broadcast_to(m[2:3], (seq_len, D_MODEL))
        return carry

    lax.fori_loop(0, SEQ_BLK, modulate, 0)
    z_scr[...] = _dot(h_scr[...].astype(BF16), wqkv_ref[...])
    cos = cos_ref[...]
    sin = sin_ref[...]
    half0 = lax.broadcasted_iota(jnp.int32, (n_tok, LANES), 1) < HEAD_DIM
    qscale = HEAD_DIM ** -0.5
    for jb in range(Q_W // LANES):
        y = _qk_norm_rope(z_scr[:, jb * LANES:(jb + 1) * LANES], qnw_ref[...], cos, sin) * qscale
        qm_scr[2 * jb] = jnp.where(half0, y, 0.0)
        qm_scr[2 * jb + 1] = jnp.where(half0, 0.0, y)
    for p in range(KV_W // LANES):
        kn_scr[:, p * LANES:(p + 1) * LANES] = _qk_norm_rope(
            z_scr[:, Q_W + p * LANES:Q_W + (p + 1) * LANES], knw_ref[...], cos, sin)

    n_q = 8 * seq_len
    tq = lax.broadcasted_iota(jnp.int32, (n_q, win), 0) % seq_len
    cc = lax.broadcasted_iota(jnp.int32, (n_q, win), 1)
    mask_old = cc > tq
    mask_new = (cc >= win - seq_len) & ((cc - (win - seq_len)) <= tq)
    half0q = lax.broadcasted_iota(jnp.int32, (n_q // 2, LANES), 1) < HEAD_DIM
    sks = [_sink_col(sink_ref[p], seq_len) for p in range(KV_W // LANES)]

    def per_seq(b, carry):
        r0 = pl.multiple_of(b * seq_len, seq_len)
        rows = pl.ds(r0, seq_len)
        kc = kc_ref[b]
        vc = vc_ref[b]
        knew = jnp.concatenate([kc[seq_len:], kn_scr[rows, :]], axis=0)
        vnew = jnp.concatenate([vc[seq_len:], z_scr[rows, Q_W + KV_W:Q_W + 2 * KV_W]], axis=0)
        kco_ref[b] = knew
        vco_ref[b] = vnew
        for p in range(KV_W // LANES):
            lanes = slice(p * LANES, (p + 1) * LANES)
            qs = jnp.concatenate([qm_scr[2 * (4 * p + i) + e, rows, :] for e in range(2) for i in range(4)],
                                 axis=0).astype(BF16)
            so = jnp.where(mask_old, _dot_nt(qs, kc[:, lanes].astype(BF16)), -jnp.inf)
            sn = jnp.where(mask_new, _dot_nt(qs, knew[:, lanes].astype(BF16)), -jnp.inf)
            sk = sks[p]
            mx = jnp.maximum(jnp.maximum(jnp.max(so, axis=-1, keepdims=True),
                                         jnp.max(sn, axis=-1, keepdims=True)), sk)
            po = jnp.exp(so - mx)
            pn = jnp.exp(sn - mx)
            den = (jnp.sum(po, axis=-1, keepdims=True) + jnp.sum(pn, axis=-1, keepdims=True)
                   + jnp.exp(sk - mx))
            o = (_dot(po.astype(BF16), vc[:, lanes].astype(BF16))
                 + _dot(pn.astype(BF16), vnew[:, lanes].astype(BF16))) * (1.0 / den)
            merged = jnp.where(half0q, o[0:n_q // 2], o[n_q // 2:])
            for i in range(4):
                o_scr[rows, (4 * p + i) * LANES:(4 * p + i + 1) * LANES] = merged[i * seq_len:(i + 1) * seq_len]
        return carry

    lax.fori_loop(0, SEQ_BLK, per_seq, 0)
    y = _dot(o_scr[...].astype(BF16), wout_ref[...])
    xo_ref[...] = x_ref[...] + g1_scr[...] * y


def _attn_sample_call(x, mod, nw, wqkv, qnw, knw, cos, sin, sink, wout, kc, vc):
    n_rows = x.shape[0]
    n_seq, win, _ = kc.shape
    seq_len = n_rows // n_seq
    blk = SEQ_BLK * seq_len
    nb = n_seq // SEQ_BLK
    const2 = lambda i: (0, 0)
    return pl.pallas_call(
        _attn_sample_kernel,
        grid=(nb,),
        in_specs=[
            pl.BlockSpec((blk, D_MODEL), lambda i: (i, 0)),
            pl.BlockSpec((SEQ_BLK, SUBLANES, D_MODEL), lambda i: (i, 0, 0)),
            pl.BlockSpec((1, D_MODEL), const2),
            pl.BlockSpec((D_MODEL, Q_W + 2 * KV_W), const2),
            pl.BlockSpec((1, LANES), const2),
            pl.BlockSpec((1, LANES), const2),
            pl.BlockSpec((blk, LANES), const2),
            pl.BlockSpec((blk, LANES), const2),
            pl.BlockSpec((2, SUBLANES, LANES), lambda i: (0, 0, 0)),
            pl.BlockSpec((Q_W, D_MODEL), const2),
            pl.BlockSpec((SEQ_BLK, win, KV_W), lambda i: (i, 0, 0)),
            pl.BlockSpec((SEQ_BLK, win, KV_W), lambda i: (i, 0, 0)),
        ],
        out_specs=[
            pl.BlockSpec((blk, D_MODEL), lambda i: (i, 0)),
            pl.BlockSpec((SEQ_BLK, win, KV_W), lambda i: (i, 0, 0)),
            pl.BlockSpec((SEQ_BLK, win, KV_W), lambda i: (i, 0, 0)),
        ],
        out_shape=[
            jax.ShapeDtypeStruct((n_rows, D_MODEL), F32),
            jax.ShapeDtypeStruct(kc.shape, F32),
            jax.ShapeDtypeStruct(vc.shape, F32),
        ],
        scratch_shapes=[
            pltpu.VMEM((blk, D_MODEL), F32),
            pltpu.VMEM((blk, D_MODEL), F32),
            pltpu.VMEM((blk, Q_W + 2 * KV_W), F32),
            pltpu.VMEM((2 * Q_W // LANES, blk, LANES), F32),
            pltpu.VMEM((blk, KV_W), F32),
            pltpu.VMEM((blk, Q_W), F32),
        ],
        compiler_params=_cparams(1),
        name="attn_sample",
    )(x, mod, nw, wqkv, qnw, knw, cos, sin, sink, wout, kc, vc)


def _rope_tables(pos):
    half = HEAD_DIM // 2
    inv = ROPE_THETA ** (-jnp.arange(half, dtype=F32) / half)
    ang = pos.astype(F32)[:, None] * inv[None, :]
    cos = jnp.cos(ang)
    sin = jnp.sin(ang)
    return jnp.tile(cos, (1, 4)), jnp.concatenate([-sin, sin, -sin, sin], axis=1)


def _prep_even(w_in, b_if, out_norm, conv_w, w_out):
    qkvo = w_in[:, :GATE_COL]
    zg = w_in[:, GATE_COL:GATE_COL + 2 * ML_HEADS]
    rest = w_in[:, GATE_COL + 2 * ML_HEADS:]
    gate_blk = jnp.concatenate(
        [zg, zg[:, ML_HEADS:], jnp.zeros((D_MODEL, LANES - 3 * ML_HEADS), w_in.dtype)], axis=1)
    win = jnp.concatenate([qkvo, gate_blk, rest], axis=1).astype(BF16)
    bif = jnp.concatenate([b_if, b_if[ML_HEADS:], jnp.zeros((LANES - 3 * ML_HEADS,), b_if.dtype)])[None]
    cw = jnp.pad(conv_w, ((0, SUBLANES - conv_w.shape[0]), (0, 0)))
    return win, bif, out_norm[None], cw, w_out.astype(BF16)


def _prep_attn(w_qkv, q_norm, k_norm, sink, w_out):
    perm = np.asarray(HEAD_PERM)
    wq = w_qkv[:, :Q_W].reshape(D_MODEL, ATT_HEADS, HEAD_DIM)[:, perm].reshape(D_MODEL, Q_W)
    wqkv = jnp.concatenate([wq, w_qkv[:, Q_W:]], axis=1).astype(BF16)
    wout = w_out.reshape(ATT_HEADS, HEAD_DIM, D_MODEL)[perm].reshape(Q_W, D_MODEL).astype(BF16)
    qnw = jnp.tile(q_norm, 2)[None]
    knw = jnp.tile(k_norm, 2)[None]
    idx = np.asarray([[perm[2 * (4 * p + i) + e] for e in range(2) for i in range(4)] for p in range(2)])
    sink_arr = jnp.broadcast_to(sink[idx][:, :, None], (2, SUBLANES, LANES)).astype(F32)
    return wqkv, qnw, knw, sink_arr, wout


def _prep_ffn(w_up, conv_w, w_down):
    wup = w_up.reshape(D_MODEL, 2, NCH, FC).transpose(2, 0, 1, 3).reshape(NCH, D_MODEL, 2 * FC).astype(BF16)
    cw = conv_w.reshape(conv_w.shape[0], 2, NCH, FC).transpose(2, 0, 1, 3).reshape(NCH, conv_w.shape[0], 2 * FC)
    cw = jnp.pad(cw, ((0, 0), (0, SUBLANES - conv_w.shape[0]), (0, 0)))
    wdn = w_down.reshape(NCH, FC, D_MODEL).astype(BF16)
    return wup, cw, wdn


def _ffn_state_to_chunks(st):
    n_seq = st.shape[0]
    return st.reshape(n_seq, 2, 2, NCH, FC).transpose(3, 1, 0, 2, 4).reshape(NCH, 2, n_seq, 2 * FC)


def _ffn_state_from_chunks(st):
    n_seq = st.shape[2]
    return st.reshape(NCH, 2, n_seq, 2, FC).transpose(2, 1, 3, 0, 4).reshape(n_seq, 2, 2 * D_FF)


def kernel(x_prompt, x_sample, c_prompt, c_sample, state_mlstm_C, state_mlstm_n, state_mlstm_m, state_sconv, cache_win_k, cache_win_v, state_ffn_conv, norm1, norm2, w_ada, b_ada, a_w_in, a_b_if, a_out_norm, a_conv_w, a_w_out, c_w_qkv, c_q_norm, c_k_norm, c_sink, c_w_out, f_w_up, f_conv_w, f_w_down):
    B, S, _ = x_prompt.shape
    NS, SL, _ = x_sample.shape
    assert S % TM == 0 and NS % SEQ_BLK == 0 and SEQ_BLK * SL == CHUNK and SL == SUBLANES

    n_c = B + NS
    rows = -(-n_c // SUBLANES) * SUBLANES
    c_all = jnp.pad(jnp.concatenate([c_prompt, c_sample], axis=0), ((0, rows - n_c), (0, 0)))
    mod = _ada_call(c_all, w_ada, b_ada).reshape(DEPTH, rows, 6, D_MODEL)
    mod = jnp.pad(mod, ((0, 0), (0, 0), (0, SUBLANES - 6), (0, 0)))
    mod_p = mod[:, :B]
    mod_s = mod[:, B:n_c]
    mod_s_kb = mod_s.transpose(0, 2, 1, 3)

    cos_p, sin_p = _rope_tables(jnp.arange(S, dtype=jnp.int32))
    cos_s, sin_s = _rope_tables(PAST_LEN + jnp.arange(SL, dtype=jnp.int32))
    cos_s = jnp.tile(cos_s, (SEQ_BLK, 1))
    sin_s = jnp.tile(sin_s, (SEQ_BLK, 1))

    xp = x_prompt
    xs = x_sample.reshape(NS * SL, D_MODEL)
    p_C, p_n, p_m, p_sc, p_wk, p_wv, p_ffn = [], [], [], [], [], [], []
    s_C, s_n, s_m, s_sc, s_wk, s_wv, s_ffn = [], [], [], [], [], [], []

    for l in range(DEPTH):
        nw1 = norm1[l][None]
        if l % 2 == 0:
            i = l // 2
            win, bif, onw, cw, wout = _prep_even(a_w_in[i], a_b_if[i], a_out_norm[i], a_conv_w[i], a_w_out[i])
            xp, co, no, mo, sco = _mix_even_prompt_call(xp, mod_p[l], nw1, win, bif, onw, cw, wout)
            p_C.append(co)
            p_n.append(no[:, :, 0, :])
            p_m.append(mo[:, :, 0, 0])
            p_sc.append(sco[:, SUBLANES - 2:, :])

            n_tok = jnp.repeat(state_mlstm_n[i].transpose(1, 0, 2), SL, axis=1)
            m_tok = jnp.broadcast_to(jnp.repeat(state_mlstm_m[i].T, SL, axis=1)[:, :, None],
                                     (ML_HEADS, NS * SL, LANES))
            sc_pad = jnp.pad(state_sconv[i], ((0, 0), (0, SL - 2), (0, 0))).reshape(NS * SL, SC_WIDTH)
            xs, co, no, mo, sco = _mix_even_sample_call(xs, mod_s[l], nw1, win, bif, onw, cw, wout,
                                                        state_mlstm_C[i], n_tok, m_tok, sc_pad)
            s_C.append(co)
            s_n.append(no[:, ::SL, :].transpose(1, 0, 2))
            s_m.append(mo[:, ::SL, 0].T)
            s_sc.append(sco.reshape(NS, SL, SC_WIDTH)[:, :2])
        else:
            j = l // 2
            wqkv, qnw, knw, sink, wout = _prep_attn(c_w_qkv[j], c_q_norm[j], c_k_norm[j], c_sink[j], c_w_out[j])
            xp, ko, vo = _attn_prompt_call(xp, mod_p[l], nw1, wqkv, qnw, knw, cos_p, sin_p, sink, wout)
            p_wk.append(ko.reshape(B, WINDOW, KV_HEADS, HEAD_DIM))
            p_wv.append(vo.reshape(B, WINDOW, KV_HEADS, HEAD_DIM))
            win_buf = cache_win_k.shape[2]
            xs, ko, vo = _attn_sample_call(xs, mod_s[l], nw1, wqkv, qnw, knw, cos_s, sin_s, sink, wout,
                                           cache_win_k[j].reshape(NS, win_buf, KV_W),
                                           cache_win_v[j].reshape(NS, win_buf, KV_W))
            s_wk.append(ko.reshape(NS, win_buf, KV_HEADS, HEAD_DIM))
            s_wv.append(vo.reshape(NS, win_buf, KV_HEADS, HEAD_DIM))

        nw2 = norm2[l][None]
        wup, cwf, wdn = _prep_ffn(f_w_up[l], f_conv_w[l], f_w_down[l])
        xp, st = _ffn_prompt_call(xp, mod_p[l], nw2, wup, cwf, wdn)
        st = st[:, :, SUBLANES - 2:, :].reshape(B, NCH, 2, 2, FC).transpose(0, 2, 3, 1, 4)
        p_ffn.append(st.reshape(B, 2, 2 * D_FF))

        xs_tb = xs.reshape(NS, SL, D_MODEL).transpose(1, 0, 2)
        xs_tb, st = _ffn_sample_call(xs_tb, mod_s_kb[l], nw2, _ffn_state_to_chunks(state_ffn_conv[l]),
                                     wup, cwf, wdn)
        xs = xs_tb.transpose(1, 0, 2).reshape(NS * SL, D_MODEL)
        s_ffn.append(_ffn_state_from_chunks(st))

    return (xp, xs.reshape(NS, SL, D_MODEL),
            jnp.stack(p_C), jnp.stack(p_n), jnp.stack(p_m), jnp.stack(p_sc),
            jnp.stack(p_wk), jnp.stack(p_wv), jnp.stack(p_ffn),
            jnp.stack(s_C), jnp.stack(s_n), jnp.stack(s_m), jnp.stack(s_sc),
            jnp.stack(s_wk), jnp.stack(s_wv), jnp.stack(s_ffn))
```

```python
import functools

import jax
import jax.numpy as jnp
import numpy as np
from jax import lax
from jax.experimental import pallas as pl
from jax.experimental.pallas import tpu as pltpu

F32 = jnp.float32
BF16 = jnp.bfloat16

D_MODEL = 1024
DEPTH = 4
PAST_LEN = 8192
ML_HEADS = 4
ML_DK = 128
ML_DV = 128
ML_WIDTH = ML_HEADS * ML_DV
SC_WIDTH = D_MODEL // 2
ATT_HEADS = 16
KV_HEADS = 4
HEAD_DIM = 64
WINDOW = 128
ROPE_THETA = 10000.0
D_FF = 2816
EPS = 1e-6

LANES = 128
SUBLANES = 8
VMEM_LIMIT = 56 * 1024 * 1024

TM = 512
CHUNK = 128
SEQ_BLK = 16
FC = 256
NCH = D_FF // FC
GATE_COL = 4 * ML_WIDTH
B_COL = GATE_COL + LANES
C_COL = B_COL + SC_WIDTH
X_COL = C_COL + SC_WIDTH
IN_W = X_COL + SC_WIDTH
Q_W = ATT_HEADS * HEAD_DIM
KV_W = KV_HEADS * HEAD_DIM
HEAD_PERM = (0, 4, 1, 5, 2, 6, 3, 7, 8, 12, 9, 13, 10, 14, 11, 15)


def _cparams(n_axes):
    return pltpu.CompilerParams(dimension_semantics=("arbitrary",) * n_axes,
                                vmem_limit_bytes=VMEM_LIMIT)


def _dot(a, b):
    return jnp.dot(a, b, preferred_element_type=F32)


def _dot_nt(a, b):
    return lax.dot_general(a, b, (((1,), (1,)), ((), ())), preferred_element_type=F32)


def _dot_tn(a, b):
    return lax.dot_general(a, b, (((0,), (0,)), ((), ())), preferred_element_type=F32)


def _dot_exact01(m, a):
    a1 = a.astype(BF16)
    r1 = a - a1.astype(F32)
    a2 = r1.astype(BF16)
    a3 = (r1 - a2.astype(F32)).astype(BF16)
    return _dot(m, a1) + _dot(m, a2) + _dot(m, a3)


def _norm_mod(x, nw, sc, sh):
    ms = jnp.mean(x * x, axis=-1, keepdims=True)
    return (x * lax.rsqrt(ms + EPS) * nw) * (1.0 + sc) + sh


def _sigmoid(x):
    return 1.0 / (1.0 + jnp.exp(-x))


def _log_sigmoid(x):
    return jnp.minimum(x, 0.0) - jnp.log(1.0 + jnp.exp(-jnp.abs(x)))


N_MOD = 6


def _ada_kernel(c_ref, w_ref, b_ref, o_ref):
    c = c_ref[...]
    s = (c * _sigmoid(c)).astype(BF16)
    o_ref[...] = _dot(s, w_ref[...].astype(BF16)) + b_ref[...]


def _ada_call(c_all, w_ada, b_ada):
    rows = c_all.shape[0]
    return pl.pallas_call(
        _ada_kernel,
        grid=(DEPTH, N_MOD),
        in_specs=[
            pl.BlockSpec((rows, D_MODEL), lambda l, k: (0, 0)),
            pl.BlockSpec((None, D_MODEL, D_MODEL), lambda l, k: (l, 0, k)),
            pl.BlockSpec((None, 1, D_MODEL), lambda l, k: (l, 0, k)),
        ],
        out_specs=pl.BlockSpec((None, None, rows, D_MODEL), lambda l, k: (l, k, 0, 0)),
        out_shape=jax.ShapeDtypeStruct((DEPTH, N_MOD, rows, D_MODEL), F32),
        compiler_params=_cparams(2),
        name="adaln_mod",
    )(c_all, w_ada, b_ada.reshape(DEPTH, 1, N_MOD * D_MODEL))


CAST_ROWS = 256


def _cast_kernel(w_ref, o_ref):
    o_ref[...] = w_ref[...].astype(BF16)


def _cast_call(w):
    n_l, rows, cols = w.shape
    tr = CAST_ROWS if rows % CAST_ROWS == 0 else rows
    return pl.pallas_call(
        _cast_kernel,
        grid=(n_l, rows // tr),
        in_specs=[pl.BlockSpec((None, tr, cols), lambda l, r: (l, r, 0))],
        out_specs=pl.BlockSpec((None, tr, cols), lambda l, r: (l, r, 0)),
        out_shape=jax.ShapeDtypeStruct(w.shape, BF16),
        compiler_params=_cparams(2),
        name="cast_bf16",
    )(w)


def _prep_win_kernel(w_ref, b_ref, o_ref, bo_ref):
    o_ref[:, 0:GATE_COL] = w_ref[:, 0:GATE_COL].astype(BF16)
    tail = w_ref[:, GATE_COL:]
    zg = tail[:, 0:2 * ML_HEADS]
    rows = zg.shape[0]
    o_ref[:, GATE_COL:B_COL] = jnp.concatenate(
        [zg, zg[:, ML_HEADS:], jnp.zeros((rows, LANES - 3 * ML_HEADS), F32)], axis=1).astype(BF16)
    o_ref[:, B_COL:IN_W] = tail[:, 2 * ML_HEADS:].astype(BF16)
    b = b_ref[...]
    bo_ref[...] = jnp.concatenate([b, b[:, ML_HEADS:], jnp.zeros((1, LANES - 3 * ML_HEADS), F32)], axis=1)


def _prep_win_call(a_w_in, a_b_if):
    n_l, _, in_a = a_w_in.shape
    return pl.pallas_call(
        _prep_win_kernel,
        grid=(n_l, D_MODEL // CAST_ROWS),
        in_specs=[pl.BlockSpec((None, CAST_ROWS, in_a), lambda l, r: (l, r, 0)),
                  pl.BlockSpec((None, 1, 2 * ML_HEADS), lambda l, r: (l, 0, 0))],
        out_specs=[pl.BlockSpec((None, CAST_ROWS, IN_W), lambda l, r: (l, r, 0)),
                   pl.BlockSpec((None, 1, LANES), lambda l, r: (l, 0, 0))],
        out_shape=[jax.ShapeDtypeStruct((n_l, D_MODEL, IN_W), BF16),
                   jax.ShapeDtypeStruct((n_l, 1, LANES), F32)],
        compiler_params=_cparams(2),
        name="prep_w_in",
    )(a_w_in, a_b_if.reshape(n_l, 1, 2 * ML_HEADS))


def _mod_row(mod_ref, kind, b):
    return mod_ref[kind, pl.ds(b, 1), :]


def _ffn_prompt_kernel(x_ref, mod_ref, nw_ref, wup_ref, cw_ref, wdn_ref,
                       xo_ref, st_ref, h_scr, act_scr, carry_scr):
    b = pl.program_id(0)
    t = pl.program_id(1)

    @pl.when(t == 0)
    def _():
        carry_scr[...] = jnp.zeros_like(carry_scr)

    x = x_ref[0]
    h_scr[...] = _norm_mod(x, nw_ref[...], _mod_row(mod_ref, 4, b), _mod_row(mod_ref, 3, b)).astype(BF16)
    for j in range(NCH):
        ys = []
        for col in (j * FC, D_FF + j * FC):
            cols = slice(col, col + FC)
            u = _dot(h_scr[...], wup_ref[:, cols])
            ys.append(_conv3_rows(u, carry_scr[:, cols], cw_ref[:, cols]))
            carry_scr[:, cols] = u[TM - SUBLANES:TM]
        g = ys[0]
        act_scr[:, j * FC:(j + 1) * FC] = (g * _sigmoid(g) * ys[1]).astype(BF16)
    y = _dot(act_scr[...], wdn_ref[...])
    xo_ref[0] = x + _mod_row(mod_ref, 5, b) * y
    st_ref[0] = carry_scr[...]


def _ffn_prompt_call(x, mod, n_seq_rows, l, nw, wup, cw, wdn):
    B, S, _ = x.shape
    nt = S // TM
    once = pl.Buffered(1)
    return pl.pallas_call(
        _ffn_prompt_kernel,
        grid=(B, nt),
        in_specs=[
            pl.BlockSpec((1, TM, D_MODEL), lambda b, t: (b, t, 0)),
            pl.BlockSpec((None, N_MOD, SUBLANES, D_MODEL), lambda b, t: (l, 0, n_seq_rows // SUBLANES, 0)),
            pl.BlockSpec((None, 1, D_MODEL), lambda b, t: (l, 0, 0)),
            pl.BlockSpec((None, D_MODEL, 2 * D_FF), lambda b, t: (l, 0, 0), pipeline_mode=once),
            pl.BlockSpec((None, 3, 2 * D_FF), lambda b, t: (l, 0, 0)),
            pl.BlockSpec((None, D_FF, D_MODEL), lambda b, t: (l, 0, 0), pipeline_mode=once),
        ],
        out_specs=[
            pl.BlockSpec((1, TM, D_MODEL), lambda b, t: (b, t, 0)),
            pl.BlockSpec((1, SUBLANES, 2 * D_FF), lambda b, t: (b, 0, 0)),
        ],
        out_shape=[
            jax.ShapeDtypeStruct((B, S, D_MODEL), F32),
            jax.ShapeDtypeStruct((B, SUBLANES, 2 * D_FF), F32),
        ],
        scratch_shapes=[
            pltpu.VMEM((TM, D_MODEL), BF16),
            pltpu.VMEM((TM, D_FF), BF16),
            pltpu.VMEM((SUBLANES, 2 * D_FF), F32),
        ],
        compiler_params=_cparams(2),
        name="ffn_prompt",
    )(x, mod, nw, wup, cw, wdn)


def _ffn_sample_kernel(x_ref, mod_ref, nw_ref, sg_ref, sa_ref, wg_ref, wa_ref, cg_ref, ca_ref, wdn_ref,
                       xo_ref, sgo_ref, sao_ref, h_scr, hb_scr, g2_scr, acc_scr):
    j = pl.program_id(0)
    n_seq = sg_ref.shape[0]
    n_rows = x_ref.shape[0]
    n_t = n_rows // n_seq

    @pl.when(j == 0)
    def _():
        def modulate(b, carry):
            rows = pl.ds(pl.multiple_of(b * n_t, n_t), n_t)
            h_scr[rows, :] = _norm_mod(x_ref[rows, :], nw_ref[...], _mod_row(mod_ref, 4, b), _mod_row(mod_ref, 3, b))
            g2_scr[rows, :] = jnp.broadcast_to(_mod_row(mod_ref, 5, b), (n_t, D_MODEL))
            return carry

        lax.fori_loop(0, n_seq, modulate, 0)
        hb_scr[...] = h_scr[...].astype(BF16)
        acc_scr[...] = jnp.zeros_like(acc_scr)

    sub = lax.broadcasted_iota(jnp.int32, (n_seq, n_t, FC), 1)
    ys = []
    for w_ref, c_ref, s_ref, so_ref in ((wg_ref, cg_ref, sg_ref, sgo_ref), (wa_ref, ca_ref, sa_ref, sao_ref)):
        u3 = _dot(hb_scr[...], w_ref[...]).reshape(n_seq, n_t, FC)
        cw = c_ref[...]
        p0 = jnp.broadcast_to(s_ref[:, 0:1, :], (n_seq, n_t, FC))
        p1 = jnp.broadcast_to(s_ref[:, 1:2, :], (n_seq, n_t, FC))
        s1 = jnp.where(sub < 1, p1, pltpu.roll(u3, 1, 1))
        s2 = jnp.where(sub < 1, p0, jnp.where(sub < 2, p1, pltpu.roll(u3, 2, 1)))
        ys.append((s2 * cw[0:1] + s1 * cw[1:2] + u3 * cw[2:3]).reshape(n_rows, FC))
        so_ref[...] = pltpu.roll(u3, 2, 1)[:, 0:2, :]
    g = ys[0]
    act = (g * _sigmoid(g) * ys[1]).astype(BF16)
    acc_scr[...] += _dot(act, wdn_ref[...])

    @pl.when(j == NCH - 1)
    def _():
        xo_ref[...] = x_ref[...] + g2_scr[...] * acc_scr[...]


def _ffn_sample_call(x, mod, l, nw, st, wup, cw, wdn):
    n_rows = x.shape[0]
    n_seq = st.shape[1]
    return pl.pallas_call(
        _ffn_sample_kernel,
        grid=(NCH,),
        in_specs=[
            pl.BlockSpec((n_rows, D_MODEL), lambda j: (0, 0)),
            pl.BlockSpec((None, N_MOD, n_seq, D_MODEL), lambda j: (l, 0, 0, 0)),
            pl.BlockSpec((None, 1, D_MODEL), lambda j: (l, 0, 0)),
            pl.BlockSpec((None, n_seq, 2, FC), lambda j: (l, 0, 0, j)),
            pl.BlockSpec((None, n_seq, 2, FC), lambda j: (l, 0, 0, NCH + j)),
            pl.BlockSpec((None, D_MODEL, FC), lambda j: (l, 0, j)),
            pl.BlockSpec((None, D_MODEL, FC), lambda j: (l, 0, NCH + j)),
            pl.BlockSpec((None, 3, FC), lambda j: (l, 0, j)),
            pl.BlockSpec((None, 3, FC), lambda j: (l, 0, NCH + j)),
            pl.BlockSpec((None, FC, D_MODEL), lambda j: (l, j, 0)),
        ],
        out_specs=[
            pl.BlockSpec((n_rows, D_MODEL), lambda j: (0, 0)),
            pl.BlockSpec((n_seq, 2, FC), lambda j: (0, 0, j)),
            pl.BlockSpec((n_seq, 2, FC), lambda j: (0, 0, j)),
        ],
        out_shape=[
            jax.ShapeDtypeStruct((n_rows, D_MODEL), F32),
            jax.ShapeDtypeStruct((n_seq, 2, D_FF), F32),
            jax.ShapeDtypeStruct((n_seq, 2, D_FF), F32),
        ],
        scratch_shapes=[
            pltpu.VMEM((n_rows, D_MODEL), F32),
            pltpu.VMEM((n_rows, D_MODEL), BF16),
            pltpu.VMEM((n_rows, D_MODEL), F32),
            pltpu.VMEM((n_rows, D_MODEL), F32),
        ],
        compiler_params=_cparams(1),
        name="ffn_sample",
    )(x, mod, nw, st, st, wup, wup, cw, cw, wdn)


def _chunk_masks(rows_per_seq):
    r = lax.broadcasted_iota(jnp.int32, (CHUNK, CHUNK), 0)
    c = lax.broadcasted_iota(jnp.int32, (CHUNK, CHUNK), 1)
    if rows_per_seq >= CHUNK:
        same = r >= 0
    else:
        same = (r // rows_per_seq) == (c // rows_per_seq)
    mask = same & (c <= r)
    return same, mask, mask.astype(BF16), same.astype(BF16)


def _gate_prep(gates, lmat, tot):
    lane = lax.broadcasted_iota(jnp.int32, (CHUNK, LANES), 1)
    a = jnp.where(lane < 4, gates, jnp.where(lane < 12, _log_sigmoid(gates), 0.0))
    bc = _dot_exact01(lmat, a)
    tc = _dot_exact01(tot, a)
    x = jnp.where(lane < 4, a, jnp.where(lane < 8, bc, tc))
    return a, bc, tc, x.T


def _head_chunk(hd, q, k, v, a, bc, tc, xt, mprev_c, qn, same, mask):
    neg = -jnp.inf
    b_c = bc[:, 4 + hd:5 + hd]
    ig_c = a[:, hd:hd + 1]
    bl_c = tc[:, 4 + hd:5 + hd]
    b_r = xt[4 + hd:5 + hd, :]
    ig_r = xt[hd:hd + 1, :]
    bl_r = xt[8 + hd:9 + hd, :]

    dl = jnp.where(mask, b_c - b_r + ig_r, neg)
    g_c = b_c + mprev_c
    m_t = jnp.maximum(jnp.max(dl, axis=-1, keepdims=True), g_c)
    w = jnp.exp(dl - m_t)
    wg = jnp.exp(g_c - m_t)
    s = _dot_nt(q.astype(BF16), k.astype(BF16)) * w

    a_r = jnp.where(same, bl_r - b_r + ig_r, neg)
    mnew_c = jnp.maximum(bl_c + mprev_c, jnp.max(a_r, axis=-1, keepdims=True))
    wc_c = jnp.exp(bl_c + mprev_c - mnew_c)
    ws_c = jnp.exp(bl_c - b_c + ig_c - mnew_c)
    kw = k * ws_c
    return s, wg, m_t, kw, wc_c, mnew_c


def _head_out(s, v, wg, m_t, inter, qn):
    num = wg * inter + _dot(s.astype(BF16), v.astype(BF16))
    den = wg * qn + jnp.sum(s, axis=-1, keepdims=True)
    return num * (1.0 / jnp.maximum(jnp.abs(den), jnp.exp(-m_t)))


def _conv3_rows(cx, prev8, cw):
    n = cx.shape[1]
    row = lax.broadcasted_iota(jnp.int32, (SUBLANES, n), 0)
    s1 = pltpu.roll(cx, 1, 0)
    s2 = pltpu.roll(cx, 2, 0)
    f1 = jnp.where(row < 1, pltpu.roll(prev8, 1, 0), s1[0:SUBLANES])
    f2 = jnp.where(row < 2, pltpu.roll(prev8, 2, 0), s2[0:SUBLANES])
    s1 = jnp.concatenate([f1, s1[SUBLANES:]], axis=0)
    s2 = jnp.concatenate([f2, s2[SUBLANES:]], axis=0)
    return s2 * cw[0:1] + s1 * cw[1:2] + cx * cw[2:3]


def _mlstm_out_norm(hm, zo, onw):
    ms = jnp.mean(hm * hm, axis=-1, keepdims=True)
    return hm * lax.rsqrt(ms + EPS) * onw * _sigmoid(zo)


def _mix_even_prompt_kernel(x_ref, mod_ref, nw_ref, win_ref, bif_ref, onw_ref, cw_ref, wout_ref,
                            xo_ref, co_ref, no_ref, mo_ref, sco_ref,
                            z_scr, hm_scr, cat_scr, c_scr, n_scr, m_scr, cc_scr):
    t = pl.program_id(1)

    @pl.when(t == 0)
    def _():
        c_scr[...] = jnp.zeros_like(c_scr)
        n_scr[...] = jnp.zeros_like(n_scr)
        m_scr[...] = jnp.zeros_like(m_scr)
        cc_scr[...] = jnp.zeros_like(cc_scr)

    x = x_ref[0]
    bi = pl.program_id(0)
    h = _norm_mod(x, nw_ref[...], _mod_row(mod_ref, 1, bi), _mod_row(mod_ref, 0, bi)).astype(BF16)
    z_scr[...] = _dot(h, win_ref[...])

    same, mask, lmat, tot = _chunk_masks(CHUNK)
    scale = ML_DK ** -0.5

    def chunk(c, carry):
        r0 = pl.multiple_of(c * CHUNK, CHUNK)
        rows = pl.ds(r0, CHUNK)
        gates = z_scr[rows, GATE_COL:GATE_COL + LANES] + bif_ref[...]
        a, bc, tc, xt = _gate_prep(gates, lmat, tot)
        for hd in range(ML_HEADS):
            col = slice(hd * ML_DK, (hd + 1) * ML_DK)
            q = z_scr[rows, col]
            k = z_scr[rows, ML_WIDTH + hd * ML_DK:ML_WIDTH + (hd + 1) * ML_DK] * scale
            v = z_scr[rows, 2 * ML_WIDTH + hd * ML_DV:2 * ML_WIDTH + (hd + 1) * ML_DV]
            n_prev = n_scr[hd, 0:1, :]
            m_prev = m_scr[hd, 0:1, :][:, 0:1]
            c_prev = c_scr[hd]
            qn = jnp.sum(q * n_prev, axis=-1, keepdims=True)
            s, wg, m_t, kw, wc_c, mnew_c = _head_chunk(hd, q, k, v, a, bc, tc, xt, m_prev, qn, same, mask)
            inter = _dot(q.astype(BF16), c_prev.astype(BF16))
            hm_scr[rows, hd * ML_DV:(hd + 1) * ML_DV] = _head_out(s, v, wg, m_t, inter, qn)
            wc = wc_c[0:1, 0:1]
            c_scr[hd] = wc * c_prev + _dot_tn(kw.astype(BF16), v.astype(BF16))
            n_new = wc * n_prev + jnp.sum(kw, axis=0, keepdims=True)
            n_scr[hd] = jnp.broadcast_to(n_new, (SUBLANES, ML_DK))
            m_scr[hd] = jnp.broadcast_to(mnew_c[0:1, 0:1], (SUBLANES, LANES))
        return carry

    lax.fori_loop(0, TM // CHUNK, chunk, 0)

    for hd in range(ML_HEADS):
        col = slice(hd * ML_DV, (hd + 1) * ML_DV)
        zo = z_scr[:, 3 * ML_WIDTH + hd * ML_DV:3 * ML_WIDTH + (hd + 1) * ML_DV]
        cat_scr[:, col] = _mlstm_out_norm(hm_scr[:, col], zo, onw_ref[:, col]).astype(BF16)

    cx = z_scr[:, C_COL:C_COL + SC_WIDTH] * z_scr[:, X_COL:X_COL + SC_WIDTH]
    u = _conv3_rows(cx, cc_scr[...], cw_ref[...])
    cc_scr[...] = cx[TM - SUBLANES:TM]
    cat_scr[:, ML_WIDTH:] = (z_scr[:, B_COL:B_COL + SC_WIDTH] * u).astype(BF16)

    y = _dot(cat_scr[...], wout_ref[...])
    xo_ref[0] = x + _mod_row(mod_ref, 2, bi) * y
    co_ref[0] = c_scr[...]
    no_ref[0] = n_scr[...]
    mo_ref[0] = m_scr[...]
    sco_ref[0] = cc_scr[...]


def _even_weight_specs(i, idx):
    once = pl.Buffered(1)
    return [
        pl.BlockSpec((None, D_MODEL, IN_W), idx, pipeline_mode=once),
        pl.BlockSpec((None, 1, LANES), idx),
        pl.BlockSpec((None, 1, ML_WIDTH), idx),
        pl.BlockSpec((None, 3, SC_WIDTH), idx),
        pl.BlockSpec((None, ML_WIDTH + SC_WIDTH, D_MODEL), idx, pipeline_mode=once),
    ]


def _mix_even_prompt_call(x, mod, n_seq_rows, l, nw, win, bif, onw, cw, wout):
    B, S, _ = x.shape
    nt = S // TM
    i = l // 2
    return pl.pallas_call(
        _mix_even_prompt_kernel,
        grid=(B, nt),
        in_specs=[
            pl.BlockSpec((1, TM, D_MODEL), lambda b, t: (b, t, 0)),
            pl.BlockSpec((None, N_MOD, SUBLANES, D_MODEL), lambda b, t: (l, 0, n_seq_rows // SUBLANES, 0)),
            pl.BlockSpec((None, 1, D_MODEL), lambda b, t: (l, 0, 0)),
        ] + _even_weight_specs(i, lambda b, t: (i, 0, 0)),
        out_specs=[
            pl.BlockSpec((1, TM, D_MODEL), lambda b, t: (b, t, 0)),
            pl.BlockSpec((1, ML_HEADS, ML_DK, ML_DV), lambda b, t: (b, 0, 0, 0)),
            pl.BlockSpec((1, ML_HEADS, SUBLANES, ML_DK), lambda b, t: (b, 0, 0, 0)),
            pl.BlockSpec((1, ML_HEADS, SUBLANES, LANES), lambda b, t: (b, 0, 0, 0)),
            pl.BlockSpec((1, SUBLANES, SC_WIDTH), lambda b, t: (b, 0, 0)),
        ],
        out_shape=[
            jax.ShapeDtypeStruct((B, S, D_MODEL), F32),
            jax.ShapeDtypeStruct((B, ML_HEADS, ML_DK, ML_DV), F32),
            jax.ShapeDtypeStruct((B, ML_HEADS, SUBLANES, ML_DK), F32),
            jax.ShapeDtypeStruct((B, ML_HEADS, SUBLANES, LANES), F32),
            jax.ShapeDtypeStruct((B, SUBLANES, SC_WIDTH), F32),
        ],
        scratch_shapes=[
            pltpu.VMEM((TM, IN_W), F32),
            pltpu.VMEM((TM, ML_WIDTH), F32),
            pltpu.VMEM((TM, ML_WIDTH + SC_WIDTH), BF16),
            pltpu.VMEM((ML_HEADS, ML_DK, ML_DV), F32),
            pltpu.VMEM((ML_HEADS, SUBLANES, ML_DK), F32),
            pltpu.VMEM((ML_HEADS, SUBLANES, LANES), F32),
            pltpu.VMEM((SUBLANES, SC_WIDTH), F32),
        ],
        compiler_params=_cparams(2),
        name="mix_even_prompt",
    )(x, mod, nw, win, bif, onw, cw, wout)


def _mix_even_sample_kernel(x_ref, mod_ref, nw_ref, win_ref, bif_ref, onw_ref, cw_ref, wout_ref,
                            c_ref, nt_ref, mt_ref, sc_ref,
                            xo_ref, co_ref, no_ref, mo_ref, sco_ref,
                            h_scr, z_scr, hm_scr, cat_scr, g1_scr, inter_scr, kw_scr, wc_scr):
    n_tok = x_ref.shape[0]
    seq_len = n_tok // SEQ_BLK

    def modulate(b, carry):
        rows = pl.ds(pl.multiple_of(b * seq_len, seq_len), seq_len)
        h_scr[rows, :] = _norm_mod(x_ref[rows, :], nw_ref[...], _mod_row(mod_ref, 1, b), _mod_row(mod_ref, 0, b))
        g1_scr[rows, :] = jnp.broadcast_to(_mod_row(mod_ref, 2, b), (seq_len, D_MODEL))
        return carry

    lax.fori_loop(0, SEQ_BLK, modulate, 0)
    z_scr[...] = _dot(h_scr[...].astype(BF16), win_ref[...])

    same, mask, lmat, tot = _chunk_masks(seq_len)
    scale = ML_DK ** -0.5
    gates = z_scr[:, GATE_COL:GATE_COL + LANES] + bif_ref[...]
    a, bc, tc, xt = _gate_prep(gates, lmat, tot)
    rowi = lax.broadcasted_iota(jnp.int32, (n_tok, LANES), 0)

    for hd in range(ML_HEADS):
        q = z_scr[:, hd * ML_DK:(hd + 1) * ML_DK]
        k = z_scr[:, ML_WIDTH + hd * ML_DK:ML_WIDTH + (hd + 1) * ML_DK] * scale
        v = z_scr[:, 2 * ML_WIDTH + hd * ML_DV:2 * ML_WIDTH + (hd + 1) * ML_DV]
        n_tok_prev = nt_ref[hd]
        m_prev_c = mt_ref[hd][:, 0:1]
        qn = jnp.sum(q * n_tok_prev, axis=-1, keepdims=True)
        s, wg, m_t, kw, wc_c, mnew_c = _head_chunk(hd, q, k, v, a, bc, tc, xt, m_prev_c, qn, same, mask)
        qb = q.astype(BF16)
        vb = v.astype(BF16)
        kw_scr[...] = kw
        wc_scr[...] = jnp.broadcast_to(wc_c, (n_tok, LANES))
        inter_scr[...] = jnp.zeros_like(inter_scr)

        def per_seq(b, carry):
            r0 = pl.multiple_of(b * seq_len, seq_len)
            sel = (rowi >= r0) & (rowi < r0 + seq_len)
            c_prev = c_ref[b, hd]
            inter_scr[...] += jnp.where(sel, _dot(qb, c_prev.astype(BF16)), 0.0)
            kw_b = jnp.where(sel, kw_scr[...], 0.0).astype(BF16)
            wc = wc_scr[pl.ds(r0, 1), :]
            co_ref[b, hd] = wc * c_prev + _dot_tn(kw_b, vb)
            return carry

        lax.fori_loop(0, SEQ_BLK, per_seq, 0)
        hm_scr[:, hd * ML_DV:(hd + 1) * ML_DV] = _head_out(s, v, wg, m_t, inter_scr[...], qn)
        no_ref[hd] = wc_c * n_tok_prev + _dot_exact01(tot, kw)
        mo_ref[hd] = jnp.broadcast_to(mnew_c, (n_tok, LANES))

    for hd in range(ML_HEADS):
        col = slice(hd * ML_DV, (hd + 1) * ML_DV)
        zo = z_scr[:, 3 * ML_WIDTH + hd * ML_DV:3 * ML_WIDTH + (hd + 1) * ML_DV]
        cat_scr[:, col] = _mlstm_out_norm(hm_scr[:, col], zo, onw_ref[:, col]).astype(BF16)

    cx = z_scr[:, C_COL:C_COL + SC_WIDTH] * z_scr[:, X_COL:X_COL + SC_WIDTH]
    sub = lax.broadcasted_iota(jnp.int32, (n_tok, SC_WIDTH), 0) % seq_len
    p1 = sc_ref[...]
    s1 = jnp.where(sub < 1, pltpu.roll(p1, n_tok - 1, 0), pltpu.roll(cx, 1, 0))
    s2 = jnp.where(sub < 2, p1, pltpu.roll(cx, 2, 0))
    cw = cw_ref[...]
    u = s2 * cw[0:1] + s1 * cw[1:2] + cx * cw[2:3]
    sco_ref[...] = pltpu.roll(cx, n_tok - (seq_len - 2), 0)
    cat_scr[:, ML_WIDTH:] = (z_scr[:, B_COL:B_COL + SC_WIDTH] * u).astype(BF16)

    y = _dot(cat_scr[...], wout_ref[...])
    xo_ref[...] = x_ref[...] + g1_scr[...] * y


def _mix_even_sample_call(x, mod, l, nw, win, bif, onw, cw, wout, c0, n_tok, m_tok, sc_pad):
    n_rows = x.shape[0]
    n_seq = c0.shape[1]
    seq_len = n_rows // n_seq
    blk = SEQ_BLK * seq_len
    nb = n_seq // SEQ_BLK
    li = l // 2
    return pl.pallas_call(
        _mix_even_sample_kernel,
        grid=(nb,),
        in_specs=[
            pl.BlockSpec((blk, D_MODEL), lambda i: (i, 0)),
            pl.BlockSpec((None, N_MOD, SEQ_BLK, D_MODEL), lambda i: (l, 0, i, 0)),
            pl.BlockSpec((None, 1, D_MODEL), lambda i: (l, 0, 0)),
        ] + _even_weight_specs(li, lambda i: (li, 0, 0)) + [
            pl.BlockSpec((None, SEQ_BLK, ML_HEADS, ML_DK, ML_DV), lambda i: (li, i, 0, 0, 0)),
            pl.BlockSpec((ML_HEADS, blk, ML_DK), lambda i: (0, i, 0)),
            pl.BlockSpec((ML_HEADS, blk, LANES), lambda i: (0, i, 0)),
            pl.BlockSpec((blk, SC_WIDTH), lambda i: (i, 0)),
        ],
        out_specs=[
            pl.BlockSpec((blk, D_MODEL), lambda i: (i, 0)),
            pl.BlockSpec((SEQ_BLK, ML_HEADS, ML_DK, ML_DV), lambda i: (i, 0, 0, 0)),
            pl.BlockSpec((ML_HEADS, blk, ML_DK), lambda i: (0, i, 0)),
            pl.BlockSpec((ML_HEADS, blk, LANES), lambda i: (0, i, 0)),
            pl.BlockSpec((blk, SC_WIDTH), lambda i: (i, 0)),
        ],
        out_shape=[
            jax.ShapeDtypeStruct((n_rows, D_MODEL), F32),
            jax.ShapeDtypeStruct(c0.shape[1:], F32),
            jax.ShapeDtypeStruct((ML_HEADS, n_rows, ML_DK), F32),
            jax.ShapeDtypeStruct((ML_HEADS, n_rows, LANES), F32),
            jax.ShapeDtypeStruct((n_rows, SC_WIDTH), F32),
        ],
        scratch_shapes=[
            pltpu.VMEM((blk, D_MODEL), F32),
            pltpu.VMEM((blk, IN_W), F32),
            pltpu.VMEM((blk, ML_WIDTH), F32),
            pltpu.VMEM((blk, ML_WIDTH + SC_WIDTH), BF16),
            pltpu.VMEM((blk, D_MODEL), F32),
            pltpu.VMEM((blk, ML_DV), F32),
            pltpu.VMEM((blk, ML_DK), F32),
            pltpu.VMEM((blk, LANES), F32),
        ],
        compiler_params=_cparams(1),
        name="mix_even_sample",
    )(x, mod, nw, win, bif, onw, cw, wout, c0, n_tok, m_tok, sc_pad)


def _qk_norm_rope(xb, gw, cos, sin):
    lane = lax.broadcasted_iota(jnp.int32, xb.shape, 1)
    half0 = lane < HEAD_DIM
    sq = xb * xb
    s0 = jnp.sum(jnp.where(half0, sq, 0.0), axis=-1, keepdims=True)
    s1 = jnp.sum(jnp.where(half0, 0.0, sq), axis=-1, keepdims=True)
    ms = jnp.where(half0, s0, s1) * (1.0 / HEAD_DIM)
    y = xb * lax.rsqrt(ms + EPS) * gw
    rot = jnp.where((lane & (HEAD_DIM - 1)) < HEAD_DIM // 2,
                    pltpu.roll(y, LANES - HEAD_DIM // 2, 1), pltpu.roll(y, HEAD_DIM // 2, 1))
    return y * cos + rot * sin


def _sink_col(sink8, reps):
    parts = [jnp.broadcast_to(sink8[r:r + 1, :], (reps, LANES)) for r in range(SUBLANES)]
    return jnp.concatenate(parts, axis=0)[:, 0:1]


def _attn_prompt_kernel(x_ref, mod_ref, nw_ref, wqkv_ref, qnw_ref, knw_ref, cos_ref, sin_ref,
                        sink_ref, wout_ref, xo_ref, ko_ref, vo_ref,
                        z_scr, qm_scr, k_scr, v_scr, o_scr):
    t = pl.program_id(1)
    n_qb = TM // WINDOW

    @pl.when(t == 0)
    def _():
        k_scr[0:WINDOW, :] = jnp.zeros((WINDOW, KV_W), BF16)
        v_scr[0:WINDOW, :] = jnp.zeros((WINDOW, KV_W), BF16)

    x = x_ref[0]
    bi = pl.program_id(0)
    h = _norm_mod(x, nw_ref[...], _mod_row(mod_ref, 1, bi), _mod_row(mod_ref, 0, bi)).astype(BF16)
    z_scr[...] = _dot(h, wqkv_ref[...])
    cos = cos_ref[...]
    sin = sin_ref[...]
    half0 = lax.broadcasted_iota(jnp.int32, (TM, LANES), 1) < HEAD_DIM
    qscale = HEAD_DIM ** -0.5
    for jb in range(Q_W // LANES):
        y = _qk_norm_rope(z_scr[:, jb * LANES:(jb + 1) * LANES], qnw_ref[...], cos, sin) * qscale
        qm_scr[2 * jb] = jnp.where(half0, y, 0.0).astype(BF16)
        qm_scr[2 * jb + 1] = jnp.where(half0, 0.0, y).astype(BF16)
    for p in range(KV_W // LANES):
        kf = _qk_norm_rope(z_scr[:, Q_W + p * LANES:Q_W + (p + 1) * LANES], knw_ref[...], cos, sin)
        ko_ref[0, :, p * LANES:(p + 1) * LANES] = kf[TM - WINDOW:TM]
        k_scr[WINDOW:WINDOW + TM, p * LANES:(p + 1) * LANES] = kf.astype(BF16)
    vf = z_scr[:, Q_W + KV_W:Q_W + 2 * KV_W]
    vo_ref[0] = vf[TM - WINDOW:TM]
    v_scr[WINDOW:WINDOW + TM, :] = vf.astype(BF16)

    r = lax.broadcasted_iota(jnp.int32, (8 * WINDOW, 2 * WINDOW), 0) % WINDOW
    c = lax.broadcasted_iota(jnp.int32, (8 * WINDOW, 2 * WINDOW), 1)
    valid = ((c < WINDOW) & (c > r)) | ((c >= WINDOW) & ((c - WINDOW) <= r))
    first_lim = jnp.where(t == 0, WINDOW, 0)
    half0q = lax.broadcasted_iota(jnp.int32, (4 * WINDOW, LANES), 1) < HEAD_DIM
    for qb in range(n_qb):
        rows = slice(qb * WINDOW, (qb + 1) * WINDOW)
        krows = slice(qb * WINDOW, (qb + 2) * WINDOW)
        vmask = (valid & (c >= first_lim)) if qb == 0 else valid
        for p in range(KV_W // LANES):
            kb = k_scr[krows, p * LANES:(p + 1) * LANES]
            vb = v_scr[krows, p * LANES:(p + 1) * LANES]
            qs = jnp.concatenate([qm_scr[2 * (4 * p + i) + e, rows, :] for e in range(2) for i in range(4)],
                                 axis=0)
            s = jnp.where(vmask, _dot_nt(qs, kb), -jnp.inf)
            sk = _sink_col(sink_ref[p], WINDOW)
            mx = jnp.maximum(jnp.max(s, axis=-1, keepdims=True), sk)
            pr = jnp.exp(s - mx)
            den = jnp.sum(pr, axis=-1, keepdims=True) + jnp.exp(sk - mx)
            o = _dot(pr.astype(BF16), vb) * (1.0 / den)
            merged = jnp.where(half0q, o[0:4 * WINDOW], o[4 * WINDOW:])
            for i in range(4):
                o_scr[rows, (4 * p + i) * LANES:(4 * p + i + 1) * LANES] = (
                    merged[i * WINDOW:(i + 1) * WINDOW].astype(BF16))

    y = _dot(o_scr[...], wout_ref[...])
    xo_ref[0] = x + _mod_row(mod_ref, 2, bi) * y
    k_scr[0:WINDOW, :] = k_scr[TM:TM + WINDOW, :]
    v_scr[0:WINDOW, :] = v_scr[TM:TM + WINDOW, :]


def _attn_prompt_call(x, mod, n_seq_rows, l, nw, wqkv, qnw, knw, cos, sin, sink, wout):
    B, S, _ = x.shape
    nt = S // TM
    const2 = lambda b, t: (0, 0)
    return pl.pallas_call(
        _attn_prompt_kernel,
        grid=(B, nt),
        in_specs=[
            pl.BlockSpec((1, TM, D_MODEL), lambda b, t: (b, t, 0)),
            pl.BlockSpec((None, N_MOD, SUBLANES, D_MODEL), lambda b, t: (l, 0, n_seq_rows // SUBLANES, 0)),
            pl.BlockSpec((None, 1, D_MODEL), lambda b, t: (l, 0, 0)),
            pl.BlockSpec((D_MODEL, Q_W + 2 * KV_W), const2),
            pl.BlockSpec((1, LANES), const2),
            pl.BlockSpec((1, LANES), const2),
            pl.BlockSpec((TM, LANES), lambda b, t: (t, 0)),
            pl.BlockSpec((TM, LANES), lambda b, t: (t, 0)),
            pl.BlockSpec((2, SUBLANES, LANES), lambda b, t: (0, 0, 0)),
            pl.BlockSpec((Q_W, D_MODEL), const2),
        ],
        out_specs=[
            pl.BlockSpec((1, TM, D_MODEL), lambda b, t: (b, t, 0)),
            pl.BlockSpec((1, WINDOW, KV_W), lambda b, t: (b, 0, 0)),
            pl.BlockSpec((1, WINDOW, KV_W), lambda b, t: (b, 0, 0)),
        ],
        out_shape=[
            jax.ShapeDtypeStruct((B, S, D_MODEL), F32),
            jax.ShapeDtypeStruct((B, WINDOW, KV_W), F32),
            jax.ShapeDtypeStruct((B, WINDOW, KV_W), F32),
        ],
        scratch_shapes=[
            pltpu.VMEM((TM, Q_W + 2 * KV_W), F32),
            pltpu.VMEM((2 * Q_W // LANES, TM, LANES), BF16),
            pltpu.VMEM((TM + WINDOW, KV_W), BF16),
            pltpu.VMEM((TM + WINDOW, KV_W), BF16),
            pltpu.VMEM((TM, Q_W), BF16),
        ],
        compiler_params=_cparams(2),
        name="attn_prompt",
    )(x, mod, nw, wqkv, qnw, knw, cos, sin, sink, wout)


def _attn_sample_kernel(x_ref, mod_ref, nw_ref, wqkv_ref, qnw_ref, knw_ref, cos_ref, sin_ref,
                        sink_ref, wout_ref, kc_ref, vc_ref,
                        xo_ref, kco_ref, vco_ref,
                        h_scr, g1_scr, z_scr, qm_scr, kn_scr, o_scr):
    n_tok = x_ref.shape[0]
    seq_len = n_tok // SEQ_BLK
    win = kc_ref.shape[1]

    def modulate(b, carry):
        rows = pl.ds(pl.multiple_of(b * seq_len, seq_len), seq_len)
        h_scr[rows, :] = _norm_mod(x_ref[rows, :], nw_ref[...], _mod_row(mod_ref, 1, b), _mod_row(mod_ref, 0, b))
        g1_scr[rows, :] = jnp.broadcast_to(_mod_row(mod_ref, 2, b), (seq_len, D_MODEL))
        return carry

    lax.fori_loop(0, SEQ_BLK, modulate, 0)
    z_scr[...] = _dot(h_scr[...].astype(BF16), wqkv_ref[...])
    cos = cos_ref[...]
    sin = sin_ref[...]
    half0 = lax.broadcasted_iota(jnp.int32, (n_tok, LANES), 1) < HEAD_DIM
    qscale = HEAD_DIM ** -0.5
    for jb in range(Q_W // LANES):
        y = _qk_norm_rope(z_scr[:, jb * LANES:(jb + 1) * LANES], qnw_ref[...], cos, sin) * qscale
        qm_scr[2 * jb] = jnp.where(half0, y, 0.0)
        qm_scr[2 * jb + 1] = jnp.where(half0, 0.0, y)
    for p in range(KV_W // LANES):
        kn_scr[:, p * LANES:(p + 1) * LANES] = _qk_norm_rope(
            z_scr[:, Q_W + p * LANES:Q_W + (p + 1) * LANES], knw_ref[...], cos, sin)

    n_q = 8 * seq_len
    tq = lax.broadcasted_iota(jnp.int32, (n_q, win), 0) % seq_len
    cc = lax.broadcasted_iota(jnp.int32, (n_q, win), 1)
    mask_old = cc > tq
    mask_new = (cc >= win - seq_len) & ((cc - (win - seq_len)) <= tq)
    half0q = lax.broadcasted_iota(jnp.int32, (n_q // 2, LANES), 1) < HEAD_DIM
    sks = [_sink_col(sink_ref[p], seq_len) for p in range(KV_W // LANES)]

    def per_seq(b, carry):
        r0 = pl.multiple_of(b * seq_len, seq_len)
        rows = pl.ds(r0, seq_len)
        kc = kc_ref[b]
        vc = vc_ref[b]
        knew = jnp.concatenate([kc[seq_len:], kn_scr[rows, :]], axis=0)
        vnew = jnp.concatenate([vc[seq_len:], z_scr[rows, Q_W + KV_W:Q_W + 2 * KV_W]], axis=0)
        kco_ref[b] = knew
        vco_ref[b] = vnew
        for p in range(KV_W // LANES):
            lanes = slice(p * LANES, (p + 1) * LANES)
            qs = jnp.concatenate([qm_scr[2 * (4 * p + i) + e, rows, :] for e in range(2) for i in range(4)],
                                 axis=0).astype(BF16)
            so = jnp.where(mask_old, _dot_nt(qs, kc[:, lanes].astype(BF16)), -jnp.inf)
            sn = jnp.where(mask_new, _dot_nt(qs, knew[:, lanes].astype(BF16)), -jnp.inf)
            sk = sks[p]
            mx = jnp.maximum(jnp.maximum(jnp.max(so, axis=-1, keepdims=True),
                                         jnp.max(sn, axis=-1, keepdims=True)), sk)
            po = jnp.exp(so - mx)
            pn = jnp.exp(sn - mx)
            den = (jnp.sum(po, axis=-1, keepdims=True) + jnp.sum(pn, axis=-1, keepdims=True)
                   + jnp.exp(sk - mx))
            o = (_dot(po.astype(BF16), vc[:, lanes].astype(BF16))
                 + _dot(pn.astype(BF16), vnew[:, lanes].astype(BF16))) * (1.0 / den)
            merged = jnp.where(half0q, o[0:n_q // 2], o[n_q // 2:])
            for i in range(4):
                o_scr[rows, (4 * p + i) * LANES:(4 * p + i + 1) * LANES] = merged[i * seq_len:(i + 1) * seq_len]
        return carry

    lax.fori_loop(0, SEQ_BLK, per_seq, 0)
    y = _dot(o_scr[...].astype(BF16), wout_ref[...])
    xo_ref[...] = x_ref[...] + g1_scr[...] * y


def _attn_sample_call(x, mod, l, nw, wqkv, qnw, knw, cos, sin, sink, wout, kc, vc):
    n_rows = x.shape[0]
    _, n_seq, win, _ = kc.shape
    lj = l // 2
    seq_len = n_rows // n_seq
    blk = SEQ_BLK * seq_len
    nb = n_seq // SEQ_BLK
    const2 = lambda i: (0, 0)
    return pl.pallas_call(
        _attn_sample_kernel,
        grid=(nb,),
        in_specs=[
            pl.BlockSpec((blk, D_MODEL), lambda i: (i, 0)),
            pl.BlockSpec((None, N_MOD, SEQ_BLK, D_MODEL), lambda i: (l, 0, i, 0)),
            pl.BlockSpec((None, 1, D_MODEL), lambda i: (l, 0, 0)),
            pl.BlockSpec((D_MODEL, Q_W + 2 * KV_W), const2),
            pl.BlockSpec((1, LANES), const2),
            pl.BlockSpec((1, LANES), const2),
            pl.BlockSpec((blk, LANES), const2),
            pl.BlockSpec((blk, LANES), const2),
            pl.BlockSpec((2, SUBLANES, LANES), lambda i: (0, 0, 0)),
            pl.BlockSpec((Q_W, D_MODEL), const2),
            pl.BlockSpec((None, SEQ_BLK, win, KV_W), lambda i: (lj, i, 0, 0)),
            pl.BlockSpec((None, SEQ_BLK, win, KV_W), lambda i: (lj, i, 0, 0)),
        ],
        out_specs=[
            pl.BlockSpec((blk, D_MODEL), lambda i: (i, 0)),
            pl.BlockSpec((SEQ_BLK, win, KV_W), lambda i: (i, 0, 0)),
            pl.BlockSpec((SEQ_BLK, win, KV_W), lambda i: (i, 0, 0)),
        ],
        out_shape=[
            jax.ShapeDtypeStruct((n_rows, D_MODEL), F32),
            jax.ShapeDtypeStruct(kc.shape[1:], F32),
            jax.ShapeDtypeStruct(vc.shape[1:], F32),
        ],
        scratch_shapes=[
            pltpu.VMEM((blk, D_MODEL), F32),
            pltpu.VMEM((blk, D_MODEL), F32),
            pltpu.VMEM((blk, Q_W + 2 * KV_W), F32),
            pltpu.VMEM((2 * Q_W // LANES, blk, LANES), F32),
            pltpu.VMEM((blk, KV_W), F32),
            pltpu.VMEM((blk, Q_W), F32),
        ],
        compiler_params=_cparams(1),
        name="attn_sample",
    )(x, mod, nw, wqkv, qnw, knw, cos, sin, sink, wout, kc, vc)


def _rope_tables(pos):
    half = HEAD_DIM // 2
    inv = ROPE_THETA ** (-jnp.arange(half, dtype=F32) / half)
    ang = pos.astype(F32)[:, None] * inv[None, :]
    cos = jnp.cos(ang)
    sin = jnp.sin(ang)
    return jnp.tile(cos, (1, 4)), jnp.concatenate([-sin, sin, -sin, sin], axis=1)


def _prep_attn(w_qkv, q_norm, k_norm, sink, w_out):
    perm = np.asarray(HEAD_PERM)
    wq = w_qkv[:, :Q_W].reshape(D_MODEL, ATT_HEADS, HEAD_DIM)[:, perm].reshape(D_MODEL, Q_W)
    wqkv = jnp.concatenate([wq, w_qkv[:, Q_W:]], axis=1).astype(BF16)
    wout = w_out.reshape(ATT_HEADS, HEAD_DIM, D_MODEL)[perm].reshape(Q_W, D_MODEL).astype(BF16)
    qnw = jnp.tile(q_norm, 2)[None]
    knw = jnp.tile(k_norm, 2)[None]
    idx = np.asarray([[perm[2 * (4 * p + i) + e] for e in range(2) for i in range(4)] for p in range(2)])
    sink_arr = jnp.broadcast_to(sink[idx][:, :, None], (2, SUBLANES, LANES)).astype(F32)
    return wqkv, qnw, knw, sink_arr, wout


def kernel(x_prompt, x_sample, c_prompt, c_sample, state_mlstm_C, state_mlstm_n, state_mlstm_m, state_sconv, cache_win_k, cache_win_v, state_ffn_conv, norm1, norm2, w_ada, b_ada, a_w_in, a_b_if, a_out_norm, a_conv_w, a_w_out, c_w_qkv, c_q_norm, c_k_norm, c_sink, c_w_out, f_w_up, f_conv_w, f_w_down):
    B, S, _ = x_prompt.shape
    NS, SL, _ = x_sample.shape
    assert S % TM == 0 and NS % SEQ_BLK == 0 and SEQ_BLK * SL == CHUNK and SL == SUBLANES

    assert B <= SUBLANES
    c_all = jnp.concatenate([c_sample, c_prompt, jnp.zeros((SUBLANES - B, D_MODEL), F32)], axis=0)
    mod = _ada_call(c_all, w_ada, b_ada)

    win_all, bif_all = _prep_win_call(a_w_in, a_b_if)
    wout_a_all = _cast_call(a_w_out)
    wup_all = _cast_call(f_w_up)
    wdn_all = _cast_call(f_w_down)
    onw_all = a_out_norm.reshape(-1, 1, ML_WIDTH)
    norm1_r = norm1.reshape(DEPTH, 1, D_MODEL)
    norm2_r = norm2.reshape(DEPTH, 1, D_MODEL)
    win_buf = cache_win_k.shape[2]
    kc_all = cache_win_k.reshape(-1, NS, win_buf, KV_W)
    vc_all = cache_win_v.reshape(-1, NS, win_buf, KV_W)

    cos_p, sin_p = _rope_tables(jnp.arange(S, dtype=jnp.int32))
    cos_s, sin_s = _rope_tables(PAST_LEN + jnp.arange(SL, dtype=jnp.int32))
    cos_s = jnp.tile(cos_s, (SEQ_BLK, 1))
    sin_s = jnp.tile(sin_s, (SEQ_BLK, 1))

    xp = x_prompt
    xs = x_sample.reshape(NS * SL, D_MODEL)
    p_C, p_n, p_m, p_sc, p_wk, p_wv, p_ffn = [], [], [], [], [], [], []
    s_C, s_n, s_m, s_sc, s_wk, s_wv, s_ffn = [], [], [], [], [], [], []

    for l in range(DEPTH):
        if l % 2 == 0:
            i = l // 2
            even_w = (norm1_r, win_all, bif_all, onw_all, a_conv_w, wout_a_all)
            xp, co, no, mo, sco = _mix_even_prompt_call(xp, mod, NS, l, *even_w)
            p_C.append(co)
            p_n.append(no[:, :, 0, :])
            p_m.append(mo[:, :, 0, 0])
            p_sc.append(sco[:, SUBLANES - 2:, :])

            n_tok = jnp.repeat(state_mlstm_n[i].transpose(1, 0, 2), SL, axis=1)
            m_tok = jnp.broadcast_to(jnp.repeat(state_mlstm_m[i].T, SL, axis=1)[:, :, None],
                                     (ML_HEADS, NS * SL, LANES))
            sc_pad = jnp.pad(state_sconv[i], ((0, 0), (0, SL - 2), (0, 0))).reshape(NS * SL, SC_WIDTH)
            xs, co, no, mo, sco = _mix_even_sample_call(xs, mod, l, *even_w, state_mlstm_C, n_tok, m_tok, sc_pad)
            s_C.append(co)
            s_n.append(no[:, ::SL, :].transpose(1, 0, 2))
            s_m.append(mo[:, ::SL, 0].T)
            s_sc.append(sco.reshape(NS, SL, SC_WIDTH)[:, :2])
        else:
            j = l // 2
            wqkv, qnw, knw, sink, wout = _prep_attn(c_w_qkv[j], c_q_norm[j], c_k_norm[j], c_sink[j], c_w_out[j])
            xp, ko, vo = _attn_prompt_call(xp, mod, NS, l, norm1_r, wqkv, qnw, knw, cos_p, sin_p, sink, wout)
            p_wk.append(ko.reshape(B, WINDOW, KV_HEADS, HEAD_DIM))
            p_wv.append(vo.reshape(B, WINDOW, KV_HEADS, HEAD_DIM))
            xs, ko, vo = _attn_sample_call(xs, mod, l, norm1_r, wqkv, qnw, knw, cos_s, sin_s, sink, wout,
                                           kc_all, vc_all)
            s_wk.append(ko.reshape(NS, win_buf, KV_HEADS, HEAD_DIM))
            s_wv.append(vo.reshape(NS, win_buf, KV_HEADS, HEAD_DIM))

        xp, st = _ffn_prompt_call(xp, mod, NS, l, norm2_r, wup_all, f_conv_w, wdn_all)
        p_ffn.append(st[:, SUBLANES - 2:, :])
        xs, sg, sa = _ffn_sample_call(xs, mod, l, norm2_r, state_ffn_conv, wup_all, f_conv_w, wdn_all)
        s_ffn.append(jnp.concatenate([sg, sa], axis=-1))

    return (xp, xs.reshape(NS, SL, D_MODEL),
            jnp.stack(p_C), jnp.stack(p_n), jnp.stack(p_m), jnp.stack(p_sc),
            jnp.stack(p_wk), jnp.stack(p_wv), jnp.stack(p_ffn),
            jnp.stack(s_C), jnp.stack(s_n), jnp.stack(s_m), jnp.stack(s_sc),
            jnp.stack(s_wk), jnp.stack(s_wv), jnp.stack(s_ffn))
```

```python
import functools

import jax
import jax.numpy as jnp
import numpy as np
from jax import lax
from jax.experimental import pallas as pl
from jax.experimental.pallas import tpu as pltpu

F32 = jnp.float32
BF16 = jnp.bfloat16

D_MODEL = 1024
DEPTH = 4
PAST_LEN = 8192
ML_HEADS = 4
ML_DK = 128
ML_DV = 128
ML_WIDTH = ML_HEADS * ML_DV
SC_WIDTH = D_MODEL // 2
ATT_HEADS = 16
KV_HEADS = 4
HEAD_DIM = 64
WINDOW = 128
ROPE_THETA = 10000.0
D_FF = 2816
EPS = 1e-6

LANES = 128
SUBLANES = 8
VMEM_LIMIT = 56 * 1024 * 1024

TM = 512
CHUNK = 128
SEQ_BLK = 16
FC = 256
NCH = D_FF // FC
Q_COL = 0
V_COL = Q_COL + ML_WIDTH
O_COL = V_COL + ML_WIDTH
G1_COL = O_COL + ML_WIDTH
G2_COL = G1_COL + LANES
B_COL = G2_COL + LANES
C_COL = B_COL + SC_WIDTH
X_COL = C_COL + SC_WIDTH
IN_W = X_COL + SC_WIDTH
KG_ROWS = ML_WIDTH + 16
SRC_K = ML_WIDTH
SRC_V = 2 * ML_WIDTH
SRC_G = 4 * ML_WIDTH
SRC_B = SRC_G + 2 * ML_HEADS
Q_W = ATT_HEADS * HEAD_DIM
KV_W = KV_HEADS * HEAD_DIM
HEAD_PERM = (0, 4, 1, 5, 2, 6, 3, 7, 8, 12, 9, 13, 10, 14, 11, 15)


def _cparams(n_axes):
    return pltpu.CompilerParams(dimension_semantics=("arbitrary",) * n_axes,
                                vmem_limit_bytes=VMEM_LIMIT)


def _stacked_call(kernel, n_in, stacked, **kw):
    prev = [(o, a) for o, a in sorted(stacked.items()) if a is not None]
    specs = list(kw.pop("in_specs")) + [pl.BlockSpec(memory_space=pl.ANY)] * len(prev)
    aliases = {n_in + k: o for k, (o, _) in enumerate(prev)}

    def body(*refs):
        return kernel(*refs[:n_in], *refs[n_in + len(prev):])

    call = pl.pallas_call(body, in_specs=specs, input_output_aliases=aliases, **kw)
    return lambda *args: call(*args, *[a for _, a in prev])


def _dot(a, b):
    return jnp.dot(a, b, preferred_element_type=F32)


def _dot_nt(a, b):
    return lax.dot_general(a, b, (((1,), (1,)), ((), ())), preferred_element_type=F32)


def _dot_tn(a, b):
    return lax.dot_general(a, b, (((0,), (0,)), ((), ())), preferred_element_type=F32)


def _dot_exact01(m, a):
    a1 = a.astype(BF16)
    r1 = a - a1.astype(F32)
    a2 = r1.astype(BF16)
    a3 = (r1 - a2.astype(F32)).astype(BF16)
    return _dot(m, a1) + _dot(m, a2) + _dot(m, a3)


def _norm_mod(x, nw, sc, sh):
    ms = jnp.mean(x * x, axis=-1, keepdims=True)
    return (x * lax.rsqrt(ms + EPS) * nw) * (1.0 + sc) + sh


def _sigmoid(x):
    return 1.0 / (1.0 + jnp.exp(-x))


def _log_sigmoid(x):
    return jnp.minimum(x, 0.0) - jnp.log(1.0 + jnp.exp(-jnp.abs(x)))


N_MOD = 6


def _ada_kernel(c_ref, w_ref, b_ref, o_ref):
    c = c_ref[...]
    s = (c * _sigmoid(c)).astype(BF16)
    o_ref[...] = _dot(s, w_ref[...].astype(BF16)) + b_ref[...]


def _ada_call(c_all, w_ada, b_ada):
    rows = c_all.shape[0]
    return pl.pallas_call(
        _ada_kernel,
        grid=(DEPTH, N_MOD),
        in_specs=[
            pl.BlockSpec((rows, D_MODEL), lambda l, k: (0, 0)),
            pl.BlockSpec((None, D_MODEL, D_MODEL), lambda l, k: (l, 0, k)),
            pl.BlockSpec((None, 1, D_MODEL), lambda l, k: (l, 0, k)),
        ],
        out_specs=pl.BlockSpec((None, None, rows, D_MODEL), lambda l, k: (l, k, 0, 0)),
        out_shape=jax.ShapeDtypeStruct((DEPTH, N_MOD, rows, D_MODEL), F32),
        compiler_params=_cparams(2),
        name="adaln_mod",
    )(c_all, w_ada, b_ada.reshape(DEPTH, 1, N_MOD * D_MODEL))


CAST_ROWS = 256


def _cast_kernel(w_ref, o_ref):
    o_ref[...] = w_ref[...].astype(BF16)


def _cast_call(w):
    n_l, rows, cols = w.shape
    tr = CAST_ROWS if rows % CAST_ROWS == 0 else rows
    return pl.pallas_call(
        _cast_kernel,
        grid=(n_l, rows // tr),
        in_specs=[pl.BlockSpec((None, tr, cols), lambda l, r: (l, r, 0))],
        out_specs=pl.BlockSpec((None, tr, cols), lambda l, r: (l, r, 0)),
        out_shape=jax.ShapeDtypeStruct(w.shape, BF16),
        compiler_params=_cparams(2),
        name="cast_bf16",
    )(w)


def _prep_win_kernel(w_ref, o_ref):
    o_ref[:, Q_COL:Q_COL + ML_WIDTH] = w_ref[:, 0:ML_WIDTH].astype(BF16)
    o_ref[:, V_COL:G1_COL] = w_ref[:, SRC_V:SRC_G].astype(BF16)
    tail = w_ref[:, SRC_G:]
    ig = tail[:, 0:ML_HEADS]
    fg = tail[:, ML_HEADS:2 * ML_HEADS]
    pad = jnp.zeros((ig.shape[0], LANES - 2 * ML_HEADS), F32)
    o_ref[:, G1_COL:G2_COL] = jnp.concatenate([ig, ig, pad], axis=1).astype(BF16)
    o_ref[:, G2_COL:B_COL] = jnp.concatenate([fg, fg, pad], axis=1).astype(BF16)
    o_ref[:, B_COL:IN_W] = tail[:, 2 * ML_HEADS:].astype(BF16)


def _prep_win_call(a_w_in):
    n_l, _, in_a = a_w_in.shape
    return pl.pallas_call(
        _prep_win_kernel,
        grid=(n_l, D_MODEL // CAST_ROWS),
        in_specs=[pl.BlockSpec((None, CAST_ROWS, in_a), lambda l, r: (l, r, 0))],
        out_specs=pl.BlockSpec((None, CAST_ROWS, IN_W), lambda l, r: (l, r, 0)),
        out_shape=jax.ShapeDtypeStruct((n_l, D_MODEL, IN_W), BF16),
        compiler_params=_cparams(2),
        name="prep_w_in",
    )(a_w_in)


def _mod_row(mod_ref, kind, b):
    return mod_ref[kind, pl.ds(b, 1), :]


def _ffn_prompt_kernel(x_ref, mod_ref, nw_ref, wup_ref, cw_ref, wdn_ref,
                       xo_ref, st_ref, h_scr, act_scr, carry_scr):
    b = pl.program_id(0)
    t = pl.program_id(1)

    @pl.when(t == 0)
    def _():
        carry_scr[...] = jnp.zeros_like(carry_scr)

    x = x_ref[0]
    h_scr[...] = _norm_mod(x, nw_ref[...], _mod_row(mod_ref, 4, b), _mod_row(mod_ref, 3, b)).astype(BF16)
    for j in range(NCH):
        ys = []
        for col in (j * FC, D_FF + j * FC):
            cols = slice(col, col + FC)
            u = _dot(h_scr[...], wup_ref[:, cols])
            ys.append(_conv3_rows(u, carry_scr[:, cols], cw_ref[:, cols]))
            carry_scr[:, cols] = u[TM - SUBLANES:TM]
        g = ys[0]
        act_scr[:, j * FC:(j + 1) * FC] = (g * _sigmoid(g) * ys[1]).astype(BF16)
    y = _dot(act_scr[...], wdn_ref[...])
    xo_ref[0] = x + _mod_row(mod_ref, 5, b) * y
    st_ref[0] = carry_scr[...]


def _ffn_prompt_call(x, mod, n_seq_rows, l, nw, wup, cw, wdn):
    B, S, _ = x.shape
    nt = S // TM
    once = pl.Buffered(1)
    return pl.pallas_call(
        _ffn_prompt_kernel,
        grid=(B, nt),
        in_specs=[
            pl.BlockSpec((1, TM, D_MODEL), lambda b, t: (b, t, 0)),
            pl.BlockSpec((None, N_MOD, SUBLANES, D_MODEL), lambda b, t: (l, 0, n_seq_rows // SUBLANES, 0)),
            pl.BlockSpec((None, 1, D_MODEL), lambda b, t: (l, 0, 0)),
            pl.BlockSpec((None, D_MODEL, 2 * D_FF), lambda b, t: (l, 0, 0), pipeline_mode=once),
            pl.BlockSpec((None, 3, 2 * D_FF), lambda b, t: (l, 0, 0)),
            pl.BlockSpec((None, D_FF, D_MODEL), lambda b, t: (l, 0, 0), pipeline_mode=once),
        ],
        out_specs=[
            pl.BlockSpec((1, TM, D_MODEL), lambda b, t: (b, t, 0)),
            pl.BlockSpec((1, SUBLANES, 2 * D_FF), lambda b, t: (b, 0, 0)),
        ],
        out_shape=[
            jax.ShapeDtypeStruct((B, S, D_MODEL), F32),
            jax.ShapeDtypeStruct((B, SUBLANES, 2 * D_FF), F32),
        ],
        scratch_shapes=[
            pltpu.VMEM((TM, D_MODEL), BF16),
            pltpu.VMEM((TM, D_FF), BF16),
            pltpu.VMEM((SUBLANES, 2 * D_FF), F32),
        ],
        compiler_params=_cparams(2),
        name="ffn_prompt",
    )(x, mod, nw, wup, cw, wdn)


def _ffn_sample_kernel(x_ref, mod_ref, nw_ref, sg_ref, sa_ref, wg_ref, wa_ref, cg_ref, ca_ref, wdn_ref,
                       xo_ref, sgo_ref, sao_ref, h_scr, hb_scr, g2_scr, acc_scr):
    j = pl.program_id(0)
    n_seq = sg_ref.shape[0]
    n_rows = x_ref.shape[0]
    n_t = n_rows // n_seq

    @pl.when(j == 0)
    def _():
        def modulate(b, carry):
            rows = pl.ds(pl.multiple_of(b * n_t, n_t), n_t)
            h_scr[rows, :] = _norm_mod(x_ref[rows, :], nw_ref[...], _mod_row(mod_ref, 4, b), _mod_row(mod_ref, 3, b))
            g2_scr[rows, :] = jnp.broadcast_to(_mod_row(mod_ref, 5, b), (n_t, D_MODEL))
            return carry

        lax.fori_loop(0, n_seq, modulate, 0)
        hb_scr[...] = h_scr[...].astype(BF16)
        acc_scr[...] = jnp.zeros_like(acc_scr)

    sub = lax.broadcasted_iota(jnp.int32, (n_seq, n_t, FC), 1)
    ys = []
    for w_ref, c_ref, s_ref, so_ref in ((wg_ref, cg_ref, sg_ref, sgo_ref), (wa_ref, ca_ref, sa_ref, sao_ref)):
        u3 = _dot(hb_scr[...], w_ref[...]).reshape(n_seq, n_t, FC)
        cw = c_ref[...]
        p0 = jnp.broadcast_to(s_ref[:, 0:1, :], (n_seq, n_t, FC))
        p1 = jnp.broadcast_to(s_ref[:, 1:2, :], (n_seq, n_t, FC))
        s1 = jnp.where(sub < 1, p1, pltpu.roll(u3, 1, 1))
        s2 = jnp.where(sub < 1, p0, jnp.where(sub < 2, p1, pltpu.roll(u3, 2, 1)))
        ys.append((s2 * cw[0:1] + s1 * cw[1:2] + u3 * cw[2:3]).reshape(n_rows, FC))
        so_ref[...] = pltpu.roll(u3, 2, 1)[:, 0:2, :]
    g = ys[0]
    act = (g * _sigmoid(g) * ys[1]).astype(BF16)
    acc_scr[...] += _dot(act, wdn_ref[...])

    @pl.when(j == NCH - 1)
    def _():
        xo_ref[...] = x_ref[...] + g2_scr[...] * acc_scr[...]


def _ffn_sample_call(x, mod, l, nw, st, wup, cw, wdn, sg_prev, sa_prev):
    n_rows = x.shape[0]
    n_seq = st.shape[1]
    return _stacked_call(
        _ffn_sample_kernel, 10, {1: sg_prev, 2: sa_prev},
        grid=(NCH,),
        in_specs=[
            pl.BlockSpec((n_rows, D_MODEL), lambda j: (0, 0)),
            pl.BlockSpec((None, N_MOD, n_seq, D_MODEL), lambda j: (l, 0, 0, 0)),
            pl.BlockSpec((None, 1, D_MODEL), lambda j: (l, 0, 0)),
            pl.BlockSpec((None, n_seq, 2, FC), lambda j: (l, 0, 0, j)),
            pl.BlockSpec((None, n_seq, 2, FC), lambda j: (l, 0, 0, NCH + j)),
            pl.BlockSpec((None, D_MODEL, FC), lambda j: (l, 0, j)),
            pl.BlockSpec((None, D_MODEL, FC), lambda j: (l, 0, NCH + j)),
            pl.BlockSpec((None, 3, FC), lambda j: (l, 0, j)),
            pl.BlockSpec((None, 3, FC), lambda j: (l, 0, NCH + j)),
            pl.BlockSpec((None, FC, D_MODEL), lambda j: (l, j, 0)),
        ],
        out_specs=[
            pl.BlockSpec((n_rows, D_MODEL), lambda j: (0, 0)),
            pl.BlockSpec((None, n_seq, 2, FC), lambda j: (l, 0, 0, j)),
            pl.BlockSpec((None, n_seq, 2, FC), lambda j: (l, 0, 0, j)),
        ],
        out_shape=[
            jax.ShapeDtypeStruct((n_rows, D_MODEL), F32),
            jax.ShapeDtypeStruct((DEPTH, n_seq, 2, D_FF), F32),
            jax.ShapeDtypeStruct((DEPTH, n_seq, 2, D_FF), F32),
        ],
        scratch_shapes=[
            pltpu.VMEM((n_rows, D_MODEL), F32),
            pltpu.VMEM((n_rows, D_MODEL), BF16),
            pltpu.VMEM((n_rows, D_MODEL), F32),
            pltpu.VMEM((n_rows, D_MODEL), F32),
        ],
        compiler_params=_cparams(1),
        name="ffn_sample",
    )(x, mod, nw, st, st, wup, wup, cw, cw, wdn)


def _chunk_consts(seq_len):
    r = lax.broadcasted_iota(jnp.int32, (CHUNK, CHUNK), 0)
    c = lax.broadcasted_iota(jnp.int32, (CHUNK, CHUNK), 1)
    if seq_len >= CHUNK:
        same = r >= 0
    else:
        same = (r // seq_len) == (c // seq_len)
    mask = same & (c <= r)
    lmat = mask.astype(BF16)
    lmat_t = (same & (r <= c)).astype(BF16)
    return mask, lmat, lmat_t, same.astype(BF16)


def _dot_exact01_r(a, m):
    a1 = a.astype(BF16)
    r1 = a - a1.astype(F32)
    a2 = r1.astype(BF16)
    a3 = (r1 - a2.astype(F32)).astype(BF16)
    return _dot(a1, m) + _dot(a2, m) + _dot(a3, m)


def _seq_max_lanes(x, seq_len):
    n = x.shape[1]
    pos = lax.broadcasted_iota(jnp.int32, x.shape, 1)
    d = 1
    while d < seq_len:
        partner = jnp.where((pos & d) == 0, pltpu.roll(x, n - d, 1), pltpu.roll(x, d, 1))
        x = jnp.maximum(x, partner)
        d *= 2
    return x


def _seq_prefix_max_rows(x, seq_len):
    pos = lax.broadcasted_iota(jnp.int32, x.shape, 0) & (seq_len - 1)
    d = 1
    while d < seq_len:
        x = jnp.where(pos >= d, jnp.maximum(x, pltpu.roll(x, d, 0)), x)
        d *= 2
    return x


def _seq_last_row(x, seq_len):
    n, w = x.shape
    if seq_len >= n:
        return jnp.broadcast_to(x[n - 1:n], x.shape)
    x3 = x.reshape(n // seq_len, seq_len, w)
    return jnp.broadcast_to(x3[:, seq_len - 1:seq_len, :], x3.shape).reshape(n, w)


def _gates_rows(gt, bias_r, mp_r, lmat_t, tot, seq_len):
    ig = gt[0:SUBLANES] + bias_r[0:SUBLANES]
    lf = _log_sigmoid(gt[SUBLANES:] + bias_r[SUBLANES:])
    b = _dot_exact01_r(lf, lmat_t)
    bl = _dot_exact01_r(lf, tot)
    v = ig - b
    mn = bl + jnp.maximum(mp_r, _seq_max_lanes(v, seq_len))
    return v, jnp.exp(bl + v - mn), jnp.exp(bl + mp_r - mn), mn


def _gates_cols(g1, g2, bias_c, mp_c, lmat, seq_len):
    lane = lax.broadcasted_iota(jnp.int32, g1.shape, 1)
    ig = g1 + bias_c[0:1]
    lf = jnp.where(lane < 2 * ML_HEADS, _log_sigmoid(g2 + bias_c[1:2]), 0.0)
    b = _dot_exact01(lmat, lf)
    cm = _seq_prefix_max_rows(ig - b, seq_len)
    g = b + mp_c
    mt = jnp.maximum(b + cm, g)
    return b, cm, b - mt, jnp.exp(g - mt), jnp.exp(-mt)


def _outer_sum_lhs(u):
    lane = lax.broadcasted_iota(jnp.int32, u.shape, 1)
    hi = u.astype(BF16).astype(F32)
    lo = u - hi
    return jnp.where(lane < ML_HEADS, hi, jnp.where(lane < 2 * ML_HEADS, lo,
                     jnp.where(lane < 4 * ML_HEADS, 1.0, 0.0))).astype(BF16)


def _outer_sum_rhs(v_r, hd):
    row = lax.broadcasted_iota(jnp.int32, v_r.shape, 0)
    hi = v_r.astype(BF16).astype(F32)
    lo = v_r - hi
    pick = (row == hd) | (row == ML_HEADS + hd)
    top = jnp.where(pick, 1.0, 0.0)
    bot = jnp.where(row == hd, hi, jnp.where(row == ML_HEADS + hd, lo, 0.0))
    r16 = jnp.concatenate([top, bot], axis=0).astype(BF16)
    return jnp.concatenate([r16, jnp.zeros((LANES - 2 * SUBLANES, v_r.shape[1]), BF16)], axis=0)


def _decayed_scores(q, kt, lhsc, v_r, hd, mask):
    e = _dot(lhsc, _outer_sum_rhs(v_r, hd))
    return _dot(q.astype(BF16), kt.astype(BF16)) * jnp.where(mask, jnp.exp(e), 0.0)


def _conv3_rows(cx, prev8, cw):
    n = cx.shape[1]
    row = lax.broadcasted_iota(jnp.int32, (SUBLANES, n), 0)
    s1 = pltpu.roll(cx, 1, 0)
    s2 = pltpu.roll(cx, 2, 0)
    f1 = jnp.where(row < 1, pltpu.roll(prev8, 1, 0), s1[0:SUBLANES])
    f2 = jnp.where(row < 2, pltpu.roll(prev8, 2, 0), s2[0:SUBLANES])
    s1 = jnp.concatenate([f1, s1[SUBLANES:]], axis=0)
    s2 = jnp.concatenate([f2, s2[SUBLANES:]], axis=0)
    return s2 * cw[0:1] + s1 * cw[1:2] + cx * cw[2:3]


def _mlstm_out_norm(hm, zo, onw):
    ms = jnp.mean(hm * hm, axis=-1, keepdims=True)
    return hm * lax.rsqrt(ms + EPS) * onw * _sigmoid(zo)


def _even_tail(z_scr, hm_scr, cat_scr, onw_ref):
    for hd in range(ML_HEADS):
        col = slice(hd * ML_DV, (hd + 1) * ML_DV)
        zo = z_scr[:, O_COL + hd * ML_DV:O_COL + (hd + 1) * ML_DV]
        cat_scr[:, col] = _mlstm_out_norm(hm_scr[:, col], zo, onw_ref[:, col]).astype(BF16)


def _mix_even_prompt_kernel(x_ref, mod_ref, nw_ref, win_ref, wkg_ref, bifc_ref, bifr_ref, onw_ref, cw_ref,
                            wout_ref, xo_ref, co_ref, no_ref, mo_ref, sco_ref,
                            z_scr, zt_scr, hm_scr, cat_scr, cn_scr, mrow_scr, mlane_scr, cc_scr):
    t = pl.program_id(1)

    @pl.when(t == 0)
    def _():
        cn_scr[...] = jnp.zeros_like(cn_scr)
        mrow_scr[...] = jnp.zeros_like(mrow_scr)
        mlane_scr[...] = jnp.zeros_like(mlane_scr)
        cc_scr[...] = jnp.zeros_like(cc_scr)

    x = x_ref[0]
    bi = pl.program_id(0)
    h = _norm_mod(x, nw_ref[...], _mod_row(mod_ref, 1, bi), _mod_row(mod_ref, 0, bi)).astype(BF16)
    z_scr[...] = _dot(h, win_ref[...])
    zt_scr[...] = _dot_nt(wkg_ref[...], h)

    mask, lmat, lmat_t, tot = _chunk_consts(CHUNK)
    scale = ML_DK ** -0.5
    ones_v = jnp.ones((CHUNK, ML_DV), BF16)

    def chunk(c, carry):
        r0 = pl.multiple_of(c * CHUNK, CHUNK)
        rows = pl.ds(r0, CHUNK)
        v_r, ws_r, wc_r, mn_r = _gates_rows(zt_scr[ML_WIDTH:KG_ROWS, rows], bifr_ref[...], mrow_scr[...],
                                            lmat_t, tot, CHUNK)
        mp_c = mlane_scr[0:1, :]
        b, cm, u, wg, pbe = _gates_cols(z_scr[rows, G1_COL:G1_COL + LANES], z_scr[rows, G2_COL:G2_COL + LANES],
                                        bifc_ref[...], mp_c, lmat, CHUNK)
        mrow_scr[...] = mn_r
        mlane_scr[...] = jnp.broadcast_to(b[CHUNK - 1:CHUNK] + jnp.maximum(mp_c, cm[CHUNK - 1:CHUNK]),
                                          (SUBLANES, LANES))
        lhsc = _outer_sum_lhs(u)
        for hd in range(ML_HEADS):
            q = z_scr[rows, Q_COL + hd * ML_DK:Q_COL + (hd + 1) * ML_DK]
            v = z_scr[rows, V_COL + hd * ML_DV:V_COL + (hd + 1) * ML_DV]
            kt = zt_scr[hd * ML_DK:(hd + 1) * ML_DK, rows] * scale
            s = _decayed_scores(q, kt, lhsc, v_r, hd, mask)
            v1 = jnp.concatenate([v.astype(BF16), ones_v], axis=1)
            cn = cn_scr[hd]
            lhs = jnp.concatenate([s.astype(BF16), (q * wg[:, hd:hd + 1]).astype(BF16)], axis=1)
            out = _dot(lhs, jnp.concatenate([v1, cn.astype(BF16)], axis=0))
            r = 1.0 / jnp.maximum(jnp.abs(out[:, ML_DV:]), pbe[:, hd:hd + 1])
            hm_scr[rows, hd * ML_DV:(hd + 1) * ML_DV] = out[:, :ML_DV] * r
            kwt = (kt * ws_r[hd:hd + 1, :]).astype(BF16)
            wc = wc_r[hd:hd + 1, :]
            cn_scr[hd] = jnp.concatenate([wc, wc], axis=1) * cn + _dot(kwt, v1)
        return carry

    lax.fori_loop(0, TM // CHUNK, chunk, 0, unroll=True)

    _even_tail(z_scr, hm_scr, cat_scr, onw_ref)
    cx = z_scr[:, C_COL:C_COL + SC_WIDTH] * z_scr[:, X_COL:X_COL + SC_WIDTH]
    u = _conv3_rows(cx, cc_scr[...], cw_ref[...])
    cc_scr[...] = cx[TM - SUBLANES:TM]
    cat_scr[:, ML_WIDTH:] = (z_scr[:, B_COL:B_COL + SC_WIDTH] * u).astype(BF16)

    y = _dot(cat_scr[...], wout_ref[...])
    xo_ref[0] = x + _mod_row(mod_ref, 2, bi) * y
    for hd in range(ML_HEADS):
        co_ref[0, hd] = cn_scr[hd, :, 0:ML_DV]
        no_ref[0, hd] = cn_scr[hd, :, ML_DV:]
    mo_ref[0] = mrow_scr[...]
    sco_ref[0] = cc_scr[...]


def _even_weight_specs(i, idx):
    once = pl.Buffered(1)
    return [
        pl.BlockSpec((None, D_MODEL, IN_W), idx, pipeline_mode=once),
        pl.BlockSpec((None, KG_ROWS, D_MODEL), idx, pipeline_mode=once),
        pl.BlockSpec((None, 2, LANES), idx),
        pl.BlockSpec((None, 2 * SUBLANES, LANES), idx),
        pl.BlockSpec((None, 1, ML_WIDTH), idx),
        pl.BlockSpec((None, 3, SC_WIDTH), idx),
        pl.BlockSpec((None, ML_WIDTH + SC_WIDTH, D_MODEL), idx, pipeline_mode=once),
    ]


def _mix_even_prompt_call(x, mod, n_seq_rows, l, nw, win, wkg, bifc, bifr, onw, cw, wout):
    B, S, _ = x.shape
    nt = S // TM
    i = l // 2
    return pl.pallas_call(
        _mix_even_prompt_kernel,
        grid=(B, nt),
        in_specs=[
            pl.BlockSpec((1, TM, D_MODEL), lambda b, t: (b, t, 0)),
            pl.BlockSpec((None, N_MOD, SUBLANES, D_MODEL), lambda b, t: (l, 0, n_seq_rows // SUBLANES, 0)),
            pl.BlockSpec((None, 1, D_MODEL), lambda b, t: (l, 0, 0)),
        ] + _even_weight_specs(i, lambda b, t: (i, 0, 0)),
        out_specs=[
            pl.BlockSpec((1, TM, D_MODEL), lambda b, t: (b, t, 0)),
            pl.BlockSpec((1, ML_HEADS, ML_DK, ML_DV), lambda b, t: (b, 0, 0, 0)),
            pl.BlockSpec((1, ML_HEADS, ML_DK, LANES), lambda b, t: (b, 0, 0, 0)),
            pl.BlockSpec((1, SUBLANES, LANES), lambda b, t: (b, 0, 0)),
            pl.BlockSpec((1, SUBLANES, SC_WIDTH), lambda b, t: (b, 0, 0)),
        ],
        out_shape=[
            jax.ShapeDtypeStruct((B, S, D_MODEL), F32),
            jax.ShapeDtypeStruct((B, ML_HEADS, ML_DK, ML_DV), F32),
            jax.ShapeDtypeStruct((B, ML_HEADS, ML_DK, LANES), F32),
            jax.ShapeDtypeStruct((B, SUBLANES, LANES), F32),
            jax.ShapeDtypeStruct((B, SUBLANES, SC_WIDTH), F32),
        ],
        scratch_shapes=[
            pltpu.VMEM((TM, IN_W), F32),
            pltpu.VMEM((KG_ROWS, TM), F32),
            pltpu.VMEM((TM, ML_WIDTH), F32),
            pltpu.VMEM((TM, ML_WIDTH + SC_WIDTH), BF16),
            pltpu.VMEM((ML_HEADS, ML_DK, ML_DV + LANES), F32),
            pltpu.VMEM((SUBLANES, LANES), F32),
            pltpu.VMEM((SUBLANES, LANES), F32),
            pltpu.VMEM((SUBLANES, SC_WIDTH), F32),
        ],
        compiler_params=_cparams(2),
        name="mix_even_prompt",
    )(x, mod, nw, win, wkg, bifc, bifr, onw, cw, wout)


def _mix_even_sample_kernel(x_ref, mod_ref, nw_ref, win_ref, wkg_ref, bifc_ref, bifr_ref, onw_ref, cw_ref,
                            wout_ref, c_ref, nt_ref, ntt_ref, mcol_ref, mrow_ref, sc_ref,
                            xo_ref, co_ref, no_ref, mo_ref, sco_ref,
                            h_scr, z_scr, zt_scr, hm_scr, cat_scr, g1_scr,
                            intra_scr, dpart_scr, pbe_scr, wgq_scr, kwt_scr, vb_scr, wcb_scr, inter_scr):
    n_tok = x_ref.shape[0]
    seq_len = n_tok // SEQ_BLK

    def modulate(b, carry):
        rows = pl.ds(pl.multiple_of(b * seq_len, seq_len), seq_len)
        h_scr[rows, :] = _norm_mod(x_ref[rows, :], nw_ref[...], _mod_row(mod_ref, 1, b), _mod_row(mod_ref, 0, b))
        g1_scr[rows, :] = jnp.broadcast_to(_mod_row(mod_ref, 2, b), (seq_len, D_MODEL))
        return carry

    lax.fori_loop(0, SEQ_BLK, modulate, 0, unroll=4)
    hb = h_scr[...].astype(BF16)
    z_scr[...] = _dot(hb, win_ref[...])
    zt_scr[...] = _dot_nt(wkg_ref[...], hb)

    mask, lmat, lmat_t, tot = _chunk_consts(seq_len)
    scale = ML_DK ** -0.5
    ones_v = jnp.ones((CHUNK, ML_DV), BF16)
    mp_r = mrow_ref[...]
    v_r, ws_r, wc_r, mn_r = _gates_rows(zt_scr[ML_WIDTH:KG_ROWS, :], bifr_ref[...], mp_r, lmat_t, tot, seq_len)
    mo_ref[...] = mn_r
    mp_c = mcol_ref[...]
    b, cm, u, wg, pbe = _gates_cols(z_scr[:, G1_COL:G1_COL + LANES], z_scr[:, G2_COL:G2_COL + LANES],
                                    bifc_ref[...], mp_c, lmat, seq_len)
    wc_c = jnp.exp(mp_c - jnp.maximum(mp_c, _seq_last_row(cm, seq_len)))
    lhsc = _outer_sum_lhs(u)

    for hd in range(ML_HEADS):
        q = z_scr[:, Q_COL + hd * ML_DK:Q_COL + (hd + 1) * ML_DK]
        v = z_scr[:, V_COL + hd * ML_DV:V_COL + (hd + 1) * ML_DV]
        kt = zt_scr[hd * ML_DK:(hd + 1) * ML_DK, :] * scale
        s = _decayed_scores(q, kt, lhsc, v_r, hd, mask)
        vb = v.astype(BF16)
        out = _dot(s.astype(BF16), jnp.concatenate([vb, ones_v], axis=1))
        wg_h = wg[:, hd:hd + 1]
        qn = jnp.sum(q * nt_ref[hd], axis=-1, keepdims=True)
        intra_scr[hd] = out[:, :ML_DV]
        dpart_scr[hd] = out[:, ML_DV:] + wg_h * qn
        pbe_scr[hd] = jnp.broadcast_to(pbe[:, hd:hd + 1], (n_tok, LANES))
        wgq_scr[hd] = q * wg_h
        kwt = kt * ws_r[hd:hd + 1, :]
        kwt_scr[hd] = kwt
        vb_scr[hd] = vb
        wcb_scr[hd] = jnp.broadcast_to(wc_c[:, hd:hd + 1], (n_tok, LANES))
        no_ref[hd] = wc_r[hd:hd + 1, :] * ntt_ref[hd] + _dot_exact01_r(kwt, tot)

    lane_i = lax.broadcasted_iota(jnp.int32, (ML_DK, n_tok), 1)

    def per_seq(bq, carry):
        r0 = pl.multiple_of(bq * seq_len, seq_len)
        rows = pl.ds(r0, seq_len)
        sel = (lane_i >= r0) & (lane_i < r0 + seq_len)
        for hd in range(ML_HEADS):
            c_prev = c_ref[bq, hd]
            inter_scr[hd, rows, :] = _dot(wgq_scr[hd, rows, :].astype(BF16), c_prev.astype(BF16))
            kw_b = jnp.where(sel, kwt_scr[hd], 0.0).astype(BF16)
            co_ref[bq, hd] = wcb_scr[hd, pl.ds(r0, 1), :] * c_prev + _dot(kw_b, vb_scr[hd])
        return carry

    lax.fori_loop(0, SEQ_BLK, per_seq, 0, unroll=4)

    for hd in range(ML_HEADS):
        num = inter_scr[hd] + intra_scr[hd]
        hm_scr[:, hd * ML_DV:(hd + 1) * ML_DV] = num * (
            1.0 / jnp.maximum(jnp.abs(dpart_scr[hd]), pbe_scr[hd]))

    _even_tail(z_scr, hm_scr, cat_scr, onw_ref)
    cx = z_scr[:, C_COL:C_COL + SC_WIDTH] * z_scr[:, X_COL:X_COL + SC_WIDTH]
    sub = lax.broadcasted_iota(jnp.int32, (n_tok, SC_WIDTH), 0) % seq_len
    p1 = sc_ref[...]
    s1 = jnp.where(sub < 1, pltpu.roll(p1, n_tok - 1, 0), pltpu.roll(cx, 1, 0))
    s2 = jnp.where(sub < 2, p1, pltpu.roll(cx, 2, 0))
    cw = cw_ref[...]
    u = s2 * cw[0:1] + s1 * cw[1:2] + cx * cw[2:3]
    sco_ref[...] = pltpu.roll(cx, n_tok - (seq_len - 2), 0)
    cat_scr[:, ML_WIDTH:] = (z_scr[:, B_COL:B_COL + SC_WIDTH] * u).astype(BF16)

    y = _dot(cat_scr[...], wout_ref[...])
    xo_ref[...] = x_ref[...] + g1_scr[...] * y


def _mix_even_sample_call(x, mod, l, nw, win, wkg, bifc, bifr, onw, cw, wout, c0, n_tok, n_tok_t, m_col, m_row,
                          sc_pad, co_prev):
    n_rows = x.shape[0]
    n_seq = c0.shape[1]
    seq_len = n_rows // n_seq
    blk = SEQ_BLK * seq_len
    nb = n_seq // SEQ_BLK
    li = l // 2
    head_blk = (ML_HEADS, blk, LANES)
    return _stacked_call(
        _mix_even_sample_kernel, 16, {1: co_prev},
        grid=(nb,),
        in_specs=[
            pl.BlockSpec((blk, D_MODEL), lambda i: (i, 0)),
            pl.BlockSpec((None, N_MOD, SEQ_BLK, D_MODEL), lambda i: (l, 0, i, 0)),
            pl.BlockSpec((None, 1, D_MODEL), lambda i: (l, 0, 0)),
        ] + _even_weight_specs(li, lambda i: (li, 0, 0)) + [
            pl.BlockSpec((None, SEQ_BLK, ML_HEADS, ML_DK, ML_DV), lambda i: (li, i, 0, 0, 0)),
            pl.BlockSpec((ML_HEADS, blk, ML_DK), lambda i: (0, i, 0)),
            pl.BlockSpec((ML_HEADS, ML_DK, blk), lambda i: (0, 0, i)),
            pl.BlockSpec((blk, LANES), lambda i: (i, 0)),
            pl.BlockSpec((SUBLANES, blk), lambda i: (0, i)),
            pl.BlockSpec((blk, SC_WIDTH), lambda i: (i, 0)),
        ],
        out_specs=[
            pl.BlockSpec((blk, D_MODEL), lambda i: (i, 0)),
            pl.BlockSpec((None, SEQ_BLK, ML_HEADS, ML_DK, ML_DV), lambda i: (li, i, 0, 0, 0)),
            pl.BlockSpec((ML_HEADS, ML_DK, blk), lambda i: (0, 0, i)),
            pl.BlockSpec((SUBLANES, blk), lambda i: (0, i)),
            pl.BlockSpec((blk, SC_WIDTH), lambda i: (i, 0)),
        ],
        out_shape=[
            jax.ShapeDtypeStruct((n_rows, D_MODEL), F32),
            jax.ShapeDtypeStruct(c0.shape, F32),
            jax.ShapeDtypeStruct((ML_HEADS, ML_DK, n_rows), F32),
            jax.ShapeDtypeStruct((SUBLANES, n_rows), F32),
            jax.ShapeDtypeStruct((n_rows, SC_WIDTH), F32),
        ],
        scratch_shapes=[
            pltpu.VMEM((blk, D_MODEL), F32),
            pltpu.VMEM((blk, IN_W), F32),
            pltpu.VMEM((KG_ROWS, blk), F32),
            pltpu.VMEM((blk, ML_WIDTH), F32),
            pltpu.VMEM((blk, ML_WIDTH + SC_WIDTH), BF16),
            pltpu.VMEM((blk, D_MODEL), F32),
            pltpu.VMEM(head_blk, F32),
            pltpu.VMEM(head_blk, F32),
            pltpu.VMEM(head_blk, F32),
            pltpu.VMEM(head_blk, F32),
            pltpu.VMEM((ML_HEADS, ML_DK, blk), F32),
            pltpu.VMEM(head_blk, BF16),
            pltpu.VMEM(head_blk, F32),
            pltpu.VMEM(head_blk, F32),
        ],
        compiler_params=_cparams(1),
        name="mix_even_sample",
    )(x, mod, nw, win, wkg, bifc, bifr, onw, cw, wout, c0, n_tok, n_tok_t, m_col, m_row, sc_pad)


def _split2(x):
    hi = x.astype(BF16)
    lo = (x - hi.astype(F32)).astype(BF16)
    return jnp.concatenate([hi, lo], axis=1)


def _head_lane_mats():
    r = lax.broadcasted_iota(jnp.int32, (2 * LANES, LANES), 0) % LANES
    c = lax.broadcasted_iota(jnp.int32, (2 * LANES, LANES), 1)
    hsum = ((r // HEAD_DIM) == (c // HEAD_DIM)).astype(BF16)
    half = HEAD_DIM // 2
    src = jnp.where((c % HEAD_DIM) < half, c + half, c - half)
    return hsum, (r == src).astype(BF16)


def _qk_norm_rope(xb, gw, cos, sin, hsum, rot_mat):
    ms = _dot(_split2(xb * xb), hsum) * (1.0 / HEAD_DIM)
    zg = xb * gw
    rot = _dot(_split2(zg), rot_mat)
    return lax.rsqrt(ms + EPS) * (zg * cos + rot * sin)


def _sink_rows(sink8, reps):
    return jnp.concatenate(
        [jnp.broadcast_to(sink8[r:r + 1, :], (reps, LANES)) for r in range(SUBLANES)], axis=0)


def _sink_col(sink8, reps):
    parts = [jnp.broadcast_to(sink8[r:r + 1, :], (reps, LANES)) for r in range(SUBLANES)]
    return jnp.concatenate(parts, axis=0)[:, 0:1]


def _attn_prompt_kernel(x_ref, mod_ref, nw_ref, wqkv_ref, qnw_ref, knw_ref, cos_ref, sin_ref,
                        sink_ref, wout_ref, xo_ref, ko_ref, vo_ref,
                        z_scr, qm_scr, k_scr, v_scr, o_scr):
    t = pl.program_id(1)
    n_qb = TM // WINDOW

    @pl.when(t == 0)
    def _():
        k_scr[0:WINDOW, :] = jnp.zeros((WINDOW, KV_W), BF16)
        v_scr[0:WINDOW, :] = jnp.zeros((WINDOW, KV_W), BF16)

    x = x_ref[0]
    bi = pl.program_id(0)
    h = _norm_mod(x, nw_ref[...], _mod_row(mod_ref, 1, bi), _mod_row(mod_ref, 0, bi)).astype(BF16)
    z_scr[...] = _dot(h, wqkv_ref[...])
    cos = cos_ref[...]
    sin = sin_ref[...]
    hmats = _head_lane_mats()
    half0 =lax.broadcasted_iota(jnp.int32, (TM, LANES), 1) < HEAD_DIM
    qscale = HEAD_DIM ** -0.5
    for jb in range(Q_W // LANES):
        y = _qk_norm_rope(z_scr[:, jb * LANES:(jb + 1) * LANES], qnw_ref[...], cos, sin, *hmats) * qscale
        qm_scr[2 * jb] = jnp.where(half0, y, 0.0).astype(BF16)
        qm_scr[2 * jb + 1] = jnp.where(half0, 0.0, y).astype(BF16)
    for p in range(KV_W // LANES):
        kf = _qk_norm_rope(z_scr[:, Q_W + p * LANES:Q_W + (p + 1) * LANES], knw_ref[...], cos, sin, *hmats)
        ko_ref[0, :, p * LANES:(p + 1) * LANES] = kf[TM - WINDOW:TM]
        k_scr[WINDOW:WINDOW + TM, p * LANES:(p + 1) * LANES] = kf.astype(BF16)
    vf = z_scr[:, Q_W + KV_W:Q_W + 2 * KV_W]
    vo_ref[0] = vf[TM - WINDOW:TM]
    v_scr[WINDOW:WINDOW + TM, :] = vf.astype(BF16)

    r = lax.broadcasted_iota(jnp.int32, (8 * WINDOW, 2 * WINDOW), 0) % WINDOW
    c = lax.broadcasted_iota(jnp.int32, (8 * WINDOW, 2 * WINDOW), 1)
    valid = ((c < WINDOW) & (c > r)) | ((c >= WINDOW) & ((c - WINDOW) <= r))
    first_lim = jnp.where(t == 0, WINDOW, 0)
    half0q = lax.broadcasted_iota(jnp.int32, (4 * WINDOW, LANES), 1) < HEAD_DIM
    ones_kv = jnp.ones((2 * WINDOW, LANES), BF16)
    for qb in range(n_qb):
        rows = slice(qb * WINDOW, (qb + 1) * WINDOW)
        krows = slice(qb * WINDOW, (qb + 2) * WINDOW)
        vmask = (valid & (c >= first_lim)) if qb == 0 else valid
        for p in range(KV_W // LANES):
            kb = k_scr[krows, p * LANES:(p + 1) * LANES]
            vb = v_scr[krows, p * LANES:(p + 1) * LANES]
            qs = jnp.concatenate([qm_scr[2 * (4 * p + i) + e, rows, :] for e in range(2) for i in range(4)],
                                 axis=0)
            s = jnp.where(vmask, _dot_nt(qs, kb), -jnp.inf)
            sk = _sink_rows(sink_ref[p], WINDOW)
            mx = jnp.maximum(jnp.max(s, axis=-1, keepdims=True), sk)
            pr = jnp.exp(s - jnp.concatenate([mx, mx], axis=1))
            o2 = _dot(pr.astype(BF16), jnp.concatenate([vb, ones_kv], axis=1))
            den = o2[:, LANES:] + jnp.exp(sk - mx)
            o = o2[:, :LANES] * (1.0 / den)
            merged = jnp.where(half0q, o[0:4 * WINDOW], o[4 * WINDOW:])
            for i in range(4):
                o_scr[rows, (4 * p + i) * LANES:(4 * p + i + 1) * LANES] = (
                    merged[i * WINDOW:(i + 1) * WINDOW].astype(BF16))

    y = _dot(o_scr[...], wout_ref[...])
    xo_ref[0] = x + _mod_row(mod_ref, 2, bi) * y
    k_scr[0:WINDOW, :] = k_scr[TM:TM + WINDOW, :]
    v_scr[0:WINDOW, :] = v_scr[TM:TM + WINDOW, :]


def _attn_prompt_call(x, mod, n_seq_rows, l, nw, wqkv, qnw, knw, cos, sin, sink, wout):
    B, S, _ = x.shape
    nt = S // TM
    const2 = lambda b, t: (0, 0)
    return pl.pallas_call(
        _attn_prompt_kernel,
        grid=(B, nt),
        in_specs=[
            pl.BlockSpec((1, TM, D_MODEL), lambda b, t: (b, t, 0)),
            pl.BlockSpec((None, N_MOD, SUBLANES, D_MODEL), lambda b, t: (l, 0, n_seq_rows // SUBLANES, 0)),
            pl.BlockSpec((None, 1, D_MODEL), lambda b, t: (l, 0, 0)),
            pl.BlockSpec((D_MODEL, Q_W + 2 * KV_W), const2),
            pl.BlockSpec((1, LANES), const2),
            pl.BlockSpec((1, LANES), const2),
            pl.BlockSpec((TM, LANES), lambda b, t: (t, 0)),
            pl.BlockSpec((TM, LANES), lambda b, t: (t, 0)),
            pl.BlockSpec((2, SUBLANES, LANES), lambda b, t: (0, 0, 0)),
            pl.BlockSpec((Q_W, D_MODEL), const2),
        ],
        out_specs=[
            pl.BlockSpec((1, TM, D_MODEL), lambda b, t: (b, t, 0)),
            pl.BlockSpec((1, WINDOW, KV_W), lambda b, t: (b, 0, 0)),
            pl.BlockSpec((1, WINDOW, KV_W), lambda b, t: (b, 0, 0)),
        ],
        out_shape=[
            jax.ShapeDtypeStruct((B, S, D_MODEL), F32),
            jax.ShapeDtypeStruct((B, WINDOW, KV_W), F32),
            jax.ShapeDtypeStruct((B, WINDOW, KV_W), F32),
        ],
        scratch_shapes=[
            pltpu.VMEM((TM, Q_W + 2 * KV_W), F32),
            pltpu.VMEM((2 * Q_W // LANES, TM, LANES), BF16),
            pltpu.VMEM((TM + WINDOW, KV_W), BF16),
            pltpu.VMEM((TM + WINDOW, KV_W), BF16),
            pltpu.VMEM((TM, Q_W), BF16),
        ],
        compiler_params=_cparams(2),
        name="attn_prompt",
    )(x, mod, nw, wqkv, qnw, knw, cos, sin, sink, wout)


def _attn_sample_kernel(x_ref, mod_ref, nw_ref, wqkv_ref, qnw_ref, knw_ref, cos_ref, sin_ref,
                        sink_ref, wout_ref, kc_ref, vc_ref,
                        xo_ref, kco_ref, vco_ref,
                        h_scr, g1_scr, z_scr, qm_scr, kn_scr, o_scr):
    n_tok = x_ref.shape[0]
    seq_len = n_tok // SEQ_BLK
    win = kc_ref.shape[1]

    def modulate(b, carry):
        rows = pl.ds(pl.multiple_of(b * seq_len, seq_len), seq_len)
        h_scr[rows, :] = _norm_mod(x_ref[rows, :], nw_ref[...], _mod_row(mod_ref, 1, b), _mod_row(mod_ref, 0, b))
        g1_scr[rows, :] = jnp.broadcast_to(_mod_row(mod_ref, 2, b), (seq_len, D_MODEL))
        return carry

    lax.fori_loop(0, SEQ_BLK, modulate, 0, unroll=4)
    z_scr[...] = _dot(h_scr[...].astype(BF16), wqkv_ref[...])
    cos = cos_ref[...]
    sin = sin_ref[...]
    hmats = _head_lane_mats()
    half0 =lax.broadcasted_iota(jnp.int32, (n_tok, LANES), 1) < HEAD_DIM
    qscale = HEAD_DIM ** -0.5
    for jb in range(Q_W // LANES):
        y = _qk_norm_rope(z_scr[:, jb * LANES:(jb + 1) * LANES], qnw_ref[...], cos, sin, *hmats) * qscale
        qm_scr[2 * jb] = jnp.where(half0, y, 0.0)
        qm_scr[2 * jb + 1] = jnp.where(half0, 0.0, y)
    for p in range(KV_W // LANES):
        kn_scr[:, p * LANES:(p + 1) * LANES] = _qk_norm_rope(
            z_scr[:, Q_W + p * LANES:Q_W + (p + 1) * LANES], knw_ref[...], cos, sin, *hmats)

    n_q = 8 * seq_len
    tq = lax.broadcasted_iota(jnp.int32, (SEQ_BLK, n_q, 2 * win), 1) % seq_len
    cc = lax.broadcasted_iota(jnp.int32, (SEQ_BLK, n_q, 2 * win), 2)
    valid = ((cc < win) & (cc > tq)) | ((cc >= 2 * win - seq_len) & ((cc - (2 * win - seq_len)) <= tq))
    half0q = lax.broadcasted_iota(jnp.int32, (SEQ_BLK, n_q // 2, LANES), 2) < HEAD_DIM
    ones_kv = jnp.ones((SEQ_BLK, 2 * win, LANES), BF16)

    kc = kc_ref[...]
    vc = vc_ref[...]
    knew = jnp.concatenate([kc[:, seq_len:], kn_scr[...].reshape(SEQ_BLK, seq_len, KV_W)], axis=1)
    vnew = jnp.concatenate(
        [vc[:, seq_len:], z_scr[:, Q_W + KV_W:Q_W + 2 * KV_W].reshape(SEQ_BLK, seq_len, KV_W)], axis=1)
    kco_ref[...] = knew
    vco_ref[...] = vnew
    for p in range(KV_W // LANES):
        lanes = slice(p * LANES, (p + 1) * LANES)
        qs = jnp.concatenate([qm_scr[2 * (4 * p + i) + e].reshape(SEQ_BLK, seq_len, LANES)
                              for e in range(2) for i in range(4)], axis=1).astype(BF16)
        kk = jnp.concatenate([kc[:, :, lanes], knew[:, :, lanes]], axis=1).astype(BF16)
        vv = jnp.concatenate([vc[:, :, lanes], vnew[:, :, lanes]], axis=1).astype(BF16)
        s = jnp.einsum("bqd,bkd->bqk", qs, kk, preferred_element_type=F32)
        s = jnp.where(valid, s, -jnp.inf)
        sk = _sink_rows(sink_ref[p], seq_len)[None]
        mx = jnp.maximum(jnp.max(s, axis=-1, keepdims=True), sk)
        pr = jnp.exp(s - jnp.concatenate([mx, mx], axis=-1))
        o2 = jnp.einsum("bqk,bkd->bqd", pr.astype(BF16), jnp.concatenate([vv, ones_kv], axis=-1),
                        preferred_element_type=F32)
        o = o2[:, :, :LANES] * (1.0 / (o2[:, :, LANES:] + jnp.exp(sk - mx)))
        merged = jnp.where(half0q, o[:, 0:n_q // 2], o[:, n_q // 2:])
        for i in range(4):
            o_scr[:, (4 * p + i) * LANES:(4 * p + i + 1) * LANES] = (
                merged[:, i * seq_len:(i + 1) * seq_len].reshape(n_tok, LANES))

    y = _dot(o_scr[...].astype(BF16), wout_ref[...])
    xo_ref[...] = x_ref[...] + g1_scr[...] * y


def _attn_sample_call(x, mod, l, nw, wqkv, qnw, knw, cos, sin, sink, wout, kc, vc, ko_prev, vo_prev):
    n_rows = x.shape[0]
    _, n_seq, win, _ = kc.shape
    lj = l // 2
    seq_len = n_rows // n_seq
    blk = SEQ_BLK * seq_len
    nb = n_seq // SEQ_BLK
    const2 = lambda i: (0, 0)
    return _stacked_call(
        _attn_sample_kernel, 12, {1: ko_prev, 2: vo_prev},
        grid=(nb,),
        in_specs=[
            pl.BlockSpec((blk, D_MODEL), lambda i: (i, 0)),
            pl.BlockSpec((None, N_MOD, SEQ_BLK, D_MODEL), lambda i: (l, 0, i, 0)),
            pl.BlockSpec((None, 1, D_MODEL), lambda i: (l, 0, 0)),
            pl.BlockSpec((D_MODEL, Q_W + 2 * KV_W), const2),
            pl.BlockSpec((1, LANES), const2),
            pl.BlockSpec((1, LANES), const2),
            pl.BlockSpec((blk, LANES), const2),
            pl.BlockSpec((blk, LANES), const2),
            pl.BlockSpec((2, SUBLANES, LANES), lambda i: (0, 0, 0)),
            pl.BlockSpec((Q_W, D_MODEL), const2),
            pl.BlockSpec((None, SEQ_BLK, win, KV_W), lambda i: (lj, i, 0, 0)),
            pl.BlockSpec((None, SEQ_BLK, win, KV_W), lambda i: (lj, i, 0, 0)),
        ],
        out_specs=[
            pl.BlockSpec((blk, D_MODEL), lambda i: (i, 0)),
            pl.BlockSpec((None, SEQ_BLK, win, KV_W), lambda i: (lj, i, 0, 0)),
            pl.BlockSpec((None, SEQ_BLK, win, KV_W), lambda i: (lj, i, 0, 0)),
        ],
        out_shape=[
            jax.ShapeDtypeStruct((n_rows, D_MODEL), F32),
            jax.ShapeDtypeStruct(kc.shape, F32),
            jax.ShapeDtypeStruct(vc.shape, F32),
        ],
        scratch_shapes=[
            pltpu.VMEM((blk, D_MODEL), F32),
            pltpu.VMEM((blk, D_MODEL), F32),
            pltpu.VMEM((blk, Q_W + 2 * KV_W), F32),
            pltpu.VMEM((2 * Q_W // LANES, blk, LANES), F32),
            pltpu.VMEM((blk, KV_W), F32),
            pltpu.VMEM((blk, Q_W), F32),
        ],
        compiler_params=_cparams(1),
        name="attn_sample",
    )(x, mod, nw, wqkv, qnw, knw, cos, sin, sink, wout, kc, vc)


def _rope_tables(pos):
    half = HEAD_DIM // 2
    inv = ROPE_THETA ** (-jnp.arange(half, dtype=F32) / half)
    ang = pos.astype(F32)[:, None] * inv[None, :]
    cos = jnp.cos(ang)
    sin = jnp.sin(ang)
    return jnp.tile(cos, (1, 4)), jnp.concatenate([-sin, sin, -sin, sin], axis=1)


def _prep_attn(w_qkv, q_norm, k_norm, sink, w_out):
    perm = np.asarray(HEAD_PERM)
    wq = w_qkv[:, :Q_W].reshape(D_MODEL, ATT_HEADS, HEAD_DIM)[:, perm].reshape(D_MODEL, Q_W)
    wqkv = jnp.concatenate([wq, w_qkv[:, Q_W:]], axis=1).astype(BF16)
    wout = w_out.reshape(ATT_HEADS, HEAD_DIM, D_MODEL)[perm].reshape(Q_W, D_MODEL).astype(BF16)
    qnw = jnp.tile(q_norm, 2)[None]
    knw = jnp.tile(k_norm, 2)[None]
    idx = np.asarray([[perm[2 * (4 * p + i) + e] for e in range(2) for i in range(4)] for p in range(2)])
    sink_arr = jnp.broadcast_to(sink[idx][:, :, None], (2, SUBLANES, LANES)).astype(F32)
    return wqkv, qnw, knw, sink_arr, wout


def kernel(x_prompt, x_sample, c_prompt, c_sample, state_mlstm_C, state_mlstm_n, state_mlstm_m, state_sconv, cache_win_k, cache_win_v, state_ffn_conv, norm1, norm2, w_ada, b_ada, a_w_in, a_b_if, a_out_norm, a_conv_w, a_w_out, c_w_qkv, c_q_norm, c_k_norm, c_sink, c_w_out, f_w_up, f_conv_w, f_w_down):
    B, S, _ = x_prompt.shape
    NS, SL, _ = x_sample.shape
    assert S % TM == 0 and NS % SEQ_BLK == 0 and SEQ_BLK * SL == CHUNK and SL == SUBLANES

    assert B <= SUBLANES
    c_all = jnp.concatenate([c_sample, c_prompt, jnp.zeros((SUBLANES - B, D_MODEL), F32)], axis=0)
    mod = _ada_call(c_all, w_ada, b_ada)

    win_all = _prep_win_call(a_w_in)
    zg_w = a_w_in[:, :, SRC_G:SRC_B]
    wkg_all = jnp.swapaxes(jnp.concatenate(
        [a_w_in[:, :, SRC_K:SRC_V], zg_w[..., :ML_HEADS], zg_w[..., :ML_HEADS],
         zg_w[..., ML_HEADS:], zg_w[..., ML_HEADS:]], axis=-1), 1, 2).astype(BF16)
    b_i, b_f = a_b_if[:, :ML_HEADS], a_b_if[:, ML_HEADS:]
    lane_pad = jnp.zeros((a_b_if.shape[0], LANES - 2 * ML_HEADS), F32)
    bifc_all = jnp.stack([jnp.concatenate([b_i, b_i, lane_pad], axis=1),
                          jnp.concatenate([b_f, b_f, lane_pad], axis=1)], axis=1)
    bifr_all = jnp.broadcast_to(jnp.concatenate([b_i, b_i, b_f, b_f], axis=1)[:, :, None],
                                (a_b_if.shape[0], 2 * SUBLANES, LANES))
    wout_a_all = _cast_call(a_w_out)
    wup_all = _cast_call(f_w_up)
    wdn_all = _cast_call(f_w_down)
    onw_all = a_out_norm.reshape(-1, 1, ML_WIDTH)
    norm1_r = norm1.reshape(DEPTH, 1, D_MODEL)
    norm2_r = norm2.reshape(DEPTH, 1, D_MODEL)
    win_buf = cache_win_k.shape[2]
    kc_all = cache_win_k.reshape(-1, NS, win_buf, KV_W)
    vc_all = cache_win_v.reshape(-1, NS, win_buf, KV_W)

    cos_p, sin_p = _rope_tables(jnp.arange(S, dtype=jnp.int32))
    cos_s, sin_s = _rope_tables(PAST_LEN + jnp.arange(SL, dtype=jnp.int32))
    cos_s = jnp.tile(cos_s, (SEQ_BLK, 1))
    sin_s = jnp.tile(sin_s, (SEQ_BLK, 1))

    xp = x_prompt
    xs = x_sample.reshape(NS * SL, D_MODEL)
    p_C, p_n, p_m, p_sc, p_wk, p_wv, p_ffn = [], [], [], [], [], [], []
    s_n, s_m, s_sc = [], [], []
    s_C = s_wk = s_wv = s_fg = s_fa = None

    for l in range(DEPTH):
        if l % 2 == 0:
            i = l // 2
            even_w = (norm1_r, win_all, wkg_all, bifc_all, bifr_all, onw_all, a_conv_w, wout_a_all)
            xp, co, no, mo, sco = _mix_even_prompt_call(xp, mod, NS, l, *even_w)
            p_C.append(co)
            p_n.append(no[:, :, :, 0])
            p_m.append(mo[:, :ML_HEADS, 0])
            p_sc.append(sco[:, SUBLANES - 2:, :])

            n_tok = jnp.repeat(state_mlstm_n[i].transpose(1, 0, 2), SL, axis=1)
            m_rep = jnp.repeat(state_mlstm_m[i], SL, axis=0)
            m_col = jnp.concatenate([m_rep, m_rep, jnp.zeros((NS * SL, LANES - 2 * ML_HEADS), F32)], axis=1)
            m_row = jnp.concatenate([m_rep, m_rep], axis=1).T
            sc_pad = jnp.pad(state_sconv[i], ((0, 0), (0, SL - 2), (0, 0))).reshape(NS * SL, SC_WIDTH)
            xs, s_C, no, mo, sco = _mix_even_sample_call(xs, mod, l, *even_w, state_mlstm_C, n_tok,
                                                         jnp.swapaxes(n_tok, 1, 2), m_col, m_row, sc_pad, s_C)
            s_n.append(no[:, :, ::SL].transpose(2, 0, 1))
            s_m.append(mo[:ML_HEADS, ::SL].T)
            s_sc.append(sco.reshape(NS, SL, SC_WIDTH)[:, :2])
        else:
            j = l // 2
            wqkv, qnw, knw, sink, wout = _prep_attn(c_w_qkv[j], c_q_norm[j], c_k_norm[j], c_sink[j], c_w_out[j])
            xp, ko, vo = _attn_prompt_call(xp, mod, NS, l, norm1_r, wqkv, qnw, knw, cos_p, sin_p, sink, wout)
            p_wk.append(ko.reshape(B, WINDOW, KV_HEADS, HEAD_DIM))
            p_wv.append(vo.reshape(B, WINDOW, KV_HEADS, HEAD_DIM))
            xs, s_wk, s_wv = _attn_sample_call(xs, mod, l, norm1_r, wqkv, qnw, knw, cos_s, sin_s, sink, wout,
                                               kc_all, vc_all, s_wk, s_wv)

        xp, st = _ffn_prompt_call(xp, mod, NS, l, norm2_r, wup_all, f_conv_w, wdn_all)
        p_ffn.append(st[:, SUBLANES - 2:, :])
        xs, s_fg, s_fa = _ffn_sample_call(xs, mod, l, norm2_r, state_ffn_conv, wup_all, f_conv_w, wdn_all,
                                          s_fg, s_fa)

    kv_shape = (-1, NS, win_buf, KV_HEADS, HEAD_DIM)
    return (xp, xs.reshape(NS, SL, D_MODEL),
            jnp.stack(p_C), jnp.stack(p_n), jnp.stack(p_m), jnp.stack(p_sc),
            jnp.stack(p_wk), jnp.stack(p_wv), jnp.stack(p_ffn),
            s_C, jnp.stack(s_n), jnp.stack(s_m), jnp.stack(s_sc),
            s_wk.reshape(kv_shape), s_wv.reshape(kv_shape), jnp.concatenate([s_fg, s_fa], axis=-1))
```

```python
import functools

import jax
import jax.numpy as jnp
import numpy as np
from jax import lax
from jax.experimental import pallas as pl
from jax.experimental.pallas import tpu as pltpu

F32 = jnp.float32
BF16 = jnp.bfloat16

D_MODEL = 1024
DEPTH = 4
PAST_LEN = 8192
ML_HEADS = 4
ML_DK = 128
ML_DV = 128
ML_WIDTH = ML_HEADS * ML_DV
SC_WIDTH = D_MODEL // 2
ATT_HEADS = 16
KV_HEADS = 4
HEAD_DIM = 64
WINDOW = 128
ROPE_THETA = 10000.0
D_FF = 2816
EPS = 1e-6

LANES = 128
SUBLANES = 8
VMEM_LIMIT = 56 * 1024 * 1024

TM = 512
CHUNK = 128
SEQ_BLK = 16
FC = 256
NCH = D_FF // FC
Q_COL = 0
V_COL = Q_COL + ML_WIDTH
O_COL = V_COL + ML_WIDTH
G1_COL = O_COL + ML_WIDTH
G2_COL = G1_COL + LANES
B_COL = G2_COL + LANES
C_COL = B_COL + SC_WIDTH
X_COL = C_COL + SC_WIDTH
IN_W = X_COL + SC_WIDTH
KG_ROWS = ML_WIDTH + 16
SRC_K = ML_WIDTH
SRC_V = 2 * ML_WIDTH
SRC_G = 4 * ML_WIDTH
SRC_B = SRC_G + 2 * ML_HEADS
Q_W = ATT_HEADS * HEAD_DIM
KV_W = KV_HEADS * HEAD_DIM
HEAD_PERM = (0, 4, 1, 5, 2, 6, 3, 7, 8, 12, 9, 13, 10, 14, 11, 15)


def _cparams(n_axes):
    return pltpu.CompilerParams(dimension_semantics=("arbitrary",) * n_axes,
                                vmem_limit_bytes=VMEM_LIMIT)


def _stacked_call(kernel, n_in, out_idx, prev, slab, slab_spec, **kw):
    specs = list(kw.pop("in_specs"))
    out_specs = list(kw.pop("out_specs"))
    n_slabs = kw["out_shape"][out_idx].shape[0]
    if prev is None:
        out_specs[out_idx] = slab_spec(n_slabs, 0)

        def body(*refs):
            refs = list(refs)
            whole = refs[n_in + out_idx]
            for s in range(n_slabs):
                if s != slab:
                    whole[s] = jnp.zeros(whole.shape[1:], whole.dtype)
            refs[n_in + out_idx] = whole.at[slab]
            return kernel(*refs)

        return pl.pallas_call(body, in_specs=specs, out_specs=out_specs, **kw)

    out_specs[out_idx] = slab_spec(None, slab)

    def body(*refs):
        return kernel(*refs[:n_in], *refs[n_in + 1:])

    call = pl.pallas_call(body, in_specs=specs + [pl.BlockSpec(memory_space=pl.ANY)], out_specs=out_specs,
                          input_output_aliases={n_in: out_idx}, **kw)
    return lambda *args: call(*args, prev)


def _dot(a, b):
    return jnp.dot(a, b, preferred_element_type=F32)


def _dot_nt(a, b):
    return lax.dot_general(a, b, (((1,), (1,)), ((), ())), preferred_element_type=F32)


def _dot_tn(a, b):
    return lax.dot_general(a, b, (((0,), (0,)), ((), ())), preferred_element_type=F32)


def _dot_exact01(m, a):
    a1 = a.astype(BF16)
    r1 = a - a1.astype(F32)
    a2 = r1.astype(BF16)
    a3 = (r1 - a2.astype(F32)).astype(BF16)
    return _dot(m, a1) + _dot(m, a2) + _dot(m, a3)


def _norm_mod(x, nw, sc, sh):
    ms = jnp.mean(x * x, axis=-1, keepdims=True)
    return (x * lax.rsqrt(ms + EPS) * nw) * (1.0 + sc) + sh


def _sigmoid(x):
    return 1.0 / (1.0 + jnp.exp(-x))


def _log_sigmoid(x):
    return jnp.minimum(x, 0.0) - jnp.log(1.0 + jnp.exp(-jnp.abs(x)))


N_MOD = 6


def _ada_kernel(c_ref, w_ref, b_ref, o_ref):
    c = c_ref[...]
    s = (c * _sigmoid(c)).astype(BF16)
    o_ref[...] = _dot(s, w_ref[...].astype(BF16)) + b_ref[...]


def _ada_call(c_all, w_ada, b_ada):
    rows = c_all.shape[0]
    return pl.pallas_call(
        _ada_kernel,
        grid=(DEPTH, N_MOD),
        in_specs=[
            pl.BlockSpec((rows, D_MODEL), lambda l, k: (0, 0)),
            pl.BlockSpec((None, D_MODEL, D_MODEL), lambda l, k: (l, 0, k)),
            pl.BlockSpec((None, 1, D_MODEL), lambda l, k: (l, 0, k)),
        ],
        out_specs=pl.BlockSpec((None, None, rows, D_MODEL), lambda l, k: (l, k, 0, 0)),
        out_shape=jax.ShapeDtypeStruct((DEPTH, N_MOD, rows, D_MODEL), F32),
        compiler_params=_cparams(2),
        name="adaln_mod",
    )(c_all, w_ada, b_ada.reshape(DEPTH, 1, N_MOD * D_MODEL))


CAST_ROWS = 256


def _cast_kernel(w_ref, o_ref):
    o_ref[...] = w_ref[...].astype(BF16)


def _cast_call(w):
    n_l, rows, cols = w.shape
    tr = CAST_ROWS if rows % CAST_ROWS == 0 else rows
    return pl.pallas_call(
        _cast_kernel,
        grid=(n_l, rows // tr),
        in_specs=[pl.BlockSpec((None, tr, cols), lambda l, r: (l, r, 0))],
        out_specs=pl.BlockSpec((None, tr, cols), lambda l, r: (l, r, 0)),
        out_shape=jax.ShapeDtypeStruct(w.shape, BF16),
        compiler_params=_cparams(2),
        name="cast_bf16",
    )(w)


def _prep_win_kernel(w_ref, o_ref):
    o_ref[:, Q_COL:Q_COL + ML_WIDTH] = w_ref[:, 0:ML_WIDTH].astype(BF16)
    o_ref[:, V_COL:G1_COL] = w_ref[:, SRC_V:SRC_G].astype(BF16)
    tail = w_ref[:, SRC_G:]
    ig = tail[:, 0:ML_HEADS]
    fg = tail[:, ML_HEADS:2 * ML_HEADS]
    pad = jnp.zeros((ig.shape[0], LANES - 2 * ML_HEADS), F32)
    o_ref[:, G1_COL:G2_COL] = jnp.concatenate([ig, ig, pad], axis=1).astype(BF16)
    o_ref[:, G2_COL:B_COL] = jnp.concatenate([fg, fg, pad], axis=1).astype(BF16)
    o_ref[:, B_COL:IN_W] = tail[:, 2 * ML_HEADS:].astype(BF16)


def _prep_win_call(a_w_in):
    n_l, _, in_a = a_w_in.shape
    return pl.pallas_call(
        _prep_win_kernel,
        grid=(n_l, D_MODEL // CAST_ROWS),
        in_specs=[pl.BlockSpec((None, CAST_ROWS, in_a), lambda l, r: (l, r, 0))],
        out_specs=pl.BlockSpec((None, CAST_ROWS, IN_W), lambda l, r: (l, r, 0)),
        out_shape=jax.ShapeDtypeStruct((n_l, D_MODEL, IN_W), BF16),
        compiler_params=_cparams(2),
        name="prep_w_in",
    )(a_w_in)


def _mod_row(mod_ref, kind, b):
    return mod_ref[kind, pl.ds(b, 1), :]


def _ffn_prompt_kernel(x_ref, mod_ref, nw_ref, wup_ref, cw_ref, wdn_ref,
                       xo_ref, st_ref, h_scr, act_scr, carry_scr):
    b = pl.program_id(0)
    t = pl.program_id(1)

    @pl.when(t == 0)
    def _():
        carry_scr[...] = jnp.zeros_like(carry_scr)

    x = x_ref[0]
    h_scr[...] = _norm_mod(x, nw_ref[...], _mod_row(mod_ref, 4, b), _mod_row(mod_ref, 3, b)).astype(BF16)
    for j in range(NCH):
        ys = []
        for col in (j * FC, D_FF + j * FC):
            cols = slice(col, col + FC)
            u = _dot(h_scr[...], wup_ref[:, cols])
            ys.append(_conv3_rows(u, carry_scr[:, cols], cw_ref[:, cols]))
            carry_scr[:, cols] = u[TM - SUBLANES:TM]
        g = ys[0]
        act_scr[:, j * FC:(j + 1) * FC] = (g * _sigmoid(g) * ys[1]).astype(BF16)
    y = _dot(act_scr[...], wdn_ref[...])
    xo_ref[0] = x + _mod_row(mod_ref, 5, b) * y
    st_ref[0] = carry_scr[...]


def _ffn_prompt_call(x, mod, n_seq_rows, l, nw, wup, cw, wdn):
    B, S, _ = x.shape
    nt = S // TM
    once = pl.Buffered(1)
    return pl.pallas_call(
        _ffn_prompt_kernel,
        grid=(B, nt),
        in_specs=[
            pl.BlockSpec((1, TM, D_MODEL), lambda b, t: (b, t, 0)),
            pl.BlockSpec((None, N_MOD, SUBLANES, D_MODEL), lambda b, t: (l, 0, n_seq_rows // SUBLANES, 0)),
            pl.BlockSpec((None, 1, D_MODEL), lambda b, t: (l, 0, 0)),
            pl.BlockSpec((None, D_MODEL, 2 * D_FF), lambda b, t: (l, 0, 0), pipeline_mode=once),
            pl.BlockSpec((None, 3, 2 * D_FF), lambda b, t: (l, 0, 0)),
            pl.BlockSpec((None, D_FF, D_MODEL), lambda b, t: (l, 0, 0), pipeline_mode=once),
        ],
        out_specs=[
            pl.BlockSpec((1, TM, D_MODEL), lambda b, t: (b, t, 0)),
            pl.BlockSpec((1, SUBLANES, 2 * D_FF), lambda b, t: (b, 0, 0)),
        ],
        out_shape=[
            jax.ShapeDtypeStruct((B, S, D_MODEL), F32),
            jax.ShapeDtypeStruct((B, SUBLANES, 2 * D_FF), F32),
        ],
        scratch_shapes=[
            pltpu.VMEM((TM, D_MODEL), BF16),
            pltpu.VMEM((TM, D_FF), BF16),
            pltpu.VMEM((SUBLANES, 2 * D_FF), F32),
        ],
        compiler_params=_cparams(2),
        name="ffn_prompt",
    )(x, mod, nw, wup, cw, wdn)


def _ffn_sample_kernel(x_ref, mod_ref, nw_ref, s_ref, w_ref, c_ref, wdn_ref,
                       xo_ref, so_ref, h_scr, hb_scr, g2_scr, acc_scr, yg_scr):
    j = pl.program_id(0)
    n_seq = s_ref.shape[0]
    n_rows = x_ref.shape[0]
    n_t = n_rows // n_seq

    @pl.when(j == 0)
    def _():
        def modulate(b, carry):
            rows = pl.ds(pl.multiple_of(b * n_t, n_t), n_t)
            h_scr[rows, :] = _norm_mod(x_ref[rows, :], nw_ref[...], _mod_row(mod_ref, 4, b), _mod_row(mod_ref, 3, b))
            g2_scr[rows, :] = jnp.broadcast_to(_mod_row(mod_ref, 5, b), (n_t, D_MODEL))
            return carry

        lax.fori_loop(0, n_seq, modulate, 0, unroll=8)
        hb_scr[...] = h_scr[...].astype(BF16)
        acc_scr[...] = jnp.zeros_like(acc_scr)

    sub = lax.broadcasted_iota(jnp.int32, (n_seq, n_t, FC), 1)
    u3 = _dot(hb_scr[...], w_ref[...]).reshape(n_seq, n_t, FC)
    cw = c_ref[...]
    p0 = jnp.broadcast_to(s_ref[:, 0:1, :], (n_seq, n_t, FC))
    p1 = jnp.broadcast_to(s_ref[:, 1:2, :], (n_seq, n_t, FC))
    s1 = jnp.where(sub < 1, p1, pltpu.roll(u3, 1, 1))
    s2 = jnp.where(sub < 1, p0, jnp.where(sub < 2, p1, pltpu.roll(u3, 2, 1)))
    y = (s2 * cw[0:1] + s1 * cw[1:2] + u3 * cw[2:3]).reshape(n_rows, FC)
    so_ref[...] = pltpu.roll(u3, 2, 1)[:, 0:2, :]

    @pl.when(j < NCH)
    def _():
        yg_scr[j] = y

    @pl.when(j >= NCH)
    def _():
        g = yg_scr[j - NCH]
        acc_scr[...] += _dot((g * _sigmoid(g) * y).astype(BF16), wdn_ref[...])

    @pl.when(j == 2 * NCH - 1)
    def _():
        xo_ref[...] = x_ref[...] + g2_scr[...] * acc_scr[...]


def _ffn_sample_call(x, mod, l, nw, st, wup, cw, wdn, so_prev):
    n_rows = x.shape[0]
    n_seq = st.shape[1]
    return _stacked_call(
        _ffn_sample_kernel, 7, 1, so_prev, l,
        lambda lead, idx: pl.BlockSpec((lead, n_seq, 2, FC), lambda j: (idx, 0, 0, j)),
        grid=(2 * NCH,),
        in_specs=[
            pl.BlockSpec((n_rows, D_MODEL), lambda j: (0, 0)),
            pl.BlockSpec((None, N_MOD, n_seq, D_MODEL), lambda j: (l, 0, 0, 0)),
            pl.BlockSpec((None, 1, D_MODEL), lambda j: (l, 0, 0)),
            pl.BlockSpec((None, n_seq, 2, FC), lambda j: (l, 0, 0, j)),
            pl.BlockSpec((None, D_MODEL, FC), lambda j: (l, 0, j)),
            pl.BlockSpec((None, 3, FC), lambda j: (l, 0, j)),
            pl.BlockSpec((None, FC, D_MODEL), lambda j: (l, jnp.maximum(j - NCH, 0), 0)),
        ],
        out_specs=[
            pl.BlockSpec((n_rows, D_MODEL), lambda j: (0, 0)),
            None,
        ],
        out_shape=[
            jax.ShapeDtypeStruct((n_rows, D_MODEL), F32),
            jax.ShapeDtypeStruct((DEPTH, n_seq, 2, 2 * D_FF), F32),
        ],
        scratch_shapes=[
            pltpu.VMEM((n_rows, D_MODEL), F32),
            pltpu.VMEM((n_rows, D_MODEL), BF16),
            pltpu.VMEM((n_rows, D_MODEL), F32),
            pltpu.VMEM((n_rows, D_MODEL), F32),
            pltpu.VMEM((NCH, n_rows, FC), F32),
        ],
        compiler_params=_cparams(1),
        name="ffn_sample",
    )(x, mod, nw, st, wup, cw, wdn)


def _chunk_consts(seq_len):
    r = lax.broadcasted_iota(jnp.int32, (CHUNK, CHUNK), 0)
    c = lax.broadcasted_iota(jnp.int32, (CHUNK, CHUNK), 1)
    if seq_len >= CHUNK:
        same = r >= 0
    else:
        same = (r // seq_len) == (c // seq_len)
    mask = same & (c <= r)
    lmat = mask.astype(BF16)
    lmat_t = (same & (r <= c)).astype(BF16)
    return mask, lmat, lmat_t, same.astype(BF16)


def _dot_exact01_r(a, m):
    a1 = a.astype(BF16)
    r1 = a - a1.astype(F32)
    a2 = r1.astype(BF16)
    a3 = (r1 - a2.astype(F32)).astype(BF16)
    return _dot(a1, m) + _dot(a2, m) + _dot(a3, m)


def _seq_max_lanes(x, seq_len):
    n = x.shape[1]
    pos = lax.broadcasted_iota(jnp.int32, x.shape, 1)
    d = 1
    while d < seq_len:
        partner = jnp.where((pos & d) == 0, pltpu.roll(x, n - d, 1), pltpu.roll(x, d, 1))
        x = jnp.maximum(x, partner)
        d *= 2
    return x


def _seq_prefix_max_rows(x, seq_len):
    pos = lax.broadcasted_iota(jnp.int32, x.shape, 0) & (seq_len - 1)
    d = 1
    while d < seq_len:
        x = jnp.where(pos >= d, jnp.maximum(x, pltpu.roll(x, d, 0)), x)
        d *= 2
    return x


def _seq_last_row(x, seq_len):
    n, w = x.shape
    if seq_len >= n:
        return jnp.broadcast_to(x[n - 1:n], x.shape)
    x3 = x.reshape(n // seq_len, seq_len, w)
    return jnp.broadcast_to(x3[:, seq_len - 1:seq_len, :], x3.shape).reshape(n, w)


def _gates_rows_pre(gt, bias_r, lmat_t, tot, seq_len):
    ig = gt[0:SUBLANES] + bias_r[0:SUBLANES]
    lf = _log_sigmoid(gt[SUBLANES:] + bias_r[SUBLANES:])
    b = _dot_exact01_r(lf, lmat_t)
    bl = _dot_exact01_r(lf, tot)
    v = ig - b
    return bl, v, _seq_max_lanes(v, seq_len)


def _gates_rows_post(pre, mp_r):
    bl, v, vm = pre
    mn = bl + jnp.maximum(mp_r, vm)
    return jnp.exp(bl + v - mn), jnp.exp(bl + mp_r - mn), mn


def _gates_rows(gt, bias_r, mp_r, lmat_t, tot, seq_len):
    pre = _gates_rows_pre(gt, bias_r, lmat_t, tot, seq_len)
    return (pre[1],) + _gates_rows_post(pre, mp_r)


def _gates_cols_pre(g1, g2, bias_c, lmat, seq_len):
    lane = lax.broadcasted_iota(jnp.int32, g1.shape, 1)
    ig = g1 + bias_c[0:1]
    lf = jnp.where(lane < 2 * ML_HEADS, _log_sigmoid(g2 + bias_c[1:2]), 0.0)
    b = _dot_exact01(lmat, lf)
    return b, _seq_prefix_max_rows(ig - b, seq_len)


def _gates_cols_post(b, cm, mp_c):
    g = b + mp_c
    mt = jnp.maximum(b + cm, g)
    return b - mt, jnp.exp(g - mt), jnp.exp(-mt)


def _gates_cols(g1, g2, bias_c, mp_c, lmat, seq_len):
    b, cm = _gates_cols_pre(g1, g2, bias_c, lmat, seq_len)
    return (b, cm) + _gates_cols_post(b, cm, mp_c)


def _outer_sum_lhs(u):
    lane = lax.broadcasted_iota(jnp.int32, u.shape, 1)
    hi = u.astype(BF16).astype(F32)
    lo = u - hi
    return jnp.where(lane < ML_HEADS, hi, jnp.where(lane < 2 * ML_HEADS, lo,
                     jnp.where(lane < 4 * ML_HEADS, 1.0, 0.0))).astype(BF16)


def _outer_sum_rhs(v_r, hd):
    row = lax.broadcasted_iota(jnp.int32, v_r.shape, 0)
    hi = v_r.astype(BF16).astype(F32)
    lo = v_r - hi
    pick = (row == hd) | (row == ML_HEADS + hd)
    top = jnp.where(pick, 1.0, 0.0)
    bot = jnp.where(row == hd, hi, jnp.where(row == ML_HEADS + hd, lo, 0.0))
    r16 = jnp.concatenate([top, bot], axis=0).astype(BF16)
    return jnp.concatenate([r16, jnp.zeros((LANES - 2 * SUBLANES, v_r.shape[1]), BF16)], axis=0)


def _decayed_scores(q, kt, lhsc, v_r, hd, mask):
    e = _dot(lhsc, _outer_sum_rhs(v_r, hd))
    return _dot(q.astype(BF16), kt.astype(BF16)) * jnp.where(mask, jnp.exp(e), 0.0)


def _conv3_rows(cx, prev8, cw):
    n = cx.shape[1]
    row = lax.broadcasted_iota(jnp.int32, (SUBLANES, n), 0)
    s1 = pltpu.roll(cx, 1, 0)
    s2 = pltpu.roll(cx, 2, 0)
    f1 = jnp.where(row < 1, pltpu.roll(prev8, 1, 0), s1[0:SUBLANES])
    f2 = jnp.where(row < 2, pltpu.roll(prev8, 2, 0), s2[0:SUBLANES])
    s1 = jnp.concatenate([f1, s1[SUBLANES:]], axis=0)
    s2 = jnp.concatenate([f2, s2[SUBLANES:]], axis=0)
    return s2 * cw[0:1] + s1 * cw[1:2] + cx * cw[2:3]


def _mlstm_out_norm(hm, zo, onw):
    ms = jnp.mean(hm * hm, axis=-1, keepdims=True)
    return hm * lax.rsqrt(ms + EPS) * onw * _sigmoid(zo)


def _even_tail(z_scr, hm_scr, cat_scr, onw_ref):
    for hd in range(ML_HEADS):
        col = slice(hd * ML_DV, (hd + 1) * ML_DV)
        zo = z_scr[:, O_COL + hd * ML_DV:O_COL + (hd + 1) * ML_DV]
        cat_scr[:, col] = _mlstm_out_norm(hm_scr[:, col], zo, onw_ref[:, col]).astype(BF16)


def _mix_even_prompt_kernel(x_ref, mod_ref, nw_ref, win_ref, wkg_ref, bifc_ref, bifr_ref, onw_ref, cw_ref,
                            wout_ref, xo_ref, co_ref, no_ref, mo_ref, sco_ref,
                            z_scr, zt_scr, hm_scr, cat_scr, cn_scr, mrow_scr, mlane_scr, cc_scr):
    t = pl.program_id(1)

    @pl.when(t == 0)
    def _():
        cn_scr[...] = jnp.zeros_like(cn_scr)
        mrow_scr[...] = jnp.zeros_like(mrow_scr)
        mlane_scr[...] = jnp.zeros_like(mlane_scr)
        cc_scr[...] = jnp.zeros_like(cc_scr)

    x = x_ref[0]
    bi = pl.program_id(0)
    h = _norm_mod(x, nw_ref[...], _mod_row(mod_ref, 1, bi), _mod_row(mod_ref, 0, bi)).astype(BF16)
    z_scr[...] = _dot(h, win_ref[...])
    zt_scr[...] = _dot_nt(wkg_ref[...], h)

    mask, lmat, lmat_t, tot = _chunk_consts(CHUNK)
    scale = ML_DK ** -0.5
    ones_v = jnp.ones((CHUNK, ML_DV), BF16)

    n_ch = TM // CHUNK
    chunk_rows = [slice(c * CHUNK, (c + 1) * CHUNK) for c in range(n_ch)]
    pre_r = [_gates_rows_pre(zt_scr[ML_WIDTH:KG_ROWS, rows], bifr_ref[...], lmat_t, tot, CHUNK)
             for rows in chunk_rows]
    pre_c = [_gates_cols_pre(z_scr[rows, G1_COL:G1_COL + LANES], z_scr[rows, G2_COL:G2_COL + LANES],
                             bifc_ref[...], lmat, CHUNK) for rows in chunk_rows]
    mp_r = mrow_scr[...]
    mp_c = mlane_scr[0:1, :]
    post_r, post_c = [], []
    for c in range(n_ch):
        post_r.append(_gates_rows_post(pre_r[c], mp_r))
        mp_r = post_r[c][2]
        b, cm = pre_c[c]
        post_c.append(_gates_cols_post(b, cm, mp_c))
        mp_c = b[CHUNK - 1:CHUNK] + jnp.maximum(mp_c, cm[CHUNK - 1:CHUNK])
    mrow_scr[...] = mp_r
    mlane_scr[...] = jnp.broadcast_to(mp_c, (SUBLANES, LANES))

    intra, upd, wgq = {}, {}, {}
    for c, rows in enumerate(chunk_rows):
        u, wg, _ = post_c[c]
        lhsc = _outer_sum_lhs(u)
        for hd in range(ML_HEADS):
            q = z_scr[rows, Q_COL + hd * ML_DK:Q_COL + (hd + 1) * ML_DK]
            v = z_scr[rows, V_COL + hd * ML_DV:V_COL + (hd + 1) * ML_DV]
            kt = zt_scr[hd * ML_DK:(hd + 1) * ML_DK, rows] * scale
            s = _decayed_scores(q, kt, lhsc, pre_r[c][1], hd, mask)
            v1 = jnp.concatenate([v.astype(BF16), ones_v], axis=1)
            intra[c, hd] = _dot(s.astype(BF16), v1)
            upd[c, hd] = _dot((kt * post_r[c][0][hd:hd + 1, :]).astype(BF16), v1)
            wgq[c, hd] = (q * wg[:, hd:hd + 1]).astype(BF16)

    for hd in range(ML_HEADS):
        cn = cn_scr[hd]
        for c, rows in enumerate(chunk_rows):
            out = intra[c, hd] + _dot(wgq[c, hd], cn.astype(BF16))
            r = 1.0 / jnp.maximum(jnp.abs(out[:, ML_DV:]), post_c[c][2][:, hd:hd + 1])
            hm_scr[rows, hd * ML_DV:(hd + 1) * ML_DV] = out[:, :ML_DV] * r
            wc = post_r[c][1][hd:hd + 1, :]
            cn = jnp.concatenate([wc, wc], axis=1) * cn + upd[c, hd]
        cn_scr[hd] = cn

    _even_tail(z_scr, hm_scr, cat_scr, onw_ref)
    cx = z_scr[:, C_COL:C_COL + SC_WIDTH] * z_scr[:, X_COL:X_COL + SC_WIDTH]
    u = _conv3_rows(cx, cc_scr[...], cw_ref[...])
    cc_scr[...] = cx[TM - SUBLANES:TM]
    cat_scr[:, ML_WIDTH:] = (z_scr[:, B_COL:B_COL + SC_WIDTH] * u).astype(BF16)

    y = _dot(cat_scr[...], wout_ref[...])
    xo_ref[0] = x + _mod_row(mod_ref, 2, bi) * y
    for hd in range(ML_HEADS):
        co_ref[0, hd] = cn_scr[hd, :, 0:ML_DV]
        no_ref[0, hd] = cn_scr[hd, :, ML_DV:]
    mo_ref[0] = mrow_scr[...]
    sco_ref[0] = cc_scr[...]


def _even_weight_specs(i, idx):
    once = pl.Buffered(1)
    return [
        pl.BlockSpec((None, D_MODEL, IN_W), idx, pipeline_mode=once),
        pl.BlockSpec((None, KG_ROWS, D_MODEL), idx, pipeline_mode=once),
        pl.BlockSpec((None, 2, LANES), idx),
        pl.BlockSpec((None, 2 * SUBLANES, LANES), idx),
        pl.BlockSpec((None, 1, ML_WIDTH), idx),
        pl.BlockSpec((None, 3, SC_WIDTH), idx),
        pl.BlockSpec((None, ML_WIDTH + SC_WIDTH, D_MODEL), idx, pipeline_mode=once),
    ]


def _mix_even_prompt_call(x, mod, n_seq_rows, l, nw, win, wkg, bifc, bifr, onw, cw, wout):
    B, S, _ = x.shape
    nt = S // TM
    i = l // 2
    return pl.pallas_call(
        _mix_even_prompt_kernel,
        grid=(B, nt),
        in_specs=[
            pl.BlockSpec((1, TM, D_MODEL), lambda b, t: (b, t, 0)),
            pl.BlockSpec((None, N_MOD, SUBLANES, D_MODEL), lambda b, t: (l, 0, n_seq_rows // SUBLANES, 0)),
            pl.BlockSpec((None, 1, D_MODEL), lambda b, t: (l, 0, 0)),
        ] + _even_weight_specs(i, lambda b, t: (i, 0, 0)),
        out_specs=[
            pl.BlockSpec((1, TM, D_MODEL), lambda b, t: (b, t, 0)),
            pl.BlockSpec((1, ML_HEADS, ML_DK, ML_DV), lambda b, t: (b, 0, 0, 0)),
            pl.BlockSpec((1, ML_HEADS, ML_DK, LANES), lambda b, t: (b, 0, 0, 0)),
            pl.BlockSpec((1, SUBLANES, LANES), lambda b, t: (b, 0, 0)),
            pl.BlockSpec((1, SUBLANES, SC_WIDTH), lambda b, t: (b, 0, 0)),
        ],
        out_shape=[
            jax.ShapeDtypeStruct((B, S, D_MODEL), F32),
            jax.ShapeDtypeStruct((B, ML_HEADS, ML_DK, ML_DV), F32),
            jax.ShapeDtypeStruct((B, ML_HEADS, ML_DK, LANES), F32),
            jax.ShapeDtypeStruct((B, SUBLANES, LANES), F32),
            jax.ShapeDtypeStruct((B, SUBLANES, SC_WIDTH), F32),
        ],
        scratch_shapes=[
            pltpu.VMEM((TM, IN_W), F32),
            pltpu.VMEM((KG_ROWS, TM), F32),
            pltpu.VMEM((TM, ML_WIDTH), F32),
            pltpu.VMEM((TM, ML_WIDTH + SC_WIDTH), BF16),
            pltpu.VMEM((ML_HEADS, ML_DK, ML_DV + LANES), F32),
            pltpu.VMEM((SUBLANES, LANES), F32),
            pltpu.VMEM((SUBLANES, LANES), F32),
            pltpu.VMEM((SUBLANES, SC_WIDTH), F32),
        ],
        compiler_params=_cparams(2),
        name="mix_even_prompt",
    )(x, mod, nw, win, wkg, bifc, bifr, onw, cw, wout)


def _mix_even_sample_kernel(x_ref, mod_ref, nw_ref, win_ref, wkg_ref, bifc_ref, bifr_ref, onw_ref, cw_ref,
                            wout_ref, c_ref, nt_ref, ntt_ref, mcol_ref, mrow_ref, sc_ref,
                            xo_ref, co_ref, no_ref, mo_ref, sco_ref,
                            h_scr, z_scr, zt_scr, hm_scr, cat_scr, g1_scr,
                            intra_scr, dpart_scr, pbe_scr, wgq_scr, kwt_scr, vb_scr, wcb_scr, inter_scr):
    n_tok = x_ref.shape[0]
    seq_len = n_tok // SEQ_BLK

    def modulate(b, carry):
        rows = pl.ds(pl.multiple_of(b * seq_len, seq_len), seq_len)
        h_scr[rows, :] = _norm_mod(x_ref[rows, :], nw_ref[...], _mod_row(mod_ref, 1, b), _mod_row(mod_ref, 0, b))
        g1_scr[rows, :] = jnp.broadcast_to(_mod_row(mod_ref, 2, b), (seq_len, D_MODEL))
        return carry

    lax.fori_loop(0, SEQ_BLK, modulate, 0, unroll=4)
    hb = h_scr[...].astype(BF16)
    z_scr[...] = _dot(hb, win_ref[...])
    zt_scr[...] = _dot_nt(wkg_ref[...], hb)

    mask, lmat, lmat_t, tot = _chunk_consts(seq_len)
    scale = ML_DK ** -0.5
    ones_v = jnp.ones((CHUNK, ML_DV), BF16)
    mp_r = mrow_ref[...]
    v_r, ws_r, wc_r, mn_r = _gates_rows(zt_scr[ML_WIDTH:KG_ROWS, :], bifr_ref[...], mp_r, lmat_t, tot, seq_len)
    mo_ref[...] = mn_r
    mp_c = mcol_ref[...]
    b, cm, u, wg, pbe = _gates_cols(z_scr[:, G1_COL:G1_COL + LANES], z_scr[:, G2_COL:G2_COL + LANES],
                                    bifc_ref[...], mp_c, lmat, seq_len)
    wc_c = jnp.exp(mp_c - jnp.maximum(mp_c, _seq_last_row(cm, seq_len)))
    lhsc = _outer_sum_lhs(u)

    for hd in range(ML_HEADS):
        q = z_scr[:, Q_COL + hd * ML_DK:Q_COL + (hd + 1) * ML_DK]
        v = z_scr[:, V_COL + hd * ML_DV:V_COL + (hd + 1) * ML_DV]
        kt = zt_scr[hd * ML_DK:(hd + 1) * ML_DK, :] * scale
        s = _decayed_scores(q, kt, lhsc, v_r, hd, mask)
        vb = v.astype(BF16)
        out = _dot(s.astype(BF16), jnp.concatenate([vb, ones_v], axis=1))
        wg_h = wg[:, hd:hd + 1]
        qn = jnp.sum(q * nt_ref[hd], axis=-1, keepdims=True)
        intra_scr[hd] = out[:, :ML_DV]
        dpart_scr[hd] = out[:, ML_DV:] + wg_h * qn
        pbe_scr[hd] = jnp.broadcast_to(pbe[:, hd:hd + 1], (n_tok, LANES))
        wgq_scr[hd] = q * wg_h
        kwt = kt * ws_r[hd:hd + 1, :]
        kwt_scr[hd] = kwt
        vb_scr[hd] = vb
        wcb_scr[hd] = jnp.broadcast_to(wc_c[:, hd:hd + 1], (n_tok, LANES))
        no_ref[hd] = wc_r[hd:hd + 1, :] * ntt_ref[hd] + _dot_exact01_r(kwt, tot)

    lane_i = lax.broadcasted_iota(jnp.int32, (ML_DK, n_tok), 1)

    def per_seq(bq, carry):
        r0 = pl.multiple_of(bq * seq_len, seq_len)
        rows = pl.ds(r0, seq_len)
        sel = (lane_i >= r0) & (lane_i < r0 + seq_len)
        for hd in range(ML_HEADS):
            c_prev = c_ref[bq, hd]
            inter_scr[hd, rows, :] = _dot(wgq_scr[hd, rows, :].astype(BF16), c_prev.astype(BF16))
            kw_b = jnp.where(sel, kwt_scr[hd], 0.0).astype(BF16)
            co_ref[bq, hd] = wcb_scr[hd, pl.ds(r0, 1), :] * c_prev + _dot(kw_b, vb_scr[hd])
        return carry

    lax.fori_loop(0, SEQ_BLK, per_seq, 0, unroll=4)

    for hd in range(ML_HEADS):
        num = inter_scr[hd] + intra_scr[hd]
        hm_scr[:, hd * ML_DV:(hd + 1) * ML_DV] = num * (
            1.0 / jnp.maximum(jnp.abs(dpart_scr[hd]), pbe_scr[hd]))

    _even_tail(z_scr, hm_scr, cat_scr, onw_ref)
    cx = z_scr[:, C_COL:C_COL + SC_WIDTH] * z_scr[:, X_COL:X_COL + SC_WIDTH]
    sub = lax.broadcasted_iota(jnp.int32, (n_tok, SC_WIDTH), 0) % seq_len
    p1 = sc_ref[...]
    s1 = jnp.where(sub < 1, pltpu.roll(p1, n_tok - 1, 0), pltpu.roll(cx, 1, 0))
    s2 = jnp.where(sub < 2, p1, pltpu.roll(cx, 2, 0))
    cw = cw_ref[...]
    u = s2 * cw[0:1] + s1 * cw[1:2] + cx * cw[2:3]
    sco_ref[...] = pltpu.roll(cx, n_tok - (seq_len - 2), 0)
    cat_scr[:, ML_WIDTH:] = (z_scr[:, B_COL:B_COL + SC_WIDTH] * u).astype(BF16)

    y = _dot(cat_scr[...], wout_ref[...])
    xo_ref[...] = x_ref[...] + g1_scr[...] * y


def _mix_even_sample_call(x, mod, l, nw, win, wkg, bifc, bifr, onw, cw, wout, c0, n_tok, n_tok_t, m_col, m_row,
                          sc_pad, co_prev):
    n_rows = x.shape[0]
    n_seq = c0.shape[1]
    seq_len = n_rows // n_seq
    blk = SEQ_BLK * seq_len
    nb = n_seq // SEQ_BLK
    li = l // 2
    head_blk = (ML_HEADS, blk, LANES)
    return _stacked_call(
        _mix_even_sample_kernel, 16, 1, co_prev, li,
        lambda lead, idx: pl.BlockSpec((lead, SEQ_BLK, ML_HEADS, ML_DK, ML_DV), lambda i: (idx, i, 0, 0, 0)),
        grid=(nb,),
        in_specs=[
            pl.BlockSpec((blk, D_MODEL), lambda i: (i, 0)),
            pl.BlockSpec((None, N_MOD, SEQ_BLK, D_MODEL), lambda i: (l, 0, i, 0)),
            pl.BlockSpec((None, 1, D_MODEL), lambda i: (l, 0, 0)),
        ] + _even_weight_specs(li, lambda i: (li, 0, 0)) + [
            pl.BlockSpec((None, SEQ_BLK, ML_HEADS, ML_DK, ML_DV), lambda i: (li, i, 0, 0, 0)),
            pl.BlockSpec((ML_HEADS, blk, ML_DK), lambda i: (0, i, 0)),
            pl.BlockSpec((ML_HEADS, ML_DK, blk), lambda i: (0, 0, i)),
            pl.BlockSpec((blk, LANES), lambda i: (i, 0)),
            pl.BlockSpec((SUBLANES, blk), lambda i: (0, i)),
            pl.BlockSpec((blk, SC_WIDTH), lambda i: (i, 0)),
        ],
        out_specs=[
            pl.BlockSpec((blk, D_MODEL), lambda i: (i, 0)),
            None,
            pl.BlockSpec((ML_HEADS, ML_DK, blk), lambda i: (0, 0, i)),
            pl.BlockSpec((SUBLANES, blk), lambda i: (0, i)),
            pl.BlockSpec((blk, SC_WIDTH), lambda i: (i, 0)),
        ],
        out_shape=[
            jax.ShapeDtypeStruct((n_rows, D_MODEL), F32),
            jax.ShapeDtypeStruct(c0.shape, F32),
            jax.ShapeDtypeStruct((ML_HEADS, ML_DK, n_rows), F32),
            jax.ShapeDtypeStruct((SUBLANES, n_rows), F32),
            jax.ShapeDtypeStruct((n_rows, SC_WIDTH), F32),
        ],
        scratch_shapes=[
            pltpu.VMEM((blk, D_MODEL), F32),
            pltpu.VMEM((blk, IN_W), F32),
            pltpu.VMEM((KG_ROWS, blk), F32),
            pltpu.VMEM((blk, ML_WIDTH), F32),
            pltpu.VMEM((blk, ML_WIDTH + SC_WIDTH), BF16),
            pltpu.VMEM((blk, D_MODEL), F32),
            pltpu.VMEM(head_blk, F32),
            pltpu.VMEM(head_blk, F32),
            pltpu.VMEM(head_blk, F32),
            pltpu.VMEM(head_blk, F32),
            pltpu.VMEM((ML_HEADS, ML_DK, blk), F32),
            pltpu.VMEM(head_blk, BF16),
            pltpu.VMEM(head_blk, F32),
            pltpu.VMEM(head_blk, F32),
        ],
        compiler_params=_cparams(1),
        name="mix_even_sample",
    )(x, mod, nw, win, wkg, bifc, bifr, onw, cw, wout, c0, n_tok, n_tok_t, m_col, m_row, sc_pad)


def _split2(x):
    hi = x.astype(BF16)
    lo = (x - hi.astype(F32)).astype(BF16)
    return jnp.concatenate([hi, lo], axis=1)


def _head_lane_mats():
    r = lax.broadcasted_iota(jnp.int32, (2 * LANES, LANES), 0) % LANES
    c = lax.broadcasted_iota(jnp.int32, (2 * LANES, LANES), 1)
    hsum = ((r // HEAD_DIM) == (c // HEAD_DIM)).astype(BF16)
    half = HEAD_DIM // 2
    src = jnp.where((c % HEAD_DIM) < half, c + half, c - half)
    return hsum, (r == src).astype(BF16)


def _q_lane_mat():
    r = lax.broadcasted_iota(jnp.int32, (2 * LANES, 2 * LANES), 0)
    c = lax.broadcasted_iota(jnp.int32, (2 * LANES, 2 * LANES), 1)
    half = HEAD_DIM // 2
    src = jnp.where((c % HEAD_DIM) < half, c + half, c - half)
    top = (r < LANES) & (c < LANES) & ((r // HEAD_DIM) == (c // HEAD_DIM))
    return (top | ((r >= LANES) & (c >= LANES) & (r == src))).astype(BF16)


def _q_norm_rope(xb, gw, cos, sin, qmat):
    zg = xb * gw
    out = _dot(jnp.concatenate([(xb * xb).astype(BF16), zg.astype(BF16)], axis=1), qmat)
    ms = out[:, :LANES] * (1.0 / HEAD_DIM)
    return lax.rsqrt(ms + EPS) * (zg * cos + out[:, LANES:] * sin)


def _qk_norm_rope(xb, gw, cos, sin, hsum, rot_mat):
    ms = _dot(_split2(xb * xb), hsum) * (1.0 / HEAD_DIM)
    zg = xb * gw
    rot = _dot(_split2(zg), rot_mat)
    return lax.rsqrt(ms + EPS) * (zg * cos + rot * sin)


def _sink_rows(sink8, reps):
    return jnp.concatenate(
        [jnp.broadcast_to(sink8[r:r + 1, :], (reps, LANES)) for r in range(SUBLANES)], axis=0)


def _sink_col(sink8, reps):
    parts = [jnp.broadcast_to(sink8[r:r + 1, :], (reps, LANES)) for r in range(SUBLANES)]
    return jnp.concatenate(parts, axis=0)[:, 0:1]


def _attn_prompt_kernel(x_ref, mod_ref, nw_ref, wqkv_ref, qnw_ref, knw_ref, cos_ref, sin_ref,
                        sink_ref, wout_ref, xo_ref, ko_ref, vo_ref,
                        z_scr, qm_scr, k_scr, v_scr, o_scr):
    t = pl.program_id(1)
    n_qb = TM // WINDOW

    @pl.when(t == 0)
    def _():
        k_scr[0:WINDOW, :] = jnp.zeros((WINDOW, KV_W), BF16)
        v_scr[0:WINDOW, :] = jnp.zeros((WINDOW, KV_W), BF16)

    x = x_ref[0]
    bi = pl.program_id(0)
    h = _norm_mod(x, nw_ref[...], _mod_row(mod_ref, 1, bi), _mod_row(mod_ref, 0, bi)).astype(BF16)
    z_scr[...] = _dot(h, wqkv_ref[...])
    cos = cos_ref[...]
    sin = sin_ref[...]
    hmats = _head_lane_mats()
    qmat = _q_lane_mat()
    half0 = lax.broadcasted_iota(jnp.int32, (TM, LANES), 1) < HEAD_DIM
    qscale = HEAD_DIM ** -0.5
    for jb in range(Q_W // LANES):
        y = _q_norm_rope(z_scr[:, jb * LANES:(jb + 1) * LANES], qnw_ref[...], cos, sin, qmat) * qscale
        qm_scr[2 * jb] = jnp.where(half0, y, 0.0).astype(BF16)
        qm_scr[2 * jb + 1] = jnp.where(half0, 0.0, y).astype(BF16)
    for p in range(KV_W // LANES):
        kf = _qk_norm_rope(z_scr[:, Q_W + p * LANES:Q_W + (p + 1) * LANES], knw_ref[...], cos, sin, *hmats)
        ko_ref[0, :, p * LANES:(p + 1) * LANES] = kf[TM - WINDOW:TM]
        k_scr[WINDOW:WINDOW + TM, p * LANES:(p + 1) * LANES] = kf.astype(BF16)
    vf = z_scr[:, Q_W + KV_W:Q_W + 2 * KV_W]
    vo_ref[0] = vf[TM - WINDOW:TM]
    v_scr[WINDOW:WINDOW + TM, :] = vf.astype(BF16)

    r = lax.broadcasted_iota(jnp.int32, (8 * WINDOW, 2 * WINDOW), 0) % WINDOW
    c = lax.broadcasted_iota(jnp.int32, (8 * WINDOW, 2 * WINDOW), 1)
    valid = ((c < WINDOW) & (c > r)) | ((c >= WINDOW) & ((c - WINDOW) <= r))
    first_lim = jnp.where(t == 0, WINDOW, 0)
    half0q = lax.broadcasted_iota(jnp.int32, (4 * WINDOW, LANES), 1) < HEAD_DIM
    ones_kv = jnp.ones((2 * WINDOW, LANES), BF16)
    for qb in range(n_qb):
        rows = slice(qb * WINDOW, (qb + 1) * WINDOW)
        krows = slice(qb * WINDOW, (qb + 2) * WINDOW)
        vmask = (valid & (c >= first_lim)) if qb == 0 else valid
        for p in range(KV_W // LANES):
            kb = k_scr[krows, p * LANES:(p + 1) * LANES]
            vb = v_scr[krows, p * LANES:(p + 1) * LANES]
            qs = jnp.concatenate([qm_scr[2 * (4 * p + i) + e, rows, :] for e in range(2) for i in range(4)],
                                 axis=0)
            s = jnp.where(vmask, _dot_nt(qs, kb), -jnp.inf)
            sk = _sink_rows(sink_ref[p], WINDOW)
            mx = jnp.maximum(jnp.max(s, axis=-1, keepdims=True), sk)
            pr = jnp.exp(s - jnp.concatenate([mx, mx], axis=1))
            o2 = _dot(pr.astype(BF16), jnp.concatenate([vb, ones_kv], axis=1))
            den = o2[:, LANES:] + jnp.exp(sk - mx)
            o = o2[:, :LANES] * (1.0 / den)
            merged = jnp.where(half0q, o[0:4 * WINDOW], o[4 * WINDOW:])
            for i in range(4):
                o_scr[rows, (4 * p + i) * LANES:(4 * p + i + 1) * LANES] = (
                    merged[i * WINDOW:(i + 1) * WINDOW].astype(BF16))

    y = _dot(o_scr[...], wout_ref[...])
    xo_ref[0] = x + _mod_row(mod_ref, 2, bi) * y
    k_scr[0:WINDOW, :] = k_scr[TM:TM + WINDOW, :]
    v_scr[0:WINDOW, :] = v_scr[TM:TM + WINDOW, :]


def _attn_prompt_call(x, mod, n_seq_rows, l, nw, wqkv, qnw, knw, cos, sin, sink, wout):
    B, S, _ = x.shape
    nt = S // TM
    const2 = lambda b, t: (0, 0)
    return pl.pallas_call(
        _attn_prompt_kernel,
        grid=(B, nt),
        in_specs=[
            pl.BlockSpec((1, TM, D_MODEL), lambda b, t: (b, t, 0)),
            pl.BlockSpec((None, N_MOD, SUBLANES, D_MODEL), lambda b, t: (l, 0, n_seq_rows // SUBLANES, 0)),
            pl.BlockSpec((None, 1, D_MODEL), lambda b, t: (l, 0, 0)),
            pl.BlockSpec((D_MODEL, Q_W + 2 * KV_W), const2),
            pl.BlockSpec((1, LANES), const2),
            pl.BlockSpec((1, LANES), const2),
            pl.BlockSpec((TM, LANES), lambda b, t: (t, 0)),
            pl.BlockSpec((TM, LANES), lambda b, t: (t, 0)),
            pl.BlockSpec((2, SUBLANES, LANES), lambda b, t: (0, 0, 0)),
            pl.BlockSpec((Q_W, D_MODEL), const2),
        ],
        out_specs=[
            pl.BlockSpec((1, TM, D_MODEL), lambda b, t: (b, t, 0)),
            pl.BlockSpec((1, WINDOW, KV_W), lambda b, t: (b, 0, 0)),
            pl.BlockSpec((1, WINDOW, KV_W), lambda b, t: (b, 0, 0)),
        ],
        out_shape=[
            jax.ShapeDtypeStruct((B, S, D_MODEL), F32),
            jax.ShapeDtypeStruct((B, WINDOW, KV_W), F32),
            jax.ShapeDtypeStruct((B, WINDOW, KV_W), F32),
        ],
        scratch_shapes=[
            pltpu.VMEM((TM, Q_W + 2 * KV_W), F32),
            pltpu.VMEM((2 * Q_W // LANES, TM, LANES), BF16),
            pltpu.VMEM((TM + WINDOW, KV_W), BF16),
            pltpu.VMEM((TM + WINDOW, KV_W), BF16),
            pltpu.VMEM((TM, Q_W), BF16),
        ],
        compiler_params=_cparams(2),
        name="attn_prompt",
    )(x, mod, nw, wqkv, qnw, knw, cos, sin, sink, wout)


def _attn_sample_kernel(x_ref, mod_ref, nw_ref, wqkv_ref, qnw_ref, knw_ref, cos_ref, sin_ref,
                        sink_ref, wout_ref, kc_ref, vc_ref,
                        xo_ref, kco_ref, vco_ref,
                        h_scr, g1_scr, z_scr, qm_scr, kn_scr, o_scr):
    n_tok = x_ref.shape[0]
    seq_len = n_tok // SEQ_BLK
    win = kc_ref.shape[1]

    def modulate(b, carry):
        rows = pl.ds(pl.multiple_of(b * seq_len, seq_len), seq_len)
        h_scr[rows, :] = _norm_mod(x_ref[rows, :], nw_ref[...], _mod_row(mod_ref, 1, b), _mod_row(mod_ref, 0, b))
        g1_scr[rows, :] = jnp.broadcast_to(_mod_row(mod_ref, 2, b), (seq_len, D_MODEL))
        return carry

    lax.fori_loop(0, SEQ_BLK, modulate, 0, unroll=4)
    z_scr[...] = _dot(h_scr[...].astype(BF16), wqkv_ref[...])
    cos = cos_ref[...]
    sin = sin_ref[...]
    hmats = _head_lane_mats()
    half0 =lax.broadcasted_iota(jnp.int32, (n_tok, LANES), 1) < HEAD_DIM
    qscale = HEAD_DIM ** -0.5
    for jb in range(Q_W // LANES):
        y = _qk_norm_rope(z_scr[:, jb * LANES:(jb + 1) * LANES], qnw_ref[...], cos, sin, *hmats) * qscale
        qm_scr[2 * jb] = jnp.where(half0, y, 0.0)
        qm_scr[2 * jb + 1] = jnp.where(half0, 0.0, y)
    for p in range(KV_W // LANES):
        kn_scr[:, p * LANES:(p + 1) * LANES] = _qk_norm_rope(
            z_scr[:, Q_W + p * LANES:Q_W + (p + 1) * LANES], knw_ref[...], cos, sin, *hmats)

    n_q = 8 * seq_len
    tq = lax.broadcasted_iota(jnp.int32, (SEQ_BLK, n_q, 2 * win), 1) % seq_len
    cc = lax.broadcasted_iota(jnp.int32, (SEQ_BLK, n_q, 2 * win), 2)
    valid = ((cc < win) & (cc > tq)) | ((cc >= 2 * win - seq_len) & ((cc - (2 * win - seq_len)) <= tq))
    half0q = lax.broadcasted_iota(jnp.int32, (SEQ_BLK, n_q // 2, LANES), 2) < HEAD_DIM
    ones_kv = jnp.ones((SEQ_BLK, 2 * win, LANES), BF16)

    kc = kc_ref[...]
    vc = vc_ref[...]
    knew = jnp.concatenate([kc[:, seq_len:], kn_scr[...].reshape(SEQ_BLK, seq_len, KV_W)], axis=1)
    vnew = jnp.concatenate(
        [vc[:, seq_len:], z_scr[:, Q_W + KV_W:Q_W + 2 * KV_W].reshape(SEQ_BLK, seq_len, KV_W)], axis=1)
    kco_ref[...] = knew
    vco_ref[...] = vnew
    for p in range(KV_W // LANES):
        lanes = slice(p * LANES, (p + 1) * LANES)
        qs = jnp.concatenate([qm_scr[2 * (4 * p + i) + e].reshape(SEQ_BLK, seq_len, LANES)
                              for e in range(2) for i in range(4)], axis=1).astype(BF16)
        kk = jnp.concatenate([kc[:, :, lanes], knew[:, :, lanes]], axis=1).astype(BF16)
        vv = jnp.concatenate([vc[:, :, lanes], vnew[:, :, lanes]], axis=1).astype(BF16)
        s = jnp.einsum("bqd,bkd->bqk", qs, kk, preferred_element_type=F32)
        s = jnp.where(valid, s, -jnp.inf)
        sk = _sink_rows(sink_ref[p], seq_len)[None]
        mx = jnp.maximum(jnp.max(s, axis=-1, keepdims=True), sk)
        pr = jnp.exp(s - jnp.concatenate([mx, mx], axis=-1))
        o2 = jnp.einsum("bqk,bkd->bqd", pr.astype(BF16), jnp.concatenate([vv, ones_kv], axis=-1),
                        preferred_element_type=F32)
        o = o2[:, :, :LANES] * (1.0 / (o2[:, :, LANES:] + jnp.exp(sk - mx)))
        merged = jnp.where(half0q, o[:, 0:n_q // 2], o[:, n_q // 2:])
        for i in range(4):
            o_scr[:, (4 * p + i) * LANES:(4 * p + i + 1) * LANES] = (
                merged[:, i * seq_len:(i + 1) * seq_len].reshape(n_tok, LANES))

    y = _dot(o_scr[...].astype(BF16), wout_ref[...])
    xo_ref[...] = x_ref[...] + g1_scr[...] * y


def _attn_sample_call(x, mod, l, nw, wqkv, qnw, knw, cos, sin, sink, wout, kc, vc):
    n_rows = x.shape[0]
    _, n_seq, win, _ = kc.shape
    lj = l // 2
    seq_len = n_rows // n_seq
    blk = SEQ_BLK * seq_len
    nb = n_seq // SEQ_BLK
    const2 = lambda i: (0, 0)
    return pl.pallas_call(
        _attn_sample_kernel,
        input_output_aliases={10: 1, 11: 2},
        grid=(nb,),
        in_specs=[
            pl.BlockSpec((blk, D_MODEL), lambda i: (i, 0)),
            pl.BlockSpec((None, N_MOD, SEQ_BLK, D_MODEL), lambda i: (l, 0, i, 0)),
            pl.BlockSpec((None, 1, D_MODEL), lambda i: (l, 0, 0)),
            pl.BlockSpec((D_MODEL, Q_W + 2 * KV_W), const2),
            pl.BlockSpec((1, LANES), const2),
            pl.BlockSpec((1, LANES), const2),
            pl.BlockSpec((blk, LANES), const2),
            pl.BlockSpec((blk, LANES), const2),
            pl.BlockSpec((2, SUBLANES, LANES), lambda i: (0, 0, 0)),
            pl.BlockSpec((Q_W, D_MODEL), const2),
            pl.BlockSpec((None, SEQ_BLK, win, KV_W), lambda i: (lj, i, 0, 0)),
            pl.BlockSpec((None, SEQ_BLK, win, KV_W), lambda i: (lj, i, 0, 0)),
        ],
        out_specs=[
            pl.BlockSpec((blk, D_MODEL), lambda i: (i, 0)),
            pl.BlockSpec((None, SEQ_BLK, win, KV_W), lambda i: (lj, i, 0, 0)),
            pl.BlockSpec((None, SEQ_BLK, win, KV_W), lambda i: (lj, i, 0, 0)),
        ],
        out_shape=[
            jax.ShapeDtypeStruct((n_rows, D_MODEL), F32),
            jax.ShapeDtypeStruct(kc.shape, F32),
            jax.ShapeDtypeStruct(vc.shape, F32),
        ],
        scratch_shapes=[
            pltpu.VMEM((blk, D_MODEL), F32),
            pltpu.VMEM((blk, D_MODEL), F32),
            pltpu.VMEM((blk, Q_W + 2 * KV_W), F32),
            pltpu.VMEM((2 * Q_W // LANES, blk, LANES), F32),
            pltpu.VMEM((blk, KV_W), F32),
            pltpu.VMEM((blk, Q_W), F32),
        ],
        compiler_params=_cparams(1),
        name="attn_sample",
    )(x, mod, nw, wqkv, qnw, knw, cos, sin, sink, wout, kc, vc)


def _rope_tables(pos):
    half = HEAD_DIM // 2
    inv = ROPE_THETA ** (-jnp.arange(half, dtype=F32) / half)
    ang = pos.astype(F32)[:, None] * inv[None, :]
    cos = jnp.cos(ang)
    sin = jnp.sin(ang)
    return jnp.tile(cos, (1, 4)), jnp.concatenate([-sin, sin, -sin, sin], axis=1)


def _prep_attn(w_qkv, q_norm, k_norm, sink, w_out):
    perm = np.asarray(HEAD_PERM)
    wq = w_qkv[:, :Q_W].reshape(D_MODEL, ATT_HEADS, HEAD_DIM)[:, perm].reshape(D_MODEL, Q_W)
    wqkv = jnp.concatenate([wq, w_qkv[:, Q_W:]], axis=1).astype(BF16)
    wout = w_out.reshape(ATT_HEADS, HEAD_DIM, D_MODEL)[perm].reshape(Q_W, D_MODEL).astype(BF16)
    qnw = jnp.tile(q_norm, 2)[None]
    knw = jnp.tile(k_norm, 2)[None]
    idx = np.asarray([[perm[2 * (4 * p + i) + e] for e in range(2) for i in range(4)] for p in range(2)])
    sink_arr = jnp.broadcast_to(sink[idx][:, :, None], (2, SUBLANES, LANES)).astype(F32)
    return wqkv, qnw, knw, sink_arr, wout


def kernel(x_prompt, x_sample, c_prompt, c_sample, state_mlstm_C, state_mlstm_n, state_mlstm_m, state_sconv, cache_win_k, cache_win_v, state_ffn_conv, norm1, norm2, w_ada, b_ada, a_w_in, a_b_if, a_out_norm, a_conv_w, a_w_out, c_w_qkv, c_q_norm, c_k_norm, c_sink, c_w_out, f_w_up, f_conv_w, f_w_down):
    B, S, _ = x_prompt.shape
    NS, SL, _ = x_sample.shape
    assert S % TM == 0 and NS % SEQ_BLK == 0 and SEQ_BLK * SL == CHUNK and SL == SUBLANES

    assert B <= SUBLANES
    c_all = jnp.concatenate([c_sample, c_prompt, jnp.zeros((SUBLANES - B, D_MODEL), F32)], axis=0)
    mod = _ada_call(c_all, w_ada, b_ada)

    win_all = _prep_win_call(a_w_in)
    zg_w = a_w_in[:, :, SRC_G:SRC_B]
    wkg_all = jnp.swapaxes(jnp.concatenate(
        [a_w_in[:, :, SRC_K:SRC_V], zg_w[..., :ML_HEADS], zg_w[..., :ML_HEADS],
         zg_w[..., ML_HEADS:], zg_w[..., ML_HEADS:]], axis=-1), 1, 2).astype(BF16)
    b_i, b_f = a_b_if[:, :ML_HEADS], a_b_if[:, ML_HEADS:]
    lane_pad = jnp.zeros((a_b_if.shape[0], LANES - 2 * ML_HEADS), F32)
    bifc_all = jnp.stack([jnp.concatenate([b_i, b_i, lane_pad], axis=1),
                          jnp.concatenate([b_f, b_f, lane_pad], axis=1)], axis=1)
    bifr_all = jnp.broadcast_to(jnp.concatenate([b_i, b_i, b_f, b_f], axis=1)[:, :, None],
                                (a_b_if.shape[0], 2 * SUBLANES, LANES))
    wout_a_all = _cast_call(a_w_out)
    wup_all = _cast_call(f_w_up)
    wdn_all = _cast_call(f_w_down)
    onw_all = a_out_norm.reshape(-1, 1, ML_WIDTH)
    norm1_r = norm1.reshape(DEPTH, 1, D_MODEL)
    norm2_r = norm2.reshape(DEPTH, 1, D_MODEL)
    win_buf = cache_win_k.shape[2]
    kc_all = cache_win_k.reshape(-1, NS, win_buf, KV_W)
    vc_all = cache_win_v.reshape(-1, NS, win_buf, KV_W)

    cos_p, sin_p = _rope_tables(jnp.arange(S, dtype=jnp.int32))
    cos_s, sin_s = _rope_tables(PAST_LEN + jnp.arange(SL, dtype=jnp.int32))
    cos_s = jnp.tile(cos_s, (SEQ_BLK, 1))
    sin_s = jnp.tile(sin_s, (SEQ_BLK, 1))

    xp = x_prompt
    xs = x_sample.reshape(NS * SL, D_MODEL)
    p_C, p_n, p_m, p_sc, p_wk, p_wv, p_ffn = [], [], [], [], [], [], []
    s_n, s_m, s_sc = [], [], []
    s_C = s_ffn = None
    s_wk, s_wv = kc_all, vc_all

    for l in range(DEPTH):
        if l % 2 == 0:
            i = l // 2
            even_w = (norm1_r, win_all, wkg_all, bifc_all, bifr_all, onw_all, a_conv_w, wout_a_all)
            xp, co, no, mo, sco = _mix_even_prompt_call(xp, mod, NS, l, *even_w)
            p_C.append(co)
            p_n.append(no[:, :, :, 0])
            p_m.append(mo[:, :ML_HEADS, 0])
            p_sc.append(sco[:, SUBLANES - 2:, :])

            n_tok = jnp.repeat(state_mlstm_n[i].transpose(1, 0, 2), SL, axis=1)
            m_rep = jnp.repeat(state_mlstm_m[i], SL, axis=0)
            m_col = jnp.concatenate([m_rep, m_rep, jnp.zeros((NS * SL, LANES - 2 * ML_HEADS), F32)], axis=1)
            m_row = jnp.concatenate([m_rep, m_rep], axis=1).T
            sc_pad = jnp.pad(state_sconv[i], ((0, 0), (0, SL - 2), (0, 0))).reshape(NS * SL, SC_WIDTH)
            xs, s_C, no, mo, sco = _mix_even_sample_call(xs, mod, l, *even_w, state_mlstm_C, n_tok,
                                                         jnp.swapaxes(n_tok, 1, 2), m_col, m_row, sc_pad, s_C)
            s_n.append(no[:, :, ::SL].transpose(2, 0, 1))
            s_m.append(mo[:ML_HEADS, ::SL].T)
            s_sc.append(sco.reshape(NS, SL, SC_WIDTH)[:, :2])
        else:
            j = l // 2
            wqkv, qnw, knw, sink, wout = _prep_attn(c_w_qkv[j], c_q_norm[j], c_k_norm[j], c_sink[j], c_w_out[j])
            xp, ko, vo = _attn_prompt_call(xp, mod, NS, l, norm1_r, wqkv, qnw, knw, cos_p, sin_p, sink, wout)
            p_wk.append(ko.reshape(B, WINDOW, KV_HEADS, HEAD_DIM))
            p_wv.append(vo.reshape(B, WINDOW, KV_HEADS, HEAD_DIM))
            xs, s_wk, s_wv = _attn_sample_call(xs, mod, l, norm1_r, wqkv, qnw, knw, cos_s, sin_s, sink, wout,
                                               s_wk, s_wv)

        xp, st = _ffn_prompt_call(xp, mod, NS, l, norm2_r, wup_all, f_conv_w, wdn_all)
        p_ffn.append(st[:, SUBLANES - 2:, :])
        xs, s_ffn = _ffn_sample_call(xs, mod, l, norm2_r, state_ffn_conv, wup_all, f_conv_w, wdn_all, s_ffn)

    kv_shape = (-1, NS, win_buf, KV_HEADS, HEAD_DIM)
    return (xp, xs.reshape(NS, SL, D_MODEL),
            jnp.stack(p_C), jnp.stack(p_n), jnp.stack(p_m), jnp.stack(p_sc),
            jnp.stack(p_wk), jnp.stack(p_wv), jnp.stack(p_ffn),
            s_C, jnp.stack(s_n), jnp.stack(s_m), jnp.stack(s_sc),
            s_wk.reshape(kv_shape), s_wv.reshape(kv_shape), s_ffn)
```

```python
import functools

import jax
import jax.numpy as jnp
import numpy as np
from jax import lax
from jax.experimental import pallas as pl
from jax.experimental.pallas import tpu as pltpu

F32 = jnp.float32
BF16 = jnp.bfloat16

D_MODEL = 1024
DEPTH = 4
PAST_LEN = 8192
ML_HEADS = 4
ML_DK = 128
ML_DV = 128
ML_WIDTH = ML_HEADS * ML_DV
SC_WIDTH = D_MODEL // 2
ATT_HEADS = 16
KV_HEADS = 4
HEAD_DIM = 64
WINDOW = 128
ROPE_THETA = 10000.0
D_FF = 2816
EPS = 1e-6

LANES = 128
SUBLANES = 8
VMEM_LIMIT = 56 * 1024 * 1024

TM = 512
TM_FFN = 1024
CHUNK = 128
SEQ_BLK = 16
FC = 256
NCH = D_FF // FC
FS_GATE = 2
FC_S = D_FF // FS_GATE
Q_COL = 0
V_COL = Q_COL + ML_WIDTH
O_COL = V_COL + ML_WIDTH
G1_COL = O_COL + ML_WIDTH
G2_COL = G1_COL + LANES
B_COL = G2_COL + LANES
C_COL = B_COL + SC_WIDTH
X_COL = C_COL + SC_WIDTH
IN_W = X_COL + SC_WIDTH
KG_ROWS = ML_WIDTH + 16
SRC_K = ML_WIDTH
SRC_V = 2 * ML_WIDTH
SRC_G = 4 * ML_WIDTH
SRC_B = SRC_G + 2 * ML_HEADS
Q_W = ATT_HEADS * HEAD_DIM
KV_W = KV_HEADS * HEAD_DIM
HEAD_PERM = (0, 4, 1, 5, 2, 6, 3, 7, 8, 12, 9, 13, 10, 14, 11, 15)


def _cparams(n_axes):
    return pltpu.CompilerParams(dimension_semantics=("arbitrary",) * n_axes,
                                vmem_limit_bytes=VMEM_LIMIT)


def _stacked_call(kernel, n_in, out_idx, prev, slab, slab_spec, **kw):
    specs = list(kw.pop("in_specs"))
    out_specs = list(kw.pop("out_specs"))
    n_slabs = kw["out_shape"][out_idx].shape[0]
    if prev is None:
        out_specs[out_idx] = slab_spec(n_slabs, 0)

        def body(*refs):
            refs = list(refs)
            whole = refs[n_in + out_idx]
            for s in range(n_slabs):
                if s != slab:
                    whole[s] = jnp.zeros(whole.shape[1:], whole.dtype)
            refs[n_in + out_idx] = whole.at[slab]
            return kernel(*refs)

        return pl.pallas_call(body, in_specs=specs, out_specs=out_specs, **kw)

    out_specs[out_idx] = slab_spec(None, slab)

    def body(*refs):
        return kernel(*refs[:n_in], *refs[n_in + 1:])

    call = pl.pallas_call(body, in_specs=specs + [pl.BlockSpec(memory_space=pl.ANY)], out_specs=out_specs,
                          input_output_aliases={n_in: out_idx}, **kw)
    return lambda *args: call(*args, prev)


def _dot(a, b):
    return jnp.dot(a, b, preferred_element_type=F32)


def _dot_nt(a, b):
    return lax.dot_general(a, b, (((1,), (1,)), ((), ())), preferred_element_type=F32)


def _dot_tn(a, b):
    return lax.dot_general(a, b, (((0,), (0,)), ((), ())), preferred_element_type=F32)


def _dot_exact01(m, a):
    a1 = a.astype(BF16)
    r1 = a - a1.astype(F32)
    a2 = r1.astype(BF16)
    a3 = (r1 - a2.astype(F32)).astype(BF16)
    return _dot(m, a1) + _dot(m, a2) + _dot(m, a3)


def _norm_mod(x, nw, sc, sh):
    ms = jnp.mean(x * x, axis=-1, keepdims=True)
    return (x * lax.rsqrt(ms + EPS) * nw) * (1.0 + sc) + sh


def _sigmoid(x):
    return 1.0 / (1.0 + jnp.exp(-x))


def _log_sigmoid(x):
    return jnp.minimum(x, 0.0) - jnp.log(1.0 + jnp.exp(-jnp.abs(x)))


N_MOD = 6


def _ada_kernel(c_ref, w_ref, b_ref, o_ref):
    c = c_ref[...]
    s = (c * _sigmoid(c)).astype(BF16)
    o_ref[...] = _dot(s, w_ref[...].astype(BF16)) + b_ref[...]


def _ada_call(c_all, w_ada, b_ada):
    rows = c_all.shape[0]
    return pl.pallas_call(
        _ada_kernel,
        grid=(DEPTH, N_MOD),
        in_specs=[
            pl.BlockSpec((rows, D_MODEL), lambda l, k: (0, 0)),
            pl.BlockSpec((None, D_MODEL, D_MODEL), lambda l, k: (l, 0, k)),
            pl.BlockSpec((None, 1, D_MODEL), lambda l, k: (l, 0, k)),
        ],
        out_specs=pl.BlockSpec((None, None, rows, D_MODEL), lambda l, k: (l, k, 0, 0)),
        out_shape=jax.ShapeDtypeStruct((DEPTH, N_MOD, rows, D_MODEL), F32),
        compiler_params=_cparams(2),
        name="adaln_mod",
    )(c_all, w_ada, b_ada.reshape(DEPTH, 1, N_MOD * D_MODEL))


CAST_ROWS = 256


def _cast_kernel(w_ref, o_ref):
    o_ref[...] = w_ref[...].astype(BF16)


def _cast_call(w):
    n_l, rows, cols = w.shape
    tr = CAST_ROWS if rows % CAST_ROWS == 0 else rows
    return pl.pallas_call(
        _cast_kernel,
        grid=(n_l, rows // tr),
        in_specs=[pl.BlockSpec((None, tr, cols), lambda l, r: (l, r, 0))],
        out_specs=pl.BlockSpec((None, tr, cols), lambda l, r: (l, r, 0)),
        out_shape=jax.ShapeDtypeStruct(w.shape, BF16),
        compiler_params=_cparams(2),
        name="cast_bf16",
    )(w)


def _prep_win_kernel(w_ref, o_ref):
    o_ref[:, Q_COL:Q_COL + ML_WIDTH] = w_ref[:, 0:ML_WIDTH].astype(BF16)
    o_ref[:, V_COL:G1_COL] = w_ref[:, SRC_V:SRC_G].astype(BF16)
    tail = w_ref[:, SRC_G:]
    ig = tail[:, 0:ML_HEADS]
    fg = tail[:, ML_HEADS:2 * ML_HEADS]
    pad = jnp.zeros((ig.shape[0], LANES - 2 * ML_HEADS), F32)
    o_ref[:, G1_COL:G2_COL] = jnp.concatenate([ig, ig, pad], axis=1).astype(BF16)
    o_ref[:, G2_COL:B_COL] = jnp.concatenate([fg, fg, pad], axis=1).astype(BF16)
    o_ref[:, B_COL:IN_W] = tail[:, 2 * ML_HEADS:].astype(BF16)


def _prep_win_call(a_w_in):
    n_l, _, in_a = a_w_in.shape
    return pl.pallas_call(
        _prep_win_kernel,
        grid=(n_l, D_MODEL // CAST_ROWS),
        in_specs=[pl.BlockSpec((None, CAST_ROWS, in_a), lambda l, r: (l, r, 0))],
        out_specs=pl.BlockSpec((None, CAST_ROWS, IN_W), lambda l, r: (l, r, 0)),
        out_shape=jax.ShapeDtypeStruct((n_l, D_MODEL, IN_W), BF16),
        compiler_params=_cparams(2),
        name="prep_w_in",
    )(a_w_in)


def _mod_row(mod_ref, kind, b):
    return mod_ref[kind, pl.ds(b, 1), :]


def _ffn_prompt_kernel(x_ref, mod_ref, nw_ref, wup_ref, cw_ref, wdn_ref,
                       xo_ref, st_ref, h_scr, act_scr, carry_scr):
    b = pl.program_id(0)
    t = pl.program_id(1)

    @pl.when(t == 0)
    def _():
        carry_scr[...] = jnp.zeros_like(carry_scr)

    x = x_ref[0]
    h_scr[...] = _norm_mod(x, nw_ref[...], _mod_row(mod_ref, 4, b), _mod_row(mod_ref, 3, b)).astype(BF16)
    for j in range(NCH):
        ys = []
        for col in (j * FC, D_FF + j * FC):
            cols = slice(col, col + FC)
            u = _dot(h_scr[...], wup_ref[:, cols])
            ys.append(_conv3_rows(u, carry_scr[:, cols], cw_ref[:, cols]))
            carry_scr[:, cols] = u[TM_FFN - SUBLANES:TM_FFN]
        g = ys[0]
        act_scr[:, j * FC:(j + 1) * FC] = (g * _sigmoid(g) * ys[1]).astype(BF16)
    y = _dot(act_scr[...], wdn_ref[...])
    xo_ref[0] = x + _mod_row(mod_ref, 5, b) * y
    st_ref[0] = carry_scr[...]


def _ffn_prompt_call(x, mod, n_seq_rows, l, nw, wup, cw, wdn):
    B, S, _ = x.shape
    assert S % TM_FFN == 0
    nt = S // TM_FFN
    once = pl.Buffered(1)
    return pl.pallas_call(
        _ffn_prompt_kernel,
        grid=(B, nt),
        in_specs=[
            pl.BlockSpec((1, TM_FFN, D_MODEL), lambda b, t: (b, t, 0)),
            pl.BlockSpec((None, N_MOD, SUBLANES, D_MODEL), lambda b, t: (l, 0, n_seq_rows // SUBLANES, 0)),
            pl.BlockSpec((None, 1, D_MODEL), lambda b, t: (l, 0, 0)),
            pl.BlockSpec((None, D_MODEL, 2 * D_FF), lambda b, t: (l, 0, 0), pipeline_mode=once),
            pl.BlockSpec((None, 3, 2 * D_FF), lambda b, t: (l, 0, 0)),
            pl.BlockSpec((None, D_FF, D_MODEL), lambda b, t: (l, 0, 0), pipeline_mode=once),
        ],
        out_specs=[
            pl.BlockSpec((1, TM_FFN, D_MODEL), lambda b, t: (b, t, 0)),
            pl.BlockSpec((1, SUBLANES, 2 * D_FF), lambda b, t: (b, 0, 0)),
        ],
        out_shape=[
            jax.ShapeDtypeStruct((B, S, D_MODEL), F32),
            jax.ShapeDtypeStruct((B, SUBLANES, 2 * D_FF), F32),
        ],
        scratch_shapes=[
            pltpu.VMEM((TM_FFN, D_MODEL), BF16),
            pltpu.VMEM((TM_FFN, D_FF), BF16),
            pltpu.VMEM((SUBLANES, 2 * D_FF), F32),
        ],
        compiler_params=_cparams(2),
        name="ffn_prompt",
    )(x, mod, nw, wup, cw, wdn)


def _ffn_sample_kernel(x_ref, mod_ref, nw_ref, s_ref, w_ref, c_ref, wdn_ref,
                       xo_ref, so_ref, h_scr, hb_scr, g2_scr, acc_scr, yg_scr):
    j = pl.program_id(1)
    n_seq = s_ref.shape[0]
    n_rows = x_ref.shape[0]
    n_t = n_rows // n_seq

    @pl.when(j == 0)
    def _():
        def modulate(b, carry):
            rows = pl.ds(pl.multiple_of(b * n_t, n_t), n_t)
            h_scr[rows, :] = _norm_mod(x_ref[rows, :], nw_ref[...], _mod_row(mod_ref, 4, b), _mod_row(mod_ref, 3, b))
            g2_scr[rows, :] = jnp.broadcast_to(_mod_row(mod_ref, 5, b), (n_t, D_MODEL))
            return carry

        lax.fori_loop(0, n_seq, modulate, 0, unroll=8)
        hb_scr[...] = h_scr[...].astype(BF16)
        acc_scr[...] = jnp.zeros_like(acc_scr)

    sub = lax.broadcasted_iota(jnp.int32, (n_seq, n_t, FC_S), 1)
    u3 = _dot(hb_scr[...], w_ref[...]).reshape(n_seq, n_t, FC_S)
    cw = c_ref[...]
    p0 = jnp.broadcast_to(s_ref[:, 0:1, :], (n_seq, n_t, FC_S))
    p1 = jnp.broadcast_to(s_ref[:, 1:2, :], (n_seq, n_t, FC_S))
    s1 = jnp.where(sub < 1, p1, pltpu.roll(u3, 1, 1))
    s2 = jnp.where(sub < 1, p0, jnp.where(sub < 2, p1, pltpu.roll(u3, 2, 1)))
    y = (s2 * cw[0:1] + s1 * cw[1:2] + u3 * cw[2:3]).reshape(n_rows, FC_S)
    so_ref[...] = pltpu.roll(u3, 2, 1)[:, 0:2, :]

    @pl.when(j < FS_GATE)
    def _():
        yg_scr[j] = y

    @pl.when(j >= FS_GATE)
    def _():
        g = yg_scr[j - FS_GATE]
        acc_scr[...] += _dot((g * _sigmoid(g) * y).astype(BF16), wdn_ref[...])

    @pl.when(j == 2 * FS_GATE - 1)
    def _():
        xo_ref[...] = x_ref[...] + g2_scr[...] * acc_scr[...]


def _ffn_sample_call(x, mod, l, nw, st, wup, cw, wdn, so_prev):
    n_seq = st.shape[1] // 2
    n_rows = x.shape[0] // 2
    return _stacked_call(
        _ffn_sample_kernel, 7, 1, so_prev, l,
        lambda lead, idx: pl.BlockSpec((lead, n_seq, 2, FC_S), lambda hf, j: (idx, hf, 0, j)),
        grid=(2, 2 * FS_GATE),
        in_specs=[
            pl.BlockSpec((n_rows, D_MODEL), lambda hf, j: (hf, 0)),
            pl.BlockSpec((None, N_MOD, n_seq, D_MODEL), lambda hf, j: (l, 0, hf, 0)),
            pl.BlockSpec((None, 1, D_MODEL), lambda hf, j: (l, 0, 0)),
            pl.BlockSpec((None, n_seq, 2, FC_S), lambda hf, j: (l, hf, 0, j)),
            pl.BlockSpec((None, D_MODEL, FC_S), lambda hf, j: (l, 0, j)),
            pl.BlockSpec((None, 3, FC_S), lambda hf, j: (l, 0, j)),
            pl.BlockSpec((None, FC_S, D_MODEL), lambda hf, j: (l, jnp.maximum(j - FS_GATE, 0), 0)),
        ],
        out_specs=[
            pl.BlockSpec((n_rows, D_MODEL), lambda hf, j: (hf, 0)),
            None,
        ],
        out_shape=[
            jax.ShapeDtypeStruct(x.shape, F32),
            jax.ShapeDtypeStruct((DEPTH, st.shape[1], 2, 2 * D_FF), F32),
        ],
        scratch_shapes=[
            pltpu.VMEM((n_rows, D_MODEL), F32),
            pltpu.VMEM((n_rows, D_MODEL), BF16),
            pltpu.VMEM((n_rows, D_MODEL), F32),
            pltpu.VMEM((n_rows, D_MODEL), F32),
            pltpu.VMEM((FS_GATE, n_rows, FC_S), F32),
        ],
        compiler_params=_cparams(2),
        name="ffn_sample",
    )(x, mod, nw, st, wup, cw, wdn)


def _chunk_consts(seq_len):
    r = lax.broadcasted_iota(jnp.int32, (CHUNK, CHUNK), 0)
    c = lax.broadcasted_iota(jnp.int32, (CHUNK, CHUNK), 1)
    if seq_len >= CHUNK:
        same = r >= 0
    else:
        same = (r // seq_len) == (c // seq_len)
    mask = same & (c <= r)
    lmat = mask.astype(BF16)
    lmat_t = (same & (r <= c)).astype(BF16)
    return mask, lmat, lmat_t, same.astype(BF16)


def _dot_exact01_r(a, m):
    a1 = a.astype(BF16)
    r1 = a - a1.astype(F32)
    a2 = r1.astype(BF16)
    a3 = (r1 - a2.astype(F32)).astype(BF16)
    return _dot(a1, m) + _dot(a2, m) + _dot(a3, m)


def _pick_cols_t(a, sel):
    a1 = a.astype(BF16)
    r1 = a - a1.astype(F32)
    a2 = r1.astype(BF16)
    a3 = (r1 - a2.astype(F32)).astype(BF16)
    return _dot_nt(sel, a1) + _dot_nt(sel, a2) + _dot_nt(sel, a3)


def _seq_max_lanes(x, seq_len):
    n = x.shape[1]
    pos = lax.broadcasted_iota(jnp.int32, x.shape, 1)
    d = 1
    while d < seq_len:
        partner = jnp.where((pos & d) == 0, pltpu.roll(x, n - d, 1), pltpu.roll(x, d, 1))
        x = jnp.maximum(x, partner)
        d *= 2
    return x


def _seq_prefix_max_rows(x, seq_len):
    pos = lax.broadcasted_iota(jnp.int32, x.shape, 0) & (seq_len - 1)
    d = 1
    while d < seq_len:
        x = jnp.where(pos >= d, jnp.maximum(x, pltpu.roll(x, d, 0)), x)
        d *= 2
    return x


def _seq_last_row(x, seq_len):
    n, w = x.shape
    if seq_len >= n:
        return jnp.broadcast_to(x[n - 1:n], x.shape)
    x3 = x.reshape(n // seq_len, seq_len, w)
    return jnp.broadcast_to(x3[:, seq_len - 1:seq_len, :], x3.shape).reshape(n, w)


def _gates_rows_pre(gt, bias_r, lmat_t, tot, seq_len):
    ig = gt[0:SUBLANES] + bias_r[0:SUBLANES]
    lf = _log_sigmoid(gt[SUBLANES:] + bias_r[SUBLANES:])
    b = _dot_exact01_r(lf, lmat_t)
    bl = _dot_exact01_r(lf, tot)
    v = ig - b
    return bl, v, _seq_max_lanes(v, seq_len)


def _gates_rows_post(pre, mp_r):
    bl, v, vm = pre
    mn = bl + jnp.maximum(mp_r, vm)
    return jnp.exp(bl + v - mn), jnp.exp(bl + mp_r - mn), mn


def _gates_rows(gt, bias_r, mp_r, lmat_t, tot, seq_len):
    pre = _gates_rows_pre(gt, bias_r, lmat_t, tot, seq_len)
    return (pre[1],) + _gates_rows_post(pre, mp_r)


def _gates_cols_pre(g1, g2, bias_c, lmat, seq_len):
    lane = lax.broadcasted_iota(jnp.int32, g1.shape, 1)
    ig = g1 + bias_c[0:1]
    lf = jnp.where(lane < 2 * ML_HEADS, _log_sigmoid(g2 + bias_c[1:2]), 0.0)
    b = _dot_exact01(lmat, lf)
    return b, _seq_prefix_max_rows(ig - b, seq_len)


def _gates_cols_post(b, cm, mp_c):
    g = b + mp_c
    mt = jnp.maximum(b + cm, g)
    return b - mt, jnp.exp(g - mt), jnp.exp(-mt)


def _gates_cols(g1, g2, bias_c, mp_c, lmat, seq_len):
    b, cm = _gates_cols_pre(g1, g2, bias_c, lmat, seq_len)
    return (b, cm) + _gates_cols_post(b, cm, mp_c)


def _outer_sum_lhs(u):
    lane = lax.broadcasted_iota(jnp.int32, u.shape, 1)
    hi = u.astype(BF16).astype(F32)
    lo = u - hi
    return jnp.where(lane < ML_HEADS, hi, jnp.where(lane < 2 * ML_HEADS, lo,
                     jnp.where(lane < 4 * ML_HEADS, 1.0, 0.0))).astype(BF16)


def _outer_sum_rhs(v_r, hd):
    row = lax.broadcasted_iota(jnp.int32, v_r.shape, 0)
    hi = v_r.astype(BF16).astype(F32)
    lo = v_r - hi
    pick = (row == hd) | (row == ML_HEADS + hd)
    top = jnp.where(pick, 1.0, 0.0)
    bot = jnp.where(row == hd, hi, jnp.where(row == ML_HEADS + hd, lo, 0.0))
    r16 = jnp.concatenate([top, bot], axis=0).astype(BF16)
    return jnp.concatenate([r16, jnp.zeros((LANES - 2 * SUBLANES, v_r.shape[1]), BF16)], axis=0)


def _decayed_scores(q, kt, lhsc, v_r, hd, mask):
    e = _dot(lhsc, _outer_sum_rhs(v_r, hd))
    return _dot(q.astype(BF16), kt.astype(BF16)) * jnp.where(mask, jnp.exp(e), 0.0)


def _conv3_rows(cx, prev8, cw):
    n = cx.shape[1]
    row = lax.broadcasted_iota(jnp.int32, (SUBLANES, n), 0)
    s1 = pltpu.roll(cx, 1, 0)
    s2 = pltpu.roll(cx, 2, 0)
    f1 = jnp.where(row < 1, pltpu.roll(prev8, 1, 0), s1[0:SUBLANES])
    f2 = jnp.where(row < 2, pltpu.roll(prev8, 2, 0), s2[0:SUBLANES])
    s1 = jnp.concatenate([f1, s1[SUBLANES:]], axis=0)
    s2 = jnp.concatenate([f2, s2[SUBLANES:]], axis=0)
    return s2 * cw[0:1] + s1 * cw[1:2] + cx * cw[2:3]


def _mlstm_out_norm(hm, zo, onw):
    ms = jnp.mean(hm * hm, axis=-1, keepdims=True)
    return hm * lax.rsqrt(ms + EPS) * onw * _sigmoid(zo)


def _even_tail(z_scr, hm_scr, cat_scr, onw_ref):
    for hd in range(ML_HEADS):
        col = slice(hd * ML_DV, (hd + 1) * ML_DV)
        zo = z_scr[:, O_COL + hd * ML_DV:O_COL + (hd + 1) * ML_DV]
        cat_scr[:, col] = _mlstm_out_norm(hm_scr[:, col], zo, onw_ref[:, col]).astype(BF16)


def _mix_even_prompt_kernel(x_ref, mod_ref, nw_ref, win_ref, wkg_ref, bifc_ref, bifr_ref, onw_ref, cw_ref,
                            wout_ref, xo_ref, co_ref, no_ref, mo_ref, sco_ref,
                            z_scr, zt_scr, hm_scr, cat_scr, cn_scr, mrow_scr, mlane_scr, cc_scr):
    t = pl.program_id(1)

    @pl.when(t == 0)
    def _():
        cn_scr[...] = jnp.zeros_like(cn_scr)
        mrow_scr[...] = jnp.zeros_like(mrow_scr)
        mlane_scr[...] = jnp.zeros_like(mlane_scr)
        cc_scr[...] = jnp.zeros_like(cc_scr)

    x = x_ref[0]
    bi = pl.program_id(0)
    h = _norm_mod(x, nw_ref[...], _mod_row(mod_ref, 1, bi), _mod_row(mod_ref, 0, bi)).astype(BF16)
    z_scr[...] = _dot(h, win_ref[...])
    zt_scr[...] = _dot_nt(wkg_ref[...], h)

    mask, lmat, lmat_t, tot = _chunk_consts(CHUNK)
    scale = ML_DK ** -0.5
    ones_v = jnp.ones((CHUNK, ML_DV), BF16)

    n_ch = TM // CHUNK
    chunk_rows = [slice(c * CHUNK, (c + 1) * CHUNK) for c in range(n_ch)]
    pre_r = [_gates_rows_pre(zt_scr[ML_WIDTH:KG_ROWS, rows], bifr_ref[...], lmat_t, tot, CHUNK)
             for rows in chunk_rows]
    pre_c = [_gates_cols_pre(z_scr[rows, G1_COL:G1_COL + LANES], z_scr[rows, G2_COL:G2_COL + LANES],
                             bifc_ref[...], lmat, CHUNK) for rows in chunk_rows]
    mp_r = mrow_scr[...]
    mp_c = mlane_scr[0:1, :]
    post_r, post_c = [], []
    for c in range(n_ch):
        post_r.append(_gates_rows_post(pre_r[c], mp_r))
        mp_r = post_r[c][2]
        b, cm = pre_c[c]
        post_c.append(_gates_cols_post(b, cm, mp_c))
        mp_c = b[CHUNK - 1:CHUNK] + jnp.maximum(mp_c, cm[CHUNK - 1:CHUNK])
    mrow_scr[...] = mp_r
    mlane_scr[...] = jnp.broadcast_to(mp_c, (SUBLANES, LANES))

    intra, upd, wgq = {}, {}, {}
    for c, rows in enumerate(chunk_rows):
        u, wg, _ = post_c[c]
        lhsc = _outer_sum_lhs(u)
        for hd in range(ML_HEADS):
            q = z_scr[rows, Q_COL + hd * ML_DK:Q_COL + (hd + 1) * ML_DK]
            v = z_scr[rows, V_COL + hd * ML_DV:V_COL + (hd + 1) * ML_DV]
            kt = zt_scr[hd * ML_DK:(hd + 1) * ML_DK, rows] * scale
            s = _decayed_scores(q, kt, lhsc, pre_r[c][1], hd, mask)
            v1 = jnp.concatenate([v.astype(BF16), ones_v], axis=1)
            intra[c, hd] = _dot(s.astype(BF16), v1)
            upd[c, hd] = _dot((kt * post_r[c][0][hd:hd + 1, :]).astype(BF16), v1)
            wgq[c, hd] = (q * wg[:, hd:hd + 1]).astype(BF16)

    for hd in range(ML_HEADS):
        cn = cn_scr[hd]
        for c, rows in enumerate(chunk_rows):
            out = intra[c, hd] + _dot(wgq[c, hd], cn.astype(BF16))
            r = 1.0 / jnp.maximum(jnp.abs(out[:, ML_DV:]), post_c[c][2][:, hd:hd + 1])
            hm_scr[rows, hd * ML_DV:(hd + 1) * ML_DV] = out[:, :ML_DV] * r
            wc = post_r[c][1][hd:hd + 1, :]
            cn = jnp.concatenate([wc, wc], axis=1) * cn + upd[c, hd]
        cn_scr[hd] = cn

    _even_tail(z_scr, hm_scr, cat_scr, onw_ref)
    cx = z_scr[:, C_COL:C_COL + SC_WIDTH] * z_scr[:, X_COL:X_COL + SC_WIDTH]
    u = _conv3_rows(cx, cc_scr[...], cw_ref[...])
    cc_scr[...] = cx[TM - SUBLANES:TM]
    cat_scr[:, ML_WIDTH:] = (z_scr[:, B_COL:B_COL + SC_WIDTH] * u).astype(BF16)

    y = _dot(cat_scr[...], wout_ref[...])
    xo_ref[0] = x + _mod_row(mod_ref, 2, bi) * y
    for hd in range(ML_HEADS):
        co_ref[0, hd] = cn_scr[hd, :, 0:ML_DV]
        no_ref[0, hd] = cn_scr[hd, :, ML_DV:]
    mo_ref[0] = mrow_scr[...]
    sco_ref[0] = cc_scr[...]


def _even_weight_specs(i, idx):
    once = pl.Buffered(1)
    return [
        pl.BlockSpec((None, D_MODEL, IN_W), idx, pipeline_mode=once),
        pl.BlockSpec((None, KG_ROWS, D_MODEL), idx, pipeline_mode=once),
        pl.BlockSpec((None, 2, LANES), idx),
        pl.BlockSpec((None, 2 * SUBLANES, LANES), idx),
        pl.BlockSpec((None, 1, ML_WIDTH), idx),
        pl.BlockSpec((None, 3, SC_WIDTH), idx),
        pl.BlockSpec((None, ML_WIDTH + SC_WIDTH, D_MODEL), idx, pipeline_mode=once),
    ]


def _mix_even_prompt_call(x, mod, n_seq_rows, l, nw, win, wkg, bifc, bifr, onw, cw, wout):
    B, S, _ = x.shape
    nt = S // TM
    i = l // 2
    return pl.pallas_call(
        _mix_even_prompt_kernel,
        grid=(B, nt),
        in_specs=[
            pl.BlockSpec((1, TM, D_MODEL), lambda b, t: (b, t, 0)),
            pl.BlockSpec((None, N_MOD, SUBLANES, D_MODEL), lambda b, t: (l, 0, n_seq_rows // SUBLANES, 0)),
            pl.BlockSpec((None, 1, D_MODEL), lambda b, t: (l, 0, 0)),
        ] + _even_weight_specs(i, lambda b, t: (i, 0, 0)),
        out_specs=[
            pl.BlockSpec((1, TM, D_MODEL), lambda b, t: (b, t, 0)),
            pl.BlockSpec((1, ML_HEADS, ML_DK, ML_DV), lambda b, t: (b, 0, 0, 0)),
            pl.BlockSpec((1, ML_HEADS, ML_DK, LANES), lambda b, t: (b, 0, 0, 0)),
            pl.BlockSpec((1, SUBLANES, LANES), lambda b, t: (b, 0, 0)),
            pl.BlockSpec((1, SUBLANES, SC_WIDTH), lambda b, t: (b, 0, 0)),
        ],
        out_shape=[
            jax.ShapeDtypeStruct((B, S, D_MODEL), F32),
            jax.ShapeDtypeStruct((B, ML_HEADS, ML_DK, ML_DV), F32),
            jax.ShapeDtypeStruct((B, ML_HEADS, ML_DK, LANES), F32),
            jax.ShapeDtypeStruct((B, SUBLANES, LANES), F32),
            jax.ShapeDtypeStruct((B, SUBLANES, SC_WIDTH), F32),
        ],
        scratch_shapes=[
            pltpu.VMEM((TM, IN_W), F32),
            pltpu.VMEM((KG_ROWS, TM), F32),
            pltpu.VMEM((TM, ML_WIDTH), F32),
            pltpu.VMEM((TM, ML_WIDTH + SC_WIDTH), BF16),
            pltpu.VMEM((ML_HEADS, ML_DK, ML_DV + LANES), F32),
            pltpu.VMEM((SUBLANES, LANES), F32),
            pltpu.VMEM((SUBLANES, LANES), F32),
            pltpu.VMEM((SUBLANES, SC_WIDTH), F32),
        ],
        compiler_params=_cparams(2),
        name="mix_even_prompt",
    )(x, mod, nw, win, wkg, bifc, bifr, onw, cw, wout)


def _mix_even_sample_kernel(x_ref, mod_ref, nw_ref, win_ref, wkg_ref, bifc_ref, bifr_ref, onw_ref, cw_ref,
                            wout_ref, c_ref, nt_ref, ntt_ref, mcol_ref, mrow_ref, sc_ref,
                            xo_ref, co_ref, no_ref, mo_ref, sco_ref,
                            h_scr, z_scr, zt_scr, hm_scr, cat_scr, g1_scr,
                            intra_scr, dpart_scr, pbe_scr, wgq_scr, kwt_scr, vb_scr, wcb_scr, inter_scr):
    n_tok = x_ref.shape[0]
    seq_len = n_tok // SEQ_BLK

    def modulate(b, carry):
        rows = pl.ds(pl.multiple_of(b * seq_len, seq_len), seq_len)
        h_scr[rows, :] = _norm_mod(x_ref[rows, :], nw_ref[...], _mod_row(mod_ref, 1, b), _mod_row(mod_ref, 0, b))
        g1_scr[rows, :] = jnp.broadcast_to(_mod_row(mod_ref, 2, b), (seq_len, D_MODEL))
        return carry

    lax.fori_loop(0, SEQ_BLK, modulate, 0, unroll=4)
    hb = h_scr[...].astype(BF16)
    z_scr[...] = _dot(hb, win_ref[...])
    zt_scr[...] = _dot_nt(wkg_ref[...], hb)

    mask, lmat, lmat_t, tot = _chunk_consts(seq_len)
    scale = ML_DK ** -0.5
    ones_v = jnp.ones((CHUNK, ML_DV), BF16)
    mp_r = mrow_ref[...]
    v_r, ws_r, wc_r, mn_r = _gates_rows(zt_scr[ML_WIDTH:KG_ROWS, :], bifr_ref[...], mp_r, lmat_t, tot, seq_len)
    mo_ref[...] = mn_r
    mp_c = mcol_ref[...]
    b, cm, u, wg, pbe = _gates_cols(z_scr[:, G1_COL:G1_COL + LANES], z_scr[:, G2_COL:G2_COL + LANES],
                                    bifc_ref[...], mp_c, lmat, seq_len)
    wc_c = jnp.exp(mp_c - jnp.maximum(mp_c, _seq_last_row(cm, seq_len)))
    lhsc = _outer_sum_lhs(u)
    first_tok = (lax.broadcasted_iota(jnp.int32, (SEQ_BLK, n_tok), 0) * seq_len
                 == lax.broadcasted_iota(jnp.int32, (SEQ_BLK, n_tok), 1)).astype(BF16)

    for hd in range(ML_HEADS):
        q = z_scr[:, Q_COL + hd * ML_DK:Q_COL + (hd + 1) * ML_DK]
        v = z_scr[:, V_COL + hd * ML_DV:V_COL + (hd + 1) * ML_DV]
        kt = zt_scr[hd * ML_DK:(hd + 1) * ML_DK, :] * scale
        s = _decayed_scores(q, kt, lhsc, v_r, hd, mask)
        vb = v.astype(BF16)
        out = _dot(s.astype(BF16), jnp.concatenate([vb, ones_v], axis=1))
        wg_h = wg[:, hd:hd + 1]
        qn = jnp.sum(q * nt_ref[hd], axis=-1, keepdims=True)
        intra_scr[hd] = out[:, :ML_DV]
        dpart_scr[hd] = out[:, ML_DV:] + wg_h * qn
        pbe_scr[hd] = jnp.broadcast_to(pbe[:, hd:hd + 1], (n_tok, LANES))
        wgq_scr[hd] = q * wg_h
        kwt = kt * ws_r[hd:hd + 1, :]
        kwt_scr[hd] = kwt
        vb_scr[hd] = vb
        wcb_scr[hd] = jnp.broadcast_to(wc_c[:, hd:hd + 1], (n_tok, LANES))
        n_new_t = wc_r[hd:hd + 1, :] * ntt_ref[hd] + _dot_exact01_r(kwt, tot)
        no_ref[hd] = _pick_cols_t(n_new_t, first_tok)

    lane_i = lax.broadcasted_iota(jnp.int32, (ML_DK, n_tok), 1)

    def per_seq(bq, carry):
        r0 = pl.multiple_of(bq * seq_len, seq_len)
        rows = pl.ds(r0, seq_len)
        sel = (lane_i >= r0) & (lane_i < r0 + seq_len)
        for hd in range(ML_HEADS):
            c_prev = c_ref[bq, hd]
            inter_scr[hd, rows, :] = _dot(wgq_scr[hd, rows, :].astype(BF16), c_prev.astype(BF16))
            kw_b = jnp.where(sel, kwt_scr[hd], 0.0).astype(BF16)
            co_ref[bq, hd] = wcb_scr[hd, pl.ds(r0, 1), :] * c_prev + _dot(kw_b, vb_scr[hd])
        return carry

    lax.fori_loop(0, SEQ_BLK, per_seq, 0, unroll=4)

    for hd in range(ML_HEADS):
        num = inter_scr[hd] + intra_scr[hd]
        hm_scr[:, hd * ML_DV:(hd + 1) * ML_DV] = num * (
            1.0 / jnp.maximum(jnp.abs(dpart_scr[hd]), pbe_scr[hd]))

    _even_tail(z_scr, hm_scr, cat_scr, onw_ref)
    cx = z_scr[:, C_COL:C_COL + SC_WIDTH] * z_scr[:, X_COL:X_COL + SC_WIDTH]
    sub = lax.broadcasted_iota(jnp.int32, (n_tok, SC_WIDTH), 0) % seq_len
    p1 = sc_ref[...]
    s1 = jnp.where(sub < 1, pltpu.roll(p1, n_tok - 1, 0), pltpu.roll(cx, 1, 0))
    s2 = jnp.where(sub < 2, p1, pltpu.roll(cx, 2, 0))
    cw = cw_ref[...]
    u = s2 * cw[0:1] + s1 * cw[1:2] + cx * cw[2:3]
    sco_ref[...] = pltpu.roll(cx, n_tok - (seq_len - 2), 0)
    cat_scr[:, ML_WIDTH:] = (z_scr[:, B_COL:B_COL + SC_WIDTH] * u).astype(BF16)

    y = _dot(cat_scr[...], wout_ref[...])
    xo_ref[...] = x_ref[...] + g1_scr[...] * y


def _mix_even_sample_call(x, mod, l, nw, win, wkg, bifc, bifr, onw, cw, wout, c0, n_tok, n_tok_t, m_col, m_row,
                          sc_pad, co_prev):
    n_rows = x.shape[0]
    n_seq = c0.shape[1]
    seq_len = n_rows // n_seq
    blk = SEQ_BLK * seq_len
    nb = n_seq // SEQ_BLK
    li = l // 2
    head_blk = (ML_HEADS, blk, LANES)
    return _stacked_call(
        _mix_even_sample_kernel, 16, 1, co_prev, li,
        lambda lead, idx: pl.BlockSpec((lead, SEQ_BLK, ML_HEADS, ML_DK, ML_DV), lambda i: (idx, i, 0, 0, 0)),
        grid=(nb,),
        in_specs=[
            pl.BlockSpec((blk, D_MODEL), lambda i: (i, 0)),
            pl.BlockSpec((None, N_MOD, SEQ_BLK, D_MODEL), lambda i: (l, 0, i, 0)),
            pl.BlockSpec((None, 1, D_MODEL), lambda i: (l, 0, 0)),
        ] + _even_weight_specs(li, lambda i: (li, 0, 0)) + [
            pl.BlockSpec((None, SEQ_BLK, ML_HEADS, ML_DK, ML_DV), lambda i: (li, i, 0, 0, 0)),
            pl.BlockSpec((ML_HEADS, blk, ML_DK), lambda i: (0, i, 0)),
            pl.BlockSpec((ML_HEADS, ML_DK, blk), lambda i: (0, 0, i)),
            pl.BlockSpec((blk, LANES), lambda i: (i, 0)),
            pl.BlockSpec((SUBLANES, blk), lambda i: (0, i)),
            pl.BlockSpec((blk, SC_WIDTH), lambda i: (i, 0)),
        ],
        out_specs=[
            pl.BlockSpec((blk, D_MODEL), lambda i: (i, 0)),
            None,
            pl.BlockSpec((ML_HEADS, SEQ_BLK, ML_DK), lambda i: (0, i, 0)),
            pl.BlockSpec((SUBLANES, blk), lambda i: (0, i)),
            pl.BlockSpec((blk, SC_WIDTH), lambda i: (i, 0)),
        ],
        out_shape=[
            jax.ShapeDtypeStruct((n_rows, D_MODEL), F32),
            jax.ShapeDtypeStruct(c0.shape, F32),
            jax.ShapeDtypeStruct((ML_HEADS, n_seq, ML_DK), F32),
            jax.ShapeDtypeStruct((SUBLANES, n_rows), F32),
            jax.ShapeDtypeStruct((n_rows, SC_WIDTH), F32),
        ],
        scratch_shapes=[
            pltpu.VMEM((blk, D_MODEL), F32),
            pltpu.VMEM((blk, IN_W), F32),
            pltpu.VMEM((KG_ROWS, blk), F32),
            pltpu.VMEM((blk, ML_WIDTH), F32),
            pltpu.VMEM((blk, ML_WIDTH + SC_WIDTH), BF16),
            pltpu.VMEM((blk, D_MODEL), F32),
            pltpu.VMEM(head_blk, F32),
            pltpu.VMEM(head_blk, F32),
            pltpu.VMEM(head_blk, F32),
            pltpu.VMEM(head_blk, F32),
            pltpu.VMEM((ML_HEADS, ML_DK, blk), F32),
            pltpu.VMEM(head_blk, BF16),
            pltpu.VMEM(head_blk, F32),
            pltpu.VMEM(head_blk, F32),
        ],
        compiler_params=_cparams(1),
        name="mix_even_sample",
    )(x, mod, nw, win, wkg, bifc, bifr, onw, cw, wout, c0, n_tok, n_tok_t, m_col, m_row, sc_pad)


def _split2(x):
    hi = x.astype(BF16)
    lo = (x - hi.astype(F32)).astype(BF16)
    return jnp.concatenate([hi, lo], axis=1)


def _head_lane_mats():
    r = lax.broadcasted_iota(jnp.int32, (2 * LANES, LANES), 0) % LANES
    c = lax.broadcasted_iota(jnp.int32, (2 * LANES, LANES), 1)
    hsum = ((r // HEAD_DIM) == (c // HEAD_DIM)).astype(BF16)
    half = HEAD_DIM // 2
    src = jnp.where((c % HEAD_DIM) < half, c + half, c - half)
    return hsum, (r == src).astype(BF16)


def _q_lane_mat():
    r = lax.broadcasted_iota(jnp.int32, (2 * LANES, 2 * LANES), 0)
    c = lax.broadcasted_iota(jnp.int32, (2 * LANES, 2 * LANES), 1)
    half = HEAD_DIM // 2
    src = jnp.where((c % HEAD_DIM) < half, c + half, c - half)
    top = (r < LANES) & (c < LANES) & ((r // HEAD_DIM) == (c // HEAD_DIM))
    return (top | ((r >= LANES) & (c >= LANES) & (r == src))).astype(BF16)


def _q_norm_rope(xb, gw, cos, sin, qmat):
    zg = xb * gw
    out = _dot(jnp.concatenate([(xb * xb).astype(BF16), zg.astype(BF16)], axis=1), qmat)
    ms = out[:, :LANES] * (1.0 / HEAD_DIM)
    return lax.rsqrt(ms + EPS) * (zg * cos + out[:, LANES:] * sin)


def _qk_norm_rope(xb, gw, cos, sin, hsum, rot_mat):
    ms = _dot(_split2(xb * xb), hsum) * (1.0 / HEAD_DIM)
    zg = xb * gw
    rot = _dot(_split2(zg), rot_mat)
    return lax.rsqrt(ms + EPS) * (zg * cos + rot * sin)


def _sink_rows(sink8, reps):
    return jnp.concatenate(
        [jnp.broadcast_to(sink8[r:r + 1, :], (reps, LANES)) for r in range(SUBLANES)], axis=0)


def _sink_col(sink8, reps):
    parts = [jnp.broadcast_to(sink8[r:r + 1, :], (reps, LANES)) for r in range(SUBLANES)]
    return jnp.concatenate(parts, axis=0)[:, 0:1]


def _attn_prompt_kernel(x_ref, mod_ref, nw_ref, wqkv_ref, qnw_ref, knw_ref, cos_ref, sin_ref,
                        sink_ref, wout_ref, xo_ref, ko_ref, vo_ref,
                        z_scr, qm_scr, k_scr, v_scr, o_scr):
    t = pl.program_id(1)
    n_qb = TM // WINDOW

    @pl.when(t == 0)
    def _():
        k_scr[0:WINDOW, :] = jnp.zeros((WINDOW, KV_W), BF16)
        v_scr[0:WINDOW, :] = jnp.zeros((WINDOW, KV_W), BF16)

    x = x_ref[0]
    bi = pl.program_id(0)
    h = _norm_mod(x, nw_ref[...], _mod_row(mod_ref, 1, bi), _mod_row(mod_ref, 0, bi)).astype(BF16)
    z_scr[...] = _dot(h, wqkv_ref[...])
    cos = cos_ref[...]
    sin = sin_ref[...]
    hmats = _head_lane_mats()
    qmat = _q_lane_mat()
    half0 = lax.broadcasted_iota(jnp.int32, (TM, LANES), 1) < HEAD_DIM
    qscale = HEAD_DIM ** -0.5
    for jb in range(Q_W // LANES):
        y = _q_norm_rope(z_scr[:, jb * LANES:(jb + 1) * LANES], qnw_ref[...], cos, sin, qmat) * qscale
        qm_scr[2 * jb] = jnp.where(half0, y, 0.0).astype(BF16)
        qm_scr[2 * jb + 1] = jnp.where(half0, 0.0, y).astype(BF16)
    for p in range(KV_W // LANES):
        kf = _qk_norm_rope(z_scr[:, Q_W + p * LANES:Q_W + (p + 1) * LANES], knw_ref[...], cos, sin, *hmats)
        ko_ref[0, :, p * LANES:(p + 1) * LANES] = kf[TM - WINDOW:TM]
        k_scr[WINDOW:WINDOW + TM, p * LANES:(p + 1) * LANES] = kf.astype(BF16)
    vf = z_scr[:, Q_W + KV_W:Q_W + 2 * KV_W]
    vo_ref[0] = vf[TM - WINDOW:TM]
    v_scr[WINDOW:WINDOW + TM, :] = vf.astype(BF16)

    r = lax.broadcasted_iota(jnp.int32, (8 * WINDOW, 2 * WINDOW), 0) % WINDOW
    c = lax.broadcasted_iota(jnp.int32, (8 * WINDOW, 2 * WINDOW), 1)
    valid = ((c < WINDOW) & (c > r)) | ((c >= WINDOW) & ((c - WINDOW) <= r))
    first_lim = jnp.where(t == 0, WINDOW, 0)
    half0q = lax.broadcasted_iota(jnp.int32, (4 * WINDOW, LANES), 1) < HEAD_DIM
    ones_kv = jnp.ones((2 * WINDOW, LANES), BF16)
    for qb in range(n_qb):
        rows = slice(qb * WINDOW, (qb + 1) * WINDOW)
        krows = slice(qb * WINDOW, (qb + 2) * WINDOW)
        vmask = (valid & (c >= first_lim)) if qb == 0 else valid
        for p in range(KV_W // LANES):
            kb = k_scr[krows, p * LANES:(p + 1) * LANES]
            vb = v_scr[krows, p * LANES:(p + 1) * LANES]
            qs = jnp.concatenate([qm_scr[2 * (4 * p + i) + e, rows, :] for e in range(2) for i in range(4)],
                                 axis=0)
            s = jnp.where(vmask, _dot_nt(qs, kb), -jnp.inf)
            sk = _sink_rows(sink_ref[p], WINDOW)
            mx = jnp.maximum(jnp.max(s, axis=-1, keepdims=True), sk)
            pr = jnp.exp(s - jnp.concatenate([mx, mx], axis=1))
            o2 = _dot(pr.astype(BF16), jnp.concatenate([vb, ones_kv], axis=1))
            den = o2[:, LANES:] + jnp.exp(sk - mx)
            o = o2[:, :LANES] * (1.0 / den)
            merged = jnp.where(half0q, o[0:4 * WINDOW], o[4 * WINDOW:])
            for i in range(4):
                o_scr[rows, (4 * p + i) * LANES:(4 * p + i + 1) * LANES] = (
                    merged[i * WINDOW:(i + 1) * WINDOW].astype(BF16))

    y = _dot(o_scr[...], wout_ref[...])
    xo_ref[0] = x + _mod_row(mod_ref, 2, bi) * y
    k_scr[0:WINDOW, :] = k_scr[TM:TM + WINDOW, :]
    v_scr[0:WINDOW, :] = v_scr[TM:TM + WINDOW, :]


def _attn_prompt_call(x, mod, n_seq_rows, l, nw, wqkv, qnw, knw, cos, sin, sink, wout):
    B, S, _ = x.shape
    nt = S // TM
    const2 = lambda b, t: (0, 0)
    return pl.pallas_call(
        _attn_prompt_kernel,
        grid=(B, nt),
        in_specs=[
            pl.BlockSpec((1, TM, D_MODEL), lambda b, t: (b, t, 0)),
            pl.BlockSpec((None, N_MOD, SUBLANES, D_MODEL), lambda b, t: (l, 0, n_seq_rows // SUBLANES, 0)),
            pl.BlockSpec((None, 1, D_MODEL), lambda b, t: (l, 0, 0)),
            pl.BlockSpec((D_MODEL, Q_W + 2 * KV_W), const2),
            pl.BlockSpec((1, LANES), const2),
            pl.BlockSpec((1, LANES), const2),
            pl.BlockSpec((TM, LANES), lambda b, t: (t, 0)),
            pl.BlockSpec((TM, LANES), lambda b, t: (t, 0)),
            pl.BlockSpec((2, SUBLANES, LANES), lambda b, t: (0, 0, 0)),
            pl.BlockSpec((Q_W, D_MODEL), const2),
        ],
        out_specs=[
            pl.BlockSpec((1, TM, D_MODEL), lambda b, t: (b, t, 0)),
            pl.BlockSpec((1, WINDOW, KV_W), lambda b, t: (b, 0, 0)),
            pl.BlockSpec((1, WINDOW, KV_W), lambda b, t: (b, 0, 0)),
        ],
        out_shape=[
            jax.ShapeDtypeStruct((B, S, D_MODEL), F32),
            jax.ShapeDtypeStruct((B, WINDOW, KV_W), F32),
            jax.ShapeDtypeStruct((B, WINDOW, KV_W), F32),
        ],
        scratch_shapes=[
            pltpu.VMEM((TM, Q_W + 2 * KV_W), F32),
            pltpu.VMEM((2 * Q_W // LANES, TM, LANES), BF16),
            pltpu.VMEM((TM + WINDOW, KV_W), BF16),
            pltpu.VMEM((TM + WINDOW, KV_W), BF16),
            pltpu.VMEM((TM, Q_W), BF16),
        ],
        compiler_params=_cparams(2),
        name="attn_prompt",
    )(x, mod, nw, wqkv, qnw, knw, cos, sin, sink, wout)


def _attn_sample_kernel(x_ref, mod_ref, nw_ref, wqkv_ref, qnw_ref, knw_ref, cos_ref, sin_ref,
                        sink_ref, wout_ref, kc_ref, vc_ref,
                        xo_ref, kco_ref, vco_ref,
                        h_scr, g1_scr, z_scr, qm_scr, kn_scr, o_scr):
    n_tok = x_ref.shape[0]
    seq_len = n_tok // SEQ_BLK
    win = kc_ref.shape[1]

    def modulate(b, carry):
        rows = pl.ds(pl.multiple_of(b * seq_len, seq_len), seq_len)
        h_scr[rows, :] = _norm_mod(x_ref[rows, :], nw_ref[...], _mod_row(mod_ref, 1, b), _mod_row(mod_ref, 0, b))
        g1_scr[rows, :] = jnp.broadcast_to(_mod_row(mod_ref, 2, b), (seq_len, D_MODEL))
        return carry

    lax.fori_loop(0, SEQ_BLK, modulate, 0, unroll=4)
    z_scr[...] = _dot(h_scr[...].astype(BF16), wqkv_ref[...])
    cos = cos_ref[...]
    sin = sin_ref[...]
    hmats = _head_lane_mats()
    half0 =lax.broadcasted_iota(jnp.int32, (n_tok, LANES), 1) < HEAD_DIM
    qscale = HEAD_DIM ** -0.5
    for jb in range(Q_W // LANES):
        y = _qk_norm_rope(z_scr[:, jb * LANES:(jb + 1) * LANES], qnw_ref[...], cos, sin, *hmats) * qscale
        qm_scr[2 * jb] = jnp.where(half0, y, 0.0)
        qm_scr[2 * jb + 1] = jnp.where(half0, 0.0, y)
    for p in range(KV_W // LANES):
        kn_scr[:, p * LANES:(p + 1) * LANES] = _qk_norm_rope(
            z_scr[:, Q_W + p * LANES:Q_W + (p + 1) * LANES], knw_ref[...], cos, sin, *hmats)

    n_q = 8 * seq_len
    tq = lax.broadcasted_iota(jnp.int32, (SEQ_BLK, n_q, 2 * win), 1) % seq_len
    cc = lax.broadcasted_iota(jnp.int32, (SEQ_BLK, n_q, 2 * win), 2)
    valid = ((cc < win) & (cc > tq)) | ((cc >= 2 * win - seq_len) & ((cc - (2 * win - seq_len)) <= tq))
    half0q = lax.broadcasted_iota(jnp.int32, (SEQ_BLK, n_q // 2, LANES), 2) < HEAD_DIM
    ones_kv = jnp.ones((SEQ_BLK, 2 * win, LANES), BF16)

    kc = kc_ref[...]
    vc = vc_ref[...]
    knew = jnp.concatenate([kc[:, seq_len:], kn_scr[...].reshape(SEQ_BLK, seq_len, KV_W)], axis=1)
    vnew = jnp.concatenate(
        [vc[:, seq_len:], z_scr[:, Q_W + KV_W:Q_W + 2 * KV_W].reshape(SEQ_BLK, seq_len, KV_W)], axis=1)
    kco_ref[...] = knew
    vco_ref[...] = vnew
    for p in range(KV_W // LANES):
        lanes = slice(p * LANES, (p + 1) * LANES)
        qs = jnp.concatenate([qm_scr[2 * (4 * p + i) + e].reshape(SEQ_BLK, seq_len, LANES)
                              for e in range(2) for i in range(4)], axis=1).astype(BF16)
        kk = jnp.concatenate([kc[:, :, lanes], knew[:, :, lanes]], axis=1).astype(BF16)
        vv = jnp.concatenate([vc[:, :, lanes], vnew[:, :, lanes]], axis=1).astype(BF16)
        s = jnp.einsum("bqd,bkd->bqk", qs, kk, preferred_element_type=F32)
        s = jnp.where(valid, s, -jnp.inf)
        sk = _sink_rows(sink_ref[p], seq_len)[None]
        mx = jnp.maximum(jnp.max(s, axis=-1, keepdims=True), sk)
        pr = jnp.exp(s - jnp.concatenate([mx, mx], axis=-1))
        o2 = jnp.einsum("bqk,bkd->bqd", pr.astype(BF16), jnp.concatenate([vv, ones_kv], axis=-1),
                        preferred_element_type=F32)
        o = o2[:, :, :LANES] * (1.0 / (o2[:, :, LANES:] + jnp.exp(sk - mx)))
        merged = jnp.where(half0q, o[:, 0:n_q // 2], o[:, n_q // 2:])
        for i in range(4):
            o_scr[:, (4 * p + i) * LANES:(4 * p + i + 1) * LANES] = (
                merged[:, i * seq_len:(i + 1) * seq_len].reshape(n_tok, LANES))

    y = _dot(o_scr[...].astype(BF16), wout_ref[...])
    xo_ref[...] = x_ref[...] + g1_scr[...] * y


def _attn_sample_call(x, mod, l, nw, wqkv, qnw, knw, cos, sin, sink, wout, kc, vc):
    n_rows = x.shape[0]
    _, n_seq, win, _ = kc.shape
    lj = l // 2
    seq_len = n_rows // n_seq
    blk = SEQ_BLK * seq_len
    nb = n_seq // SEQ_BLK
    const2 = lambda i: (0, 0)
    return pl.pallas_call(
        _attn_sample_kernel,
        input_output_aliases={10: 1, 11: 2},
        grid=(nb,),
        in_specs=[
            pl.BlockSpec((blk, D_MODEL), lambda i: (i, 0)),
            pl.BlockSpec((None, N_MOD, SEQ_BLK, D_MODEL), lambda i: (l, 0, i, 0)),
            pl.BlockSpec((None, 1, D_MODEL), lambda i: (l, 0, 0)),
            pl.BlockSpec((D_MODEL, Q_W + 2 * KV_W), const2),
            pl.BlockSpec((1, LANES), const2),
            pl.BlockSpec((1, LANES), const2),
            pl.BlockSpec((blk, LANES), const2),
            pl.BlockSpec((blk, LANES), const2),
            pl.BlockSpec((2, SUBLANES, LANES), lambda i: (0, 0, 0)),
            pl.BlockSpec((Q_W, D_MODEL), const2),
            pl.BlockSpec((None, SEQ_BLK, win, KV_W), lambda i: (lj, i, 0, 0)),
            pl.BlockSpec((None, SEQ_BLK, win, KV_W), lambda i: (lj, i, 0, 0)),
        ],
        out_specs=[
            pl.BlockSpec((blk, D_MODEL), lambda i: (i, 0)),
            pl.BlockSpec((None, SEQ_BLK, win, KV_W), lambda i: (lj, i, 0, 0)),
            pl.BlockSpec((None, SEQ_BLK, win, KV_W), lambda i: (lj, i, 0, 0)),
        ],
        out_shape=[
            jax.ShapeDtypeStruct((n_rows, D_MODEL), F32),
            jax.ShapeDtypeStruct(kc.shape, F32),
            jax.ShapeDtypeStruct(vc.shape, F32),
        ],
        scratch_shapes=[
            pltpu.VMEM((blk, D_MODEL), F32),
            pltpu.VMEM((blk, D_MODEL), F32),
            pltpu.VMEM((blk, Q_W + 2 * KV_W), F32),
            pltpu.VMEM((2 * Q_W // LANES, blk, LANES), F32),
            pltpu.VMEM((blk, KV_W), F32),
            pltpu.VMEM((blk, Q_W), F32),
        ],
        compiler_params=_cparams(1),
        name="attn_sample",
    )(x, mod, nw, wqkv, qnw, knw, cos, sin, sink, wout, kc, vc)


def _rope_tables(pos):
    half = HEAD_DIM // 2
    inv = ROPE_THETA ** (-jnp.arange(half, dtype=F32) / half)
    ang = pos.astype(F32)[:, None] * inv[None, :]
    cos = jnp.cos(ang)
    sin = jnp.sin(ang)
    return jnp.tile(cos, (1, 4)), jnp.concatenate([-sin, sin, -sin, sin], axis=1)


def _prep_attn(w_qkv, q_norm, k_norm, sink, w_out):
    perm = np.asarray(HEAD_PERM)
    wq = w_qkv[:, :Q_W].reshape(D_MODEL, ATT_HEADS, HEAD_DIM)[:, perm].reshape(D_MODEL, Q_W)
    wqkv = jnp.concatenate([wq, w_qkv[:, Q_W:]], axis=1).astype(BF16)
    wout = w_out.reshape(ATT_HEADS, HEAD_DIM, D_MODEL)[perm].reshape(Q_W, D_MODEL).astype(BF16)
    qnw = jnp.tile(q_norm, 2)[None]
    knw = jnp.tile(k_norm, 2)[None]
    idx = np.asarray([[perm[2 * (4 * p + i) + e] for e in range(2) for i in range(4)] for p in range(2)])
    sink_arr = jnp.broadcast_to(sink[idx][:, :, None], (2, SUBLANES, LANES)).astype(F32)
    return wqkv, qnw, knw, sink_arr, wout


def kernel(x_prompt, x_sample, c_prompt, c_sample, state_mlstm_C, state_mlstm_n, state_mlstm_m, state_sconv, cache_win_k, cache_win_v, state_ffn_conv, norm1, norm2, w_ada, b_ada, a_w_in, a_b_if, a_out_norm, a_conv_w, a_w_out, c_w_qkv, c_q_norm, c_k_norm, c_sink, c_w_out, f_w_up, f_conv_w, f_w_down):
    B, S, _ = x_prompt.shape
    NS, SL, _ = x_sample.shape
    assert S % TM == 0 and NS % SEQ_BLK == 0 and SEQ_BLK * SL == CHUNK and SL == SUBLANES

    assert B <= SUBLANES
    c_all = jnp.concatenate([c_sample, c_prompt, jnp.zeros((SUBLANES - B, D_MODEL), F32)], axis=0)
    mod = _ada_call(c_all, w_ada, b_ada)

    win_all = _prep_win_call(a_w_in)
    zg_w = a_w_in[:, :, SRC_G:SRC_B]
    wkg_all = jnp.swapaxes(jnp.concatenate(
        [a_w_in[:, :, SRC_K:SRC_V], zg_w[..., :ML_HEADS], zg_w[..., :ML_HEADS],
         zg_w[..., ML_HEADS:], zg_w[..., ML_HEADS:]], axis=-1), 1, 2).astype(BF16)
    b_i, b_f = a_b_if[:, :ML_HEADS], a_b_if[:, ML_HEADS:]
    lane_pad = jnp.zeros((a_b_if.shape[0], LANES - 2 * ML_HEADS), F32)
    bifc_all = jnp.stack([jnp.concatenate([b_i, b_i, lane_pad], axis=1),
                          jnp.concatenate([b_f, b_f, lane_pad], axis=1)], axis=1)
    bifr_all = jnp.broadcast_to(jnp.concatenate([b_i, b_i, b_f, b_f], axis=1)[:, :, None],
                                (a_b_if.shape[0], 2 * SUBLANES, LANES))
    wout_a_all = _cast_call(a_w_out)
    wup_all = _cast_call(f_w_up)
    wdn_all = _cast_call(f_w_down)
    onw_all = a_out_norm.reshape(-1, 1, ML_WIDTH)
    norm1_r = norm1.reshape(DEPTH, 1, D_MODEL)
    norm2_r = norm2.reshape(DEPTH, 1, D_MODEL)
    win_buf = cache_win_k.shape[2]
    kc_all = cache_win_k.reshape(-1, NS, win_buf, KV_W)
    vc_all = cache_win_v.reshape(-1, NS, win_buf, KV_W)

    cos_p, sin_p = _rope_tables(jnp.arange(S, dtype=jnp.int32))
    cos_s, sin_s = _rope_tables(PAST_LEN + jnp.arange(SL, dtype=jnp.int32))
    cos_s = jnp.tile(cos_s, (SEQ_BLK, 1))
    sin_s = jnp.tile(sin_s, (SEQ_BLK, 1))

    xp = x_prompt
    xs = x_sample.reshape(NS * SL, D_MODEL)
    p_C, p_n, p_m, p_sc, p_wk, p_wv, p_ffn = [], [], [], [], [], [], []
    s_n, s_m, s_sc = [], [], []
    s_C = s_ffn = None
    s_wk, s_wv = kc_all, vc_all

    for l in range(DEPTH):
        if l % 2 == 0:
            i = l // 2
            even_w = (norm1_r, win_all, wkg_all, bifc_all, bifr_all, onw_all, a_conv_w, wout_a_all)
            xp, co, no, mo, sco = _mix_even_prompt_call(xp, mod, NS, l, *even_w)
            p_C.append(co)
            p_n.append(no[:, :, :, 0])
            p_m.append(mo[:, :ML_HEADS, 0])
            p_sc.append(sco[:, SUBLANES - 2:, :])

            n_tok = jnp.repeat(state_mlstm_n[i].transpose(1, 0, 2), SL, axis=1)
            m_rep = jnp.repeat(state_mlstm_m[i], SL, axis=0)
            m_col = jnp.concatenate([m_rep, m_rep, jnp.zeros((NS * SL, LANES - 2 * ML_HEADS), F32)], axis=1)
            m_row = jnp.concatenate([m_rep, m_rep], axis=1).T
            sc_pad = jnp.pad(state_sconv[i], ((0, 0), (0, SL - 2), (0, 0))).reshape(NS * SL, SC_WIDTH)
            xs, s_C, no, mo, sco = _mix_even_sample_call(xs, mod, l, *even_w, state_mlstm_C, n_tok,
                                                         jnp.swapaxes(n_tok, 1, 2), m_col, m_row, sc_pad, s_C)
            s_n.append(no.transpose(1, 0, 2))
            s_m.append(mo[:ML_HEADS, ::SL].T)
            s_sc.append(sco.reshape(NS, SL, SC_WIDTH)[:, :2])
        else:
            j = l // 2
            wqkv, qnw, knw, sink, wout = _prep_attn(c_w_qkv[j], c_q_norm[j], c_k_norm[j], c_sink[j], c_w_out[j])
            xp, ko, vo = _attn_prompt_call(xp, mod, NS, l, norm1_r, wqkv, qnw, knw, cos_p, sin_p, sink, wout)
            p_wk.append(ko.reshape(B, WINDOW, KV_HEADS, HEAD_DIM))
            p_wv.append(vo.reshape(B, WINDOW, KV_HEADS, HEAD_DIM))
            xs, s_wk, s_wv = _attn_sample_call(xs, mod, l, norm1_r, wqkv, qnw, knw, cos_s, sin_s, sink, wout,
                                               s_wk, s_wv)

        xp, st = _ffn_prompt_call(xp, mod, NS, l, norm2_r, wup_all, f_conv_w, wdn_all)
        p_ffn.append(st[:, SUBLANES - 2:, :])
        xs, s_ffn = _ffn_sample_call(xs, mod, l, norm2_r, state_ffn_conv, wup_all, f_conv_w, wdn_all, s_ffn)

    kv_shape = (-1, NS, win_buf, KV_HEADS, HEAD_DIM)
    return (xp, xs.reshape(NS, SL, D_MODEL),
            jnp.stack(p_C), jnp.stack(p_n), jnp.stack(p_m), jnp.stack(p_sc),
            jnp.stack(p_wk), jnp.stack(p_wv), jnp.stack(p_ffn),
            s_C, jnp.stack(s_n), jnp.stack(s_m), jnp.stack(s_sc),
            s_wk.reshape(kv_shape), s_wv.reshape(kv_shape), s_ffn)
```

```python
import functools

import jax
import jax.numpy as jnp
import numpy as np
from jax import lax
from jax.experimental import pallas as pl
from jax.experimental.pallas import tpu as pltpu

F32 = jnp.float32
BF16 = jnp.bfloat16

D_MODEL = 1024
DEPTH = 4
PAST_LEN = 8192
ML_HEADS = 4
ML_DK = 128
ML_DV = 128
ML_WIDTH = ML_HEADS * ML_DV
SC_WIDTH = D_MODEL // 2
ATT_HEADS = 16
KV_HEADS = 4
HEAD_DIM = 64
WINDOW = 128
ROPE_THETA = 10000.0
D_FF = 2816
EPS = 1e-6

LANES = 128
SUBLANES = 8
VMEM_LIMIT = 56 * 1024 * 1024

TM = 1024
TM_FFN = 1024
CHUNK = 128
SEQ_BLK = 16
FC = 256
NCH = D_FF // FC
FS_GATE = 2
FC_S = D_FF // FS_GATE
Q_COL = 0
V_COL = Q_COL + ML_WIDTH
O_COL = V_COL + ML_WIDTH
G1_COL = O_COL + ML_WIDTH
G2_COL = G1_COL + LANES
B_COL = G2_COL + LANES
C_COL = B_COL + SC_WIDTH
X_COL = C_COL + SC_WIDTH
IN_W = X_COL + SC_WIDTH
KG_ROWS = ML_WIDTH + 16
SRC_K = ML_WIDTH
SRC_V = 2 * ML_WIDTH
SRC_G = 4 * ML_WIDTH
SRC_B = SRC_G + 2 * ML_HEADS
Q_W = ATT_HEADS * HEAD_DIM
KV_W = KV_HEADS * HEAD_DIM
HEAD_PERM = (0, 4, 1, 5, 2, 6, 3, 7, 8, 12, 9, 13, 10, 14, 11, 15)


def _cparams(n_axes):
    return pltpu.CompilerParams(dimension_semantics=("arbitrary",) * n_axes,
                                vmem_limit_bytes=VMEM_LIMIT)


def _stacked_call(kernel, n_in, out_idx, prev, slab, slab_spec, **kw):
    specs = list(kw.pop("in_specs"))
    out_specs = list(kw.pop("out_specs"))
    n_slabs = kw["out_shape"][out_idx].shape[0]
    if prev is None:
        out_specs[out_idx] = slab_spec(n_slabs, 0)

        def body(*refs):
            refs = list(refs)
            whole = refs[n_in + out_idx]
            for s in range(n_slabs):
                if s != slab:
                    whole[s] = jnp.zeros(whole.shape[1:], whole.dtype)
            refs[n_in + out_idx] = whole.at[slab]
            return kernel(*refs)

        return pl.pallas_call(body, in_specs=specs, out_specs=out_specs, **kw)

    out_specs[out_idx] = slab_spec(None, slab)

    def body(*refs):
        return kernel(*refs[:n_in], *refs[n_in + 1:])

    call = pl.pallas_call(body, in_specs=specs + [pl.BlockSpec(memory_space=pl.ANY)], out_specs=out_specs,
                          input_output_aliases={n_in: out_idx}, **kw)
    return lambda *args: call(*args, prev)


def _dot(a, b):
    return jnp.dot(a, b, preferred_element_type=F32)


def _dot_nt(a, b):
    return lax.dot_general(a, b, (((1,), (1,)), ((), ())), preferred_element_type=F32)


def _dot_tn(a, b):
    return lax.dot_general(a, b, (((0,), (0,)), ((), ())), preferred_element_type=F32)


def _dot_exact01(m, a):
    a1 = a.astype(BF16)
    r1 = a - a1.astype(F32)
    a2 = r1.astype(BF16)
    a3 = (r1 - a2.astype(F32)).astype(BF16)
    return _dot(m, a1) + _dot(m, a2) + _dot(m, a3)


def _norm_mod(x, nw, sc, sh):
    ms = jnp.mean(x * x, axis=-1, keepdims=True)
    return (x * lax.rsqrt(ms + EPS) * nw) * (1.0 + sc) + sh


def _sigmoid(x):
    return 1.0 / (1.0 + jnp.exp(-x))


def _log_sigmoid(x):
    return jnp.minimum(x, 0.0) - jnp.log(1.0 + jnp.exp(-jnp.abs(x)))


N_MOD = 6


ADA_KINDS = 3


def _ada_kernel(c_ref, w_ref, b_ref, o_ref):
    c = c_ref[...]
    s = (c * _sigmoid(c)).astype(BF16)
    for k in range(ADA_KINDS):
        cols = slice(k * D_MODEL, (k + 1) * D_MODEL)
        o_ref[k] = _dot(s, w_ref[:, cols].astype(BF16)) + b_ref[:, cols]


def _ada_call(c_all, w_ada, b_ada):
    rows = c_all.shape[0]
    return pl.pallas_call(
        _ada_kernel,
        grid=(DEPTH, N_MOD // ADA_KINDS),
        in_specs=[
            pl.BlockSpec((rows, D_MODEL), lambda l, k: (0, 0)),
            pl.BlockSpec((None, D_MODEL, ADA_KINDS * D_MODEL), lambda l, k: (l, 0, k)),
            pl.BlockSpec((None, 1, ADA_KINDS * D_MODEL), lambda l, k: (l, 0, k)),
        ],
        out_specs=pl.BlockSpec((None, ADA_KINDS, rows, D_MODEL), lambda l, k: (l, k, 0, 0)),
        out_shape=jax.ShapeDtypeStruct((DEPTH, N_MOD, rows, D_MODEL), F32),
        compiler_params=_cparams(2),
        name="adaln_mod",
    )(c_all, w_ada, b_ada.reshape(DEPTH, 1, N_MOD * D_MODEL))


CAST_ROWS = 256


def _cast_kernel(w_ref, o_ref):
    o_ref[...] = w_ref[...].astype(BF16)


def _cast_call(w):
    n_l, rows, cols = w.shape
    tr = CAST_ROWS if rows % CAST_ROWS == 0 else rows
    return pl.pallas_call(
        _cast_kernel,
        grid=(n_l, rows // tr),
        in_specs=[pl.BlockSpec((None, tr, cols), lambda l, r: (l, r, 0))],
        out_specs=pl.BlockSpec((None, tr, cols), lambda l, r: (l, r, 0)),
        out_shape=jax.ShapeDtypeStruct(w.shape, BF16),
        compiler_params=_cparams(2),
        name="cast_bf16",
    )(w)


def _prep_win_kernel(w_ref, o_ref, kg_ref):
    o_ref[:, Q_COL:Q_COL + ML_WIDTH] = w_ref[:, 0:ML_WIDTH].astype(BF16)
    kg_ref[:, 0:ML_WIDTH] = w_ref[:, SRC_K:SRC_V].astype(BF16)
    o_ref[:, V_COL:G1_COL] = w_ref[:, SRC_V:SRC_G].astype(BF16)
    tail = w_ref[:, SRC_G:]
    ig = tail[:, 0:ML_HEADS]
    fg = tail[:, ML_HEADS:2 * ML_HEADS]
    pad = jnp.zeros((ig.shape[0], LANES - 2 * ML_HEADS), F32)
    o_ref[:, G1_COL:G2_COL] = jnp.concatenate([ig, ig, pad], axis=1).astype(BF16)
    o_ref[:, G2_COL:B_COL] = jnp.concatenate([fg, fg, pad], axis=1).astype(BF16)
    o_ref[:, B_COL:IN_W] = tail[:, 2 * ML_HEADS:].astype(BF16)
    kg_ref[:, ML_WIDTH:KG_ROWS] = jnp.concatenate([ig, ig, fg, fg], axis=1).astype(BF16)


def _prep_win_call(a_w_in):
    n_l, _, in_a = a_w_in.shape
    return pl.pallas_call(
        _prep_win_kernel,
        grid=(n_l, D_MODEL // CAST_ROWS),
        in_specs=[pl.BlockSpec((None, CAST_ROWS, in_a), lambda l, r: (l, r, 0))],
        out_specs=[pl.BlockSpec((None, CAST_ROWS, IN_W), lambda l, r: (l, r, 0)),
                   pl.BlockSpec((None, CAST_ROWS, KG_ROWS), lambda l, r: (l, r, 0))],
        out_shape=[jax.ShapeDtypeStruct((n_l, D_MODEL, IN_W), BF16),
                   jax.ShapeDtypeStruct((n_l, D_MODEL, KG_ROWS), BF16)],
        compiler_params=_cparams(2),
        name="prep_w_in",
    )(a_w_in)


def _mod_row(mod_ref, kind, b):
    return mod_ref[kind, pl.ds(b, 1), :]


def _ffn_prompt_kernel(x_ref, mod_ref, nw_ref, wup_ref, cw_ref, wdn_ref,
                       xo_ref, st_ref, h_scr, act_scr, carry_scr):
    b = pl.program_id(0)
    t = pl.program_id(1)

    @pl.when(t == 0)
    def _():
        carry_scr[...] = jnp.zeros_like(carry_scr)

    x = x_ref[0]
    h_scr[...] = _norm_mod(x, nw_ref[...], _mod_row(mod_ref, 4, b), _mod_row(mod_ref, 3, b)).astype(BF16)
    for j in range(NCH):
        ys = []
        for col in (j * FC, D_FF + j * FC):
            cols = slice(col, col + FC)
            u = _dot(h_scr[...], wup_ref[:, cols])
            ys.append(_conv3_rows(u, carry_scr[:, cols], cw_ref[:, cols]))
            carry_scr[:, cols] = u[TM_FFN - SUBLANES:TM_FFN]
        g = ys[0]
        act_scr[:, j * FC:(j + 1) * FC] = (g * _sigmoid(g) * ys[1]).astype(BF16)
    y = _dot(act_scr[...], wdn_ref[...])
    xo_ref[0] = x + _mod_row(mod_ref, 5, b) * y
    st_ref[0] = carry_scr[...]


def _ffn_prompt_call(x, mod, n_seq_rows, l, nw, wup, cw, wdn):
    B, S, _ = x.shape
    assert S % TM_FFN == 0
    nt = S // TM_FFN
    once = pl.Buffered(1)
    return pl.pallas_call(
        _ffn_prompt_kernel,
        grid=(B, nt),
        in_specs=[
            pl.BlockSpec((1, TM_FFN, D_MODEL), lambda b, t: (b, t, 0)),
            pl.BlockSpec((None, N_MOD, SUBLANES, D_MODEL), lambda b, t: (l, 0, n_seq_rows // SUBLANES, 0)),
            pl.BlockSpec((None, 1, D_MODEL), lambda b, t: (l, 0, 0)),
            pl.BlockSpec((None, D_MODEL, 2 * D_FF), lambda b, t: (l, 0, 0), pipeline_mode=once),
            pl.BlockSpec((None, 3, 2 * D_FF), lambda b, t: (l, 0, 0)),
            pl.BlockSpec((None, D_FF, D_MODEL), lambda b, t: (l, 0, 0), pipeline_mode=once),
        ],
        out_specs=[
            pl.BlockSpec((1, TM_FFN, D_MODEL), lambda b, t: (b, t, 0)),
            pl.BlockSpec((1, SUBLANES, 2 * D_FF), lambda b, t: (b, 0, 0)),
        ],
        out_shape=[
            jax.ShapeDtypeStruct((B, S, D_MODEL), F32),
            jax.ShapeDtypeStruct((B, SUBLANES, 2 * D_FF), F32),
        ],
        scratch_shapes=[
            pltpu.VMEM((TM_FFN, D_MODEL), BF16),
            pltpu.VMEM((TM_FFN, D_FF), BF16),
            pltpu.VMEM((SUBLANES, 2 * D_FF), F32),
        ],
        compiler_params=_cparams(2),
        name="ffn_prompt",
    )(x, mod, nw, wup, cw, wdn)


def _ffn_sample_kernel(x_ref, mod_ref, nw_ref, s_ref, w_ref, c_ref, wdn_ref,
                       xo_ref, so_ref, h_scr, hb_scr, g2_scr, acc_scr, yg_scr):
    j = pl.program_id(1)
    n_seq = s_ref.shape[0]
    n_rows = x_ref.shape[0]
    n_t = n_rows // n_seq

    @pl.when(j == 0)
    def _():
        def modulate(b, carry):
            rows = pl.ds(pl.multiple_of(b * n_t, n_t), n_t)
            h_scr[rows, :] = _norm_mod(x_ref[rows, :], nw_ref[...], _mod_row(mod_ref, 4, b), _mod_row(mod_ref, 3, b))
            g2_scr[rows, :] = jnp.broadcast_to(_mod_row(mod_ref, 5, b), (n_t, D_MODEL))
            return carry

        lax.fori_loop(0, n_seq, modulate, 0, unroll=8)
        hb_scr[...] = h_scr[...].astype(BF16)
        acc_scr[...] = jnp.zeros_like(acc_scr)

    sub = lax.broadcasted_iota(jnp.int32, (n_seq, n_t, FC_S), 1)
    u3 = _dot(hb_scr[...], w_ref[...]).reshape(n_seq, n_t, FC_S)
    cw = c_ref[...]
    p0 = jnp.broadcast_to(s_ref[:, 0:1, :], (n_seq, n_t, FC_S))
    p1 = jnp.broadcast_to(s_ref[:, 1:2, :], (n_seq, n_t, FC_S))
    s1 = jnp.where(sub < 1, p1, pltpu.roll(u3, 1, 1))
    s2 = jnp.where(sub < 1, p0, jnp.where(sub < 2, p1, pltpu.roll(u3, 2, 1)))
    y = (s2 * cw[0:1] + s1 * cw[1:2] + u3 * cw[2:3]).reshape(n_rows, FC_S)
    so_ref[...] = pltpu.roll(u3, 2, 1)[:, 0:2, :]

    @pl.when(j < FS_GATE)
    def _():
        yg_scr[j] = y

    @pl.when(j >= FS_GATE)
    def _():
        g = yg_scr[j - FS_GATE]
        acc_scr[...] += _dot((g * _sigmoid(g) * y).astype(BF16), wdn_ref[...])

    @pl.when(j == 2 * FS_GATE - 1)
    def _():
        xo_ref[...] = x_ref[...] + g2_scr[...] * acc_scr[...]


def _ffn_sample_call(x, mod, l, nw, st, wup, cw, wdn, so_prev):
    n_seq = st.shape[1] // 2
    n_rows = x.shape[0] // 2
    return _stacked_call(
        _ffn_sample_kernel, 7, 1, so_prev, l,
        lambda lead, idx: pl.BlockSpec((lead, n_seq, 2, FC_S), lambda hf, j: (idx, hf, 0, j)),
        grid=(2, 2 * FS_GATE),
        in_specs=[
            pl.BlockSpec((n_rows, D_MODEL), lambda hf, j: (hf, 0)),
            pl.BlockSpec((None, N_MOD, n_seq, D_MODEL), lambda hf, j: (l, 0, hf, 0)),
            pl.BlockSpec((None, 1, D_MODEL), lambda hf, j: (l, 0, 0)),
            pl.BlockSpec((None, n_seq, 2, FC_S), lambda hf, j: (l, hf, 0, j)),
            pl.BlockSpec((None, D_MODEL, FC_S), lambda hf, j: (l, 0, j)),
            pl.BlockSpec((None, 3, FC_S), lambda hf, j: (l, 0, j)),
            pl.BlockSpec((None, FC_S, D_MODEL), lambda hf, j: (l, jnp.maximum(j - FS_GATE, 0), 0)),
        ],
        out_specs=[
            pl.BlockSpec((n_rows, D_MODEL), lambda hf, j: (hf, 0)),
            None,
        ],
        out_shape=[
            jax.ShapeDtypeStruct(x.shape, F32),
            jax.ShapeDtypeStruct((DEPTH, st.shape[1], 2, 2 * D_FF), F32),
        ],
        scratch_shapes=[
            pltpu.VMEM((n_rows, D_MODEL), F32),
            pltpu.VMEM((n_rows, D_MODEL), BF16),
            pltpu.VMEM((n_rows, D_MODEL), F32),
            pltpu.VMEM((n_rows, D_MODEL), F32),
            pltpu.VMEM((FS_GATE, n_rows, FC_S), F32),
        ],
        compiler_params=_cparams(2),
        name="ffn_sample",
    )(x, mod, nw, st, wup, cw, wdn)


def _chunk_consts(seq_len):
    r = lax.broadcasted_iota(jnp.int32, (CHUNK, CHUNK), 0)
    c = lax.broadcasted_iota(jnp.int32, (CHUNK, CHUNK), 1)
    if seq_len >= CHUNK:
        same = r >= 0
    else:
        same = (r // seq_len) == (c // seq_len)
    mask = same & (c <= r)
    lmat = mask.astype(BF16)
    lmat_t = (same & (r <= c)).astype(BF16)
    return mask, lmat, lmat_t, same.astype(BF16)


def _dot_exact01_r(a, m):
    a1 = a.astype(BF16)
    r1 = a - a1.astype(F32)
    a2 = r1.astype(BF16)
    a3 = (r1 - a2.astype(F32)).astype(BF16)
    return _dot(a1, m) + _dot(a2, m) + _dot(a3, m)


def _pick_cols_t(a, sel):
    a1 = a.astype(BF16)
    r1 = a - a1.astype(F32)
    a2 = r1.astype(BF16)
    a3 = (r1 - a2.astype(F32)).astype(BF16)
    return _dot_nt(sel, a1) + _dot_nt(sel, a2) + _dot_nt(sel, a3)


def _seq_max_lanes(x, seq_len):
    n = x.shape[1]
    pos = lax.broadcasted_iota(jnp.int32, x.shape, 1)
    d = 1
    while d < seq_len:
        partner = jnp.where((pos & d) == 0, pltpu.roll(x, n - d, 1), pltpu.roll(x, d, 1))
        x = jnp.maximum(x, partner)
        d *= 2
    return x


def _seq_prefix_max_rows(x, seq_len):
    pos = lax.broadcasted_iota(jnp.int32, x.shape, 0) & (seq_len - 1)
    d = 1
    while d < seq_len:
        x = jnp.where(pos >= d, jnp.maximum(x, pltpu.roll(x, d, 0)), x)
        d *= 2
    return x


def _seq_last_row(x, seq_len):
    n, w = x.shape
    if seq_len >= n:
        return jnp.broadcast_to(x[n - 1:n], x.shape)
    x3 = x.reshape(n // seq_len, seq_len, w)
    return jnp.broadcast_to(x3[:, seq_len - 1:seq_len, :], x3.shape).reshape(n, w)


def _gates_rows_pre(gt, bias_r, lmat_t, tot, seq_len):
    ig = gt[0:SUBLANES] + bias_r[0:SUBLANES]
    lf = _log_sigmoid(gt[SUBLANES:] + bias_r[SUBLANES:])
    b = _dot_exact01_r(lf, lmat_t)
    bl = _dot_exact01_r(lf, tot)
    v = ig - b
    return bl, v, _seq_max_lanes(v, seq_len)


def _gates_rows_post(pre, mp_r):
    bl, v, vm = pre
    mn = bl + jnp.maximum(mp_r, vm)
    return jnp.exp(bl + v - mn), jnp.exp(bl + mp_r - mn), mn


def _gates_rows(gt, bias_r, mp_r, lmat_t, tot, seq_len):
    pre = _gates_rows_pre(gt, bias_r, lmat_t, tot, seq_len)
    return (pre[1],) + _gates_rows_post(pre, mp_r)


def _gates_cols_pre(g1, g2, bias_c, lmat, seq_len):
    lane = lax.broadcasted_iota(jnp.int32, g1.shape, 1)
    ig = g1 + bias_c[0:1]
    lf = jnp.where(lane < 2 * ML_HEADS, _log_sigmoid(g2 + bias_c[1:2]), 0.0)
    b = _dot_exact01(lmat, lf)
    return b, _seq_prefix_max_rows(ig - b, seq_len)


def _gates_cols_post(b, cm, mp_c):
    g = b + mp_c
    mt = jnp.maximum(b + cm, g)
    return b - mt, jnp.exp(g - mt), jnp.exp(-mt)


def _gates_cols(g1, g2, bias_c, mp_c, lmat, seq_len):
    b, cm = _gates_cols_pre(g1, g2, bias_c, lmat, seq_len)
    return (b, cm) + _gates_cols_post(b, cm, mp_c)


def _outer_sum_lhs(u):
    lane = lax.broadcasted_iota(jnp.int32, u.shape, 1)
    hi = u.astype(BF16).astype(F32)
    lo = u - hi
    return jnp.where(lane < ML_HEADS, hi, jnp.where(lane < 2 * ML_HEADS, lo,
                     jnp.where(lane < 4 * ML_HEADS, 1.0, 0.0))).astype(BF16)


def _outer_sum_rhs(v_r, hd):
    row = lax.broadcasted_iota(jnp.int32, v_r.shape, 0)
    hi = v_r.astype(BF16).astype(F32)
    lo = v_r - hi
    pick = (row == hd) | (row == ML_HEADS + hd)
    top = jnp.where(pick, 1.0, 0.0)
    bot = jnp.where(row == hd, hi, jnp.where(row == ML_HEADS + hd, lo, 0.0))
    r16 = jnp.concatenate([top, bot], axis=0).astype(BF16)
    return jnp.concatenate([r16, jnp.zeros((LANES - 2 * SUBLANES, v_r.shape[1]), BF16)], axis=0)


def _decayed_scores(q, kt, lhsc, v_r, hd, mask):
    e = _dot(lhsc, _outer_sum_rhs(v_r, hd))
    return _dot(q.astype(BF16), kt.astype(BF16)) * jnp.where(mask, jnp.exp(e), 0.0)


def _conv3_rows(cx, prev8, cw):
    n = cx.shape[1]
    row = lax.broadcasted_iota(jnp.int32, (SUBLANES, n), 0)
    s1 = pltpu.roll(cx, 1, 0)
    s2 = pltpu.roll(cx, 2, 0)
    f1 = jnp.where(row < 1, pltpu.roll(prev8, 1, 0), s1[0:SUBLANES])
    f2 = jnp.where(row < 2, pltpu.roll(prev8, 2, 0), s2[0:SUBLANES])
    s1 = jnp.concatenate([f1, s1[SUBLANES:]], axis=0)
    s2 = jnp.concatenate([f2, s2[SUBLANES:]], axis=0)
    return s2 * cw[0:1] + s1 * cw[1:2] + cx * cw[2:3]


def _mlstm_out_norm(hm, zo, onw):
    ms = jnp.mean(hm * hm, axis=-1, keepdims=True)
    return hm * lax.rsqrt(ms + EPS) * onw * _sigmoid(zo)


def _even_tail(z_scr, hm_scr, cat_scr, onw_ref):
    for hd in range(ML_HEADS):
        col = slice(hd * ML_DV, (hd + 1) * ML_DV)
        zo = z_scr[:, O_COL + hd * ML_DV:O_COL + (hd + 1) * ML_DV]
        cat_scr[:, col] = _mlstm_out_norm(hm_scr[:, col], zo, onw_ref[:, col]).astype(BF16)


def _mix_even_prompt_kernel(x_ref, mod_ref, nw_ref, win_ref, wkg_ref, bifc_ref, bifr_ref, onw_ref, cw_ref,
                            wout_ref, xo_ref, co_ref, no_ref, mo_ref, sco_ref,
                            z_scr, zt_scr, hm_scr, cat_scr, cn_scr, mrow_scr, mlane_scr, cc_scr):
    t = pl.program_id(1)

    @pl.when(t == 0)
    def _():
        cn_scr[...] = jnp.zeros_like(cn_scr)
        mrow_scr[...] = jnp.zeros_like(mrow_scr)
        mlane_scr[...] = jnp.zeros_like(mlane_scr)
        cc_scr[...] = jnp.zeros_like(cc_scr)

    x = x_ref[0]
    bi = pl.program_id(0)
    h = _norm_mod(x, nw_ref[...], _mod_row(mod_ref, 1, bi), _mod_row(mod_ref, 0, bi)).astype(BF16)
    z_scr[...] = _dot(h, win_ref[...])
    zt_scr[...] = _dot_nt(wkg_ref[...], h)

    mask, lmat, lmat_t, tot = _chunk_consts(CHUNK)
    scale = ML_DK ** -0.5
    ones_v = jnp.ones((CHUNK, ML_DV), BF16)

    n_ch = TM // CHUNK
    chunk_rows = [slice(c * CHUNK, (c + 1) * CHUNK) for c in range(n_ch)]
    pre_r = [_gates_rows_pre(zt_scr[ML_WIDTH:KG_ROWS, rows], bifr_ref[...], lmat_t, tot, CHUNK)
             for rows in chunk_rows]
    pre_c = [_gates_cols_pre(z_scr[rows, G1_COL:G1_COL + LANES], z_scr[rows, G2_COL:G2_COL + LANES],
                             bifc_ref[...], lmat, CHUNK) for rows in chunk_rows]
    mp_r = mrow_scr[...]
    mp_c = mlane_scr[0:1, :]
    post_r, post_c = [], []
    for c in range(n_ch):
        post_r.append(_gates_rows_post(pre_r[c], mp_r))
        mp_r = post_r[c][2]
        b, cm = pre_c[c]
        post_c.append(_gates_cols_post(b, cm, mp_c))
        mp_c = b[CHUNK - 1:CHUNK] + jnp.maximum(mp_c, cm[CHUNK - 1:CHUNK])
    mrow_scr[...] = mp_r
    mlane_scr[...] = jnp.broadcast_to(mp_c, (SUBLANES, LANES))

    intra, upd, wgq = {}, {}, {}
    for c, rows in enumerate(chunk_rows):
        u, wg, _ = post_c[c]
        lhsc = _outer_sum_lhs(u)
        for hd in range(ML_HEADS):
            q = z_scr[rows, Q_COL + hd * ML_DK:Q_COL + (hd + 1) * ML_DK]
            v = z_scr[rows, V_COL + hd * ML_DV:V_COL + (hd + 1) * ML_DV]
            kt = zt_scr[hd * ML_DK:(hd + 1) * ML_DK, rows] * scale
            s = _decayed_scores(q, kt, lhsc, pre_r[c][1], hd, mask)
            v1 = jnp.concatenate([v.astype(BF16), ones_v], axis=1)
            intra[c, hd] = _dot(s.astype(BF16), v1)
            upd[c, hd] = _dot((kt * post_r[c][0][hd:hd + 1, :]).astype(BF16), v1)
            wgq[c, hd] = (q * wg[:, hd:hd + 1]).astype(BF16)

    for hd in range(ML_HEADS):
        cn = cn_scr[hd]
        for c, rows in enumerate(chunk_rows):
            out = intra[c, hd] + _dot(wgq[c, hd], cn.astype(BF16))
            r = 1.0 / jnp.maximum(jnp.abs(out[:, ML_DV:]), post_c[c][2][:, hd:hd + 1])
            hm_scr[rows, hd * ML_DV:(hd + 1) * ML_DV] = out[:, :ML_DV] * r
            wc = post_r[c][1][hd:hd + 1, :]
            cn = jnp.concatenate([wc, wc], axis=1) * cn + upd[c, hd]
        cn_scr[hd] = cn

    _even_tail(z_scr, hm_scr, cat_scr, onw_ref)
    cx = z_scr[:, C_COL:C_COL + SC_WIDTH] * z_scr[:, X_COL:X_COL + SC_WIDTH]
    u = _conv3_rows(cx, cc_scr[...], cw_ref[...])
    cc_scr[...] = cx[TM - SUBLANES:TM]
    cat_scr[:, ML_WIDTH:] = (z_scr[:, B_COL:B_COL + SC_WIDTH] * u).astype(BF16)

    y = _dot(cat_scr[...], wout_ref[...])
    xo_ref[0] = x + _mod_row(mod_ref, 2, bi) * y
    for hd in range(ML_HEADS):
        co_ref[0, hd] = cn_scr[hd, :, 0:ML_DV]
        no_ref[0, hd] = cn_scr[hd, :, ML_DV:]
    mo_ref[0] = mrow_scr[...]
    sco_ref[0] = cc_scr[...]


def _even_weight_specs(i, idx):
    once = pl.Buffered(1)
    return [
        pl.BlockSpec((None, D_MODEL, IN_W), idx, pipeline_mode=once),
        pl.BlockSpec((None, KG_ROWS, D_MODEL), idx, pipeline_mode=once),
        pl.BlockSpec((None, 2, LANES), idx),
        pl.BlockSpec((None, 2 * SUBLANES, LANES), idx),
        pl.BlockSpec((None, 1, ML_WIDTH), idx),
        pl.BlockSpec((None, 3, SC_WIDTH), idx),
        pl.BlockSpec((None, ML_WIDTH + SC_WIDTH, D_MODEL), idx, pipeline_mode=once),
    ]


def _mix_even_prompt_call(x, mod, n_seq_rows, l, nw, win, wkg, bifc, bifr, onw, cw, wout):
    B, S, _ = x.shape
    nt = S // TM
    i = l // 2
    return pl.pallas_call(
        _mix_even_prompt_kernel,
        grid=(B, nt),
        in_specs=[
            pl.BlockSpec((1, TM, D_MODEL), lambda b, t: (b, t, 0)),
            pl.BlockSpec((None, N_MOD, SUBLANES, D_MODEL), lambda b, t: (l, 0, n_seq_rows // SUBLANES, 0)),
            pl.BlockSpec((None, 1, D_MODEL), lambda b, t: (l, 0, 0)),
        ] + _even_weight_specs(i, lambda b, t: (i, 0, 0)),
        out_specs=[
            pl.BlockSpec((1, TM, D_MODEL), lambda b, t: (b, t, 0)),
            pl.BlockSpec((1, ML_HEADS, ML_DK, ML_DV), lambda b, t: (b, 0, 0, 0)),
            pl.BlockSpec((1, ML_HEADS, ML_DK, LANES), lambda b, t: (b, 0, 0, 0)),
            pl.BlockSpec((1, SUBLANES, LANES), lambda b, t: (b, 0, 0)),
            pl.BlockSpec((1, SUBLANES, SC_WIDTH), lambda b, t: (b, 0, 0)),
        ],
        out_shape=[
            jax.ShapeDtypeStruct((B, S, D_MODEL), F32),
            jax.ShapeDtypeStruct((B, ML_HEADS, ML_DK, ML_DV), F32),
            jax.ShapeDtypeStruct((B, ML_HEADS, ML_DK, LANES), F32),
            jax.ShapeDtypeStruct((B, SUBLANES, LANES), F32),
            jax.ShapeDtypeStruct((B, SUBLANES, SC_WIDTH), F32),
        ],
        scratch_shapes=[
            pltpu.VMEM((TM, IN_W), F32),
            pltpu.VMEM((KG_ROWS, TM), F32),
            pltpu.VMEM((TM, ML_WIDTH), F32),
            pltpu.VMEM((TM, ML_WIDTH + SC_WIDTH), BF16),
            pltpu.VMEM((ML_HEADS, ML_DK, ML_DV + LANES), F32),
            pltpu.VMEM((SUBLANES, LANES), F32),
            pltpu.VMEM((SUBLANES, LANES), F32),
            pltpu.VMEM((SUBLANES, SC_WIDTH), F32),
        ],
        compiler_params=_cparams(2),
        name="mix_even_prompt",
    )(x, mod, nw, win, wkg, bifc, bifr, onw, cw, wout)


def _mix_even_sample_kernel(x_ref, mod_ref, nw_ref, win_ref, wkg_ref, bifc_ref, bifr_ref, onw_ref, cw_ref,
                            wout_ref, c_ref, nt_ref, ntt_ref, mcol_ref, mrow_ref, sc_ref,
                            xo_ref, co_ref, no_ref, mo_ref, sco_ref,
                            h_scr, z_scr, zt_scr, hm_scr, cat_scr, g1_scr,
                            intra_scr, dpart_scr, pbe_scr, wgq_scr, kwt_scr, vb_scr, wcb_scr, inter_scr):
    n_tok = x_ref.shape[0]
    seq_len = n_tok // SEQ_BLK

    def modulate(b, carry):
        rows = pl.ds(pl.multiple_of(b * seq_len, seq_len), seq_len)
        h_scr[rows, :] = _norm_mod(x_ref[rows, :], nw_ref[...], _mod_row(mod_ref, 1, b), _mod_row(mod_ref, 0, b))
        g1_scr[rows, :] = jnp.broadcast_to(_mod_row(mod_ref, 2, b), (seq_len, D_MODEL))
        return carry

    lax.fori_loop(0, SEQ_BLK, modulate, 0, unroll=4)
    hb = h_scr[...].astype(BF16)
    z_scr[...] = _dot(hb, win_ref[...])
    zt_scr[...] = _dot_nt(wkg_ref[...], hb)

    mask, lmat, lmat_t, tot = _chunk_consts(seq_len)
    scale = ML_DK ** -0.5
    ones_v = jnp.ones((CHUNK, ML_DV), BF16)
    mp_r = mrow_ref[...]
    v_r, ws_r, wc_r, mn_r = _gates_rows(zt_scr[ML_WIDTH:KG_ROWS, :], bifr_ref[...], mp_r, lmat_t, tot, seq_len)
    mo_ref[...] = mn_r
    mp_c = mcol_ref[...]
    b, cm, u, wg, pbe = _gates_cols(z_scr[:, G1_COL:G1_COL + LANES], z_scr[:, G2_COL:G2_COL + LANES],
                                    bifc_ref[...], mp_c, lmat, seq_len)
    wc_c = jnp.exp(mp_c - jnp.maximum(mp_c, _seq_last_row(cm, seq_len)))
    lhsc = _outer_sum_lhs(u)
    first_tok = (lax.broadcasted_iota(jnp.int32, (SEQ_BLK, n_tok), 0) * seq_len
                 == lax.broadcasted_iota(jnp.int32, (SEQ_BLK, n_tok), 1)).astype(BF16)

    for hd in range(ML_HEADS):
        q = z_scr[:, Q_COL + hd * ML_DK:Q_COL + (hd + 1) * ML_DK]
        v = z_scr[:, V_COL + hd * ML_DV:V_COL + (hd + 1) * ML_DV]
        kt = zt_scr[hd * ML_DK:(hd + 1) * ML_DK, :] * scale
        s = _decayed_scores(q, kt, lhsc, v_r, hd, mask)
        vb = v.astype(BF16)
        out = _dot(s.astype(BF16), jnp.concatenate([vb, ones_v], axis=1))
        wg_h = wg[:, hd:hd + 1]
        qn = jnp.sum(q * nt_ref[hd], axis=-1, keepdims=True)
        intra_scr[hd] = out[:, :ML_DV]
        dpart_scr[hd] = out[:, ML_DV:] + wg_h * qn
        pbe_scr[hd] = jnp.broadcast_to(pbe[:, hd:hd + 1], (n_tok, LANES))
        wgq_scr[hd] = q * wg_h
        kwt = kt * ws_r[hd:hd + 1, :]
        kwt_scr[hd] = kwt
        vb_scr[hd] = vb
        wcb_scr[hd] = jnp.broadcast_to(wc_c[:, hd:hd + 1], (n_tok, LANES))
        n_new_t = wc_r[hd:hd + 1, :] * ntt_ref[hd] + _dot_exact01_r(kwt, tot)
        no_ref[hd] = _pick_cols_t(n_new_t, first_tok)

    lane_i = lax.broadcasted_iota(jnp.int32, (ML_DK, n_tok), 1)

    def per_seq(bq, carry):
        r0 = pl.multiple_of(bq * seq_len, seq_len)
        rows = pl.ds(r0, seq_len)
        sel = (lane_i >= r0) & (lane_i < r0 + seq_len)
        for hd in range(ML_HEADS):
            c_prev = c_ref[bq, hd]
            inter_scr[hd, rows, :] = _dot(wgq_scr[hd, rows, :].astype(BF16), c_prev.astype(BF16))
            kw_b = jnp.where(sel, kwt_scr[hd], 0.0).astype(BF16)
            co_ref[bq, hd] = wcb_scr[hd, pl.ds(r0, 1), :] * c_prev + _dot(kw_b, vb_scr[hd])
        return carry

    lax.fori_loop(0, SEQ_BLK, per_seq, 0, unroll=4)

    for hd in range(ML_HEADS):
        num = inter_scr[hd] + intra_scr[hd]
        hm_scr[:, hd * ML_DV:(hd + 1) * ML_DV] = num * (
            1.0 / jnp.maximum(jnp.abs(dpart_scr[hd]), pbe_scr[hd]))

    _even_tail(z_scr, hm_scr, cat_scr, onw_ref)
    cx = z_scr[:, C_COL:C_COL + SC_WIDTH] * z_scr[:, X_COL:X_COL + SC_WIDTH]
    sub = lax.broadcasted_iota(jnp.int32, (n_tok, SC_WIDTH), 0) % seq_len
    p1 = sc_ref[...]
    s1 = jnp.where(sub < 1, pltpu.roll(p1, n_tok - 1, 0), pltpu.roll(cx, 1, 0))
    s2 = jnp.where(sub < 2, p1, pltpu.roll(cx, 2, 0))
    cw = cw_ref[...]
    u = s2 * cw[0:1] + s1 * cw[1:2] + cx * cw[2:3]
    sco_ref[...] = pltpu.roll(cx, n_tok - (seq_len - 2), 0)
    cat_scr[:, ML_WIDTH:] = (z_scr[:, B_COL:B_COL + SC_WIDTH] * u).astype(BF16)

    y = _dot(cat_scr[...], wout_ref[...])
    xo_ref[...] = x_ref[...] + g1_scr[...] * y


def _mix_even_sample_call(x, mod, l, nw, win, wkg, bifc, bifr, onw, cw, wout, c0, n_tok, n_tok_t, m_col, m_row,
                          sc_pad, co_prev):
    n_rows = x.shape[0]
    n_seq = c0.shape[1]
    seq_len = n_rows // n_seq
    blk = SEQ_BLK * seq_len
    nb = n_seq // SEQ_BLK
    li = l // 2
    head_blk = (ML_HEADS, blk, LANES)
    return _stacked_call(
        _mix_even_sample_kernel, 16, 1, co_prev, li,
        lambda lead, idx: pl.BlockSpec((lead, SEQ_BLK, ML_HEADS, ML_DK, ML_DV), lambda i: (idx, i, 0, 0, 0)),
        grid=(nb,),
        in_specs=[
            pl.BlockSpec((blk, D_MODEL), lambda i: (i, 0)),
            pl.BlockSpec((None, N_MOD, SEQ_BLK, D_MODEL), lambda i: (l, 0, i, 0)),
            pl.BlockSpec((None, 1, D_MODEL), lambda i: (l, 0, 0)),
        ] + _even_weight_specs(li, lambda i: (li, 0, 0)) + [
            pl.BlockSpec((None, SEQ_BLK, ML_HEADS, ML_DK, ML_DV), lambda i: (li, i, 0, 0, 0)),
            pl.BlockSpec((ML_HEADS, blk, ML_DK), lambda i: (0, i, 0)),
            pl.BlockSpec((ML_HEADS, ML_DK, blk), lambda i: (0, 0, i)),
            pl.BlockSpec((blk, LANES), lambda i: (i, 0)),
            pl.BlockSpec((SUBLANES, blk), lambda i: (0, i)),
            pl.BlockSpec((blk, SC_WIDTH), lambda i: (i, 0)),
        ],
        out_specs=[
            pl.BlockSpec((blk, D_MODEL), lambda i: (i, 0)),
            None,
            pl.BlockSpec((ML_HEADS, SEQ_BLK, ML_DK), lambda i: (0, i, 0)),
            pl.BlockSpec((SUBLANES, blk), lambda i: (0, i)),
            pl.BlockSpec((blk, SC_WIDTH), lambda i: (i, 0)),
        ],
        out_shape=[
            jax.ShapeDtypeStruct((n_rows, D_MODEL), F32),
            jax.ShapeDtypeStruct(c0.shape, F32),
            jax.ShapeDtypeStruct((ML_HEADS, n_seq, ML_DK), F32),
            jax.ShapeDtypeStruct((SUBLANES, n_rows), F32),
            jax.ShapeDtypeStruct((n_rows, SC_WIDTH), F32),
        ],
        scratch_shapes=[
            pltpu.VMEM((blk, D_MODEL), F32),
            pltpu.VMEM((blk, IN_W), F32),
            pltpu.VMEM((KG_ROWS, blk), F32),
            pltpu.VMEM((blk, ML_WIDTH), F32),
            pltpu.VMEM((blk, ML_WIDTH + SC_WIDTH), BF16),
            pltpu.VMEM((blk, D_MODEL), F32),
            pltpu.VMEM(head_blk, F32),
            pltpu.VMEM(head_blk, F32),
            pltpu.VMEM(head_blk, F32),
            pltpu.VMEM(head_blk, F32),
            pltpu.VMEM((ML_HEADS, ML_DK, blk), F32),
            pltpu.VMEM(head_blk, BF16),
            pltpu.VMEM(head_blk, F32),
            pltpu.VMEM(head_blk, F32),
        ],
        compiler_params=_cparams(1),
        name="mix_even_sample",
    )(x, mod, nw, win, wkg, bifc, bifr, onw, cw, wout, c0, n_tok, n_tok_t, m_col, m_row, sc_pad)


def _split2(x):
    hi = x.astype(BF16)
    lo = (x - hi.astype(F32)).astype(BF16)
    return jnp.concatenate([hi, lo], axis=1)


def _head_lane_mats():
    r = lax.broadcasted_iota(jnp.int32, (2 * LANES, LANES), 0) % LANES
    c = lax.broadcasted_iota(jnp.int32, (2 * LANES, LANES), 1)
    hsum = ((r // HEAD_DIM) == (c // HEAD_DIM)).astype(BF16)
    half = HEAD_DIM // 2
    src = jnp.where((c % HEAD_DIM) < half, c + half, c - half)
    return hsum, (r == src).astype(BF16)


def _q_lane_mat():
    r = lax.broadcasted_iota(jnp.int32, (2 * LANES, 2 * LANES), 0)
    c = lax.broadcasted_iota(jnp.int32, (2 * LANES, 2 * LANES), 1)
    half = HEAD_DIM // 2
    src = jnp.where((c % HEAD_DIM) < half, c + half, c - half)
    top = (r < LANES) & (c < LANES) & ((r // HEAD_DIM) == (c // HEAD_DIM))
    return (top | ((r >= LANES) & (c >= LANES) & (r == src))).astype(BF16)


def _q_norm_rope(xb, gw, cos, sin, qmat):
    zg = xb * gw
    out = _dot(jnp.concatenate([(xb * xb).astype(BF16), zg.astype(BF16)], axis=1), qmat)
    ms = out[:, :LANES] * (1.0 / HEAD_DIM)
    return lax.rsqrt(ms + EPS) * (zg * cos + out[:, LANES:] * sin)


def _qk_norm_rope(xb, gw, cos, sin, hsum, rot_mat):
    ms = _dot(_split2(xb * xb), hsum) * (1.0 / HEAD_DIM)
    zg = xb * gw
    rot = _dot(_split2(zg), rot_mat)
    return lax.rsqrt(ms + EPS) * (zg * cos + rot * sin)


def _sink_rows(sink8, reps):
    return jnp.concatenate(
        [jnp.broadcast_to(sink8[r:r + 1, :], (reps, LANES)) for r in range(SUBLANES)], axis=0)


def _sink_col(sink8, reps):
    parts = [jnp.broadcast_to(sink8[r:r + 1, :], (reps, LANES)) for r in range(SUBLANES)]
    return jnp.concatenate(parts, axis=0)[:, 0:1]


def _attn_prompt_kernel(x_ref, mod_ref, nw_ref, wqkv_ref, qnw_ref, knw_ref, cos_ref, sin_ref,
                        sink_ref, wout_ref, xo_ref, ko_ref, vo_ref,
                        z_scr, qm_scr, k_scr, v_scr, o_scr):
    t = pl.program_id(1)
    n_qb = TM // WINDOW

    @pl.when(t == 0)
    def _():
        k_scr[0:WINDOW, :] = jnp.zeros((WINDOW, KV_W), BF16)
        v_scr[0:WINDOW, :] = jnp.zeros((WINDOW, KV_W), BF16)

    x = x_ref[0]
    bi = pl.program_id(0)
    h = _norm_mod(x, nw_ref[...], _mod_row(mod_ref, 1, bi), _mod_row(mod_ref, 0, bi)).astype(BF16)
    z_scr[...] = _dot(h, wqkv_ref[...])
    cos = cos_ref[...]
    sin = sin_ref[...]
    hmats = _head_lane_mats()
    qmat = _q_lane_mat()
    half0 = lax.broadcasted_iota(jnp.int32, (TM, LANES), 1) < HEAD_DIM
    qscale = HEAD_DIM ** -0.5
    for jb in range(Q_W // LANES):
        y = _q_norm_rope(z_scr[:, jb * LANES:(jb + 1) * LANES], qnw_ref[...], cos, sin, qmat) * qscale
        qm_scr[2 * jb] = jnp.where(half0, y, 0.0).astype(BF16)
        qm_scr[2 * jb + 1] = jnp.where(half0, 0.0, y).astype(BF16)
    for p in range(KV_W // LANES):
        kf = _qk_norm_rope(z_scr[:, Q_W + p * LANES:Q_W + (p + 1) * LANES], knw_ref[...], cos, sin, *hmats)
        ko_ref[0, :, p * LANES:(p + 1) * LANES] = kf[TM - WINDOW:TM]
        k_scr[WINDOW:WINDOW + TM, p * LANES:(p + 1) * LANES] = kf.astype(BF16)
    vf = z_scr[:, Q_W + KV_W:Q_W + 2 * KV_W]
    vo_ref[0] = vf[TM - WINDOW:TM]
    v_scr[WINDOW:WINDOW + TM, :] = vf.astype(BF16)

    r = lax.broadcasted_iota(jnp.int32, (8 * WINDOW, 2 * WINDOW), 0) % WINDOW
    c = lax.broadcasted_iota(jnp.int32, (8 * WINDOW, 2 * WINDOW), 1)
    valid = ((c < WINDOW) & (c > r)) | ((c >= WINDOW) & ((c - WINDOW) <= r))
    first_lim = jnp.where(t == 0, WINDOW, 0)
    half0q = lax.broadcasted_iota(jnp.int32, (4 * WINDOW, LANES), 1) < HEAD_DIM
    ones_kv = jnp.ones((2 * WINDOW, LANES), BF16)
    for qb in range(n_qb):
        rows = slice(qb * WINDOW, (qb + 1) * WINDOW)
        krows = slice(qb * WINDOW, (qb + 2) * WINDOW)
        vmask = (valid & (c >= first_lim)) if qb == 0 else valid
        for p in range(KV_W // LANES):
            kb = k_scr[krows, p * LANES:(p + 1) * LANES]
            vb = v_scr[krows, p * LANES:(p + 1) * LANES]
            qs = jnp.concatenate([qm_scr[2 * (4 * p + i) + e, rows, :] for e in range(2) for i in range(4)],
                                 axis=0)
            s = jnp.where(vmask, _dot_nt(qs, kb), -jnp.inf)
            sk = _sink_rows(sink_ref[p], WINDOW)
            mx = jnp.maximum(jnp.max(s, axis=-1, keepdims=True), sk)
            pr = jnp.exp(s - jnp.concatenate([mx, mx], axis=1))
            o2 = _dot(pr.astype(BF16), jnp.concatenate([vb, ones_kv], axis=1))
            den = o2[:, LANES:] + jnp.exp(sk - mx)
            o = o2[:, :LANES] * (1.0 / den)
            merged = jnp.where(half0q, o[0:4 * WINDOW], o[4 * WINDOW:])
            for i in range(4):
                o_scr[rows, (4 * p + i) * LANES:(4 * p + i + 1) * LANES] = (
                    merged[i * WINDOW:(i + 1) * WINDOW].astype(BF16))

    y = _dot(o_scr[...], wout_ref[...])
    xo_ref[0] = x + _mod_row(mod_ref, 2, bi) * y
    k_scr[0:WINDOW, :] = k_scr[TM:TM + WINDOW, :]
    v_scr[0:WINDOW, :] = v_scr[TM:TM + WINDOW, :]


def _attn_prompt_call(x, mod, n_seq_rows, l, nw, wqkv, qnw, knw, cos, sin, sink, wout):
    B, S, _ = x.shape
    nt = S // TM
    const2 = lambda b, t: (0, 0)
    return pl.pallas_call(
        _attn_prompt_kernel,
        grid=(B, nt),
        in_specs=[
            pl.BlockSpec((1, TM, D_MODEL), lambda b, t: (b, t, 0)),
            pl.BlockSpec((None, N_MOD, SUBLANES, D_MODEL), lambda b, t: (l, 0, n_seq_rows // SUBLANES, 0)),
            pl.BlockSpec((None, 1, D_MODEL), lambda b, t: (l, 0, 0)),
            pl.BlockSpec((D_MODEL, Q_W + 2 * KV_W), const2),
            pl.BlockSpec((1, LANES), const2),
            pl.BlockSpec((1, LANES), const2),
            pl.BlockSpec((TM, LANES), lambda b, t: (t, 0)),
            pl.BlockSpec((TM, LANES), lambda b, t: (t, 0)),
            pl.BlockSpec((2, SUBLANES, LANES), lambda b, t: (0, 0, 0)),
            pl.BlockSpec((Q_W, D_MODEL), const2),
        ],
        out_specs=[
            pl.BlockSpec((1, TM, D_MODEL), lambda b, t: (b, t, 0)),
            pl.BlockSpec((1, WINDOW, KV_W), lambda b, t: (b, 0, 0)),
            pl.BlockSpec((1, WINDOW, KV_W), lambda b, t: (b, 0, 0)),
        ],
        out_shape=[
            jax.ShapeDtypeStruct((B, S, D_MODEL), F32),
            jax.ShapeDtypeStruct((B, WINDOW, KV_W), F32),
            jax.ShapeDtypeStruct((B, WINDOW, KV_W), F32),
        ],
        scratch_shapes=[
            pltpu.VMEM((TM, Q_W + 2 * KV_W), F32),
            pltpu.VMEM((2 * Q_W // LANES, TM, LANES), BF16),
            pltpu.VMEM((TM + WINDOW, KV_W), BF16),
            pltpu.VMEM((TM + WINDOW, KV_W), BF16),
            pltpu.VMEM((TM, Q_W), BF16),
        ],
        compiler_params=_cparams(2),
        name="attn_prompt",
    )(x, mod, nw, wqkv, qnw, knw, cos, sin, sink, wout)


def _attn_sample_kernel(x_ref, mod_ref, nw_ref, wqkv_ref, qnw_ref, knw_ref, cos_ref, sin_ref,
                        sink_ref, wout_ref, kc_ref, vc_ref,
                        xo_ref, kco_ref, vco_ref,
                        h_scr, g1_scr, z_scr, qm_scr, kn_scr, o_scr):
    n_tok = x_ref.shape[0]
    seq_len = n_tok // SEQ_BLK
    win = kc_ref.shape[1]

    def modulate(b, carry):
        rows = pl.ds(pl.multiple_of(b * seq_len, seq_len), seq_len)
        h_scr[rows, :] = _norm_mod(x_ref[rows, :], nw_ref[...], _mod_row(mod_ref, 1, b), _mod_row(mod_ref, 0, b))
        g1_scr[rows, :] = jnp.broadcast_to(_mod_row(mod_ref, 2, b), (seq_len, D_MODEL))
        return carry

    lax.fori_loop(0, SEQ_BLK, modulate, 0, unroll=4)
    z_scr[...] = _dot(h_scr[...].astype(BF16), wqkv_ref[...])
    cos = cos_ref[...]
    sin = sin_ref[...]
    hmats = _head_lane_mats()
    half0 =lax.broadcasted_iota(jnp.int32, (n_tok, LANES), 1) < HEAD_DIM
    qscale = HEAD_DIM ** -0.5
    for jb in range(Q_W // LANES):
        y = _qk_norm_rope(z_scr[:, jb * LANES:(jb + 1) * LANES], qnw_ref[...], cos, sin, *hmats) * qscale
        qm_scr[2 * jb] = jnp.where(half0, y, 0.0)
        qm_scr[2 * jb + 1] = jnp.where(half0, 0.0, y)
    for p in range(KV_W // LANES):
        kn_scr[:, p * LANES:(p + 1) * LANES] = _qk_norm_rope(
            z_scr[:, Q_W + p * LANES:Q_W + (p + 1) * LANES], knw_ref[...], cos, sin, *hmats)

    n_q = 8 * seq_len
    tq = lax.broadcasted_iota(jnp.int32, (SEQ_BLK, n_q, 2 * win), 1) % seq_len
    cc = lax.broadcasted_iota(jnp.int32, (SEQ_BLK, n_q, 2 * win), 2)
    valid = ((cc < win) & (cc > tq)) | ((cc >= 2 * win - seq_len) & ((cc - (2 * win - seq_len)) <= tq))
    half0q = lax.broadcasted_iota(jnp.int32, (SEQ_BLK, n_q // 2, LANES), 2) < HEAD_DIM
    ones_kv = jnp.ones((SEQ_BLK, 2 * win, LANES), BF16)

    kc = kc_ref[...]
    vc = vc_ref[...]
    knew = jnp.concatenate([kc[:, seq_len:], kn_scr[...].reshape(SEQ_BLK, seq_len, KV_W)], axis=1)
    vnew = jnp.concatenate(
        [vc[:, seq_len:], z_scr[:, Q_W + KV_W:Q_W + 2 * KV_W].reshape(SEQ_BLK, seq_len, KV_W)], axis=1)
    kco_ref[...] = knew
    vco_ref[...] = vnew
    for p in range(KV_W // LANES):
        lanes = slice(p * LANES, (p + 1) * LANES)
        qs = jnp.concatenate([qm_scr[2 * (4 * p + i) + e].reshape(SEQ_BLK, seq_len, LANES)
                              for e in range(2) for i in range(4)], axis=1).astype(BF16)
        kk = jnp.concatenate([kc[:, :, lanes], knew[:, :, lanes]], axis=1).astype(BF16)
        vv = jnp.concatenate([vc[:, :, lanes], vnew[:, :, lanes]], axis=1).astype(BF16)
        s = jnp.einsum("bqd,bkd->bqk", qs, kk, preferred_element_type=F32)
        s = jnp.where(valid, s, -jnp.inf)
        sk = _sink_rows(sink_ref[p], seq_len)[None]
        mx = jnp.maximum(jnp.max(s, axis=-1, keepdims=True), sk)
        pr = jnp.exp(s - jnp.concatenate([mx, mx], axis=-1))
        o2 = jnp.einsum("bqk,bkd->bqd", pr.astype(BF16), jnp.concatenate([vv, ones_kv], axis=-1),
                        preferred_element_type=F32)
        o = o2[:, :, :LANES] * (1.0 / (o2[:, :, LANES:] + jnp.exp(sk - mx)))
        merged = jnp.where(half0q, o[:, 0:n_q // 2], o[:, n_q // 2:])
        for i in range(4):
            o_scr[:, (4 * p + i) * LANES:(4 * p + i + 1) * LANES] = (
                merged[:, i * seq_len:(i + 1) * seq_len].reshape(n_tok, LANES))

    y = _dot(o_scr[...].astype(BF16), wout_ref[...])
    xo_ref[...] = x_ref[...] + g1_scr[...] * y


def _attn_sample_call(x, mod, l, nw, wqkv, qnw, knw, cos, sin, sink, wout, kc, vc):
    n_rows = x.shape[0]
    _, n_seq, win, _ = kc.shape
    lj = l // 2
    seq_len = n_rows // n_seq
    blk = SEQ_BLK * seq_len
    nb = n_seq // SEQ_BLK
    const2 = lambda i: (0, 0)
    return pl.pallas_call(
        _attn_sample_kernel,
        input_output_aliases={10: 1, 11: 2},
        grid=(nb,),
        in_specs=[
            pl.BlockSpec((blk, D_MODEL), lambda i: (i, 0)),
            pl.BlockSpec((None, N_MOD, SEQ_BLK, D_MODEL), lambda i: (l, 0, i, 0)),
            pl.BlockSpec((None, 1, D_MODEL), lambda i: (l, 0, 0)),
            pl.BlockSpec((D_MODEL, Q_W + 2 * KV_W), const2),
            pl.BlockSpec((1, LANES), const2),
            pl.BlockSpec((1, LANES), const2),
            pl.BlockSpec((blk, LANES), const2),
            pl.BlockSpec((blk, LANES), const2),
            pl.BlockSpec((2, SUBLANES, LANES), lambda i: (0, 0, 0)),
            pl.BlockSpec((Q_W, D_MODEL), const2),
            pl.BlockSpec((None, SEQ_BLK, win, KV_W), lambda i: (lj, i, 0, 0)),
            pl.BlockSpec((None, SEQ_BLK, win, KV_W), lambda i: (lj, i, 0, 0)),
        ],
        out_specs=[
            pl.BlockSpec((blk, D_MODEL), lambda i: (i, 0)),
            pl.BlockSpec((None, SEQ_BLK, win, KV_W), lambda i: (lj, i, 0, 0)),
            pl.BlockSpec((None, SEQ_BLK, win, KV_W), lambda i: (lj, i, 0, 0)),
        ],
        out_shape=[
            jax.ShapeDtypeStruct((n_rows, D_MODEL), F32),
            jax.ShapeDtypeStruct(kc.shape, F32),
            jax.ShapeDtypeStruct(vc.shape, F32),
        ],
        scratch_shapes=[
            pltpu.VMEM((blk, D_MODEL), F32),
            pltpu.VMEM((blk, D_MODEL), F32),
            pltpu.VMEM((blk, Q_W + 2 * KV_W), F32),
            pltpu.VMEM((2 * Q_W // LANES, blk, LANES), F32),
            pltpu.VMEM((blk, KV_W), F32),
            pltpu.VMEM((blk, Q_W), F32),
        ],
        compiler_params=_cparams(1),
        name="attn_sample",
    )(x, mod, nw, wqkv, qnw, knw, cos, sin, sink, wout, kc, vc)


def _rope_tables(pos):
    half = HEAD_DIM // 2
    inv = ROPE_THETA ** (-jnp.arange(half, dtype=F32) / half)
    ang = pos.astype(F32)[:, None] * inv[None, :]
    cos = jnp.cos(ang)
    sin = jnp.sin(ang)
    return jnp.tile(cos, (1, 4)), jnp.concatenate([-sin, sin, -sin, sin], axis=1)


def _rope_tables_range(n):
    half = HEAD_DIM // 2
    inv = ROPE_THETA ** (-jnp.arange(half, dtype=F32) / half)
    a_hi = (jnp.arange(n // WINDOW, dtype=jnp.int32) * WINDOW).astype(F32)[:, None] * inv[None, :]
    a_lo = jnp.arange(WINDOW, dtype=jnp.int32).astype(F32)[:, None] * inv[None, :]
    ch, sh = jnp.cos(a_hi)[:, None, :], jnp.sin(a_hi)[:, None, :]
    cl, sl = jnp.cos(a_lo)[None], jnp.sin(a_lo)[None]
    cos = (ch * cl - sh * sl).reshape(n, half)
    sin = (sh * cl + ch * sl).reshape(n, half)
    return jnp.tile(cos, (1, 4)), jnp.concatenate([-sin, sin, -sin, sin], axis=1)


def _prep_attn(w_qkv, q_norm, k_norm, sink, w_out):
    perm = np.asarray(HEAD_PERM)
    wq = w_qkv[:, :Q_W].reshape(D_MODEL, ATT_HEADS, HEAD_DIM)[:, perm].reshape(D_MODEL, Q_W)
    wqkv = jnp.concatenate([wq, w_qkv[:, Q_W:]], axis=1).astype(BF16)
    wout = w_out.reshape(ATT_HEADS, HEAD_DIM, D_MODEL)[perm].reshape(Q_W, D_MODEL).astype(BF16)
    qnw = jnp.tile(q_norm, 2)[None]
    knw = jnp.tile(k_norm, 2)[None]
    idx = np.asarray([[perm[2 * (4 * p + i) + e] for e in range(2) for i in range(4)] for p in range(2)])
    sink_arr = jnp.broadcast_to(sink[idx][:, :, None], (2, SUBLANES, LANES)).astype(F32)
    return wqkv, qnw, knw, sink_arr, wout


def kernel(x_prompt, x_sample, c_prompt, c_sample, state_mlstm_C, state_mlstm_n, state_mlstm_m, state_sconv, cache_win_k, cache_win_v, state_ffn_conv, norm1, norm2, w_ada, b_ada, a_w_in, a_b_if, a_out_norm, a_conv_w, a_w_out, c_w_qkv, c_q_norm, c_k_norm, c_sink, c_w_out, f_w_up, f_conv_w, f_w_down):
    B, S, _ = x_prompt.shape
    NS, SL, _ = x_sample.shape
    assert S % TM == 0 and NS % SEQ_BLK == 0 and SEQ_BLK * SL == CHUNK and SL == SUBLANES

    assert B <= SUBLANES
    c_all = jnp.concatenate([c_sample, c_prompt, jnp.zeros((SUBLANES - B, D_MODEL), F32)], axis=0)
    mod = _ada_call(c_all, w_ada, b_ada)

    win_all, kg_all = _prep_win_call(a_w_in)
    wkg_all = jnp.swapaxes(kg_all, 1, 2)
    b_i, b_f = a_b_if[:, :ML_HEADS], a_b_if[:, ML_HEADS:]
    lane_pad = jnp.zeros((a_b_if.shape[0], LANES - 2 * ML_HEADS), F32)
    bifc_all = jnp.stack([jnp.concatenate([b_i, b_i, lane_pad], axis=1),
                          jnp.concatenate([b_f, b_f, lane_pad], axis=1)], axis=1)
    bifr_all = jnp.broadcast_to(jnp.concatenate([b_i, b_i, b_f, b_f], axis=1)[:, :, None],
                                (a_b_if.shape[0], 2 * SUBLANES, LANES))
    wout_a_all = _cast_call(a_w_out)
    wup_all = _cast_call(f_w_up)
    wdn_all = _cast_call(f_w_down)
    onw_all = a_out_norm.reshape(-1, 1, ML_WIDTH)
    norm1_r = norm1.reshape(DEPTH, 1, D_MODEL)
    norm2_r = norm2.reshape(DEPTH, 1, D_MODEL)
    win_buf = cache_win_k.shape[2]
    kc_all = cache_win_k.reshape(-1, NS, win_buf, KV_W)
    vc_all = cache_win_v.reshape(-1, NS, win_buf, KV_W)

    cos_p, sin_p = _rope_tables_range(S)
    cos_s, sin_s = _rope_tables(PAST_LEN + jnp.arange(SL, dtype=jnp.int32))
    cos_s = jnp.tile(cos_s, (SEQ_BLK, 1))
    sin_s = jnp.tile(sin_s, (SEQ_BLK, 1))

    xp = x_prompt
    xs = x_sample.reshape(NS * SL, D_MODEL)
    p_C, p_n, p_m, p_sc, p_wk, p_wv, p_ffn = [], [], [], [], [], [], []
    s_n, s_m, s_sc = [], [], []
    s_C = s_ffn = None
    s_wk, s_wv = kc_all, vc_all

    for l in range(DEPTH):
        if l % 2 == 0:
            i = l // 2
            even_w = (norm1_r, win_all, wkg_all, bifc_all, bifr_all, onw_all, a_conv_w, wout_a_all)
            xp, co, no, mo, sco = _mix_even_prompt_call(xp, mod, NS, l, *even_w)
            p_C.append(co)
            p_n.append(no[:, :, :, 0])
            p_m.append(mo[:, :ML_HEADS, 0])
            p_sc.append(sco[:, SUBLANES - 2:, :])

            n_tok = jnp.repeat(state_mlstm_n[i].transpose(1, 0, 2), SL, axis=1)
            m_rep = jnp.repeat(state_mlstm_m[i], SL, axis=0)
            m_col = jnp.concatenate([m_rep, m_rep, jnp.zeros((NS * SL, LANES - 2 * ML_HEADS), F32)], axis=1)
            m_row = jnp.concatenate([m_rep, m_rep], axis=1).T
            sc_pad = jnp.pad(state_sconv[i], ((0, 0), (0, SL - 2), (0, 0))).reshape(NS * SL, SC_WIDTH)
            xs, s_C, no, mo, sco = _mix_even_sample_call(xs, mod, l, *even_w, state_mlstm_C, n_tok,
                                                         jnp.swapaxes(n_tok, 1, 2), m_col, m_row, sc_pad, s_C)
            s_n.append(no.transpose(1, 0, 2))
            s_m.append(mo[:ML_HEADS, ::SL].T)
            s_sc.append(sco.reshape(NS, SL, SC_WIDTH)[:, :2])
        else:
            j = l // 2
            wqkv, qnw, knw, sink, wout = _prep_attn(c_w_qkv[j], c_q_norm[j], c_k_norm[j], c_sink[j], c_w_out[j])
            xp, ko, vo = _attn_prompt_call(xp, mod, NS, l, norm1_r, wqkv, qnw, knw, cos_p, sin_p, sink, wout)
            p_wk.append(ko.reshape(B, WINDOW, KV_HEADS, HEAD_DIM))
            p_wv.append(vo.reshape(B, WINDOW, KV_HEADS, HEAD_DIM))
            xs, s_wk, s_wv = _attn_sample_call(xs, mod, l, norm1_r, wqkv, qnw, knw, cos_s, sin_s, sink, wout,
                                               s_wk, s_wv)

        xp, st = _ffn_prompt_call(xp, mod, NS, l, norm2_r, wup_all, f_conv_w, wdn_all)
        p_ffn.append(st[:, SUBLANES - 2:, :])
        xs, s_ffn = _ffn_sample_call(xs, mod, l, norm2_r, state_ffn_conv, wup_all, f_conv_w, wdn_all, s_ffn)

    kv_shape = (-1, NS, win_buf, KV_HEADS, HEAD_DIM)
    return (xp, xs.reshape(NS, SL, D_MODEL),
            jnp.stack(p_C), jnp.stack(p_n), jnp.stack(p_m), jnp.stack(p_sc),
            jnp.stack(p_wk), jnp.stack(p_wv), jnp.stack(p_ffn),
            s_C, jnp.stack(s_n), jnp.stack(s_m), jnp.stack(s_sc),
            s_wk.reshape(kv_shape), s_wv.reshape(kv_shape), s_ffn)
```

```python
import functools

import jax
import jax.numpy as jnp
import numpy as np
from jax import lax
from jax.experimental import pallas as pl
from jax.experimental.pallas import tpu as pltpu

F32 = jnp.float32
BF16 = jnp.bfloat16

D_MODEL = 1024
DEPTH = 4
PAST_LEN = 8192
ML_HEADS = 4
ML_DK = 128
ML_DV = 128
ML_WIDTH = ML_HEADS * ML_DV
SC_WIDTH = D_MODEL // 2
ATT_HEADS = 16
KV_HEADS = 4
HEAD_DIM = 64
WINDOW = 128
ROPE_THETA = 10000.0
D_FF = 2816
EPS = 1e-6

LANES = 128
SUBLANES = 8
VMEM_LIMIT = 56 * 1024 * 1024

TM = 1024
TM_FFN = 1024
CHUNK = 128
SEQ_BLK = 16
FC = 256
NCH = D_FF // FC
FS_GATE = 2
FC_S = D_FF // FS_GATE
Q_COL = 0
V_COL = Q_COL + ML_WIDTH
O_COL = V_COL + ML_WIDTH
G1_COL = O_COL + ML_WIDTH
G2_COL = G1_COL + LANES
B_COL = G2_COL + LANES
C_COL = B_COL + SC_WIDTH
X_COL = C_COL + SC_WIDTH
IN_W = X_COL + SC_WIDTH
KG_ROWS = ML_WIDTH + 16
SRC_K = ML_WIDTH
SRC_V = 2 * ML_WIDTH
SRC_G = 4 * ML_WIDTH
SRC_B = SRC_G + 2 * ML_HEADS
Q_W = ATT_HEADS * HEAD_DIM
KV_W = KV_HEADS * HEAD_DIM
HEAD_PERM = (0, 4, 1, 5, 2, 6, 3, 7, 8, 12, 9, 13, 10, 14, 11, 15)


def _cparams(n_axes):
    return pltpu.CompilerParams(dimension_semantics=("arbitrary",) * n_axes,
                                vmem_limit_bytes=VMEM_LIMIT)


def _stacked_call(kernel, n_in, out_idx, prev, slab, slab_spec, **kw):
    specs = list(kw.pop("in_specs"))
    out_specs = list(kw.pop("out_specs"))
    n_slabs = kw["out_shape"][out_idx].shape[0]
    if prev is None:
        out_specs[out_idx] = slab_spec(n_slabs, 0)

        def body(*refs):
            refs = list(refs)
            whole = refs[n_in + out_idx]
            for s in range(n_slabs):
                if s != slab:
                    whole[s] = jnp.zeros(whole.shape[1:], whole.dtype)
            refs[n_in + out_idx] = whole.at[slab]
            return kernel(*refs)

        return pl.pallas_call(body, in_specs=specs, out_specs=out_specs, **kw)

    out_specs[out_idx] = slab_spec(None, slab)

    def body(*refs):
        return kernel(*refs[:n_in], *refs[n_in + 1:])

    call = pl.pallas_call(body, in_specs=specs + [pl.BlockSpec(memory_space=pl.ANY)], out_specs=out_specs,
                          input_output_aliases={n_in: out_idx}, **kw)
    return lambda *args: call(*args, prev)


def _dot(a, b):
    return jnp.dot(a, b, preferred_element_type=F32)


def _dot_nt(a, b):
    return lax.dot_general(a, b, (((1,), (1,)), ((), ())), preferred_element_type=F32)


def _dot_tn(a, b):
    return lax.dot_general(a, b, (((0,), (0,)), ((), ())), preferred_element_type=F32)


def _dot_exact01(m, a):
    a1 = a.astype(BF16)
    r1 = a - a1.astype(F32)
    a2 = r1.astype(BF16)
    a3 = (r1 - a2.astype(F32)).astype(BF16)
    return _dot(m, a1) + _dot(m, a2) + _dot(m, a3)


def _norm_mod(x, nw, sc, sh):
    ms = jnp.mean(x * x, axis=-1, keepdims=True)
    return (x * lax.rsqrt(ms + EPS) * nw) * (1.0 + sc) + sh


def _sigmoid(x):
    return 1.0 / (1.0 + jnp.exp(-x))


def _log_sigmoid(x):
    return jnp.minimum(x, 0.0) - jnp.log(1.0 + jnp.exp(-jnp.abs(x)))


N_MOD = 6


ADA_KINDS = 3


def _ada_kernel(c_ref, w_ref, b_ref, o_ref):
    c = c_ref[...]
    s = (c * _sigmoid(c)).astype(BF16)
    for k in range(ADA_KINDS):
        cols = slice(k * D_MODEL, (k + 1) * D_MODEL)
        o_ref[k] = _dot(s, w_ref[:, cols].astype(BF16)) + b_ref[:, cols]


def _ada_call(c_all, w_ada, b_ada):
    rows = c_all.shape[0]
    return pl.pallas_call(
        _ada_kernel,
        grid=(DEPTH, N_MOD // ADA_KINDS),
        in_specs=[
            pl.BlockSpec((rows, D_MODEL), lambda l, k: (0, 0)),
            pl.BlockSpec((None, D_MODEL, ADA_KINDS * D_MODEL), lambda l, k: (l, 0, k)),
            pl.BlockSpec((None, 1, ADA_KINDS * D_MODEL), lambda l, k: (l, 0, k)),
        ],
        out_specs=pl.BlockSpec((None, ADA_KINDS, rows, D_MODEL), lambda l, k: (l, k, 0, 0)),
        out_shape=jax.ShapeDtypeStruct((DEPTH, N_MOD, rows, D_MODEL), F32),
        compiler_params=_cparams(2),
        name="adaln_mod",
    )(c_all, w_ada, b_ada.reshape(DEPTH, 1, N_MOD * D_MODEL))


CAST_ROWS = 256


def _cast_kernel(w_ref, o_ref):
    o_ref[...] = w_ref[...].astype(BF16)


CAST_BLOCK_BYTES = 6 * 1024 * 1024


def _cast_call(w):
    n_l, rows, cols = w.shape
    fits = [r for r in range(2 * SUBLANES, rows + 1, 2 * SUBLANES)
            if rows % r == 0 and r * cols * 4 <= CAST_BLOCK_BYTES]
    tr = max(fits) if fits else rows
    return pl.pallas_call(
        _cast_kernel,
        grid=(n_l, rows // tr),
        in_specs=[pl.BlockSpec((None, tr, cols), lambda l, r: (l, r, 0))],
        out_specs=pl.BlockSpec((None, tr, cols), lambda l, r: (l, r, 0)),
        out_shape=jax.ShapeDtypeStruct(w.shape, BF16),
        compiler_params=_cparams(2),
        name="cast_bf16",
    )(w)


def _prep_win_kernel(w_ref, o_ref, kg_ref):
    o_ref[:, Q_COL:Q_COL + ML_WIDTH] = w_ref[:, 0:ML_WIDTH].astype(BF16)
    kg_ref[:, 0:ML_WIDTH] = w_ref[:, SRC_K:SRC_V].astype(BF16)
    o_ref[:, V_COL:G1_COL] = w_ref[:, SRC_V:SRC_G].astype(BF16)
    tail = w_ref[:, SRC_G:]
    ig = tail[:, 0:ML_HEADS]
    fg = tail[:, ML_HEADS:2 * ML_HEADS]
    pad = jnp.zeros((ig.shape[0], LANES - 2 * ML_HEADS), F32)
    o_ref[:, G1_COL:G2_COL] = jnp.concatenate([ig, ig, pad], axis=1).astype(BF16)
    o_ref[:, G2_COL:B_COL] = jnp.concatenate([fg, fg, pad], axis=1).astype(BF16)
    o_ref[:, B_COL:IN_W] = tail[:, 2 * ML_HEADS:].astype(BF16)
    kg_ref[:, ML_WIDTH:KG_ROWS] = jnp.concatenate([ig, ig, fg, fg], axis=1).astype(BF16)


def _prep_win_call(a_w_in):
    n_l, _, in_a = a_w_in.shape
    return pl.pallas_call(
        _prep_win_kernel,
        grid=(n_l, D_MODEL // CAST_ROWS),
        in_specs=[pl.BlockSpec((None, CAST_ROWS, in_a), lambda l, r: (l, r, 0))],
        out_specs=[pl.BlockSpec((None, CAST_ROWS, IN_W), lambda l, r: (l, r, 0)),
                   pl.BlockSpec((None, CAST_ROWS, KG_ROWS), lambda l, r: (l, r, 0))],
        out_shape=[jax.ShapeDtypeStruct((n_l, D_MODEL, IN_W), BF16),
                   jax.ShapeDtypeStruct((n_l, D_MODEL, KG_ROWS), BF16)],
        compiler_params=_cparams(2),
        name="prep_w_in",
    )(a_w_in)


def _mod_row(mod_ref, kind, b):
    return mod_ref[kind, pl.ds(b, 1), :]


def _ffn_prompt_kernel(x_ref, mod_ref, nw_ref, wup_ref, cw_ref, wdn_ref,
                       xo_ref, st_ref, h_scr, act_scr, carry_scr):
    b = pl.program_id(0)
    t = pl.program_id(1)

    @pl.when(t == 0)
    def _():
        carry_scr[...] = jnp.zeros_like(carry_scr)

    x = x_ref[0]
    h_scr[...] = _norm_mod(x, nw_ref[...], _mod_row(mod_ref, 4, b), _mod_row(mod_ref, 3, b)).astype(BF16)
    for j in range(NCH):
        ys = []
        for col in (j * FC, D_FF + j * FC):
            cols = slice(col, col + FC)
            u = _dot(h_scr[...], wup_ref[:, cols])
            ys.append(_conv3_rows(u, carry_scr[:, cols], cw_ref[:, cols]))
            carry_scr[:, cols] = u[TM_FFN - SUBLANES:TM_FFN]
        g = ys[0]
        act_scr[:, j * FC:(j + 1) * FC] = (g * _sigmoid(g) * ys[1]).astype(BF16)
    y = _dot(act_scr[...], wdn_ref[...])
    xo_ref[0] = x + _mod_row(mod_ref, 5, b) * y
    st_ref[0] = carry_scr[...]


def _ffn_prompt_call(x, mod, n_seq_rows, l, nw, wup, cw, wdn):
    B, S, _ = x.shape
    assert S % TM_FFN == 0
    nt = S // TM_FFN
    once = pl.Buffered(1)
    return pl.pallas_call(
        _ffn_prompt_kernel,
        grid=(B, nt),
        in_specs=[
            pl.BlockSpec((1, TM_FFN, D_MODEL), lambda b, t: (b, t, 0)),
            pl.BlockSpec((None, N_MOD, SUBLANES, D_MODEL), lambda b, t: (l, 0, n_seq_rows // SUBLANES, 0)),
            pl.BlockSpec((None, 1, D_MODEL), lambda b, t: (l, 0, 0)),
            pl.BlockSpec((None, D_MODEL, 2 * D_FF), lambda b, t: (l, 0, 0), pipeline_mode=once),
            pl.BlockSpec((None, 3, 2 * D_FF), lambda b, t: (l, 0, 0)),
            pl.BlockSpec((None, D_FF, D_MODEL), lambda b, t: (l, 0, 0), pipeline_mode=once),
        ],
        out_specs=[
            pl.BlockSpec((1, TM_FFN, D_MODEL), lambda b, t: (b, t, 0)),
            pl.BlockSpec((1, SUBLANES, 2 * D_FF), lambda b, t: (b, 0, 0)),
        ],
        out_shape=[
            jax.ShapeDtypeStruct((B, S, D_MODEL), F32),
            jax.ShapeDtypeStruct((B, SUBLANES, 2 * D_FF), F32),
        ],
        scratch_shapes=[
            pltpu.VMEM((TM_FFN, D_MODEL), BF16),
            pltpu.VMEM((TM_FFN, D_FF), BF16),
            pltpu.VMEM((SUBLANES, 2 * D_FF), F32),
        ],
        compiler_params=_cparams(2),
        name="ffn_prompt",
    )(x, mod, nw, wup, cw, wdn)


def _ffn_sample_kernel(x_ref, mod_ref, nw_ref, s_ref, w_ref, c_ref, wdn_ref,
                       xo_ref, so_ref, h_scr, hb_scr, g2_scr, acc_scr, yg_scr):
    j = pl.program_id(1)
    n_seq = s_ref.shape[0]
    n_rows = x_ref.shape[0]
    n_t = n_rows // n_seq

    @pl.when(j == 0)
    def _():
        def modulate(b, carry):
            rows = pl.ds(pl.multiple_of(b * n_t, n_t), n_t)
            h_scr[rows, :] = _norm_mod(x_ref[rows, :], nw_ref[...], _mod_row(mod_ref, 4, b), _mod_row(mod_ref, 3, b))
            g2_scr[rows, :] = jnp.broadcast_to(_mod_row(mod_ref, 5, b), (n_t, D_MODEL))
            return carry

        lax.fori_loop(0, n_seq, modulate, 0, unroll=8)
        hb_scr[...] = h_scr[...].astype(BF16)
        acc_scr[...] = jnp.zeros_like(acc_scr)

    sub = lax.broadcasted_iota(jnp.int32, (n_seq, n_t, FC_S), 1)
    u3 = _dot(hb_scr[...], w_ref[...]).reshape(n_seq, n_t, FC_S)
    cw = c_ref[...]
    p0 = jnp.broadcast_to(s_ref[:, 0:1, :], (n_seq, n_t, FC_S))
    p1 = jnp.broadcast_to(s_ref[:, 1:2, :], (n_seq, n_t, FC_S))
    s1 = jnp.where(sub < 1, p1, pltpu.roll(u3, 1, 1))
    s2 = jnp.where(sub < 1, p0, jnp.where(sub < 2, p1, pltpu.roll(u3, 2, 1)))
    y = (s2 * cw[0:1] + s1 * cw[1:2] + u3 * cw[2:3]).reshape(n_rows, FC_S)
    so_ref[...] = pltpu.roll(u3, 2, 1)[:, 0:2, :]

    @pl.when(j < FS_GATE)
    def _():
        yg_scr[j] = y

    @pl.when(j >= FS_GATE)
    def _():
        g = yg_scr[j - FS_GATE]
        acc_scr[...] += _dot((g * _sigmoid(g) * y).astype(BF16), wdn_ref[...])

    @pl.when(j == 2 * FS_GATE - 1)
    def _():
        xo_ref[...] = x_ref[...] + g2_scr[...] * acc_scr[...]


def _ffn_sample_call(x, mod, l, nw, st, wup, cw, wdn, so_prev):
    n_seq = st.shape[1] // 2
    n_rows = x.shape[0] // 2
    return _stacked_call(
        _ffn_sample_kernel, 7, 1, so_prev, l,
        lambda lead, idx: pl.BlockSpec((lead, n_seq, 2, FC_S), lambda hf, j: (idx, hf, 0, j)),
        grid=(2, 2 * FS_GATE),
        in_specs=[
            pl.BlockSpec((n_rows, D_MODEL), lambda hf, j: (hf, 0)),
            pl.BlockSpec((None, N_MOD, n_seq, D_MODEL), lambda hf, j: (l, 0, hf, 0)),
            pl.BlockSpec((None, 1, D_MODEL), lambda hf, j: (l, 0, 0)),
            pl.BlockSpec((None, n_seq, 2, FC_S), lambda hf, j: (l, hf, 0, j)),
            pl.BlockSpec((None, D_MODEL, FC_S), lambda hf, j: (l, 0, j)),
            pl.BlockSpec((None, 3, FC_S), lambda hf, j: (l, 0, j)),
            pl.BlockSpec((None, FC_S, D_MODEL), lambda hf, j: (l, jnp.maximum(j - FS_GATE, 0), 0)),
        ],
        out_specs=[
            pl.BlockSpec((n_rows, D_MODEL), lambda hf, j: (hf, 0)),
            None,
        ],
        out_shape=[
            jax.ShapeDtypeStruct(x.shape, F32),
            jax.ShapeDtypeStruct((DEPTH, st.shape[1], 2, 2 * D_FF), F32),
        ],
        scratch_shapes=[
            pltpu.VMEM((n_rows, D_MODEL), F32),
            pltpu.VMEM((n_rows, D_MODEL), BF16),
            pltpu.VMEM((n_rows, D_MODEL), F32),
            pltpu.VMEM((n_rows, D_MODEL), F32),
            pltpu.VMEM((FS_GATE, n_rows, FC_S), F32),
        ],
        compiler_params=_cparams(2),
        name="ffn_sample",
    )(x, mod, nw, st, wup, cw, wdn)


def _chunk_consts(seq_len):
    r = lax.broadcasted_iota(jnp.int32, (CHUNK, CHUNK), 0)
    c = lax.broadcasted_iota(jnp.int32, (CHUNK, CHUNK), 1)
    if seq_len >= CHUNK:
        same = r >= 0
    else:
        same = (r // seq_len) == (c // seq_len)
    mask = same & (c <= r)
    lmat = mask.astype(BF16)
    lmat_t = (same & (r <= c)).astype(BF16)
    return mask, lmat, lmat_t, same.astype(BF16)


def _dot_exact01_r(a, m):
    a1 = a.astype(BF16)
    r1 = a - a1.astype(F32)
    a2 = r1.astype(BF16)
    a3 = (r1 - a2.astype(F32)).astype(BF16)
    return _dot(a1, m) + _dot(a2, m) + _dot(a3, m)


def _pick_cols_t(a, sel):
    a1 = a.astype(BF16)
    r1 = a - a1.astype(F32)
    a2 = r1.astype(BF16)
    a3 = (r1 - a2.astype(F32)).astype(BF16)
    return _dot_nt(sel, a1) + _dot_nt(sel, a2) + _dot_nt(sel, a3)


def _seq_max_lanes(x, seq_len):
    n = x.shape[1]
    pos = lax.broadcasted_iota(jnp.int32, x.shape, 1)
    d = 1
    while d < seq_len:
        partner = jnp.where((pos & d) == 0, pltpu.roll(x, n - d, 1), pltpu.roll(x, d, 1))
        x = jnp.maximum(x, partner)
        d *= 2
    return x


def _seq_prefix_max_rows(x, seq_len):
    pos = lax.broadcasted_iota(jnp.int32, x.shape, 0) & (seq_len - 1)
    d = 1
    while d < seq_len:
        x = jnp.where(pos >= d, jnp.maximum(x, pltpu.roll(x, d, 0)), x)
        d *= 2
    return x


def _seq_last_row(x, seq_len):
    n, w = x.shape
    if seq_len >= n:
        return jnp.broadcast_to(x[n - 1:n], x.shape)
    x3 = x.reshape(n // seq_len, seq_len, w)
    return jnp.broadcast_to(x3[:, seq_len - 1:seq_len, :], x3.shape).reshape(n, w)


def _gates_rows_pre(gt, bias_r, lmat_t, tot, seq_len):
    ig = gt[0:SUBLANES] + bias_r[0:SUBLANES]
    lf = _log_sigmoid(gt[SUBLANES:] + bias_r[SUBLANES:])
    b = _dot_exact01_r(lf, lmat_t)
    bl = _dot_exact01_r(lf, tot)
    v = ig - b
    return bl, v, _seq_max_lanes(v, seq_len)


def _gates_rows_post(pre, mp_r):
    bl, v, vm = pre
    mn = bl + jnp.maximum(mp_r, vm)
    return jnp.exp(bl + v - mn), jnp.exp(bl + mp_r - mn), mn


def _gates_rows(gt, bias_r, mp_r, lmat_t, tot, seq_len):
    pre = _gates_rows_pre(gt, bias_r, lmat_t, tot, seq_len)
    return (pre[1],) + _gates_rows_post(pre, mp_r)


def _gates_cols_pre(g1, g2, bias_c, lmat, seq_len):
    lane = lax.broadcasted_iota(jnp.int32, g1.shape, 1)
    ig = g1 + bias_c[0:1]
    lf = jnp.where(lane < 2 * ML_HEADS, _log_sigmoid(g2 + bias_c[1:2]), 0.0)
    b = _dot_exact01(lmat, lf)
    return b, _seq_prefix_max_rows(ig - b, seq_len)


def _gates_cols_post(b, cm, mp_c):
    g = b + mp_c
    mt = jnp.maximum(b + cm, g)
    return b - mt, jnp.exp(g - mt), jnp.exp(-mt)


def _gates_cols(g1, g2, bias_c, mp_c, lmat, seq_len):
    b, cm = _gates_cols_pre(g1, g2, bias_c, lmat, seq_len)
    return (b, cm) + _gates_cols_post(b, cm, mp_c)


def _outer_sum_lhs(u):
    lane = lax.broadcasted_iota(jnp.int32, u.shape, 1)
    hi = u.astype(BF16).astype(F32)
    lo = u - hi
    return jnp.where(lane < ML_HEADS, hi, jnp.where(lane < 2 * ML_HEADS, lo,
                     jnp.where(lane < 4 * ML_HEADS, 1.0, 0.0))).astype(BF16)


def _outer_sum_rhs(v_r, hd):
    row = lax.broadcasted_iota(jnp.int32, v_r.shape, 0)
    hi = v_r.astype(BF16).astype(F32)
    lo = v_r - hi
    pick = (row == hd) | (row == ML_HEADS + hd)
    top = jnp.where(pick, 1.0, 0.0)
    bot = jnp.where(row == hd, hi, jnp.where(row == ML_HEADS + hd, lo, 0.0))
    r16 = jnp.concatenate([top, bot], axis=0).astype(BF16)
    return jnp.concatenate([r16, jnp.zeros((LANES - 2 * SUBLANES, v_r.shape[1]), BF16)], axis=0)


def _decayed_scores(q, kt, lhsc, v_r, hd, mask):
    e = _dot(lhsc, _outer_sum_rhs(v_r, hd))
    return _dot(q.astype(BF16), kt.astype(BF16)) * jnp.where(mask, jnp.exp(e), 0.0)


def _conv3_rows(cx, prev8, cw):
    n = cx.shape[1]
    row = lax.broadcasted_iota(jnp.int32, (SUBLANES, n), 0)
    s1 = pltpu.roll(cx, 1, 0)
    s2 = pltpu.roll(cx, 2, 0)
    f1 = jnp.where(row < 1, pltpu.roll(prev8, 1, 0), s1[0:SUBLANES])
    f2 = jnp.where(row < 2, pltpu.roll(prev8, 2, 0), s2[0:SUBLANES])
    s1 = jnp.concatenate([f1, s1[SUBLANES:]], axis=0)
    s2 = jnp.concatenate([f2, s2[SUBLANES:]], axis=0)
    return s2 * cw[0:1] + s1 * cw[1:2] + cx * cw[2:3]


def _mlstm_out_norm(hm, zo, onw):
    ms = jnp.mean(hm * hm, axis=-1, keepdims=True)
    return hm * lax.rsqrt(ms + EPS) * onw * _sigmoid(zo)


def _even_tail(z_scr, hm_scr, cat_scr, onw_ref):
    for hd in range(ML_HEADS):
        col = slice(hd * ML_DV, (hd + 1) * ML_DV)
        zo = z_scr[:, O_COL + hd * ML_DV:O_COL + (hd + 1) * ML_DV]
        cat_scr[:, col] = _mlstm_out_norm(hm_scr[:, col], zo, onw_ref[:, col]).astype(BF16)


def _mix_even_prompt_kernel(x_ref, mod_ref, nw_ref, win_ref, wkg_ref, bifc_ref, bifr_ref, onw_ref, cw_ref,
                            wout_ref, xo_ref, co_ref, no_ref, mo_ref, sco_ref,
                            z_scr, zt_scr, hm_scr, cat_scr, cn_scr, mrow_scr, mlane_scr, cc_scr):
    t = pl.program_id(1)

    @pl.when(t == 0)
    def _():
        cn_scr[...] = jnp.zeros_like(cn_scr)
        mrow_scr[...] = jnp.zeros_like(mrow_scr)
        mlane_scr[...] = jnp.zeros_like(mlane_scr)
        cc_scr[...] = jnp.zeros_like(cc_scr)

    x = x_ref[0]
    bi = pl.program_id(0)
    h = _norm_mod(x, nw_ref[...], _mod_row(mod_ref, 1, bi), _mod_row(mod_ref, 0, bi)).astype(BF16)
    z_scr[...] = _dot(h, win_ref[...])
    zt_scr[...] = _dot_nt(wkg_ref[...], h)

    mask, lmat, lmat_t, tot = _chunk_consts(CHUNK)
    scale = ML_DK ** -0.5
    ones_v = jnp.ones((CHUNK, ML_DV), BF16)

    n_ch = TM // CHUNK
    chunk_rows = [slice(c * CHUNK, (c + 1) * CHUNK) for c in range(n_ch)]
    pre_r = [_gates_rows_pre(zt_scr[ML_WIDTH:KG_ROWS, rows], bifr_ref[...], lmat_t, tot, CHUNK)
             for rows in chunk_rows]
    pre_c = [_gates_cols_pre(z_scr[rows, G1_COL:G1_COL + LANES], z_scr[rows, G2_COL:G2_COL + LANES],
                             bifc_ref[...], lmat, CHUNK) for rows in chunk_rows]
    mp_r = mrow_scr[...]
    mp_c = mlane_scr[0:1, :]
    post_r, post_c = [], []
    for c in range(n_ch):
        post_r.append(_gates_rows_post(pre_r[c], mp_r))
        mp_r = post_r[c][2]
        b, cm = pre_c[c]
        post_c.append(_gates_cols_post(b, cm, mp_c))
        mp_c = b[CHUNK - 1:CHUNK] + jnp.maximum(mp_c, cm[CHUNK - 1:CHUNK])
    mrow_scr[...] = mp_r
    mlane_scr[...] = jnp.broadcast_to(mp_c, (SUBLANES, LANES))

    intra, upd, wgq = {}, {}, {}
    for c, rows in enumerate(chunk_rows):
        u, wg, _ = post_c[c]
        lhsc = _outer_sum_lhs(u)
        for hd in range(ML_HEADS):
            q = z_scr[rows, Q_COL + hd * ML_DK:Q_COL + (hd + 1) * ML_DK]
            v = z_scr[rows, V_COL + hd * ML_DV:V_COL + (hd + 1) * ML_DV]
            kt = zt_scr[hd * ML_DK:(hd + 1) * ML_DK, rows] * scale
            s = _decayed_scores(q, kt, lhsc, pre_r[c][1], hd, mask)
            v1 = jnp.concatenate([v.astype(BF16), ones_v], axis=1)
            intra[c, hd] = _dot(s.astype(BF16), v1)
            upd[c, hd] = _dot((kt * post_r[c][0][hd:hd + 1, :]).astype(BF16), v1)
            wgq[c, hd] = (q * wg[:, hd:hd + 1]).astype(BF16)

    for hd in range(ML_HEADS):
        cn = cn_scr[hd]
        for c, rows in enumerate(chunk_rows):
            out = intra[c, hd] + _dot(wgq[c, hd], cn.astype(BF16))
            r = 1.0 / jnp.maximum(jnp.abs(out[:, ML_DV:]), post_c[c][2][:, hd:hd + 1])
            hm_scr[rows, hd * ML_DV:(hd + 1) * ML_DV] = out[:, :ML_DV] * r
            wc = post_r[c][1][hd:hd + 1, :]
            cn = jnp.concatenate([wc, wc], axis=1) * cn + upd[c, hd]
        cn_scr[hd] = cn

    _even_tail(z_scr, hm_scr, cat_scr, onw_ref)
    cx = z_scr[:, C_COL:C_COL + SC_WIDTH] * z_scr[:, X_COL:X_COL + SC_WIDTH]
    u = _conv3_rows(cx, cc_scr[...], cw_ref[...])
    cc_scr[...] = cx[TM - SUBLANES:TM]
    cat_scr[:, ML_WIDTH:] = (z_scr[:, B_COL:B_COL + SC_WIDTH] * u).astype(BF16)

    y = _dot(cat_scr[...], wout_ref[...])
    xo_ref[0] = x + _mod_row(mod_ref, 2, bi) * y
    for hd in range(ML_HEADS):
        co_ref[0, hd] = cn_scr[hd, :, 0:ML_DV]
        no_ref[0, hd] = cn_scr[hd, :, ML_DV:]
    mo_ref[0] = mrow_scr[...]
    sco_ref[0] = cc_scr[...]


def _even_weight_specs(i, idx):
    once = pl.Buffered(1)
    return [
        pl.BlockSpec((None, D_MODEL, IN_W), idx, pipeline_mode=once),
        pl.BlockSpec((None, KG_ROWS, D_MODEL), idx, pipeline_mode=once),
        pl.BlockSpec((None, 2, LANES), idx),
        pl.BlockSpec((None, 2 * SUBLANES, LANES), idx),
        pl.BlockSpec((None, 1, ML_WIDTH), idx),
        pl.BlockSpec((None, 3, SC_WIDTH), idx),
        pl.BlockSpec((None, ML_WIDTH + SC_WIDTH, D_MODEL), idx, pipeline_mode=once),
    ]


def _mix_even_prompt_call(x, mod, n_seq_rows, l, nw, win, wkg, bifc, bifr, onw, cw, wout):
    B, S, _ = x.shape
    nt = S // TM
    i = l // 2
    return pl.pallas_call(
        _mix_even_prompt_kernel,
        grid=(B, nt),
        in_specs=[
            pl.BlockSpec((1, TM, D_MODEL), lambda b, t: (b, t, 0)),
            pl.BlockSpec((None, N_MOD, SUBLANES, D_MODEL), lambda b, t: (l, 0, n_seq_rows // SUBLANES, 0)),
            pl.BlockSpec((None, 1, D_MODEL), lambda b, t: (l, 0, 0)),
        ] + _even_weight_specs(i, lambda b, t: (i, 0, 0)),
        out_specs=[
            pl.BlockSpec((1, TM, D_MODEL), lambda b, t: (b, t, 0)),
            pl.BlockSpec((1, ML_HEADS, ML_DK, ML_DV), lambda b, t: (b, 0, 0, 0)),
            pl.BlockSpec((1, ML_HEADS, ML_DK, LANES), lambda b, t: (b, 0, 0, 0)),
            pl.BlockSpec((1, SUBLANES, LANES), lambda b, t: (b, 0, 0)),
            pl.BlockSpec((1, SUBLANES, SC_WIDTH), lambda b, t: (b, 0, 0)),
        ],
        out_shape=[
            jax.ShapeDtypeStruct((B, S, D_MODEL), F32),
            jax.ShapeDtypeStruct((B, ML_HEADS, ML_DK, ML_DV), F32),
            jax.ShapeDtypeStruct((B, ML_HEADS, ML_DK, LANES), F32),
            jax.ShapeDtypeStruct((B, SUBLANES, LANES), F32),
            jax.ShapeDtypeStruct((B, SUBLANES, SC_WIDTH), F32),
        ],
        scratch_shapes=[
            pltpu.VMEM((TM, IN_W), F32),
            pltpu.VMEM((KG_ROWS, TM), F32),
            pltpu.VMEM((TM, ML_WIDTH), F32),
            pltpu.VMEM((TM, ML_WIDTH + SC_WIDTH), BF16),
            pltpu.VMEM((ML_HEADS, ML_DK, ML_DV + LANES), F32),
            pltpu.VMEM((SUBLANES, LANES), F32),
            pltpu.VMEM((SUBLANES, LANES), F32),
            pltpu.VMEM((SUBLANES, SC_WIDTH), F32),
        ],
        compiler_params=_cparams(2),
        name="mix_even_prompt",
    )(x, mod, nw, win, wkg, bifc, bifr, onw, cw, wout)


def _mix_even_sample_kernel(x_ref, mod_ref, nw_ref, win_ref, wkg_ref, bifc_ref, bifr_ref, onw_ref, cw_ref,
                            wout_ref, c_ref, nt_ref, ntt_ref, mcol_ref, mrow_ref, sc_ref,
                            xo_ref, co_ref, no_ref, mo_ref, sco_ref,
                            h_scr, z_scr, zt_scr, hm_scr, cat_scr, g1_scr,
                            intra_scr, dpart_scr, pbe_scr, wgq_scr, kwt_scr, vb_scr, wcb_scr, inter_scr):
    n_tok = x_ref.shape[0]
    seq_len = n_tok // SEQ_BLK

    def modulate(b, carry):
        rows = pl.ds(pl.multiple_of(b * seq_len, seq_len), seq_len)
        h_scr[rows, :] = _norm_mod(x_ref[rows, :], nw_ref[...], _mod_row(mod_ref, 1, b), _mod_row(mod_ref, 0, b))
        g1_scr[rows, :] = jnp.broadcast_to(_mod_row(mod_ref, 2, b), (seq_len, D_MODEL))
        return carry

    lax.fori_loop(0, SEQ_BLK, modulate, 0, unroll=8)
    hb = h_scr[...].astype(BF16)
    z_scr[...] = _dot(hb, win_ref[...])
    zt_scr[...] = _dot_nt(wkg_ref[...], hb)

    mask, lmat, lmat_t, tot = _chunk_consts(seq_len)
    scale = ML_DK ** -0.5
    ones_v = jnp.ones((CHUNK, ML_DV), BF16)
    mp_r = mrow_ref[...]
    v_r, ws_r, wc_r, mn_r = _gates_rows(zt_scr[ML_WIDTH:KG_ROWS, :], bifr_ref[...], mp_r, lmat_t, tot, seq_len)
    mo_ref[...] = mn_r
    mp_c = mcol_ref[...]
    b, cm, u, wg, pbe = _gates_cols(z_scr[:, G1_COL:G1_COL + LANES], z_scr[:, G2_COL:G2_COL + LANES],
                                    bifc_ref[...], mp_c, lmat, seq_len)
    wc_c = jnp.exp(mp_c - jnp.maximum(mp_c, _seq_last_row(cm, seq_len)))
    lhsc = _outer_sum_lhs(u)
    first_tok = (lax.broadcasted_iota(jnp.int32, (SEQ_BLK, n_tok), 0) * seq_len
                 == lax.broadcasted_iota(jnp.int32, (SEQ_BLK, n_tok), 1)).astype(BF16)

    for hd in range(ML_HEADS):
        q = z_scr[:, Q_COL + hd * ML_DK:Q_COL + (hd + 1) * ML_DK]
        v = z_scr[:, V_COL + hd * ML_DV:V_COL + (hd + 1) * ML_DV]
        kt = zt_scr[hd * ML_DK:(hd + 1) * ML_DK, :] * scale
        s = _decayed_scores(q, kt, lhsc, v_r, hd, mask)
        vb = v.astype(BF16)
        out = _dot(s.astype(BF16), jnp.concatenate([vb, ones_v], axis=1))
        wg_h = wg[:, hd:hd + 1]
        qn = jnp.sum(q * nt_ref[hd], axis=-1, keepdims=True)
        intra_scr[hd] = out[:, :ML_DV]
        dpart_scr[hd] = out[:, ML_DV:] + wg_h * qn
        pbe_scr[hd] = jnp.broadcast_to(pbe[:, hd:hd + 1], (n_tok, LANES))
        wgq_scr[hd] = q * wg_h
        kwt = kt * ws_r[hd:hd + 1, :]
        kwt_scr[hd] = kwt
        vb_scr[hd] = vb
        wcb_scr[hd] = jnp.broadcast_to(wc_c[:, hd:hd + 1], (n_tok, LANES))
        n_new_t = wc_r[hd:hd + 1, :] * ntt_ref[hd] + _dot_exact01_r(kwt, tot)
        no_ref[hd] = _pick_cols_t(n_new_t, first_tok)

    lane_i = lax.broadcasted_iota(jnp.int32, (ML_DK, n_tok), 1)

    def per_seq(bq, carry):
        r0 = pl.multiple_of(bq * seq_len, seq_len)
        rows = pl.ds(r0, seq_len)
        sel = (lane_i >= r0) & (lane_i < r0 + seq_len)
        for hd in range(ML_HEADS):
            c_prev = c_ref[bq, hd]
            inter_scr[hd, rows, :] = _dot(wgq_scr[hd, rows, :].astype(BF16), c_prev.astype(BF16))
            kw_b = jnp.where(sel, kwt_scr[hd], 0.0).astype(BF16)
            co_ref[bq, hd] = wcb_scr[hd, pl.ds(r0, 1), :] * c_prev + _dot(kw_b, vb_scr[hd])
        return carry

    lax.fori_loop(0, SEQ_BLK, per_seq, 0, unroll=8)

    for hd in range(ML_HEADS):
        num = inter_scr[hd] + intra_scr[hd]
        hm_scr[:, hd * ML_DV:(hd + 1) * ML_DV] = num * (
            1.0 / jnp.maximum(jnp.abs(dpart_scr[hd]), pbe_scr[hd]))

    _even_tail(z_scr, hm_scr, cat_scr, onw_ref)
    cx = z_scr[:, C_COL:C_COL + SC_WIDTH] * z_scr[:, X_COL:X_COL + SC_WIDTH]
    sub = lax.broadcasted_iota(jnp.int32, (n_tok, SC_WIDTH), 0) % seq_len
    p1 = sc_ref[...]
    s1 = jnp.where(sub < 1, pltpu.roll(p1, n_tok - 1, 0), pltpu.roll(cx, 1, 0))
    s2 = jnp.where(sub < 2, p1, pltpu.roll(cx, 2, 0))
    cw = cw_ref[...]
    u = s2 * cw[0:1] + s1 * cw[1:2] + cx * cw[2:3]
    sco_ref[...] = pltpu.roll(cx, n_tok - (seq_len - 2), 0)
    cat_scr[:, ML_WIDTH:] = (z_scr[:, B_COL:B_COL + SC_WIDTH] * u).astype(BF16)

    y = _dot(cat_scr[...], wout_ref[...])
    xo_ref[...] = x_ref[...] + g1_scr[...] * y


def _mix_even_sample_call(x, mod, l, nw, win, wkg, bifc, bifr, onw, cw, wout, c0, n_tok, n_tok_t, m_col, m_row,
                          sc_pad, co_prev):
    n_rows = x.shape[0]
    n_seq = c0.shape[1]
    seq_len = n_rows // n_seq
    blk = SEQ_BLK * seq_len
    nb = n_seq // SEQ_BLK
    li = l // 2
    head_blk = (ML_HEADS, blk, LANES)
    return _stacked_call(
        _mix_even_sample_kernel, 16, 1, co_prev, li,
        lambda lead, idx: pl.BlockSpec((lead, SEQ_BLK, ML_HEADS, ML_DK, ML_DV), lambda i: (idx, i, 0, 0, 0)),
        grid=(nb,),
        in_specs=[
            pl.BlockSpec((blk, D_MODEL), lambda i: (i, 0)),
            pl.BlockSpec((None, N_MOD, SEQ_BLK, D_MODEL), lambda i: (l, 0, i, 0)),
            pl.BlockSpec((None, 1, D_MODEL), lambda i: (l, 0, 0)),
        ] + _even_weight_specs(li, lambda i: (li, 0, 0)) + [
            pl.BlockSpec((None, SEQ_BLK, ML_HEADS, ML_DK, ML_DV), lambda i: (li, i, 0, 0, 0)),
            pl.BlockSpec((ML_HEADS, blk, ML_DK), lambda i: (0, i, 0)),
            pl.BlockSpec((ML_HEADS, ML_DK, blk), lambda i: (0, 0, i)),
            pl.BlockSpec((blk, LANES), lambda i: (i, 0)),
            pl.BlockSpec((SUBLANES, blk), lambda i: (0, i)),
            pl.BlockSpec((blk, SC_WIDTH), lambda i: (i, 0)),
        ],
        out_specs=[
            pl.BlockSpec((blk, D_MODEL), lambda i: (i, 0)),
            None,
            pl.BlockSpec((ML_HEADS, SEQ_BLK, ML_DK), lambda i: (0, i, 0)),
            pl.BlockSpec((SUBLANES, blk), lambda i: (0, i)),
            pl.BlockSpec((blk, SC_WIDTH), lambda i: (i, 0)),
        ],
        out_shape=[
            jax.ShapeDtypeStruct((n_rows, D_MODEL), F32),
            jax.ShapeDtypeStruct(c0.shape, F32),
            jax.ShapeDtypeStruct((ML_HEADS, n_seq, ML_DK), F32),
            jax.ShapeDtypeStruct((SUBLANES, n_rows), F32),
            jax.ShapeDtypeStruct((n_rows, SC_WIDTH), F32),
        ],
        scratch_shapes=[
            pltpu.VMEM((blk, D_MODEL), F32),
            pltpu.VMEM((blk, IN_W), F32),
            pltpu.VMEM((KG_ROWS, blk), F32),
            pltpu.VMEM((blk, ML_WIDTH), F32),
            pltpu.VMEM((blk, ML_WIDTH + SC_WIDTH), BF16),
            pltpu.VMEM((blk, D_MODEL), F32),
            pltpu.VMEM(head_blk, F32),
            pltpu.VMEM(head_blk, F32),
            pltpu.VMEM(head_blk, F32),
            pltpu.VMEM(head_blk, F32),
            pltpu.VMEM((ML_HEADS, ML_DK, blk), F32),
            pltpu.VMEM(head_blk, BF16),
            pltpu.VMEM(head_blk, F32),
            pltpu.VMEM(head_blk, F32),
        ],
        compiler_params=_cparams(1),
        name="mix_even_sample",
    )(x, mod, nw, win, wkg, bifc, bifr, onw, cw, wout, c0, n_tok, n_tok_t, m_col, m_row, sc_pad)


def _split2(x):
    hi = x.astype(BF16)
    lo = (x - hi.astype(F32)).astype(BF16)
    return jnp.concatenate([hi, lo], axis=1)


def _head_lane_mats():
    r = lax.broadcasted_iota(jnp.int32, (2 * LANES, LANES), 0) % LANES
    c = lax.broadcasted_iota(jnp.int32, (2 * LANES, LANES), 1)
    hsum = ((r // HEAD_DIM) == (c // HEAD_DIM)).astype(BF16)
    half = HEAD_DIM // 2
    src = jnp.where((c % HEAD_DIM) < half, c + half, c - half)
    return hsum, (r == src).astype(BF16)


def _q_lane_mat():
    r = lax.broadcasted_iota(jnp.int32, (2 * LANES, 2 * LANES), 0)
    c = lax.broadcasted_iota(jnp.int32, (2 * LANES, 2 * LANES), 1)
    half = HEAD_DIM // 2
    src = jnp.where((c % HEAD_DIM) < half, c + half, c - half)
    top = (r < LANES) & (c < LANES) & ((r // HEAD_DIM) == (c // HEAD_DIM))
    return (top | ((r >= LANES) & (c >= LANES) & (r == src))).astype(BF16)


def _q_norm_rope(xb, gw, cos, sin, qmat):
    zg = xb * gw
    out = _dot(jnp.concatenate([(xb * xb).astype(BF16), zg.astype(BF16)], axis=1), qmat)
    ms = out[:, :LANES] * (1.0 / HEAD_DIM)
    return lax.rsqrt(ms + EPS) * (zg * cos + out[:, LANES:] * sin)


def _qk_norm_rope(xb, gw, cos, sin, hsum, rot_mat):
    ms = _dot(_split2(xb * xb), hsum) * (1.0 / HEAD_DIM)
    zg = xb * gw
    rot = _dot(_split2(zg), rot_mat)
    return lax.rsqrt(ms + EPS) * (zg * cos + rot * sin)


def _sink_rows(sink8, reps):
    return jnp.concatenate(
        [jnp.broadcast_to(sink8[r:r + 1, :], (reps, LANES)) for r in range(SUBLANES)], axis=0)


def _sink_col(sink8, reps):
    parts = [jnp.broadcast_to(sink8[r:r + 1, :], (reps, LANES)) for r in range(SUBLANES)]
    return jnp.concatenate(parts, axis=0)[:, 0:1]


def _attn_prompt_kernel(x_ref, mod_ref, nw_ref, wqkv_ref, qnw_ref, knw_ref, cos_ref, sin_ref,
                        sink_ref, wout_ref, xo_ref, ko_ref, vo_ref,
                        z_scr, qm_scr, k_scr, v_scr, o_scr):
    t = pl.program_id(1)
    n_qb = TM // WINDOW

    @pl.when(t == 0)
    def _():
        k_scr[0:WINDOW, :] = jnp.zeros((WINDOW, KV_W), BF16)
        v_scr[0:WINDOW, :] = jnp.zeros((WINDOW, KV_W), BF16)

    x = x_ref[0]
    bi = pl.program_id(0)
    h = _norm_mod(x, nw_ref[...], _mod_row(mod_ref, 1, bi), _mod_row(mod_ref, 0, bi)).astype(BF16)
    z_scr[...] = _dot(h, wqkv_ref[...])
    cos = cos_ref[...]
    sin = sin_ref[...]
    hmats = _head_lane_mats()
    qmat = _q_lane_mat()
    half0 = lax.broadcasted_iota(jnp.int32, (TM, LANES), 1) < HEAD_DIM
    qscale = HEAD_DIM ** -0.5
    for jb in range(Q_W // LANES):
        y = _q_norm_rope(z_scr[:, jb * LANES:(jb + 1) * LANES], qnw_ref[...], cos, sin, qmat) * qscale
        qm_scr[2 * jb] = jnp.where(half0, y, 0.0).astype(BF16)
        qm_scr[2 * jb + 1] = jnp.where(half0, 0.0, y).astype(BF16)
    for p in range(KV_W // LANES):
        kf = _qk_norm_rope(z_scr[:, Q_W + p * LANES:Q_W + (p + 1) * LANES], knw_ref[...], cos, sin, *hmats)
        ko_ref[0, :, p * LANES:(p + 1) * LANES] = kf[TM - WINDOW:TM]
        k_scr[WINDOW:WINDOW + TM, p * LANES:(p + 1) * LANES] = kf.astype(BF16)
    vf = z_scr[:, Q_W + KV_W:Q_W + 2 * KV_W]
    vo_ref[0] = vf[TM - WINDOW:TM]
    v_scr[WINDOW:WINDOW + TM, :] = vf.astype(BF16)

    r = lax.broadcasted_iota(jnp.int32, (8 * WINDOW, 2 * WINDOW), 0) % WINDOW
    c = lax.broadcasted_iota(jnp.int32, (8 * WINDOW, 2 * WINDOW), 1)
    valid = ((c < WINDOW) & (c > r)) | ((c >= WINDOW) & ((c - WINDOW) <= r))
    first_lim = jnp.where(t == 0, WINDOW, 0)
    half0q = lax.broadcasted_iota(jnp.int32, (4 * WINDOW, LANES), 1) < HEAD_DIM
    ones_kv = jnp.ones((2 * WINDOW, LANES), BF16)
    for qb in range(n_qb):
        rows = slice(qb * WINDOW, (qb + 1) * WINDOW)
        krows = slice(qb * WINDOW, (qb + 2) * WINDOW)
        vmask = (valid & (c >= first_lim)) if qb == 0 else valid
        for p in range(KV_W // LANES):
            kb = k_scr[krows, p * LANES:(p + 1) * LANES]
            vb = v_scr[krows, p * LANES:(p + 1) * LANES]
            qs = jnp.concatenate([qm_scr[2 * (4 * p + i) + e, rows, :] for e in range(2) for i in range(4)],
                                 axis=0)
            s = jnp.where(vmask, _dot_nt(qs, kb), -jnp.inf)
            sk = _sink_rows(sink_ref[p], WINDOW)
            mx = jnp.maximum(jnp.max(s, axis=-1, keepdims=True), sk)
            pr = jnp.exp(s - jnp.concatenate([mx, mx], axis=1))
            o2 = _dot(pr.astype(BF16), jnp.concatenate([vb, ones_kv], axis=1))
            den = o2[:, LANES:] + jnp.exp(sk - mx)
            o = o2[:, :LANES] * (1.0 / den)
            merged = jnp.where(half0q, o[0:4 * WINDOW], o[4 * WINDOW:])
            for i in range(4):
                o_scr[rows, (4 * p + i) * LANES:(4 * p + i + 1) * LANES] = (
                    merged[i * WINDOW:(i + 1) * WINDOW].astype(BF16))

    y = _dot(o_scr[...], wout_ref[...])
    xo_ref[0] = x + _mod_row(mod_ref, 2, bi) * y
    k_scr[0:WINDOW, :] = k_scr[TM:TM + WINDOW, :]
    v_scr[0:WINDOW, :] = v_scr[TM:TM + WINDOW, :]


def _attn_prompt_call(x, mod, n_seq_rows, l, nw, wqkv, qnw, knw, cos, sin, sink, wout):
    B, S, _ = x.shape
    nt = S // TM
    const2 = lambda b, t: (0, 0)
    return pl.pallas_call(
        _attn_prompt_kernel,
        grid=(B, nt),
        in_specs=[
            pl.BlockSpec((1, TM, D_MODEL), lambda b, t: (b, t, 0)),
            pl.BlockSpec((None, N_MOD, SUBLANES, D_MODEL), lambda b, t: (l, 0, n_seq_rows // SUBLANES, 0)),
            pl.BlockSpec((None, 1, D_MODEL), lambda b, t: (l, 0, 0)),
            pl.BlockSpec((D_MODEL, Q_W + 2 * KV_W), const2),
            pl.BlockSpec((1, LANES), const2),
            pl.BlockSpec((1, LANES), const2),
            pl.BlockSpec((TM, LANES), lambda b, t: (t, 0)),
            pl.BlockSpec((TM, LANES), lambda b, t: (t, 0)),
            pl.BlockSpec((2, SUBLANES, LANES), lambda b, t: (0, 0, 0)),
            pl.BlockSpec((Q_W, D_MODEL), const2),
        ],
        out_specs=[
            pl.BlockSpec((1, TM, D_MODEL), lambda b, t: (b, t, 0)),
            pl.BlockSpec((1, WINDOW, KV_W), lambda b, t: (b, 0, 0)),
            pl.BlockSpec((1, WINDOW, KV_W), lambda b, t: (b, 0, 0)),
        ],
        out_shape=[
            jax.ShapeDtypeStruct((B, S, D_MODEL), F32),
            jax.ShapeDtypeStruct((B, WINDOW, KV_W), F32),
            jax.ShapeDtypeStruct((B, WINDOW, KV_W), F32),
        ],
        scratch_shapes=[
            pltpu.VMEM((TM, Q_W + 2 * KV_W), F32),
            pltpu.VMEM((2 * Q_W // LANES, TM, LANES), BF16),
            pltpu.VMEM((TM + WINDOW, KV_W), BF16),
            pltpu.VMEM((TM + WINDOW, KV_W), BF16),
            pltpu.VMEM((TM, Q_W), BF16),
        ],
        compiler_params=_cparams(2),
        name="attn_prompt",
    )(x, mod, nw, wqkv, qnw, knw, cos, sin, sink, wout)


def _attn_sample_kernel(x_ref, mod_ref, nw_ref, wqkv_ref, qnw_ref, knw_ref, cos_ref, sin_ref,
                        sink_ref, wout_ref, kc_ref, vc_ref,
                        xo_ref, kco_ref, vco_ref,
                        h_scr, g1_scr, z_scr, qm_scr, kn_scr, o_scr):
    n_tok = x_ref.shape[0]
    seq_len = n_tok // SEQ_BLK
    win = kc_ref.shape[1]

    def modulate(b, carry):
        rows = pl.ds(pl.multiple_of(b * seq_len, seq_len), seq_len)
        h_scr[rows, :] = _norm_mod(x_ref[rows, :], nw_ref[...], _mod_row(mod_ref, 1, b), _mod_row(mod_ref, 0, b))
        g1_scr[rows, :] = jnp.broadcast_to(_mod_row(mod_ref, 2, b), (seq_len, D_MODEL))
        return carry

    lax.fori_loop(0, SEQ_BLK, modulate, 0, unroll=8)
    z_scr[...] = _dot(h_scr[...].astype(BF16), wqkv_ref[...])
    cos = cos_ref[...]
    sin = sin_ref[...]
    hmats = _head_lane_mats()
    half0 =lax.broadcasted_iota(jnp.int32, (n_tok, LANES), 1) < HEAD_DIM
    qscale = HEAD_DIM ** -0.5
    for jb in range(Q_W // LANES):
        y = _qk_norm_rope(z_scr[:, jb * LANES:(jb + 1) * LANES], qnw_ref[...], cos, sin, *hmats) * qscale
        qm_scr[2 * jb] = jnp.where(half0, y, 0.0)
        qm_scr[2 * jb + 1] = jnp.where(half0, 0.0, y)
    for p in range(KV_W // LANES):
        kn_scr[:, p * LANES:(p + 1) * LANES] = _qk_norm_rope(
            z_scr[:, Q_W + p * LANES:Q_W + (p + 1) * LANES], knw_ref[...], cos, sin, *hmats)

    n_q = 8 * seq_len
    tq = lax.broadcasted_iota(jnp.int32, (SEQ_BLK, n_q, 2 * win), 1) % seq_len
    cc = lax.broadcasted_iota(jnp.int32, (SEQ_BLK, n_q, 2 * win), 2)
    valid = ((cc < win) & (cc > tq)) | ((cc >= 2 * win - seq_len) & ((cc - (2 * win - seq_len)) <= tq))
    half0q = lax.broadcasted_iota(jnp.int32, (SEQ_BLK, n_q // 2, LANES), 2) < HEAD_DIM
    ones_kv = jnp.ones((SEQ_BLK, 2 * win, LANES), BF16)

    kc = kc_ref[...]
    vc = vc_ref[...]
    knew = jnp.concatenate([kc[:, seq_len:], kn_scr[...].reshape(SEQ_BLK, seq_len, KV_W)], axis=1)
    vnew = jnp.concatenate(
        [vc[:, seq_len:], z_scr[:, Q_W + KV_W:Q_W + 2 * KV_W].reshape(SEQ_BLK, seq_len, KV_W)], axis=1)
    kco_ref[...] = knew
    vco_ref[...] = vnew
    for p in range(KV_W // LANES):
        lanes = slice(p * LANES, (p + 1) * LANES)
        qs = jnp.concatenate([qm_scr[2 * (4 * p + i) + e].reshape(SEQ_BLK, seq_len, LANES)
                              for e in range(2) for i in range(4)], axis=1).astype(BF16)
        kk = jnp.concatenate([kc[:, :, lanes], knew[:, :, lanes]], axis=1).astype(BF16)
        vv = jnp.concatenate([vc[:, :, lanes], vnew[:, :, lanes]], axis=1).astype(BF16)
        s = jnp.einsum("bqd,bkd->bqk", qs, kk, preferred_element_type=F32)
        s = jnp.where(valid, s, -jnp.inf)
        sk = _sink_rows(sink_ref[p], seq_len)[None]
        mx = jnp.maximum(jnp.max(s, axis=-1, keepdims=True), sk)
        pr = jnp.exp(s - jnp.concatenate([mx, mx], axis=-1))
        o2 = jnp.einsum("bqk,bkd->bqd", pr.astype(BF16), jnp.concatenate([vv, ones_kv], axis=-1),
                        preferred_element_type=F32)
        o = o2[:, :, :LANES] * (1.0 / (o2[:, :, LANES:] + jnp.exp(sk - mx)))
        merged = jnp.where(half0q, o[:, 0:n_q // 2], o[:, n_q // 2:])
        for i in range(4):
            o_scr[:, (4 * p + i) * LANES:(4 * p + i + 1) * LANES] = (
                merged[:, i * seq_len:(i + 1) * seq_len].reshape(n_tok, LANES))

    y = _dot(o_scr[...].astype(BF16), wout_ref[...])
    xo_ref[...] = x_ref[...] + g1_scr[...] * y


def _attn_sample_call(x, mod, l, nw, wqkv, qnw, knw, cos, sin, sink, wout, kc, vc):
    n_rows = x.shape[0]
    _, n_seq, win, _ = kc.shape
    lj = l // 2
    seq_len = n_rows // n_seq
    blk = SEQ_BLK * seq_len
    nb = n_seq // SEQ_BLK
    const2 = lambda i: (0, 0)
    return pl.pallas_call(
        _attn_sample_kernel,
        input_output_aliases={10: 1, 11: 2},
        grid=(nb,),
        in_specs=[
            pl.BlockSpec((blk, D_MODEL), lambda i: (i, 0)),
            pl.BlockSpec((None, N_MOD, SEQ_BLK, D_MODEL), lambda i: (l, 0, i, 0)),
            pl.BlockSpec((None, 1, D_MODEL), lambda i: (l, 0, 0)),
            pl.BlockSpec((D_MODEL, Q_W + 2 * KV_W), const2),
            pl.BlockSpec((1, LANES), const2),
            pl.BlockSpec((1, LANES), const2),
            pl.BlockSpec((blk, LANES), const2),
            pl.BlockSpec((blk, LANES), const2),
            pl.BlockSpec((2, SUBLANES, LANES), lambda i: (0, 0, 0)),
            pl.BlockSpec((Q_W, D_MODEL), const2),
            pl.BlockSpec((None, SEQ_BLK, win, KV_W), lambda i: (lj, i, 0, 0)),
            pl.BlockSpec((None, SEQ_BLK, win, KV_W), lambda i: (lj, i, 0, 0)),
        ],
        out_specs=[
            pl.BlockSpec((blk, D_MODEL), lambda i: (i, 0)),
            pl.BlockSpec((None, SEQ_BLK, win, KV_W), lambda i: (lj, i, 0, 0)),
            pl.BlockSpec((None, SEQ_BLK, win, KV_W), lambda i: (lj, i, 0, 0)),
        ],
        out_shape=[
            jax.ShapeDtypeStruct((n_rows, D_MODEL), F32),
            jax.ShapeDtypeStruct(kc.shape, F32),
            jax.ShapeDtypeStruct(vc.shape, F32),
        ],
        scratch_shapes=[
            pltpu.VMEM((blk, D_MODEL), F32),
            pltpu.VMEM((blk, D_MODEL), F32),
            pltpu.VMEM((blk, Q_W + 2 * KV_W), F32),
            pltpu.VMEM((2 * Q_W // LANES, blk, LANES), F32),
            pltpu.VMEM((blk, KV_W), F32),
            pltpu.VMEM((blk, Q_W), F32),
        ],
        compiler_params=_cparams(1),
        name="attn_sample",
    )(x, mod, nw, wqkv, qnw, knw, cos, sin, sink, wout, kc, vc)


def _rope_tables(pos):
    half = HEAD_DIM // 2
    inv = ROPE_THETA ** (-jnp.arange(half, dtype=F32) / half)
    ang = pos.astype(F32)[:, None] * inv[None, :]
    cos = jnp.cos(ang)
    sin = jnp.sin(ang)
    return jnp.tile(cos, (1, 4)), jnp.concatenate([-sin, sin, -sin, sin], axis=1)


def _rope_tables_range(n):
    half = HEAD_DIM // 2
    inv = ROPE_THETA ** (-jnp.arange(half, dtype=F32) / half)
    a_hi = (jnp.arange(n // WINDOW, dtype=jnp.int32) * WINDOW).astype(F32)[:, None] * inv[None, :]
    a_lo = jnp.arange(WINDOW, dtype=jnp.int32).astype(F32)[:, None] * inv[None, :]
    ch, sh = jnp.cos(a_hi)[:, None, :], jnp.sin(a_hi)[:, None, :]
    cl, sl = jnp.cos(a_lo)[None], jnp.sin(a_lo)[None]
    cos = (ch * cl - sh * sl).reshape(n, half)
    sin = (sh * cl + ch * sl).reshape(n, half)
    return jnp.tile(cos, (1, 4)), jnp.concatenate([-sin, sin, -sin, sin], axis=1)


def _prep_attn(w_qkv, q_norm, k_norm, sink, w_out):
    perm = np.asarray(HEAD_PERM)
    wq = w_qkv[:, :Q_W].reshape(D_MODEL, ATT_HEADS, HEAD_DIM)[:, perm].reshape(D_MODEL, Q_W)
    wqkv = jnp.concatenate([wq, w_qkv[:, Q_W:]], axis=1).astype(BF16)
    wout = w_out.reshape(ATT_HEADS, HEAD_DIM, D_MODEL)[perm].reshape(Q_W, D_MODEL).astype(BF16)
    qnw = jnp.tile(q_norm, 2)[None]
    knw = jnp.tile(k_norm, 2)[None]
    idx = np.asarray([[perm[2 * (4 * p + i) + e] for e in range(2) for i in range(4)] for p in range(2)])
    sink_arr = jnp.broadcast_to(sink[idx][:, :, None], (2, SUBLANES, LANES)).astype(F32)
    return wqkv, qnw, knw, sink_arr, wout


def kernel(x_prompt, x_sample, c_prompt, c_sample, state_mlstm_C, state_mlstm_n, state_mlstm_m, state_sconv, cache_win_k, cache_win_v, state_ffn_conv, norm1, norm2, w_ada, b_ada, a_w_in, a_b_if, a_out_norm, a_conv_w, a_w_out, c_w_qkv, c_q_norm, c_k_norm, c_sink, c_w_out, f_w_up, f_conv_w, f_w_down):
    B, S, _ = x_prompt.shape
    NS, SL, _ = x_sample.shape
    assert S % TM == 0 and NS % SEQ_BLK == 0 and SEQ_BLK * SL == CHUNK and SL == SUBLANES

    assert B <= SUBLANES
    c_all = jnp.concatenate([c_sample, c_prompt, jnp.zeros((SUBLANES - B, D_MODEL), F32)], axis=0)
    mod = _ada_call(c_all, w_ada, b_ada)

    win_all, kg_all = _prep_win_call(a_w_in)
    wkg_all = jnp.swapaxes(kg_all, 1, 2)
    b_i, b_f = a_b_if[:, :ML_HEADS], a_b_if[:, ML_HEADS:]
    lane_pad = jnp.zeros((a_b_if.shape[0], LANES - 2 * ML_HEADS), F32)
    bifc_all = jnp.stack([jnp.concatenate([b_i, b_i, lane_pad], axis=1),
                          jnp.concatenate([b_f, b_f, lane_pad], axis=1)], axis=1)
    bifr_all = jnp.broadcast_to(jnp.concatenate([b_i, b_i, b_f, b_f], axis=1)[:, :, None],
                                (a_b_if.shape[0], 2 * SUBLANES, LANES))
    wout_a_all = _cast_call(a_w_out)
    wup_all = _cast_call(f_w_up)
    wdn_all = _cast_call(f_w_down)
    onw_all = a_out_norm.reshape(-1, 1, ML_WIDTH)
    norm1_r = norm1.reshape(DEPTH, 1, D_MODEL)
    norm2_r = norm2.reshape(DEPTH, 1, D_MODEL)
    win_buf = cache_win_k.shape[2]
    kc_all = cache_win_k.reshape(-1, NS, win_buf, KV_W)
    vc_all = cache_win_v.reshape(-1, NS, win_buf, KV_W)

    cos_p, sin_p = _rope_tables_range(S)
    cos_s, sin_s = _rope_tables(PAST_LEN + jnp.arange(SL, dtype=jnp.int32))
    cos_s = jnp.tile(cos_s, (SEQ_BLK, 1))
    sin_s = jnp.tile(sin_s, (SEQ_BLK, 1))

    xp = x_prompt
    xs = x_sample.reshape(NS * SL, D_MODEL)
    p_C, p_n, p_m, p_sc, p_wk, p_wv, p_ffn = [], [], [], [], [], [], []
    s_n, s_m, s_sc = [], [], []
    s_C = s_ffn = None
    s_wk, s_wv = kc_all, vc_all

    for l in range(DEPTH):
        if l % 2 == 0:
            i = l // 2
            even_w = (norm1_r, win_all, wkg_all, bifc_all, bifr_all, onw_all, a_conv_w, wout_a_all)
            xp, co, no, mo, sco = _mix_even_prompt_call(xp, mod, NS, l, *even_w)
            p_C.append(co)
            p_n.append(no[:, :, :, 0])
            p_m.append(mo[:, :ML_HEADS, 0])
            p_sc.append(sco[:, SUBLANES - 2:, :])

            n_tok = jnp.repeat(state_mlstm_n[i].transpose(1, 0, 2), SL, axis=1)
            m_rep = jnp.repeat(state_mlstm_m[i], SL, axis=0)
            m_col = jnp.concatenate([m_rep, m_rep, jnp.zeros((NS * SL, LANES - 2 * ML_HEADS), F32)], axis=1)
            m_row = jnp.concatenate([m_rep, m_rep], axis=1).T
            sc_pad = jnp.pad(state_sconv[i], ((0, 0), (0, SL - 2), (0, 0))).reshape(NS * SL, SC_WIDTH)
            xs, s_C, no, mo, sco = _mix_even_sample_call(xs, mod, l, *even_w, state_mlstm_C, n_tok,
                                                         jnp.swapaxes(n_tok, 1, 2), m_col, m_row, sc_pad, s_C)
            s_n.append(no.transpose(1, 0, 2))
            s_m.append(mo[:ML_HEADS, ::SL].T)
            s_sc.append(sco.reshape(NS, SL, SC_WIDTH)[:, :2])
        else:
            j = l // 2
            wqkv, qnw, knw, sink, wout = _prep_attn(c_w_qkv[j], c_q_norm[j], c_k_norm[j], c_sink[j], c_w_out[j])
            xp, ko, vo = _attn_prompt_call(xp, mod, NS, l, norm1_r, wqkv, qnw, knw, cos_p, sin_p, sink, wout)
            p_wk.append(ko.reshape(B, WINDOW, KV_HEADS, HEAD_DIM))
            p_wv.append(vo.reshape(B, WINDOW, KV_HEADS, HEAD_DIM))
            xs, s_wk, s_wv = _attn_sample_call(xs, mod, l, norm1_r, wqkv, qnw, knw, cos_s, sin_s, sink, wout,
                                               s_wk, s_wv)

        xp, st = _ffn_prompt_call(xp, mod, NS, l, norm2_r, wup_all, f_conv_w, wdn_all)
        p_ffn.append(st[:, SUBLANES - 2:, :])
        xs, s_ffn = _ffn_sample_call(xs, mod, l, norm2_r, state_ffn_conv, wup_all, f_conv_w, wdn_all, s_ffn)

    kv_shape = (-1, NS, win_buf, KV_HEADS, HEAD_DIM)
    return (xp, xs.reshape(NS, SL, D_MODEL),
            jnp.stack(p_C), jnp.stack(p_n), jnp.stack(p_m), jnp.stack(p_sc),
            jnp.stack(p_wk), jnp.stack(p_wv), jnp.stack(p_ffn),
            s_C, jnp.stack(s_n), jnp.stack(s_m), jnp.stack(s_sc),
            s_wk.reshape(kv_shape), s_wv.reshape(kv_shape), s_ffn)
```

```python
import functools

import jax
import jax.numpy as jnp
import numpy as np
from jax import lax
from jax.experimental import pallas as pl
from jax.experimental.pallas import tpu as pltpu

F32 = jnp.float32
BF16 = jnp.bfloat16

D_MODEL = 1024
DEPTH = 4
PAST_LEN = 8192
ML_HEADS = 4
ML_DK = 128
ML_DV = 128
ML_WIDTH = ML_HEADS * ML_DV
SC_WIDTH = D_MODEL // 2
ATT_HEADS = 16
KV_HEADS = 4
HEAD_DIM = 64
WINDOW = 128
ROPE_THETA = 10000.0
D_FF = 2816
EPS = 1e-6

LANES = 128
SUBLANES = 8
VMEM_LIMIT = 56 * 1024 * 1024

TM = 1024
TM_FFN = 1024
CHUNK = 128
SEQ_BLK = 16
FC = 256
NCH = D_FF // FC
FS_GATE = 2
FC_S = D_FF // FS_GATE
Q_COL = 0
V_COL = Q_COL + ML_WIDTH
O_COL = V_COL + ML_WIDTH
G1_COL = O_COL + ML_WIDTH
G2_COL = G1_COL + LANES
B_COL = G2_COL + LANES
C_COL = B_COL + SC_WIDTH
X_COL = C_COL + SC_WIDTH
IN_W = X_COL + SC_WIDTH
KG_ROWS = ML_WIDTH + 16
SRC_K = ML_WIDTH
SRC_V = 2 * ML_WIDTH
SRC_G = 4 * ML_WIDTH
SRC_B = SRC_G + 2 * ML_HEADS
Q_W = ATT_HEADS * HEAD_DIM
KV_W = KV_HEADS * HEAD_DIM
HEAD_PERM = (0, 4, 1, 5, 2, 6, 3, 7, 8, 12, 9, 13, 10, 14, 11, 15)


def _cparams(n_axes):
    return pltpu.CompilerParams(dimension_semantics=("arbitrary",) * n_axes,
                                vmem_limit_bytes=VMEM_LIMIT)


def _stacked_call(kernel, n_in, out_idx, prev, slab, slab_spec, **kw):
    specs = list(kw.pop("in_specs"))
    out_specs = list(kw.pop("out_specs"))
    n_slabs = kw["out_shape"][out_idx].shape[0]
    if prev is None:
        out_specs[out_idx] = slab_spec(n_slabs, 0)

        def body(*refs):
            refs = list(refs)
            whole = refs[n_in + out_idx]
            for s in range(n_slabs):
                if s != slab:
                    whole[s] = jnp.zeros(whole.shape[1:], whole.dtype)
            refs[n_in + out_idx] = whole.at[slab]
            return kernel(*refs)

        return pl.pallas_call(body, in_specs=specs, out_specs=out_specs, **kw)

    out_specs[out_idx] = slab_spec(None, slab)

    def body(*refs):
        return kernel(*refs[:n_in], *refs[n_in + 1:])

    call = pl.pallas_call(body, in_specs=specs + [pl.BlockSpec(memory_space=pl.ANY)], out_specs=out_specs,
                          input_output_aliases={n_in: out_idx}, **kw)
    return lambda *args: call(*args, prev)


def _dot(a, b):
    return jnp.dot(a, b, preferred_element_type=F32)


def _dot_nt(a, b):
    return lax.dot_general(a, b, (((1,), (1,)), ((), ())), preferred_element_type=F32)


def _dot_tn(a, b):
    return lax.dot_general(a, b, (((0,), (0,)), ((), ())), preferred_element_type=F32)


def _dot_exact01(m, a):
    a1 = a.astype(BF16)
    r1 = a - a1.astype(F32)
    a2 = r1.astype(BF16)
    a3 = (r1 - a2.astype(F32)).astype(BF16)
    return _dot(m, a1) + _dot(m, a2) + _dot(m, a3)


def _norm_mod(x, nw, sc, sh):
    ms = jnp.mean(x * x, axis=-1, keepdims=True)
    return (x * lax.rsqrt(ms + EPS) * nw) * (1.0 + sc) + sh


def _sigmoid(x):
    return 1.0 / (1.0 + jnp.exp(-x))


def _log_sigmoid(x):
    return jnp.minimum(x, 0.0) - jnp.log(1.0 + jnp.exp(-jnp.abs(x)))


N_MOD = 6


ADA_KINDS = 3


def _ada_kernel(c_ref, w_ref, b_ref, o_ref):
    c = c_ref[...]
    s = (c * _sigmoid(c)).astype(BF16)
    for k in range(ADA_KINDS):
        cols = slice(k * D_MODEL, (k + 1) * D_MODEL)
        o_ref[k] = _dot(s, w_ref[:, cols].astype(BF16)) + b_ref[:, cols]


def _ada_call(c_all, w_ada, b_ada):
    rows = c_all.shape[0]
    return pl.pallas_call(
        _ada_kernel,
        grid=(DEPTH, N_MOD // ADA_KINDS),
        in_specs=[
            pl.BlockSpec((rows, D_MODEL), lambda l, k: (0, 0)),
            pl.BlockSpec((None, D_MODEL, ADA_KINDS * D_MODEL), lambda l, k: (l, 0, k)),
            pl.BlockSpec((None, 1, ADA_KINDS * D_MODEL), lambda l, k: (l, 0, k)),
        ],
        out_specs=pl.BlockSpec((None, ADA_KINDS, rows, D_MODEL), lambda l, k: (l, k, 0, 0)),
        out_shape=jax.ShapeDtypeStruct((DEPTH, N_MOD, rows, D_MODEL), F32),
        compiler_params=_cparams(2),
        name="adaln_mod",
    )(c_all, w_ada, b_ada.reshape(DEPTH, 1, N_MOD * D_MODEL))


CAST_ROWS = 256


def _cast_kernel(w_ref, o_ref):
    o_ref[...] = w_ref[...].astype(BF16)


CAST_BLOCK_BYTES = 6 * 1024 * 1024


def _cast_call(w):
    n_l, rows, cols = w.shape
    fits = [r for r in range(2 * SUBLANES, rows + 1, 2 * SUBLANES)
            if rows % r == 0 and r * cols * 4 <= CAST_BLOCK_BYTES]
    tr = max(fits) if fits else rows
    return pl.pallas_call(
        _cast_kernel,
        grid=(n_l, rows // tr),
        in_specs=[pl.BlockSpec((None, tr, cols), lambda l, r: (l, r, 0))],
        out_specs=pl.BlockSpec((None, tr, cols), lambda l, r: (l, r, 0)),
        out_shape=jax.ShapeDtypeStruct(w.shape, BF16),
        compiler_params=_cparams(2),
        name="cast_bf16",
    )(w)


def _prep_win_kernel(w_ref, o_ref, kg_ref):
    o_ref[:, Q_COL:Q_COL + ML_WIDTH] = w_ref[:, 0:ML_WIDTH].astype(BF16)
    kg_ref[:, 0:ML_WIDTH] = w_ref[:, SRC_K:SRC_V].astype(BF16)
    o_ref[:, V_COL:G1_COL] = w_ref[:, SRC_V:SRC_G].astype(BF16)
    tail = w_ref[:, SRC_G:]
    ig = tail[:, 0:ML_HEADS]
    fg = tail[:, ML_HEADS:2 * ML_HEADS]
    pad = jnp.zeros((ig.shape[0], LANES - 2 * ML_HEADS), F32)
    o_ref[:, G1_COL:G2_COL] = jnp.concatenate([ig, ig, pad], axis=1).astype(BF16)
    o_ref[:, G2_COL:B_COL] = jnp.concatenate([fg, fg, pad], axis=1).astype(BF16)
    o_ref[:, B_COL:IN_W] = tail[:, 2 * ML_HEADS:].astype(BF16)
    kg_ref[:, ML_WIDTH:KG_ROWS] = jnp.concatenate([ig, ig, fg, fg], axis=1).astype(BF16)


def _prep_win_call(a_w_in):
    n_l, _, in_a = a_w_in.shape
    return pl.pallas_call(
        _prep_win_kernel,
        grid=(n_l, D_MODEL // CAST_ROWS),
        in_specs=[pl.BlockSpec((None, CAST_ROWS, in_a), lambda l, r: (l, r, 0))],
        out_specs=[pl.BlockSpec((None, CAST_ROWS, IN_W), lambda l, r: (l, r, 0)),
                   pl.BlockSpec((None, CAST_ROWS, KG_ROWS), lambda l, r: (l, r, 0))],
        out_shape=[jax.ShapeDtypeStruct((n_l, D_MODEL, IN_W), BF16),
                   jax.ShapeDtypeStruct((n_l, D_MODEL, KG_ROWS), BF16)],
        compiler_params=_cparams(2),
        name="prep_w_in",
    )(a_w_in)


def _mod_row(mod_ref, kind, b):
    return mod_ref[kind, pl.ds(b, 1), :]


def _ffn_prompt_kernel(x_ref, mod_ref, nw_ref, wup_ref, cw_ref, wdn_ref,
                       xo_ref, st_ref, h_scr, act_scr, carry_scr):
    b = pl.program_id(0)
    t = pl.program_id(1)

    @pl.when(t == 0)
    def _():
        carry_scr[...] = jnp.zeros_like(carry_scr)

    x = x_ref[0]
    h_scr[...] = _norm_mod(x, nw_ref[...], _mod_row(mod_ref, 4, b), _mod_row(mod_ref, 3, b)).astype(BF16)
    for j in range(NCH):
        ys = []
        for col in (j * FC, D_FF + j * FC):
            cols = slice(col, col + FC)
            u = _dot(h_scr[...], wup_ref[:, cols])
            ys.append(_conv3_rows(u, carry_scr[:, cols], cw_ref[:, cols]))
            carry_scr[:, cols] = u[TM_FFN - SUBLANES:TM_FFN]
        g = ys[0]
        act_scr[:, j * FC:(j + 1) * FC] = (g * _sigmoid(g) * ys[1]).astype(BF16)
    y = _dot(act_scr[...], wdn_ref[...])
    xo_ref[0] = x + _mod_row(mod_ref, 5, b) * y
    st_ref[0] = carry_scr[...]


def _ffn_prompt_call(x, mod, n_seq_rows, l, nw, wup, cw, wdn):
    B, S, _ = x.shape
    assert S % TM_FFN == 0
    nt = S // TM_FFN
    once = pl.Buffered(1)
    return pl.pallas_call(
        _ffn_prompt_kernel,
        grid=(B, nt),
        in_specs=[
            pl.BlockSpec((1, TM_FFN, D_MODEL), lambda b, t: (b, t, 0)),
            pl.BlockSpec((None, N_MOD, SUBLANES, D_MODEL), lambda b, t: (l, 0, n_seq_rows // SUBLANES, 0)),
            pl.BlockSpec((None, 1, D_MODEL), lambda b, t: (l, 0, 0)),
            pl.BlockSpec((None, D_MODEL, 2 * D_FF), lambda b, t: (l, 0, 0), pipeline_mode=once),
            pl.BlockSpec((None, 3, 2 * D_FF), lambda b, t: (l, 0, 0)),
            pl.BlockSpec((None, D_FF, D_MODEL), lambda b, t: (l, 0, 0), pipeline_mode=once),
        ],
        out_specs=[
            pl.BlockSpec((1, TM_FFN, D_MODEL), lambda b, t: (b, t, 0)),
            pl.BlockSpec((1, SUBLANES, 2 * D_FF), lambda b, t: (b, 0, 0)),
        ],
        out_shape=[
            jax.ShapeDtypeStruct((B, S, D_MODEL), F32),
            jax.ShapeDtypeStruct((B, SUBLANES, 2 * D_FF), F32),
        ],
        scratch_shapes=[
            pltpu.VMEM((TM_FFN, D_MODEL), BF16),
            pltpu.VMEM((TM_FFN, D_FF), BF16),
            pltpu.VMEM((SUBLANES, 2 * D_FF), F32),
        ],
        compiler_params=_cparams(2),
        name="ffn_prompt",
    )(x, mod, nw, wup, cw, wdn)


def _ffn_sample_kernel(x_ref, mod_ref, nw_ref, s_ref, w_ref, c_ref, wdn_ref,
                       xo_ref, so_ref, h_scr, hb_scr, g2_scr, acc_scr, yg_scr):
    j = pl.program_id(1)
    n_seq = s_ref.shape[0]
    n_rows = x_ref.shape[0]
    n_t = n_rows // n_seq

    @pl.when(j == 0)
    def _():
        def modulate(b, carry):
            rows = pl.ds(pl.multiple_of(b * n_t, n_t), n_t)
            h_scr[rows, :] = _norm_mod(x_ref[rows, :], nw_ref[...], _mod_row(mod_ref, 4, b), _mod_row(mod_ref, 3, b))
            g2_scr[rows, :] = jnp.broadcast_to(_mod_row(mod_ref, 5, b), (n_t, D_MODEL))
            return carry

        lax.fori_loop(0, n_seq, modulate, 0, unroll=8)
        hb_scr[...] = h_scr[...].astype(BF16)
        acc_scr[...] = jnp.zeros_like(acc_scr)

    sub = lax.broadcasted_iota(jnp.int32, (n_seq, n_t, FC_S), 1)
    u3 = _dot(hb_scr[...], w_ref[...]).reshape(n_seq, n_t, FC_S)
    cw = c_ref[...]
    p0 = jnp.broadcast_to(s_ref[:, 0:1, :], (n_seq, n_t, FC_S))
    p1 = jnp.broadcast_to(s_ref[:, 1:2, :], (n_seq, n_t, FC_S))
    s1 = jnp.where(sub < 1, p1, pltpu.roll(u3, 1, 1))
    s2 = jnp.where(sub < 1, p0, jnp.where(sub < 2, p1, pltpu.roll(u3, 2, 1)))
    y = (s2 * cw[0:1] + s1 * cw[1:2] + u3 * cw[2:3]).reshape(n_rows, FC_S)
    so_ref[...] = pltpu.roll(u3, 2, 1)[:, 0:2, :]

    @pl.when(j < FS_GATE)
    def _():
        yg_scr[j] = y

    @pl.when(j >= FS_GATE)
    def _():
        g = yg_scr[j - FS_GATE]
        acc_scr[...] += _dot((g * _sigmoid(g) * y).astype(BF16), wdn_ref[...])

    @pl.when(j == 2 * FS_GATE - 1)
    def _():
        xo_ref[...] = x_ref[...] + g2_scr[...] * acc_scr[...]


def _ffn_sample_call(x, mod, l, nw, st, wup, cw, wdn, so_prev):
    n_seq = st.shape[1] // 2
    n_rows = x.shape[0] // 2
    return _stacked_call(
        _ffn_sample_kernel, 7, 1, so_prev, l,
        lambda lead, idx: pl.BlockSpec((lead, n_seq, 2, FC_S), lambda hf, j: (idx, hf, 0, j)),
        grid=(2, 2 * FS_GATE),
        in_specs=[
            pl.BlockSpec((n_rows, D_MODEL), lambda hf, j: (hf, 0)),
            pl.BlockSpec((None, N_MOD, n_seq, D_MODEL), lambda hf, j: (l, 0, hf, 0)),
            pl.BlockSpec((None, 1, D_MODEL), lambda hf, j: (l, 0, 0)),
            pl.BlockSpec((None, n_seq, 2, FC_S), lambda hf, j: (l, hf, 0, j)),
            pl.BlockSpec((None, D_MODEL, FC_S), lambda hf, j: (l, 0, j)),
            pl.BlockSpec((None, 3, FC_S), lambda hf, j: (l, 0, j)),
            pl.BlockSpec((None, FC_S, D_MODEL), lambda hf, j: (l, jnp.maximum(j - FS_GATE, 0), 0)),
        ],
        out_specs=[
            pl.BlockSpec((n_rows, D_MODEL), lambda hf, j: (hf, 0)),
            None,
        ],
        out_shape=[
            jax.ShapeDtypeStruct(x.shape, F32),
            jax.ShapeDtypeStruct((DEPTH, st.shape[1], 2, 2 * D_FF), F32),
        ],
        scratch_shapes=[
            pltpu.VMEM((n_rows, D_MODEL), F32),
            pltpu.VMEM((n_rows, D_MODEL), BF16),
            pltpu.VMEM((n_rows, D_MODEL), F32),
            pltpu.VMEM((n_rows, D_MODEL), F32),
            pltpu.VMEM((FS_GATE, n_rows, FC_S), F32),
        ],
        compiler_params=_cparams(2),
        name="ffn_sample",
    )(x, mod, nw, st, wup, cw, wdn)


def _chunk_consts(seq_len):
    r = lax.broadcasted_iota(jnp.int32, (CHUNK, CHUNK), 0)
    c = lax.broadcasted_iota(jnp.int32, (CHUNK, CHUNK), 1)
    if seq_len >= CHUNK:
        same = r >= 0
    else:
        same = (r // seq_len) == (c // seq_len)
    mask = same & (c <= r)
    lmat = mask.astype(BF16)
    lmat_t = (same & (r <= c)).astype(BF16)
    return mask, lmat, lmat_t, same.astype(BF16)


def _dot_exact01_r(a, m):
    a1 = a.astype(BF16)
    r1 = a - a1.astype(F32)
    a2 = r1.astype(BF16)
    a3 = (r1 - a2.astype(F32)).astype(BF16)
    return _dot(a1, m) + _dot(a2, m) + _dot(a3, m)


def _pick_cols_t(a, sel):
    a1 = a.astype(BF16)
    r1 = a - a1.astype(F32)
    a2 = r1.astype(BF16)
    a3 = (r1 - a2.astype(F32)).astype(BF16)
    return _dot_nt(sel, a1) + _dot_nt(sel, a2) + _dot_nt(sel, a3)


def _seq_max_lanes(x, seq_len):
    n = x.shape[1]
    pos = lax.broadcasted_iota(jnp.int32, x.shape, 1)
    d = 1
    while d < seq_len:
        partner = jnp.where((pos & d) == 0, pltpu.roll(x, n - d, 1), pltpu.roll(x, d, 1))
        x = jnp.maximum(x, partner)
        d *= 2
    return x


def _seq_prefix_max_rows(x, seq_len):
    pos = lax.broadcasted_iota(jnp.int32, x.shape, 0) & (seq_len - 1)
    d = 1
    while d < seq_len:
        x = jnp.where(pos >= d, jnp.maximum(x, pltpu.roll(x, d, 0)), x)
        d *= 2
    return x


def _seq_last_row(x, seq_len):
    n, w = x.shape
    if seq_len >= n:
        return jnp.broadcast_to(x[n - 1:n], x.shape)
    x3 = x.reshape(n // seq_len, seq_len, w)
    return jnp.broadcast_to(x3[:, seq_len - 1:seq_len, :], x3.shape).reshape(n, w)


def _gates_rows_pre(gt, bias_r, lmat_t, tot, seq_len):
    ig = gt[0:SUBLANES] + bias_r[0:SUBLANES]
    lf = _log_sigmoid(gt[SUBLANES:] + bias_r[SUBLANES:])
    b = _dot_exact01_r(lf, lmat_t)
    bl = _dot_exact01_r(lf, tot)
    v = ig - b
    return bl, v, _seq_max_lanes(v, seq_len)


def _gates_rows_post(pre, mp_r):
    bl, v, vm = pre
    mn = bl + jnp.maximum(mp_r, vm)
    return jnp.exp(bl + v - mn), jnp.exp(bl + mp_r - mn), mn


def _gates_rows(gt, bias_r, mp_r, lmat_t, tot, seq_len):
    pre = _gates_rows_pre(gt, bias_r, lmat_t, tot, seq_len)
    return (pre[1],) + _gates_rows_post(pre, mp_r)


def _gates_cols_pre(g1, g2, bias_c, lmat, seq_len):
    lane = lax.broadcasted_iota(jnp.int32, g1.shape, 1)
    ig = g1 + bias_c[0:1]
    lf = jnp.where(lane < 2 * ML_HEADS, _log_sigmoid(g2 + bias_c[1:2]), 0.0)
    b = _dot_exact01(lmat, lf)
    return b, _seq_prefix_max_rows(ig - b, seq_len)


def _gates_cols_post(b, cm, mp_c):
    g = b + mp_c
    mt = jnp.maximum(b + cm, g)
    return b - mt, jnp.exp(g - mt), jnp.exp(-mt)


def _gates_cols(g1, g2, bias_c, mp_c, lmat, seq_len):
    b, cm = _gates_cols_pre(g1, g2, bias_c, lmat, seq_len)
    return (b, cm) + _gates_cols_post(b, cm, mp_c)


def _outer_sum_lhs(u):
    lane = lax.broadcasted_iota(jnp.int32, u.shape, 1)
    hi = u.astype(BF16).astype(F32)
    lo = u - hi
    return jnp.where(lane < ML_HEADS, hi, jnp.where(lane < 2 * ML_HEADS, lo,
                     jnp.where(lane < 4 * ML_HEADS, 1.0, 0.0))).astype(BF16)


def _outer_sum_rhs(v_r, hd):
    row = lax.broadcasted_iota(jnp.int32, v_r.shape, 0)
    hi = v_r.astype(BF16).astype(F32)
    lo = v_r - hi
    pick = (row == hd) | (row == ML_HEADS + hd)
    top = jnp.where(pick, 1.0, 0.0)
    bot = jnp.where(row == hd, hi, jnp.where(row == ML_HEADS + hd, lo, 0.0))
    r16 = jnp.concatenate([top, bot], axis=0).astype(BF16)
    return jnp.concatenate([r16, jnp.zeros((LANES - 2 * SUBLANES, v_r.shape[1]), BF16)], axis=0)


def _decayed_scores(q, kt, lhsc, v_r, hd, mask):
    e = _dot(lhsc, _outer_sum_rhs(v_r, hd))
    return _dot(q.astype(BF16), kt.astype(BF16)) * jnp.where(mask, jnp.exp(e), 0.0)


def _conv3_rows(cx, prev8, cw):
    n = cx.shape[1]
    row = lax.broadcasted_iota(jnp.int32, (SUBLANES, n), 0)
    s1 = pltpu.roll(cx, 1, 0)
    s2 = pltpu.roll(cx, 2, 0)
    f1 = jnp.where(row < 1, pltpu.roll(prev8, 1, 0), s1[0:SUBLANES])
    f2 = jnp.where(row < 2, pltpu.roll(prev8, 2, 0), s2[0:SUBLANES])
    s1 = jnp.concatenate([f1, s1[SUBLANES:]], axis=0)
    s2 = jnp.concatenate([f2, s2[SUBLANES:]], axis=0)
    return s2 * cw[0:1] + s1 * cw[1:2] + cx * cw[2:3]


def _mlstm_out_norm(hm, zo, onw):
    ms = jnp.mean(hm * hm, axis=-1, keepdims=True)
    return hm * lax.rsqrt(ms + EPS) * onw * _sigmoid(zo)


def _even_tail(z_scr, hm_scr, cat_scr, onw_ref):
    for hd in range(ML_HEADS):
        col = slice(hd * ML_DV, (hd + 1) * ML_DV)
        zo = z_scr[:, O_COL + hd * ML_DV:O_COL + (hd + 1) * ML_DV]
        cat_scr[:, col] = _mlstm_out_norm(hm_scr[:, col], zo, onw_ref[:, col]).astype(BF16)


def _mix_even_prompt_kernel(x_ref, mod_ref, nw_ref, win_ref, wkg_ref, bifc_ref, bifr_ref, onw_ref, cw_ref,
                            wout_ref, xo_ref, co_ref, no_ref, mo_ref, sco_ref,
                            z_scr, zt_scr, hm_scr, cat_scr, cn_scr, mrow_scr, mlane_scr, cc_scr):
    t = pl.program_id(1)

    @pl.when(t == 0)
    def _():
        cn_scr[...] = jnp.zeros_like(cn_scr)
        mrow_scr[...] = jnp.zeros_like(mrow_scr)
        mlane_scr[...] = jnp.zeros_like(mlane_scr)
        cc_scr[...] = jnp.zeros_like(cc_scr)

    x = x_ref[0]
    bi = pl.program_id(0)
    h = _norm_mod(x, nw_ref[...], _mod_row(mod_ref, 1, bi), _mod_row(mod_ref, 0, bi)).astype(BF16)
    z_scr[...] = _dot(h, win_ref[...])
    zt_scr[...] = _dot_nt(wkg_ref[...], h)

    mask, lmat, lmat_t, tot = _chunk_consts(CHUNK)
    scale = ML_DK ** -0.5
    ones_v = jnp.ones((CHUNK, ML_DV), BF16)

    n_ch = TM // CHUNK
    chunk_rows = [slice(c * CHUNK, (c + 1) * CHUNK) for c in range(n_ch)]
    pre_r = [_gates_rows_pre(zt_scr[ML_WIDTH:KG_ROWS, rows], bifr_ref[...], lmat_t, tot, CHUNK)
             for rows in chunk_rows]
    pre_c = [_gates_cols_pre(z_scr[rows, G1_COL:G1_COL + LANES], z_scr[rows, G2_COL:G2_COL + LANES],
                             bifc_ref[...], lmat, CHUNK) for rows in chunk_rows]
    mp_r = mrow_scr[...]
    mp_c = mlane_scr[0:1, :]
    post_r, post_c = [], []
    for c in range(n_ch):
        post_r.append(_gates_rows_post(pre_r[c], mp_r))
        mp_r = post_r[c][2]
        b, cm = pre_c[c]
        post_c.append(_gates_cols_post(b, cm, mp_c))
        mp_c = b[CHUNK - 1:CHUNK] + jnp.maximum(mp_c, cm[CHUNK - 1:CHUNK])
    mrow_scr[...] = mp_r
    mlane_scr[...] = jnp.broadcast_to(mp_c, (SUBLANES, LANES))

    intra, upd, wgq = {}, {}, {}
    for c, rows in enumerate(chunk_rows):
        u, wg, _ = post_c[c]
        lhsc = _outer_sum_lhs(u)
        for hd in range(ML_HEADS):
            q = z_scr[rows, Q_COL + hd * ML_DK:Q_COL + (hd + 1) * ML_DK]
            v = z_scr[rows, V_COL + hd * ML_DV:V_COL + (hd + 1) * ML_DV]
            kt = zt_scr[hd * ML_DK:(hd + 1) * ML_DK, rows] * scale
            s = _decayed_scores(q, kt, lhsc, pre_r[c][1], hd, mask)
            v1 = jnp.concatenate([v.astype(BF16), ones_v], axis=1)
            intra[c, hd] = _dot(s.astype(BF16), v1)
            upd[c, hd] = _dot((kt * post_r[c][0][hd:hd + 1, :]).astype(BF16), v1)
            wgq[c, hd] = (q * wg[:, hd:hd + 1]).astype(BF16)

    for hd in range(ML_HEADS):
        cn = cn_scr[hd]
        for c, rows in enumerate(chunk_rows):
            out = intra[c, hd] + _dot(wgq[c, hd], cn.astype(BF16))
            r = 1.0 / jnp.maximum(jnp.abs(out[:, ML_DV:]), post_c[c][2][:, hd:hd + 1])
            hm_scr[rows, hd * ML_DV:(hd + 1) * ML_DV] = out[:, :ML_DV] * r
            wc = post_r[c][1][hd:hd + 1, :]
            cn = jnp.concatenate([wc, wc], axis=1) * cn + upd[c, hd]
        cn_scr[hd] = cn

    _even_tail(z_scr, hm_scr, cat_scr, onw_ref)
    cx = z_scr[:, C_COL:C_COL + SC_WIDTH] * z_scr[:, X_COL:X_COL + SC_WIDTH]
    u = _conv3_rows(cx, cc_scr[...], cw_ref[...])
    cc_scr[...] = cx[TM - SUBLANES:TM]
    cat_scr[:, ML_WIDTH:] = (z_scr[:, B_COL:B_COL + SC_WIDTH] * u).astype(BF16)

    y = _dot(cat_scr[...], wout_ref[...])
    xo_ref[0] = x + _mod_row(mod_ref, 2, bi) * y
    for hd in range(ML_HEADS):
        co_ref[0, hd] = cn_scr[hd, :, 0:ML_DV]
        no_ref[0, hd] = cn_scr[hd, :, ML_DV:]
    mo_ref[0] = mrow_scr[...]
    sco_ref[0] = cc_scr[...]


def _even_weight_specs(i, idx):
    once = pl.Buffered(1)
    return [
        pl.BlockSpec((None, D_MODEL, IN_W), idx, pipeline_mode=once),
        pl.BlockSpec((None, KG_ROWS, D_MODEL), idx, pipeline_mode=once),
        pl.BlockSpec((None, 2, LANES), idx),
        pl.BlockSpec((None, 2 * SUBLANES, LANES), idx),
        pl.BlockSpec((None, 1, ML_WIDTH), idx),
        pl.BlockSpec((None, 3, SC_WIDTH), idx),
        pl.BlockSpec((None, ML_WIDTH + SC_WIDTH, D_MODEL), idx, pipeline_mode=once),
    ]


def _mix_even_prompt_call(x, mod, n_seq_rows, l, nw, win, wkg, bifc, bifr, onw, cw, wout):
    B, S, _ = x.shape
    nt = S // TM
    i = l // 2
    return pl.pallas_call(
        _mix_even_prompt_kernel,
        grid=(B, nt),
        in_specs=[
            pl.BlockSpec((1, TM, D_MODEL), lambda b, t: (b, t, 0)),
            pl.BlockSpec((None, N_MOD, SUBLANES, D_MODEL), lambda b, t: (l, 0, n_seq_rows // SUBLANES, 0)),
            pl.BlockSpec((None, 1, D_MODEL), lambda b, t: (l, 0, 0)),
        ] + _even_weight_specs(i, lambda b, t: (i, 0, 0)),
        out_specs=[
            pl.BlockSpec((1, TM, D_MODEL), lambda b, t: (b, t, 0)),
            pl.BlockSpec((1, ML_HEADS, ML_DK, ML_DV), lambda b, t: (b, 0, 0, 0)),
            pl.BlockSpec((1, ML_HEADS, ML_DK, LANES), lambda b, t: (b, 0, 0, 0)),
            pl.BlockSpec((1, SUBLANES, LANES), lambda b, t: (b, 0, 0)),
            pl.BlockSpec((1, SUBLANES, SC_WIDTH), lambda b, t: (b, 0, 0)),
        ],
        out_shape=[
            jax.ShapeDtypeStruct((B, S, D_MODEL), F32),
            jax.ShapeDtypeStruct((B, ML_HEADS, ML_DK, ML_DV), F32),
            jax.ShapeDtypeStruct((B, ML_HEADS, ML_DK, LANES), F32),
            jax.ShapeDtypeStruct((B, SUBLANES, LANES), F32),
            jax.ShapeDtypeStruct((B, SUBLANES, SC_WIDTH), F32),
        ],
        scratch_shapes=[
            pltpu.VMEM((TM, IN_W), F32),
            pltpu.VMEM((KG_ROWS, TM), F32),
            pltpu.VMEM((TM, ML_WIDTH), F32),
            pltpu.VMEM((TM, ML_WIDTH + SC_WIDTH), BF16),
            pltpu.VMEM((ML_HEADS, ML_DK, ML_DV + LANES), F32),
            pltpu.VMEM((SUBLANES, LANES), F32),
            pltpu.VMEM((SUBLANES, LANES), F32),
            pltpu.VMEM((SUBLANES, SC_WIDTH), F32),
        ],
        compiler_params=_cparams(2),
        name="mix_even_prompt",
    )(x, mod, nw, win, wkg, bifc, bifr, onw, cw, wout)


def _mix_even_sample_kernel(x_ref, mod_ref, nw_ref, win_ref, wkg_ref, bifc_ref, bifr_ref, onw_ref, cw_ref,
                            wout_ref, c_ref, nt_ref, ntt_ref, mcol_ref, mrow_ref, sc_ref,
                            xo_ref, co_ref, no_ref, mo_ref, sco_ref,
                            h_scr, z_scr, zt_scr, hm_scr, cat_scr, g1_scr,
                            intra_scr, dpart_scr, pbe_scr, wgq_scr, kwt_scr, vb_scr, wcb_scr, inter_scr):
    n_tok = x_ref.shape[0]
    seq_len = n_tok // SEQ_BLK

    def modulate(b, carry):
        rows = pl.ds(pl.multiple_of(b * seq_len, seq_len), seq_len)
        h_scr[rows, :] = _norm_mod(x_ref[rows, :], nw_ref[...], _mod_row(mod_ref, 1, b), _mod_row(mod_ref, 0, b))
        g1_scr[rows, :] = jnp.broadcast_to(_mod_row(mod_ref, 2, b), (seq_len, D_MODEL))
        return carry

    lax.fori_loop(0, SEQ_BLK, modulate, 0, unroll=8)
    hb = h_scr[...].astype(BF16)
    z_scr[...] = _dot(hb, win_ref[...])
    zt_scr[...] = _dot_nt(wkg_ref[...], hb)

    mask, lmat, lmat_t, tot = _chunk_consts(seq_len)
    scale = ML_DK ** -0.5
    ones_v = jnp.ones((CHUNK, ML_DV), BF16)
    mp_r = mrow_ref[...]
    v_r, ws_r, wc_r, mn_r = _gates_rows(zt_scr[ML_WIDTH:KG_ROWS, :], bifr_ref[...], mp_r, lmat_t, tot, seq_len)
    mo_ref[...] = mn_r
    mp_c = mcol_ref[...]
    b, cm, u, wg, pbe = _gates_cols(z_scr[:, G1_COL:G1_COL + LANES], z_scr[:, G2_COL:G2_COL + LANES],
                                    bifc_ref[...], mp_c, lmat, seq_len)
    wc_c = jnp.exp(mp_c - jnp.maximum(mp_c, _seq_last_row(cm, seq_len)))
    lhsc = _outer_sum_lhs(u)
    first_tok = (lax.broadcasted_iota(jnp.int32, (SEQ_BLK, n_tok), 0) * seq_len
                 == lax.broadcasted_iota(jnp.int32, (SEQ_BLK, n_tok), 1)).astype(BF16)

    for hd in range(ML_HEADS):
        q = z_scr[:, Q_COL + hd * ML_DK:Q_COL + (hd + 1) * ML_DK]
        v = z_scr[:, V_COL + hd * ML_DV:V_COL + (hd + 1) * ML_DV]
        kt = zt_scr[hd * ML_DK:(hd + 1) * ML_DK, :] * scale
        s = _decayed_scores(q, kt, lhsc, v_r, hd, mask)
        vb = v.astype(BF16)
        out = _dot(s.astype(BF16), jnp.concatenate([vb, ones_v], axis=1))
        wg_h = wg[:, hd:hd + 1]
        qn = jnp.sum(q * nt_ref[hd], axis=-1, keepdims=True)
        intra_scr[hd] = out[:, :ML_DV]
        dpart_scr[hd] = out[:, ML_DV:] + wg_h * qn
        pbe_scr[hd] = jnp.broadcast_to(pbe[:, hd:hd + 1], (n_tok, LANES))
        wgq_scr[hd] = q * wg_h
        kwt = kt * ws_r[hd:hd + 1, :]
        kwt_scr[hd] = kwt
        vb_scr[hd] = vb
        wcb_scr[hd] = jnp.broadcast_to(wc_c[:, hd:hd + 1], (n_tok, LANES))
        n_new_t = wc_r[hd:hd + 1, :] * ntt_ref[hd] + _dot_exact01_r(kwt, tot)
        no_ref[hd] = _pick_cols_t(n_new_t, first_tok)

    lane_i = lax.broadcasted_iota(jnp.int32, (ML_DK, n_tok), 1)

    def per_seq(bq, carry):
        r0 = pl.multiple_of(bq * seq_len, seq_len)
        rows = pl.ds(r0, seq_len)
        sel = (lane_i >= r0) & (lane_i < r0 + seq_len)
        for hd in range(ML_HEADS):
            c_prev = c_ref[bq, hd]
            inter_scr[hd, rows, :] = _dot(wgq_scr[hd, rows, :].astype(BF16), c_prev.astype(BF16))
            kw_b = jnp.where(sel, kwt_scr[hd], 0.0).astype(BF16)
            co_ref[bq, hd] = wcb_scr[hd, pl.ds(r0, 1), :] * c_prev + _dot(kw_b, vb_scr[hd])
        return carry

    lax.fori_loop(0, SEQ_BLK, per_seq, 0, unroll=8)

    for hd in range(ML_HEADS):
        num = inter_scr[hd] + intra_scr[hd]
        hm_scr[:, hd * ML_DV:(hd + 1) * ML_DV] = num * (
            1.0 / jnp.maximum(jnp.abs(dpart_scr[hd]), pbe_scr[hd]))

    _even_tail(z_scr, hm_scr, cat_scr, onw_ref)
    cx = z_scr[:, C_COL:C_COL + SC_WIDTH] * z_scr[:, X_COL:X_COL + SC_WIDTH]
    sub = lax.broadcasted_iota(jnp.int32, (n_tok, SC_WIDTH), 0) % seq_len
    p1 = sc_ref[...]
    s1 = jnp.where(sub < 1, pltpu.roll(p1, n_tok - 1, 0), pltpu.roll(cx, 1, 0))
    s2 = jnp.where(sub < 2, p1, pltpu.roll(cx, 2, 0))
    cw = cw_ref[...]
    u = s2 * cw[0:1] + s1 * cw[1:2] + cx * cw[2:3]
    sco_ref[...] = pltpu.roll(cx, n_tok - (seq_len - 2), 0)
    cat_scr[:, ML_WIDTH:] = (z_scr[:, B_COL:B_COL + SC_WIDTH] * u).astype(BF16)

    y = _dot(cat_scr[...], wout_ref[...])
    xo_ref[...] = x_ref[...] + g1_scr[...] * y


def _mix_even_sample_call(x, mod, l, nw, win, wkg, bifc, bifr, onw, cw, wout, c0, n_tok, n_tok_t, m_col, m_row,
                          sc_pad, co_prev):
    n_rows = x.shape[0]
    n_seq = c0.shape[1]
    seq_len = n_rows // n_seq
    blk = SEQ_BLK * seq_len
    nb = n_seq // SEQ_BLK
    li = l // 2
    head_blk = (ML_HEADS, blk, LANES)
    return _stacked_call(
        _mix_even_sample_kernel, 16, 1, co_prev, li,
        lambda lead, idx: pl.BlockSpec((lead, SEQ_BLK, ML_HEADS, ML_DK, ML_DV), lambda i: (idx, i, 0, 0, 0)),
        grid=(nb,),
        in_specs=[
            pl.BlockSpec((blk, D_MODEL), lambda i: (i, 0)),
            pl.BlockSpec((None, N_MOD, SEQ_BLK, D_MODEL), lambda i: (l, 0, i, 0)),
            pl.BlockSpec((None, 1, D_MODEL), lambda i: (l, 0, 0)),
        ] + _even_weight_specs(li, lambda i: (li, 0, 0)) + [
            pl.BlockSpec((None, SEQ_BLK, ML_HEADS, ML_DK, ML_DV), lambda i: (li, i, 0, 0, 0)),
            pl.BlockSpec((None, ML_HEADS, blk, ML_DK), lambda i: (li, 0, i, 0)),
            pl.BlockSpec((None, ML_HEADS, ML_DK, blk), lambda i: (li, 0, 0, i)),
            pl.BlockSpec((None, blk, LANES), lambda i: (li, i, 0)),
            pl.BlockSpec((None, SUBLANES, blk), lambda i: (li, 0, i)),
            pl.BlockSpec((None, blk, SC_WIDTH), lambda i: (li, i, 0)),
        ],
        out_specs=[
            pl.BlockSpec((blk, D_MODEL), lambda i: (i, 0)),
            None,
            pl.BlockSpec((ML_HEADS, SEQ_BLK, ML_DK), lambda i: (0, i, 0)),
            pl.BlockSpec((SUBLANES, blk), lambda i: (0, i)),
            pl.BlockSpec((blk, SC_WIDTH), lambda i: (i, 0)),
        ],
        out_shape=[
            jax.ShapeDtypeStruct((n_rows, D_MODEL), F32),
            jax.ShapeDtypeStruct(c0.shape, F32),
            jax.ShapeDtypeStruct((ML_HEADS, n_seq, ML_DK), F32),
            jax.ShapeDtypeStruct((SUBLANES, n_rows), F32),
            jax.ShapeDtypeStruct((n_rows, SC_WIDTH), F32),
        ],
        scratch_shapes=[
            pltpu.VMEM((blk, D_MODEL), F32),
            pltpu.VMEM((blk, IN_W), F32),
            pltpu.VMEM((KG_ROWS, blk), F32),
            pltpu.VMEM((blk, ML_WIDTH), F32),
            pltpu.VMEM((blk, ML_WIDTH + SC_WIDTH), BF16),
            pltpu.VMEM((blk, D_MODEL), F32),
            pltpu.VMEM(head_blk, F32),
            pltpu.VMEM(head_blk, F32),
            pltpu.VMEM(head_blk, F32),
            pltpu.VMEM(head_blk, F32),
            pltpu.VMEM((ML_HEADS, ML_DK, blk), F32),
            pltpu.VMEM(head_blk, BF16),
            pltpu.VMEM(head_blk, F32),
            pltpu.VMEM(head_blk, F32),
        ],
        compiler_params=_cparams(1),
        name="mix_even_sample",
    )(x, mod, nw, win, wkg, bifc, bifr, onw, cw, wout, c0, n_tok, n_tok_t, m_col, m_row, sc_pad)


def _split2(x):
    hi = x.astype(BF16)
    lo = (x - hi.astype(F32)).astype(BF16)
    return jnp.concatenate([hi, lo], axis=1)


def _head_lane_mats():
    r = lax.broadcasted_iota(jnp.int32, (2 * LANES, LANES), 0) % LANES
    c = lax.broadcasted_iota(jnp.int32, (2 * LANES, LANES), 1)
    hsum = ((r // HEAD_DIM) == (c // HEAD_DIM)).astype(BF16)
    half = HEAD_DIM // 2
    src = jnp.where((c % HEAD_DIM) < half, c + half, c - half)
    return hsum, (r == src).astype(BF16)


def _q_lane_mat():
    r = lax.broadcasted_iota(jnp.int32, (2 * LANES, 2 * LANES), 0)
    c = lax.broadcasted_iota(jnp.int32, (2 * LANES, 2 * LANES), 1)
    half = HEAD_DIM // 2
    src = jnp.where((c % HEAD_DIM) < half, c + half, c - half)
    top = (r < LANES) & (c < LANES) & ((r // HEAD_DIM) == (c // HEAD_DIM))
    return (top | ((r >= LANES) & (c >= LANES) & (r == src))).astype(BF16)


def _q_norm_rope(xb, gw, cos, sin, qmat):
    zg = xb * gw
    out = _dot(jnp.concatenate([(xb * xb).astype(BF16), zg.astype(BF16)], axis=1), qmat)
    ms = out[:, :LANES] * (1.0 / HEAD_DIM)
    return lax.rsqrt(ms + EPS) * (zg * cos + out[:, LANES:] * sin)


def _qk_norm_rope(xb, gw, cos, sin, hsum, rot_mat):
    ms = _dot(_split2(xb * xb), hsum) * (1.0 / HEAD_DIM)
    zg = xb * gw
    rot = _dot(_split2(zg), rot_mat)
    return lax.rsqrt(ms + EPS) * (zg * cos + rot * sin)


def _sink_rows(sink8, reps):
    return jnp.concatenate(
        [jnp.broadcast_to(sink8[r:r + 1, :], (reps, LANES)) for r in range(SUBLANES)], axis=0)


def _sink_col(sink8, reps):
    parts = [jnp.broadcast_to(sink8[r:r + 1, :], (reps, LANES)) for r in range(SUBLANES)]
    return jnp.concatenate(parts, axis=0)[:, 0:1]


def _attn_prompt_kernel(x_ref, mod_ref, nw_ref, wqkv_ref, qnw_ref, knw_ref, cos_ref, sin_ref,
                        sink_ref, wout_ref, xo_ref, ko_ref, vo_ref,
                        z_scr, qm_scr, k_scr, v_scr, o_scr):
    t = pl.program_id(1)
    n_qb = TM // WINDOW

    @pl.when(t == 0)
    def _():
        k_scr[0:WINDOW, :] = jnp.zeros((WINDOW, KV_W), BF16)
        v_scr[0:WINDOW, :] = jnp.zeros((WINDOW, KV_W), BF16)

    x = x_ref[0]
    bi = pl.program_id(0)
    h = _norm_mod(x, nw_ref[...], _mod_row(mod_ref, 1, bi), _mod_row(mod_ref, 0, bi)).astype(BF16)
    z_scr[...] = _dot(h, wqkv_ref[...])
    cos = cos_ref[...]
    sin = sin_ref[...]
    hmats = _head_lane_mats()
    qmat = _q_lane_mat()
    half0 = lax.broadcasted_iota(jnp.int32, (TM, LANES), 1) < HEAD_DIM
    qscale = HEAD_DIM ** -0.5
    for jb in range(Q_W // LANES):
        y = _q_norm_rope(z_scr[:, jb * LANES:(jb + 1) * LANES], qnw_ref[...], cos, sin, qmat) * qscale
        qm_scr[2 * jb] = jnp.where(half0, y, 0.0).astype(BF16)
        qm_scr[2 * jb + 1] = jnp.where(half0, 0.0, y).astype(BF16)
    for p in range(KV_W // LANES):
        kf = _qk_norm_rope(z_scr[:, Q_W + p * LANES:Q_W + (p + 1) * LANES], knw_ref[...], cos, sin, *hmats)
        ko_ref[0, :, p * LANES:(p + 1) * LANES] = kf[TM - WINDOW:TM]
        k_scr[WINDOW:WINDOW + TM, p * LANES:(p + 1) * LANES] = kf.astype(BF16)
    vf = z_scr[:, Q_W + KV_W:Q_W + 2 * KV_W]
    vo_ref[0] = vf[TM - WINDOW:TM]
    v_scr[WINDOW:WINDOW + TM, :] = vf.astype(BF16)

    r = lax.broadcasted_iota(jnp.int32, (8 * WINDOW, 2 * WINDOW), 0) % WINDOW
    c = lax.broadcasted_iota(jnp.int32, (8 * WINDOW, 2 * WINDOW), 1)
    valid = ((c < WINDOW) & (c > r)) | ((c >= WINDOW) & ((c - WINDOW) <= r))
    first_lim = jnp.where(t == 0, WINDOW, 0)
    half0q = lax.broadcasted_iota(jnp.int32, (4 * WINDOW, LANES), 1) < HEAD_DIM
    ones_kv = jnp.ones((2 * WINDOW, LANES), BF16)
    for qb in range(n_qb):
        rows = slice(qb * WINDOW, (qb + 1) * WINDOW)
        krows = slice(qb * WINDOW, (qb + 2) * WINDOW)
        vmask = (valid & (c >= first_lim)) if qb == 0 else valid
        for p in range(KV_W // LANES):
            kb = k_scr[krows, p * LANES:(p + 1) * LANES]
            vb = v_scr[krows, p * LANES:(p + 1) * LANES]
            qs = jnp.concatenate([qm_scr[2 * (4 * p + i) + e, rows, :] for e in range(2) for i in range(4)],
                                 axis=0)
            s = jnp.where(vmask, _dot_nt(qs, kb), -jnp.inf)
            sk = _sink_rows(sink_ref[p], WINDOW)
            mx = jnp.maximum(jnp.max(s, axis=-1, keepdims=True), sk)
            pr = jnp.exp(s - jnp.concatenate([mx, mx], axis=1))
            o2 = _dot(pr.astype(BF16), jnp.concatenate([vb, ones_kv], axis=1))
            den = o2[:, LANES:] + jnp.exp(sk - mx)
            o = o2[:, :LANES] * (1.0 / den)
            merged = jnp.where(half0q, o[0:4 * WINDOW], o[4 * WINDOW:])
            for i in range(4):
                o_scr[rows, (4 * p + i) * LANES:(4 * p + i + 1) * LANES] = (
                    merged[i * WINDOW:(i + 1) * WINDOW].astype(BF16))

    y = _dot(o_scr[...], wout_ref[...])
    xo_ref[0] = x + _mod_row(mod_ref, 2, bi) * y
    k_scr[0:WINDOW, :] = k_scr[TM:TM + WINDOW, :]
    v_scr[0:WINDOW, :] = v_scr[TM:TM + WINDOW, :]


def _attn_prompt_call(x, mod, n_seq_rows, l, nw, wqkv, qnw, knw, cos, sin, sink, wout):
    B, S, _ = x.shape
    nt = S // TM
    lj = l // 2
    layer3 = lambda b, t: (lj, 0, 0)
    return pl.pallas_call(
        _attn_prompt_kernel,
        grid=(B, nt),
        in_specs=[
            pl.BlockSpec((1, TM, D_MODEL), lambda b, t: (b, t, 0)),
            pl.BlockSpec((None, N_MOD, SUBLANES, D_MODEL), lambda b, t: (l, 0, n_seq_rows // SUBLANES, 0)),
            pl.BlockSpec((None, 1, D_MODEL), lambda b, t: (l, 0, 0)),
            pl.BlockSpec((None, D_MODEL, Q_W + 2 * KV_W), layer3),
            pl.BlockSpec((None, 1, LANES), layer3),
            pl.BlockSpec((None, 1, LANES), layer3),
            pl.BlockSpec((TM, LANES), lambda b, t: (t, 0)),
            pl.BlockSpec((TM, LANES), lambda b, t: (t, 0)),
            pl.BlockSpec((None, 2, SUBLANES, LANES), lambda b, t: (lj, 0, 0, 0)),
            pl.BlockSpec((None, Q_W, D_MODEL), layer3),
        ],
        out_specs=[
            pl.BlockSpec((1, TM, D_MODEL), lambda b, t: (b, t, 0)),
            pl.BlockSpec((1, WINDOW, KV_W), lambda b, t: (b, 0, 0)),
            pl.BlockSpec((1, WINDOW, KV_W), lambda b, t: (b, 0, 0)),
        ],
        out_shape=[
            jax.ShapeDtypeStruct((B, S, D_MODEL), F32),
            jax.ShapeDtypeStruct((B, WINDOW, KV_W), F32),
            jax.ShapeDtypeStruct((B, WINDOW, KV_W), F32),
        ],
        scratch_shapes=[
            pltpu.VMEM((TM, Q_W + 2 * KV_W), F32),
            pltpu.VMEM((2 * Q_W // LANES, TM, LANES), BF16),
            pltpu.VMEM((TM + WINDOW, KV_W), BF16),
            pltpu.VMEM((TM + WINDOW, KV_W), BF16),
            pltpu.VMEM((TM, Q_W), BF16),
        ],
        compiler_params=_cparams(2),
        name="attn_prompt",
    )(x, mod, nw, wqkv, qnw, knw, cos, sin, sink, wout)


def _attn_sample_kernel(x_ref, mod_ref, nw_ref, wqkv_ref, qnw_ref, knw_ref, cos_ref, sin_ref,
                        sink_ref, wout_ref, kc_ref, vc_ref,
                        xo_ref, kco_ref, vco_ref,
                        h_scr, g1_scr, z_scr, qm_scr, kn_scr, o_scr):
    n_tok = x_ref.shape[0]
    seq_len = n_tok // SEQ_BLK
    win = kc_ref.shape[1]

    def modulate(b, carry):
        rows = pl.ds(pl.multiple_of(b * seq_len, seq_len), seq_len)
        h_scr[rows, :] = _norm_mod(x_ref[rows, :], nw_ref[...], _mod_row(mod_ref, 1, b), _mod_row(mod_ref, 0, b))
        g1_scr[rows, :] = jnp.broadcast_to(_mod_row(mod_ref, 2, b), (seq_len, D_MODEL))
        return carry

    lax.fori_loop(0, SEQ_BLK, modulate, 0, unroll=8)
    z_scr[...] = _dot(h_scr[...].astype(BF16), wqkv_ref[...])
    cos = cos_ref[...]
    sin = sin_ref[...]
    hmats = _head_lane_mats()
    half0 =lax.broadcasted_iota(jnp.int32, (n_tok, LANES), 1) < HEAD_DIM
    qscale = HEAD_DIM ** -0.5
    for jb in range(Q_W // LANES):
        y = _qk_norm_rope(z_scr[:, jb * LANES:(jb + 1) * LANES], qnw_ref[...], cos, sin, *hmats) * qscale
        qm_scr[2 * jb] = jnp.where(half0, y, 0.0)
        qm_scr[2 * jb + 1] = jnp.where(half0, 0.0, y)
    for p in range(KV_W // LANES):
        kn_scr[:, p * LANES:(p + 1) * LANES] = _qk_norm_rope(
            z_scr[:, Q_W + p * LANES:Q_W + (p + 1) * LANES], knw_ref[...], cos, sin, *hmats)

    n_q = 8 * seq_len
    tq = lax.broadcasted_iota(jnp.int32, (SEQ_BLK, n_q, 2 * win), 1) % seq_len
    cc = lax.broadcasted_iota(jnp.int32, (SEQ_BLK, n_q, 2 * win), 2)
    valid = ((cc < win) & (cc > tq)) | ((cc >= 2 * win - seq_len) & ((cc - (2 * win - seq_len)) <= tq))
    half0q = lax.broadcasted_iota(jnp.int32, (SEQ_BLK, n_q // 2, LANES), 2) < HEAD_DIM
    ones_kv = jnp.ones((SEQ_BLK, 2 * win, LANES), BF16)

    kc = kc_ref[...]
    vc = vc_ref[...]
    knew = jnp.concatenate([kc[:, seq_len:], kn_scr[...].reshape(SEQ_BLK, seq_len, KV_W)], axis=1)
    vnew = jnp.concatenate(
        [vc[:, seq_len:], z_scr[:, Q_W + KV_W:Q_W + 2 * KV_W].reshape(SEQ_BLK, seq_len, KV_W)], axis=1)
    kco_ref[...] = knew
    vco_ref[...] = vnew
    for p in range(KV_W // LANES):
        lanes = slice(p * LANES, (p + 1) * LANES)
        qs = jnp.concatenate([qm_scr[2 * (4 * p + i) + e].reshape(SEQ_BLK, seq_len, LANES)
                              for e in range(2) for i in range(4)], axis=1).astype(BF16)
        kk = jnp.concatenate([kc[:, :, lanes], knew[:, :, lanes]], axis=1).astype(BF16)
        vv = jnp.concatenate([vc[:, :, lanes], vnew[:, :, lanes]], axis=1).astype(BF16)
        s = jnp.einsum("bqd,bkd->bqk", qs, kk, preferred_element_type=F32)
        s = jnp.where(valid, s, -jnp.inf)
        sk = _sink_rows(sink_ref[p], seq_len)[None]
        mx = jnp.maximum(jnp.max(s, axis=-1, keepdims=True), sk)
        pr = jnp.exp(s - jnp.concatenate([mx, mx], axis=-1))
        o2 = jnp.einsum("bqk,bkd->bqd", pr.astype(BF16), jnp.concatenate([vv, ones_kv], axis=-1),
                        preferred_element_type=F32)
        o = o2[:, :, :LANES] * (1.0 / (o2[:, :, LANES:] + jnp.exp(sk - mx)))
        merged = jnp.where(half0q, o[:, 0:n_q // 2], o[:, n_q // 2:])
        for i in range(4):
            o_scr[:, (4 * p + i) * LANES:(4 * p + i + 1) * LANES] = (
                merged[:, i * seq_len:(i + 1) * seq_len].reshape(n_tok, LANES))

    y = _dot(o_scr[...].astype(BF16), wout_ref[...])
    xo_ref[...] = x_ref[...] + g1_scr[...] * y


def _attn_sample_call(x, mod, l, nw, wqkv, qnw, knw, cos, sin, sink, wout, kc, vc):
    n_rows = x.shape[0]
    _, n_seq, win, _ = kc.shape
    lj = l // 2
    seq_len = n_rows // n_seq
    blk = SEQ_BLK * seq_len
    nb = n_seq // SEQ_BLK
    const2 = lambda i: (0, 0)
    return pl.pallas_call(
        _attn_sample_kernel,
        input_output_aliases={10: 1, 11: 2},
        grid=(nb,),
        in_specs=[
            pl.BlockSpec((blk, D_MODEL), lambda i: (i, 0)),
            pl.BlockSpec((None, N_MOD, SEQ_BLK, D_MODEL), lambda i: (l, 0, i, 0)),
            pl.BlockSpec((None, 1, D_MODEL), lambda i: (l, 0, 0)),
            pl.BlockSpec((None, D_MODEL, Q_W + 2 * KV_W), lambda i: (lj, 0, 0)),
            pl.BlockSpec((None, 1, LANES), lambda i: (lj, 0, 0)),
            pl.BlockSpec((None, 1, LANES), lambda i: (lj, 0, 0)),
            pl.BlockSpec((blk, LANES), const2),
            pl.BlockSpec((blk, LANES), const2),
            pl.BlockSpec((None, 2, SUBLANES, LANES), lambda i: (lj, 0, 0, 0)),
            pl.BlockSpec((None, Q_W, D_MODEL), lambda i: (lj, 0, 0)),
            pl.BlockSpec((None, SEQ_BLK, win, KV_W), lambda i: (lj, i, 0, 0)),
            pl.BlockSpec((None, SEQ_BLK, win, KV_W), lambda i: (lj, i, 0, 0)),
        ],
        out_specs=[
            pl.BlockSpec((blk, D_MODEL), lambda i: (i, 0)),
            pl.BlockSpec((None, SEQ_BLK, win, KV_W), lambda i: (lj, i, 0, 0)),
            pl.BlockSpec((None, SEQ_BLK, win, KV_W), lambda i: (lj, i, 0, 0)),
        ],
        out_shape=[
            jax.ShapeDtypeStruct((n_rows, D_MODEL), F32),
            jax.ShapeDtypeStruct(kc.shape, F32),
            jax.ShapeDtypeStruct(vc.shape, F32),
        ],
        scratch_shapes=[
            pltpu.VMEM((blk, D_MODEL), F32),
            pltpu.VMEM((blk, D_MODEL), F32),
            pltpu.VMEM((blk, Q_W + 2 * KV_W), F32),
            pltpu.VMEM((2 * Q_W // LANES, blk, LANES), F32),
            pltpu.VMEM((blk, KV_W), F32),
            pltpu.VMEM((blk, Q_W), F32),
        ],
        compiler_params=_cparams(1),
        name="attn_sample",
    )(x, mod, nw, wqkv, qnw, knw, cos, sin, sink, wout, kc, vc)


def _rope_tables(pos):
    half = HEAD_DIM // 2
    inv = ROPE_THETA ** (-jnp.arange(half, dtype=F32) / half)
    ang = pos.astype(F32)[:, None] * inv[None, :]
    cos = jnp.cos(ang)
    sin = jnp.sin(ang)
    return jnp.tile(cos, (1, 4)), jnp.concatenate([-sin, sin, -sin, sin], axis=1)


def _rope_tables_range(n):
    half = HEAD_DIM // 2
    inv = ROPE_THETA ** (-jnp.arange(half, dtype=F32) / half)
    a_hi = (jnp.arange(n // WINDOW, dtype=jnp.int32) * WINDOW).astype(F32)[:, None] * inv[None, :]
    a_lo = jnp.arange(WINDOW, dtype=jnp.int32).astype(F32)[:, None] * inv[None, :]
    ch, sh = jnp.cos(a_hi)[:, None, :], jnp.sin(a_hi)[:, None, :]
    cl, sl = jnp.cos(a_lo)[None], jnp.sin(a_lo)[None]
    cos = (ch * cl - sh * sl).reshape(n, half)
    sin = (sh * cl + ch * sl).reshape(n, half)
    return jnp.tile(cos, (1, 4)), jnp.concatenate([-sin, sin, -sin, sin], axis=1)


def _prep_attn(w_qkv, q_norm, k_norm, sink, w_out):
    n_l = w_qkv.shape[0]
    perm = np.asarray(HEAD_PERM)
    wq = w_qkv[:, :, :Q_W].reshape(n_l, D_MODEL, ATT_HEADS, HEAD_DIM)[:, :, perm].reshape(n_l, D_MODEL, Q_W)
    wqkv = jnp.concatenate([wq, w_qkv[:, :, Q_W:]], axis=2).astype(BF16)
    wout = w_out.reshape(n_l, ATT_HEADS, HEAD_DIM, D_MODEL)[:, perm].reshape(n_l, Q_W, D_MODEL).astype(BF16)
    qnw = jnp.tile(q_norm, (1, 2))[:, None, :]
    knw = jnp.tile(k_norm, (1, 2))[:, None, :]
    idx = np.asarray([[perm[2 * (4 * p + i) + e] for e in range(2) for i in range(4)] for p in range(2)])
    sink_arr = jnp.broadcast_to(sink[:, idx][..., None], (n_l, 2, SUBLANES, LANES)).astype(F32)
    return wqkv, qnw, knw, sink_arr, wout


def kernel(x_prompt, x_sample, c_prompt, c_sample, state_mlstm_C, state_mlstm_n, state_mlstm_m, state_sconv, cache_win_k, cache_win_v, state_ffn_conv, norm1, norm2, w_ada, b_ada, a_w_in, a_b_if, a_out_norm, a_conv_w, a_w_out, c_w_qkv, c_q_norm, c_k_norm, c_sink, c_w_out, f_w_up, f_conv_w, f_w_down):
    B, S, _ = x_prompt.shape
    NS, SL, _ = x_sample.shape
    assert S % TM == 0 and NS % SEQ_BLK == 0 and SEQ_BLK * SL == CHUNK and SL == SUBLANES

    assert B <= SUBLANES
    c_all = jnp.concatenate([c_sample, c_prompt, jnp.zeros((SUBLANES - B, D_MODEL), F32)], axis=0)
    mod = _ada_call(c_all, w_ada, b_ada)

    win_all, kg_all = _prep_win_call(a_w_in)
    wkg_all = jnp.swapaxes(kg_all, 1, 2)
    b_i, b_f = a_b_if[:, :ML_HEADS], a_b_if[:, ML_HEADS:]
    lane_pad = jnp.zeros((a_b_if.shape[0], LANES - 2 * ML_HEADS), F32)
    bifc_all = jnp.stack([jnp.concatenate([b_i, b_i, lane_pad], axis=1),
                          jnp.concatenate([b_f, b_f, lane_pad], axis=1)], axis=1)
    bifr_all = jnp.broadcast_to(jnp.concatenate([b_i, b_i, b_f, b_f], axis=1)[:, :, None],
                                (a_b_if.shape[0], 2 * SUBLANES, LANES))
    wout_a_all = _cast_call(a_w_out)
    wup_all = _cast_call(f_w_up)
    wdn_all = _cast_call(f_w_down)
    onw_all = a_out_norm.reshape(-1, 1, ML_WIDTH)
    norm1_r = norm1.reshape(DEPTH, 1, D_MODEL)
    norm2_r = norm2.reshape(DEPTH, 1, D_MODEL)
    win_buf = cache_win_k.shape[2]
    kc_all = cache_win_k.reshape(-1, NS, win_buf, KV_W)
    vc_all = cache_win_v.reshape(-1, NS, win_buf, KV_W)

    cos_p, sin_p = _rope_tables_range(S)
    cos_s, sin_s = _rope_tables(PAST_LEN + jnp.arange(SL, dtype=jnp.int32))
    cos_s = jnp.tile(cos_s, (SEQ_BLK, 1))
    sin_s = jnp.tile(sin_s, (SEQ_BLK, 1))

    n_tok = jnp.repeat(state_mlstm_n.transpose(0, 2, 1, 3), SL, axis=2)
    n_tok_t = jnp.swapaxes(n_tok, 2, 3)
    m_rep = jnp.repeat(state_mlstm_m, SL, axis=1)
    m_col = jnp.concatenate([m_rep, m_rep, jnp.zeros(m_rep.shape[:2] + (LANES - 2 * ML_HEADS,), F32)], axis=2)
    m_row = jnp.swapaxes(jnp.concatenate([m_rep, m_rep], axis=2), 1, 2)
    sc_pad = jnp.pad(state_sconv, ((0, 0), (0, 0), (0, SL - 2), (0, 0))).reshape(-1, NS * SL, SC_WIDTH)

    wqkv, qnw, knw, sink, wout = _prep_attn(c_w_qkv, c_q_norm, c_k_norm, c_sink, c_w_out)

    xp = x_prompt
    xs = x_sample.reshape(NS * SL, D_MODEL)
    p_C, p_n, p_m, p_sc, p_wk, p_wv, p_ffn = [], [], [], [], [], [], []
    s_n, s_m, s_sc = [], [], []
    s_C = s_ffn = None
    s_wk, s_wv = kc_all, vc_all

    for l in range(DEPTH):
        if l % 2 == 0:
            i = l // 2
            even_w = (norm1_r, win_all, wkg_all, bifc_all, bifr_all, onw_all, a_conv_w, wout_a_all)
            xp, co, no, mo, sco = _mix_even_prompt_call(xp, mod, NS, l, *even_w)
            p_C.append(co)
            p_n.append(no[:, :, :, 0])
            p_m.append(mo[:, :ML_HEADS, 0])
            p_sc.append(sco[:, SUBLANES - 2:, :])

            xs, s_C, no, mo, sco = _mix_even_sample_call(xs, mod, l, *even_w, state_mlstm_C, n_tok,
                                                         n_tok_t, m_col, m_row, sc_pad, s_C)
            s_n.append(no.transpose(1, 0, 2))
            s_m.append(mo[:ML_HEADS, ::SL].T)
            s_sc.append(sco.reshape(NS, SL, SC_WIDTH)[:, :2])
        else:
            xp, ko, vo = _attn_prompt_call(xp, mod, NS, l, norm1_r, wqkv, qnw, knw, cos_p, sin_p, sink, wout)
            p_wk.append(ko.reshape(B, WINDOW, KV_HEADS, HEAD_DIM))
            p_wv.append(vo.reshape(B, WINDOW, KV_HEADS, HEAD_DIM))
            xs, s_wk, s_wv = _attn_sample_call(xs, mod, l, norm1_r, wqkv, qnw, knw, cos_s, sin_s, sink, wout,
                                               s_wk, s_wv)

        xp, st = _ffn_prompt_call(xp, mod, NS, l, norm2_r, wup_all, f_conv_w, wdn_all)
        p_ffn.append(st[:, SUBLANES - 2:, :])
        xs, s_ffn = _ffn_sample_call(xs, mod, l, norm2_r, state_ffn_conv, wup_all, f_conv_w, wdn_all, s_ffn)

    kv_shape = (-1, NS, win_buf, KV_HEADS, HEAD_DIM)
    return (xp, xs.reshape(NS, SL, D_MODEL),
            jnp.stack(p_C), jnp.stack(p_n), jnp.stack(p_m), jnp.stack(p_sc),
            jnp.stack(p_wk), jnp.stack(p_wv), jnp.stack(p_ffn),
            s_C, jnp.stack(s_n), jnp.stack(s_m), jnp.stack(s_sc),
            s_wk.reshape(kv_shape), s_wv.reshape(kv_shape), s_ffn)
```

```python
import functools

import jax
import jax.numpy as jnp
import numpy as np
from jax import lax
from jax.experimental import pallas as pl
from jax.experimental.pallas import tpu as pltpu

F32 = jnp.float32
BF16 = jnp.bfloat16

D_MODEL = 1024
DEPTH = 4
PAST_LEN = 8192
ML_HEADS = 4
ML_DK = 128
ML_DV = 128
ML_WIDTH = ML_HEADS * ML_DV
SC_WIDTH = D_MODEL // 2
ATT_HEADS = 16
KV_HEADS = 4
HEAD_DIM = 64
WINDOW = 128
ROPE_THETA = 10000.0
D_FF = 2816
EPS = 1e-6

LANES = 128
SUBLANES = 8
VMEM_LIMIT = 56 * 1024 * 1024

TM = 1024
TM_FFN = 1024
CHUNK = 128
SEQ_BLK = 16
FC = 256
NCH = D_FF // FC
FS_GATE = 2
FC_S = D_FF // FS_GATE
Q_COL = 0
V_COL = Q_COL + ML_WIDTH
O_COL = V_COL + ML_WIDTH
G1_COL = O_COL + ML_WIDTH
G2_COL = G1_COL + LANES
B_COL = G2_COL + LANES
C_COL = B_COL + SC_WIDTH
X_COL = C_COL + SC_WIDTH
IN_W = X_COL + SC_WIDTH
KG_ROWS = ML_WIDTH + 16
SRC_K = ML_WIDTH
SRC_V = 2 * ML_WIDTH
SRC_G = 4 * ML_WIDTH
SRC_B = SRC_G + 2 * ML_HEADS
Q_W = ATT_HEADS * HEAD_DIM
KV_W = KV_HEADS * HEAD_DIM
HEAD_PERM = (0, 4, 1, 5, 2, 6, 3, 7, 8, 12, 9, 13, 10, 14, 11, 15)


def _cparams(n_axes):
    return pltpu.CompilerParams(dimension_semantics=("arbitrary",) * n_axes,
                                vmem_limit_bytes=VMEM_LIMIT)


def _stacked_call(kernel, n_in, out_idx, prev, slab, slab_spec, **kw):
    specs = list(kw.pop("in_specs"))
    out_specs = list(kw.pop("out_specs"))
    n_slabs = kw["out_shape"][out_idx].shape[0]
    if prev is None:
        out_specs[out_idx] = slab_spec(n_slabs, 0)

        def body(*refs):
            refs = list(refs)
            whole = refs[n_in + out_idx]
            for s in range(n_slabs):
                if s != slab:
                    whole[s] = jnp.zeros(whole.shape[1:], whole.dtype)
            refs[n_in + out_idx] = whole.at[slab]
            return kernel(*refs)

        return pl.pallas_call(body, in_specs=specs, out_specs=out_specs, **kw)

    out_specs[out_idx] = slab_spec(None, slab)

    def body(*refs):
        return kernel(*refs[:n_in], *refs[n_in + 1:])

    call = pl.pallas_call(body, in_specs=specs + [pl.BlockSpec(memory_space=pl.ANY)], out_specs=out_specs,
                          input_output_aliases={n_in: out_idx}, **kw)
    return lambda *args: call(*args, prev)


def _dot(a, b):
    return jnp.dot(a, b, preferred_element_type=F32)


def _dot_nt(a, b):
    return lax.dot_general(a, b, (((1,), (1,)), ((), ())), preferred_element_type=F32)


def _dot_tn(a, b):
    return lax.dot_general(a, b, (((0,), (0,)), ((), ())), preferred_element_type=F32)


def _dot_exact01(m, a):
    a1 = a.astype(BF16)
    r1 = a - a1.astype(F32)
    a2 = r1.astype(BF16)
    a3 = (r1 - a2.astype(F32)).astype(BF16)
    return _dot(m, a1) + _dot(m, a2) + _dot(m, a3)


def _norm_mod(x, nw, sc, sh):
    ms = jnp.mean(x * x, axis=-1, keepdims=True)
    return (x * lax.rsqrt(ms + EPS) * nw) * (1.0 + sc) + sh


def _sigmoid(x):
    return 1.0 / (1.0 + jnp.exp(-x))


def _log_sigmoid(x):
    return jnp.minimum(x, 0.0) - jnp.log(1.0 + jnp.exp(-jnp.abs(x)))


N_MOD = 6


ADA_KINDS = 3


def _ada_kernel(c_ref, w_ref, b_ref, o_ref):
    c = c_ref[...]
    s = (c * _sigmoid(c)).astype(BF16)
    for k in range(ADA_KINDS):
        cols = slice(k * D_MODEL, (k + 1) * D_MODEL)
        o_ref[k] = _dot(s, w_ref[:, cols].astype(BF16)) + b_ref[:, cols]


def _ada_call(c_all, w_ada, b_ada):
    rows = c_all.shape[0]
    return pl.pallas_call(
        _ada_kernel,
        grid=(DEPTH, N_MOD // ADA_KINDS),
        in_specs=[
            pl.BlockSpec((rows, D_MODEL), lambda l, k: (0, 0)),
            pl.BlockSpec((None, D_MODEL, ADA_KINDS * D_MODEL), lambda l, k: (l, 0, k)),
            pl.BlockSpec((None, 1, ADA_KINDS * D_MODEL), lambda l, k: (l, 0, k)),
        ],
        out_specs=pl.BlockSpec((None, ADA_KINDS, rows, D_MODEL), lambda l, k: (l, k, 0, 0)),
        out_shape=jax.ShapeDtypeStruct((DEPTH, N_MOD, rows, D_MODEL), F32),
        compiler_params=_cparams(2),
        name="adaln_mod",
    )(c_all, w_ada, b_ada.reshape(DEPTH, 1, N_MOD * D_MODEL))


CAST_ROWS = 256


def _cast_kernel(w_ref, o_ref):
    o_ref[...] = w_ref[...].astype(BF16)


CAST_BLOCK_BYTES = 6 * 1024 * 1024


def _cast_call(w):
    n_l, rows, cols = w.shape
    fits = [r for r in range(2 * SUBLANES, rows + 1, 2 * SUBLANES)
            if rows % r == 0 and r * cols * 4 <= CAST_BLOCK_BYTES]
    tr = max(fits) if fits else rows
    return pl.pallas_call(
        _cast_kernel,
        grid=(n_l, rows // tr),
        in_specs=[pl.BlockSpec((None, tr, cols), lambda l, r: (l, r, 0))],
        out_specs=pl.BlockSpec((None, tr, cols), lambda l, r: (l, r, 0)),
        out_shape=jax.ShapeDtypeStruct(w.shape, BF16),
        compiler_params=_cparams(2),
        name="cast_bf16",
    )(w)


def _prep_win_kernel(wt_ref, o_ref, kg_ref):
    def put(dst_col, src_row, n):
        for c in range(0, n, LANES):
            o_ref[:, dst_col + c:dst_col + c + LANES] = wt_ref[src_row + c:src_row + c + LANES, :].T.astype(BF16)

    put(Q_COL, 0, ML_WIDTH)
    put(V_COL, SRC_V, 2 * ML_WIDTH)
    put(B_COL, SRC_B, 3 * SC_WIDTH)
    kg_ref[0:ML_WIDTH, :] = wt_ref[SRC_K:SRC_V, :].astype(BF16)

    g8 = wt_ref[SRC_G:SRC_G + SUBLANES, :]
    g16 = jnp.concatenate([g8, g8], axis=0)
    row = lax.broadcasted_iota(jnp.int32, g16.shape, 0)
    mid = (row >= ML_HEADS) & (row < 3 * ML_HEADS)
    kg_ref[ML_WIDTH:KG_ROWS, :] = jnp.where(mid, pltpu.roll(g16, ML_HEADS, 0), g16).astype(BF16)

    gt = wt_ref[SRC_G:SRC_G + LANES, :].T
    ig = gt[:, 0:ML_HEADS]
    fg = gt[:, ML_HEADS:2 * ML_HEADS]
    pad = jnp.zeros((gt.shape[0], LANES - 2 * ML_HEADS), F32)
    o_ref[:, G1_COL:G2_COL] = jnp.concatenate([ig, ig, pad], axis=1).astype(BF16)
    o_ref[:, G2_COL:B_COL] = jnp.concatenate([fg, fg, pad], axis=1).astype(BF16)


def _prep_win_call(a_w_in_t):
    n_l, in_a, _ = a_w_in_t.shape
    return pl.pallas_call(
        _prep_win_kernel,
        grid=(n_l,),
        in_specs=[pl.BlockSpec((None, in_a, D_MODEL), lambda l: (l, 0, 0))],
        out_specs=[pl.BlockSpec((None, D_MODEL, IN_W), lambda l: (l, 0, 0)),
                   pl.BlockSpec((None, KG_ROWS, D_MODEL), lambda l: (l, 0, 0))],
        out_shape=[jax.ShapeDtypeStruct((n_l, D_MODEL, IN_W), BF16),
                   jax.ShapeDtypeStruct((n_l, KG_ROWS, D_MODEL), BF16)],
        compiler_params=_cparams(1),
        name="prep_w_in",
    )(a_w_in_t)


def _mod_row(mod_ref, kind, b):
    return mod_ref[kind, pl.ds(b, 1), :]


def _ffn_prompt_kernel(x_ref, mod_ref, nw_ref, wup_ref, cw_ref, wdn_ref,
                       xo_ref, st_ref, h_scr, act_scr, carry_scr):
    b = pl.program_id(0)
    t = pl.program_id(1)

    @pl.when(t == 0)
    def _():
        carry_scr[...] = jnp.zeros_like(carry_scr)

    x = x_ref[0]
    h_scr[...] = _norm_mod(x, nw_ref[...], _mod_row(mod_ref, 4, b), _mod_row(mod_ref, 3, b)).astype(BF16)
    for j in range(NCH):
        ys = []
        for col in (j * FC, D_FF + j * FC):
            cols = slice(col, col + FC)
            u = _dot(h_scr[...], wup_ref[:, cols])
            ys.append(_conv3_rows(u, carry_scr[:, cols], cw_ref[:, cols]))
            carry_scr[:, cols] = u[TM_FFN - SUBLANES:TM_FFN]
        g = ys[0]
        act_scr[:, j * FC:(j + 1) * FC] = (g * _sigmoid(g) * ys[1]).astype(BF16)
    y = _dot(act_scr[...], wdn_ref[...])
    xo_ref[0] = x + _mod_row(mod_ref, 5, b) * y
    st_ref[0] = carry_scr[...]


def _ffn_prompt_call(x, mod, n_seq_rows, l, nw, wup, cw, wdn):
    B, S, _ = x.shape
    assert S % TM_FFN == 0
    nt = S // TM_FFN
    once = pl.Buffered(1)
    return pl.pallas_call(
        _ffn_prompt_kernel,
        grid=(B, nt),
        in_specs=[
            pl.BlockSpec((1, TM_FFN, D_MODEL), lambda b, t: (b, t, 0)),
            pl.BlockSpec((None, N_MOD, SUBLANES, D_MODEL), lambda b, t: (l, 0, n_seq_rows // SUBLANES, 0)),
            pl.BlockSpec((None, 1, D_MODEL), lambda b, t: (l, 0, 0)),
            pl.BlockSpec((None, D_MODEL, 2 * D_FF), lambda b, t: (l, 0, 0), pipeline_mode=once),
            pl.BlockSpec((None, 3, 2 * D_FF), lambda b, t: (l, 0, 0)),
            pl.BlockSpec((None, D_FF, D_MODEL), lambda b, t: (l, 0, 0), pipeline_mode=once),
        ],
        out_specs=[
            pl.BlockSpec((1, TM_FFN, D_MODEL), lambda b, t: (b, t, 0)),
            pl.BlockSpec((1, SUBLANES, 2 * D_FF), lambda b, t: (b, 0, 0)),
        ],
        out_shape=[
            jax.ShapeDtypeStruct((B, S, D_MODEL), F32),
            jax.ShapeDtypeStruct((B, SUBLANES, 2 * D_FF), F32),
        ],
        scratch_shapes=[
            pltpu.VMEM((TM_FFN, D_MODEL), BF16),
            pltpu.VMEM((TM_FFN, D_FF), BF16),
            pltpu.VMEM((SUBLANES, 2 * D_FF), F32),
        ],
        compiler_params=_cparams(2),
        name="ffn_prompt",
    )(x, mod, nw, wup, cw, wdn)


def _ffn_sample_kernel(x_ref, mod_ref, nw_ref, s_ref, w_ref, c_ref, wdn_ref,
                       xo_ref, so_ref, h_scr, hb_scr, g2_scr, acc_scr, yg_scr):
    j = pl.program_id(1)
    n_seq = s_ref.shape[0]
    n_rows = x_ref.shape[0]
    n_t = n_rows // n_seq

    @pl.when(j == 0)
    def _():
        def modulate(b, carry):
            rows = pl.ds(pl.multiple_of(b * n_t, n_t), n_t)
            h_scr[rows, :] = _norm_mod(x_ref[rows, :], nw_ref[...], _mod_row(mod_ref, 4, b), _mod_row(mod_ref, 3, b))
            g2_scr[rows, :] = jnp.broadcast_to(_mod_row(mod_ref, 5, b), (n_t, D_MODEL))
            return carry

        lax.fori_loop(0, n_seq, modulate, 0, unroll=8)
        hb_scr[...] = h_scr[...].astype(BF16)
        acc_scr[...] = jnp.zeros_like(acc_scr)

    sub = lax.broadcasted_iota(jnp.int32, (n_seq, n_t, FC_S), 1)
    u3 = _dot(hb_scr[...], w_ref[...]).reshape(n_seq, n_t, FC_S)
    cw = c_ref[...]
    p0 = jnp.broadcast_to(s_ref[:, 0:1, :], (n_seq, n_t, FC_S))
    p1 = jnp.broadcast_to(s_ref[:, 1:2, :], (n_seq, n_t, FC_S))
    s1 = jnp.where(sub < 1, p1, pltpu.roll(u3, 1, 1))
    s2 = jnp.where(sub < 1, p0, jnp.where(sub < 2, p1, pltpu.roll(u3, 2, 1)))
    y = (s2 * cw[0:1] + s1 * cw[1:2] + u3 * cw[2:3]).reshape(n_rows, FC_S)
    so_ref[...] = pltpu.roll(u3, 2, 1)[:, 0:2, :]

    @pl.when(j < FS_GATE)
    def _():
        yg_scr[j] = y

    @pl.when(j >= FS_GATE)
    def _():
        g = yg_scr[j - FS_GATE]
        acc_scr[...] += _dot((g * _sigmoid(g) * y).astype(BF16), wdn_ref[...])

    @pl.when(j == 2 * FS_GATE - 1)
    def _():
        xo_ref[...] = x_ref[...] + g2_scr[...] * acc_scr[...]


def _ffn_sample_call(x, mod, l, nw, st, wup, cw, wdn, so_prev):
    n_seq = st.shape[1] // 2
    n_rows = x.shape[0] // 2
    return _stacked_call(
        _ffn_sample_kernel, 7, 1, so_prev, l,
        lambda lead, idx: pl.BlockSpec((lead, n_seq, 2, FC_S), lambda hf, j: (idx, hf, 0, j)),
        grid=(2, 2 * FS_GATE),
        in_specs=[
            pl.BlockSpec((n_rows, D_MODEL), lambda hf, j: (hf, 0)),
            pl.BlockSpec((None, N_MOD, n_seq, D_MODEL), lambda hf, j: (l, 0, hf, 0)),
            pl.BlockSpec((None, 1, D_MODEL), lambda hf, j: (l, 0, 0)),
            pl.BlockSpec((None, n_seq, 2, FC_S), lambda hf, j: (l, hf, 0, j)),
            pl.BlockSpec((None, D_MODEL, FC_S), lambda hf, j: (l, 0, j)),
            pl.BlockSpec((None, 3, FC_S), lambda hf, j: (l, 0, j)),
            pl.BlockSpec((None, FC_S, D_MODEL), lambda hf, j: (l, jnp.maximum(j - FS_GATE, 0), 0)),
        ],
        out_specs=[
            pl.BlockSpec((n_rows, D_MODEL), lambda hf, j: (hf, 0)),
            None,
        ],
        out_shape=[
            jax.ShapeDtypeStruct(x.shape, F32),
            jax.ShapeDtypeStruct((DEPTH, st.shape[1], 2, 2 * D_FF), F32),
        ],
        scratch_shapes=[
            pltpu.VMEM((n_rows, D_MODEL), F32),
            pltpu.VMEM((n_rows, D_MODEL), BF16),
            pltpu.VMEM((n_rows, D_MODEL), F32),
            pltpu.VMEM((n_rows, D_MODEL), F32),
            pltpu.VMEM((FS_GATE, n_rows, FC_S), F32),
        ],
        compiler_params=_cparams(2),
        name="ffn_sample",
    )(x, mod, nw, st, wup, cw, wdn)


def _chunk_consts(seq_len):
    r = lax.broadcasted_iota(jnp.int32, (CHUNK, CHUNK), 0)
    c = lax.broadcasted_iota(jnp.int32, (CHUNK, CHUNK), 1)
    if seq_len >= CHUNK:
        same = r >= 0
    else:
        same = (r // seq_len) == (c // seq_len)
    mask = same & (c <= r)
    lmat = mask.astype(BF16)
    lmat_t = (same & (r <= c)).astype(BF16)
    return mask, lmat, lmat_t, same.astype(BF16)


def _dot_exact01_r(a, m):
    a1 = a.astype(BF16)
    r1 = a - a1.astype(F32)
    a2 = r1.astype(BF16)
    a3 = (r1 - a2.astype(F32)).astype(BF16)
    return _dot(a1, m) + _dot(a2, m) + _dot(a3, m)


def _pick_cols_t(a, sel):
    a1 = a.astype(BF16)
    r1 = a - a1.astype(F32)
    a2 = r1.astype(BF16)
    a3 = (r1 - a2.astype(F32)).astype(BF16)
    return _dot_nt(sel, a1) + _dot_nt(sel, a2) + _dot_nt(sel, a3)


def _seq_max_lanes(x, seq_len):
    n = x.shape[1]
    pos = lax.broadcasted_iota(jnp.int32, x.shape, 1)
    d = 1
    while d < seq_len:
        partner = jnp.where((pos & d) == 0, pltpu.roll(x, n - d, 1), pltpu.roll(x, d, 1))
        x = jnp.maximum(x, partner)
        d *= 2
    return x


def _seq_prefix_max_rows(x, seq_len):
    pos = lax.broadcasted_iota(jnp.int32, x.shape, 0) & (seq_len - 1)
    d = 1
    while d < seq_len:
        x = jnp.where(pos >= d, jnp.maximum(x, pltpu.roll(x, d, 0)), x)
        d *= 2
    return x


def _seq_last_row(x, seq_len):
    n, w = x.shape
    if seq_len >= n:
        return jnp.broadcast_to(x[n - 1:n], x.shape)
    x3 = x.reshape(n // seq_len, seq_len, w)
    return jnp.broadcast_to(x3[:, seq_len - 1:seq_len, :], x3.shape).reshape(n, w)


def _gates_rows_pre(gt, bias_r, lmat_t, tot, seq_len):
    ig = gt[0:SUBLANES] + bias_r[0:SUBLANES]
    lf = _log_sigmoid(gt[SUBLANES:] + bias_r[SUBLANES:])
    b = _dot_exact01_r(lf, lmat_t)
    bl = _dot_exact01_r(lf, tot)
    v = ig - b
    return bl, v, _seq_max_lanes(v, seq_len)


def _gates_rows_post(pre, mp_r):
    bl, v, vm = pre
    mn = bl + jnp.maximum(mp_r, vm)
    return jnp.exp(bl + v - mn), jnp.exp(bl + mp_r - mn), mn


def _gates_rows(gt, bias_r, mp_r, lmat_t, tot, seq_len):
    pre = _gates_rows_pre(gt, bias_r, lmat_t, tot, seq_len)
    return (pre[1],) + _gates_rows_post(pre, mp_r)


def _gates_cols_pre(g1, g2, bias_c, lmat, seq_len):
    lane = lax.broadcasted_iota(jnp.int32, g1.shape, 1)
    ig = g1 + bias_c[0:1]
    lf = jnp.where(lane < 2 * ML_HEADS, _log_sigmoid(g2 + bias_c[1:2]), 0.0)
    b = _dot_exact01(lmat, lf)
    return b, _seq_prefix_max_rows(ig - b, seq_len)


def _gates_cols_post(b, cm, mp_c):
    g = b + mp_c
    mt = jnp.maximum(b + cm, g)
    return b - mt, jnp.exp(g - mt), jnp.exp(-mt)


def _gates_cols(g1, g2, bias_c, mp_c, lmat, seq_len):
    b, cm = _gates_cols_pre(g1, g2, bias_c, lmat, seq_len)
    return (b, cm) + _gates_cols_post(b, cm, mp_c)


def _outer_sum_lhs(u):
    lane = lax.broadcasted_iota(jnp.int32, u.shape, 1)
    hi = u.astype(BF16).astype(F32)
    lo = u - hi
    return jnp.where(lane < ML_HEADS, hi, jnp.where(lane < 2 * ML_HEADS, lo,
                     jnp.where(lane < 4 * ML_HEADS, 1.0, 0.0))).astype(BF16)


def _outer_sum_rhs(v_r, hd):
    row = lax.broadcasted_iota(jnp.int32, v_r.shape, 0)
    hi = v_r.astype(BF16).astype(F32)
    lo = v_r - hi
    pick = (row == hd) | (row == ML_HEADS + hd)
    top = jnp.where(pick, 1.0, 0.0)
    bot = jnp.where(row == hd, hi, jnp.where(row == ML_HEADS + hd, lo, 0.0))
    r16 = jnp.concatenate([top, bot], axis=0).astype(BF16)
    return jnp.concatenate([r16, jnp.zeros((LANES - 2 * SUBLANES, v_r.shape[1]), BF16)], axis=0)


def _decayed_scores(q, kt, lhsc, v_r, hd, mask):
    e = _dot(lhsc, _outer_sum_rhs(v_r, hd))
    return _dot(q.astype(BF16), kt.astype(BF16)) * jnp.where(mask, jnp.exp(e), 0.0)


def _conv3_rows(cx, prev8, cw):
    n = cx.shape[1]
    row = lax.broadcasted_iota(jnp.int32, (SUBLANES, n), 0)
    s1 = pltpu.roll(cx, 1, 0)
    s2 = pltpu.roll(cx, 2, 0)
    f1 = jnp.where(row < 1, pltpu.roll(prev8, 1, 0), s1[0:SUBLANES])
    f2 = jnp.where(row < 2, pltpu.roll(prev8, 2, 0), s2[0:SUBLANES])
    s1 = jnp.concatenate([f1, s1[SUBLANES:]], axis=0)
    s2 = jnp.concatenate([f2, s2[SUBLANES:]], axis=0)
    return s2 * cw[0:1] + s1 * cw[1:2] + cx * cw[2:3]


def _mlstm_out_norm(hm, zo, onw):
    ms = jnp.mean(hm * hm, axis=-1, keepdims=True)
    return hm * lax.rsqrt(ms + EPS) * onw * _sigmoid(zo)


def _even_tail(z_scr, hm_scr, cat_scr, onw_ref):
    for hd in range(ML_HEADS):
        col = slice(hd * ML_DV, (hd + 1) * ML_DV)
        zo = z_scr[:, O_COL + hd * ML_DV:O_COL + (hd + 1) * ML_DV]
        cat_scr[:, col] = _mlstm_out_norm(hm_scr[:, col], zo, onw_ref[:, col]).astype(BF16)


def _mix_even_prompt_kernel(x_ref, mod_ref, nw_ref, win_ref, wkg_ref, bifc_ref, bifr_ref, onw_ref, cw_ref,
                            wout_ref, xo_ref, co_ref, no_ref, mo_ref, sco_ref,
                            z_scr, zt_scr, hm_scr, cat_scr, cn_scr, mrow_scr, mlane_scr, cc_scr):
    t = pl.program_id(1)

    @pl.when(t == 0)
    def _():
        cn_scr[...] = jnp.zeros_like(cn_scr)
        mrow_scr[...] = jnp.zeros_like(mrow_scr)
        mlane_scr[...] = jnp.zeros_like(mlane_scr)
        cc_scr[...] = jnp.zeros_like(cc_scr)

    x = x_ref[0]
    bi = pl.program_id(0)
    h = _norm_mod(x, nw_ref[...], _mod_row(mod_ref, 1, bi), _mod_row(mod_ref, 0, bi)).astype(BF16)
    z_scr[...] = _dot(h, win_ref[...])
    zt_scr[...] = _dot_nt(wkg_ref[...], h)

    mask, lmat, lmat_t, tot = _chunk_consts(CHUNK)
    scale = ML_DK ** -0.5
    ones_v = jnp.ones((CHUNK, ML_DV), BF16)

    n_ch = TM // CHUNK
    chunk_rows = [slice(c * CHUNK, (c + 1) * CHUNK) for c in range(n_ch)]
    pre_r = [_gates_rows_pre(zt_scr[ML_WIDTH:KG_ROWS, rows], bifr_ref[...], lmat_t, tot, CHUNK)
             for rows in chunk_rows]
    pre_c = [_gates_cols_pre(z_scr[rows, G1_COL:G1_COL + LANES], z_scr[rows, G2_COL:G2_COL + LANES],
                             bifc_ref[...], lmat, CHUNK) for rows in chunk_rows]
    mp_r = mrow_scr[...]
    mp_c = mlane_scr[0:1, :]
    post_r, post_c = [], []
    for c in range(n_ch):
        post_r.append(_gates_rows_post(pre_r[c], mp_r))
        mp_r = post_r[c][2]
        b, cm = pre_c[c]
        post_c.append(_gates_cols_post(b, cm, mp_c))
        mp_c = b[CHUNK - 1:CHUNK] + jnp.maximum(mp_c, cm[CHUNK - 1:CHUNK])
    mrow_scr[...] = mp_r
    mlane_scr[...] = jnp.broadcast_to(mp_c, (SUBLANES, LANES))

    intra, upd, wgq = {}, {}, {}
    for c, rows in enumerate(chunk_rows):
        u, wg, _ = post_c[c]
        lhsc = _outer_sum_lhs(u)
        for hd in range(ML_HEADS):
            q = z_scr[rows, Q_COL + hd * ML_DK:Q_COL + (hd + 1) * ML_DK]
            v = z_scr[rows, V_COL + hd * ML_DV:V_COL + (hd + 1) * ML_DV]
            kt = zt_scr[hd * ML_DK:(hd + 1) * ML_DK, rows] * scale
            s = _decayed_scores(q, kt, lhsc, pre_r[c][1], hd, mask)
            v1 = jnp.concatenate([v.astype(BF16), ones_v], axis=1)
            intra[c, hd] = _dot(s.astype(BF16), v1)
            upd[c, hd] = _dot((kt * post_r[c][0][hd:hd + 1, :]).astype(BF16), v1)
            wgq[c, hd] = (q * wg[:, hd:hd + 1]).astype(BF16)

    for hd in range(ML_HEADS):
        cn = cn_scr[hd]
        for c, rows in enumerate(chunk_rows):
            out = intra[c, hd] + _dot(wgq[c, hd], cn.astype(BF16))
            r = 1.0 / jnp.maximum(jnp.abs(out[:, ML_DV:]), post_c[c][2][:, hd:hd + 1])
            hm_scr[rows, hd * ML_DV:(hd + 1) * ML_DV] = out[:, :ML_DV] * r
            wc = post_r[c][1][hd:hd + 1, :]
            cn = jnp.concatenate([wc, wc], axis=1) * cn + upd[c, hd]
        cn_scr[hd] = cn

    _even_tail(z_scr, hm_scr, cat_scr, onw_ref)
    cx = z_scr[:, C_COL:C_COL + SC_WIDTH] * z_scr[:, X_COL:X_COL + SC_WIDTH]
    u = _conv3_rows(cx, cc_scr[...], cw_ref[...])
    cc_scr[...] = cx[TM - SUBLANES:TM]
    cat_scr[:, ML_WIDTH:] = (z_scr[:, B_COL:B_COL + SC_WIDTH] * u).astype(BF16)

    y = _dot(cat_scr[...], wout_ref[...])
    xo_ref[0] = x + _mod_row(mod_ref, 2, bi) * y
    for hd in range(ML_HEADS):
        co_ref[0, hd] = cn_scr[hd, :, 0:ML_DV]
        no_ref[0, hd] = cn_scr[hd, :, ML_DV:]
    mo_ref[0] = mrow_scr[...]
    sco_ref[0] = cc_scr[...]


def _even_weight_specs(i, idx):
    once = pl.Buffered(1)
    return [
        pl.BlockSpec((None, D_MODEL, IN_W), idx, pipeline_mode=once),
        pl.BlockSpec((None, KG_ROWS, D_MODEL), idx, pipeline_mode=once),
        pl.BlockSpec((None, 2, LANES), idx),
        pl.BlockSpec((None, 2 * SUBLANES, LANES), idx),
        pl.BlockSpec((None, 1, ML_WIDTH), idx),
        pl.BlockSpec((None, 3, SC_WIDTH), idx),
        pl.BlockSpec((None, ML_WIDTH + SC_WIDTH, D_MODEL), idx, pipeline_mode=once),
    ]


def _mix_even_prompt_call(x, mod, n_seq_rows, l, nw, win, wkg, bifc, bifr, onw, cw, wout):
    B, S, _ = x.shape
    nt = S // TM
    i = l // 2
    return pl.pallas_call(
        _mix_even_prompt_kernel,
        grid=(B, nt),
        in_specs=[
            pl.BlockSpec((1, TM, D_MODEL), lambda b, t: (b, t, 0)),
            pl.BlockSpec((None, N_MOD, SUBLANES, D_MODEL), lambda b, t: (l, 0, n_seq_rows // SUBLANES, 0)),
            pl.BlockSpec((None, 1, D_MODEL), lambda b, t: (l, 0, 0)),
        ] + _even_weight_specs(i, lambda b, t: (i, 0, 0)),
        out_specs=[
            pl.BlockSpec((1, TM, D_MODEL), lambda b, t: (b, t, 0)),
            pl.BlockSpec((1, ML_HEADS, ML_DK, ML_DV), lambda b, t: (b, 0, 0, 0)),
            pl.BlockSpec((1, ML_HEADS, ML_DK, LANES), lambda b, t: (b, 0, 0, 0)),
            pl.BlockSpec((1, SUBLANES, LANES), lambda b, t: (b, 0, 0)),
            pl.BlockSpec((1, SUBLANES, SC_WIDTH), lambda b, t: (b, 0, 0)),
        ],
        out_shape=[
            jax.ShapeDtypeStruct((B, S, D_MODEL), F32),
            jax.ShapeDtypeStruct((B, ML_HEADS, ML_DK, ML_DV), F32),
            jax.ShapeDtypeStruct((B, ML_HEADS, ML_DK, LANES), F32),
            jax.ShapeDtypeStruct((B, SUBLANES, LANES), F32),
            jax.ShapeDtypeStruct((B, SUBLANES, SC_WIDTH), F32),
        ],
        scratch_shapes=[
            pltpu.VMEM((TM, IN_W), F32),
            pltpu.VMEM((KG_ROWS, TM), F32),
            pltpu.VMEM((TM, ML_WIDTH), F32),
            pltpu.VMEM((TM, ML_WIDTH + SC_WIDTH), BF16),
            pltpu.VMEM((ML_HEADS, ML_DK, ML_DV + LANES), F32),
            pltpu.VMEM((SUBLANES, LANES), F32),
            pltpu.VMEM((SUBLANES, LANES), F32),
            pltpu.VMEM((SUBLANES, SC_WIDTH), F32),
        ],
        compiler_params=_cparams(2),
        name="mix_even_prompt",
    )(x, mod, nw, win, wkg, bifc, bifr, onw, cw, wout)


def _mix_even_sample_kernel(x_ref, mod_ref, nw_ref, win_ref, wkg_ref, bifc_ref, bifr_ref, onw_ref, cw_ref,
                            wout_ref, c_ref, nt_ref, ntt_ref, mcol_ref, mrow_ref, sc_ref,
                            xo_ref, co_ref, no_ref, mo_ref, sco_ref,
                            h_scr, z_scr, zt_scr, hm_scr, cat_scr, g1_scr,
                            intra_scr, dpart_scr, pbe_scr, wgq_scr, kwt_scr, vb_scr, wcb_scr, inter_scr):
    n_tok = x_ref.shape[0]
    seq_len = n_tok // SEQ_BLK

    def modulate(b, carry):
        rows = pl.ds(pl.multiple_of(b * seq_len, seq_len), seq_len)
        h_scr[rows, :] = _norm_mod(x_ref[rows, :], nw_ref[...], _mod_row(mod_ref, 1, b), _mod_row(mod_ref, 0, b))
        g1_scr[rows, :] = jnp.broadcast_to(_mod_row(mod_ref, 2, b), (seq_len, D_MODEL))
        return carry

    lax.fori_loop(0, SEQ_BLK, modulate, 0, unroll=8)
    hb = h_scr[...].astype(BF16)
    z_scr[...] = _dot(hb, win_ref[...])
    zt_scr[...] = _dot_nt(wkg_ref[...], hb)

    mask, lmat, lmat_t, tot = _chunk_consts(seq_len)
    scale = ML_DK ** -0.5
    ones_v = jnp.ones((CHUNK, ML_DV), BF16)
    mp_r = mrow_ref[...]
    v_r, ws_r, wc_r, mn_r = _gates_rows(zt_scr[ML_WIDTH:KG_ROWS, :], bifr_ref[...], mp_r, lmat_t, tot, seq_len)
    mo_ref[...] = mn_r
    mp_c = mcol_ref[...]
    b, cm, u, wg, pbe = _gates_cols(z_scr[:, G1_COL:G1_COL + LANES], z_scr[:, G2_COL:G2_COL + LANES],
                                    bifc_ref[...], mp_c, lmat, seq_len)
    wc_c = jnp.exp(mp_c - jnp.maximum(mp_c, _seq_last_row(cm, seq_len)))
    lhsc = _outer_sum_lhs(u)
    first_tok = (lax.broadcasted_iota(jnp.int32, (SEQ_BLK, n_tok), 0) * seq_len
                 == lax.broadcasted_iota(jnp.int32, (SEQ_BLK, n_tok), 1)).astype(BF16)

    for hd in range(ML_HEADS):
        q = z_scr[:, Q_COL + hd * ML_DK:Q_COL + (hd + 1) * ML_DK]
        v = z_scr[:, V_COL + hd * ML_DV:V_COL + (hd + 1) * ML_DV]
        kt = zt_scr[hd * ML_DK:(hd + 1) * ML_DK, :] * scale
        s = _decayed_scores(q, kt, lhsc, v_r, hd, mask)
        vb = v.astype(BF16)
        out = _dot(s.astype(BF16), jnp.concatenate([vb, ones_v], axis=1))
        wg_h = wg[:, hd:hd + 1]
        qn = jnp.sum(q * nt_ref[hd], axis=-1, keepdims=True)
        intra_scr[hd] = out[:, :ML_DV]
        dpart_scr[hd] = out[:, ML_DV:] + wg_h * qn
        pbe_scr[hd] = jnp.broadcast_to(pbe[:, hd:hd + 1], (n_tok, LANES))
        wgq_scr[hd] = q * wg_h
        kwt = kt * ws_r[hd:hd + 1, :]
        kwt_scr[hd] = kwt
        vb_scr[hd] = vb
        wcb_scr[hd] = jnp.broadcast_to(wc_c[:, hd:hd + 1], (n_tok, LANES))
        n_new_t = wc_r[hd:hd + 1, :] * ntt_ref[hd] + _dot_exact01_r(kwt, tot)
        no_ref[hd] = _pick_cols_t(n_new_t, first_tok)

    lane_i = lax.broadcasted_iota(jnp.int32, (ML_DK, n_tok), 1)

    def per_seq(bq, carry):
        r0 = pl.multiple_of(bq * seq_len, seq_len)
        rows = pl.ds(r0, seq_len)
        sel = (lane_i >= r0) & (lane_i < r0 + seq_len)
        for hd in range(ML_HEADS):
            c_prev = c_ref[bq, hd]
            inter_scr[hd, rows, :] = _dot(wgq_scr[hd, rows, :].astype(BF16), c_prev.astype(BF16))
            kw_b = jnp.where(sel, kwt_scr[hd], 0.0).astype(BF16)
            co_ref[bq, hd] = wcb_scr[hd, pl.ds(r0, 1), :] * c_prev + _dot(kw_b, vb_scr[hd])
        return carry

    lax.fori_loop(0, SEQ_BLK, per_seq, 0, unroll=8)

    for hd in range(ML_HEADS):
        num = inter_scr[hd] + intra_scr[hd]
        hm_scr[:, hd * ML_DV:(hd + 1) * ML_DV] = num * (
            1.0 / jnp.maximum(jnp.abs(dpart_scr[hd]), pbe_scr[hd]))

    _even_tail(z_scr, hm_scr, cat_scr, onw_ref)
    cx = z_scr[:, C_COL:C_COL + SC_WIDTH] * z_scr[:, X_COL:X_COL + SC_WIDTH]
    sub = lax.broadcasted_iota(jnp.int32, (n_tok, SC_WIDTH), 0) % seq_len
    p1 = sc_ref[...]
    s1 = jnp.where(sub < 1, pltpu.roll(p1, n_tok - 1, 0), pltpu.roll(cx, 1, 0))
    s2 = jnp.where(sub < 2, p1, pltpu.roll(cx, 2, 0))
    cw = cw_ref[...]
    u = s2 * cw[0:1] + s1 * cw[1:2] + cx * cw[2:3]
    sco_ref[...] = pltpu.roll(cx, n_tok - (seq_len - 2), 0)
    cat_scr[:, ML_WIDTH:] = (z_scr[:, B_COL:B_COL + SC_WIDTH] * u).astype(BF16)

    y = _dot(cat_scr[...], wout_ref[...])
    xo_ref[...] = x_ref[...] + g1_scr[...] * y


def _mix_even_sample_call(x, mod, l, nw, win, wkg, bifc, bifr, onw, cw, wout, c0, n_tok, n_tok_t, m_col, m_row,
                          sc_pad, co_prev):
    n_rows = x.shape[0]
    n_seq = c0.shape[1]
    seq_len = n_rows // n_seq
    blk = SEQ_BLK * seq_len
    nb = n_seq // SEQ_BLK
    li = l // 2
    head_blk = (ML_HEADS, blk, LANES)
    return _stacked_call(
        _mix_even_sample_kernel, 16, 1, co_prev, li,
        lambda lead, idx: pl.BlockSpec((lead, SEQ_BLK, ML_HEADS, ML_DK, ML_DV), lambda i: (idx, i, 0, 0, 0)),
        grid=(nb,),
        in_specs=[
            pl.BlockSpec((blk, D_MODEL), lambda i: (i, 0)),
            pl.BlockSpec((None, N_MOD, SEQ_BLK, D_MODEL), lambda i: (l, 0, i, 0)),
            pl.BlockSpec((None, 1, D_MODEL), lambda i: (l, 0, 0)),
        ] + _even_weight_specs(li, lambda i: (li, 0, 0)) + [
            pl.BlockSpec((None, SEQ_BLK, ML_HEADS, ML_DK, ML_DV), lambda i: (li, i, 0, 0, 0)),
            pl.BlockSpec((None, ML_HEADS, blk, ML_DK), lambda i: (li, 0, i, 0)),
            pl.BlockSpec((None, ML_HEADS, ML_DK, blk), lambda i: (li, 0, 0, i)),
            pl.BlockSpec((None, blk, LANES), lambda i: (li, i, 0)),
            pl.BlockSpec((None, SUBLANES, blk), lambda i: (li, 0, i)),
            pl.BlockSpec((None, blk, SC_WIDTH), lambda i: (li, i, 0)),
        ],
        out_specs=[
            pl.BlockSpec((blk, D_MODEL), lambda i: (i, 0)),
            None,
            pl.BlockSpec((ML_HEADS, SEQ_BLK, ML_DK), lambda i: (0, i, 0)),
            pl.BlockSpec((SUBLANES, blk), lambda i: (0, i)),
            pl.BlockSpec((blk, SC_WIDTH), lambda i: (i, 0)),
        ],
        out_shape=[
            jax.ShapeDtypeStruct((n_rows, D_MODEL), F32),
            jax.ShapeDtypeStruct(c0.shape, F32),
            jax.ShapeDtypeStruct((ML_HEADS, n_seq, ML_DK), F32),
            jax.ShapeDtypeStruct((SUBLANES, n_rows), F32),
            jax.ShapeDtypeStruct((n_rows, SC_WIDTH), F32),
        ],
        scratch_shapes=[
            pltpu.VMEM((blk, D_MODEL), F32),
            pltpu.VMEM((blk, IN_W), F32),
            pltpu.VMEM((KG_ROWS, blk), F32),
            pltpu.VMEM((blk, ML_WIDTH), F32),
            pltpu.VMEM((blk, ML_WIDTH + SC_WIDTH), BF16),
            pltpu.VMEM((blk, D_MODEL), F32),
            pltpu.VMEM(head_blk, F32),
            pltpu.VMEM(head_blk, F32),
            pltpu.VMEM(head_blk, F32),
            pltpu.VMEM(head_blk, F32),
            pltpu.VMEM((ML_HEADS, ML_DK, blk), F32),
            pltpu.VMEM(head_blk, BF16),
            pltpu.VMEM(head_blk, F32),
            pltpu.VMEM(head_blk, F32),
        ],
        compiler_params=_cparams(1),
        name="mix_even_sample",
    )(x, mod, nw, win, wkg, bifc, bifr, onw, cw, wout, c0, n_tok, n_tok_t, m_col, m_row, sc_pad)


def _split2(x):
    hi = x.astype(BF16)
    lo = (x - hi.astype(F32)).astype(BF16)
    return jnp.concatenate([hi, lo], axis=1)


def _head_lane_mats():
    r = lax.broadcasted_iota(jnp.int32, (2 * LANES, LANES), 0) % LANES
    c = lax.broadcasted_iota(jnp.int32, (2 * LANES, LANES), 1)
    hsum = ((r // HEAD_DIM) == (c // HEAD_DIM)).astype(BF16)
    half = HEAD_DIM // 2
    src = jnp.where((c % HEAD_DIM) < half, c + half, c - half)
    return hsum, (r == src).astype(BF16)


def _q_lane_mat():
    r = lax.broadcasted_iota(jnp.int32, (2 * LANES, 2 * LANES), 0)
    c = lax.broadcasted_iota(jnp.int32, (2 * LANES, 2 * LANES), 1)
    half = HEAD_DIM // 2
    src = jnp.where((c % HEAD_DIM) < half, c + half, c - half)
    top = (r < LANES) & (c < LANES) & ((r // HEAD_DIM) == (c // HEAD_DIM))
    return (top | ((r >= LANES) & (c >= LANES) & (r == src))).astype(BF16)


def _q_norm_rope(xb, gw, cos, sin, qmat):
    zg = xb * gw
    out = _dot(jnp.concatenate([(xb * xb).astype(BF16), zg.astype(BF16)], axis=1), qmat)
    ms = out[:, :LANES] * (1.0 / HEAD_DIM)
    return lax.rsqrt(ms + EPS) * (zg * cos + out[:, LANES:] * sin)


def _qk_norm_rope(xb, gw, cos, sin, hsum, rot_mat):
    ms = _dot(_split2(xb * xb), hsum) * (1.0 / HEAD_DIM)
    zg = xb * gw
    rot = _dot(_split2(zg), rot_mat)
    return lax.rsqrt(ms + EPS) * (zg * cos + rot * sin)


def _sink_rows(sink8, reps):
    return jnp.concatenate(
        [jnp.broadcast_to(sink8[r:r + 1, :], (reps, LANES)) for r in range(SUBLANES)], axis=0)


def _sink_col(sink8, reps):
    parts = [jnp.broadcast_to(sink8[r:r + 1, :], (reps, LANES)) for r in range(SUBLANES)]
    return jnp.concatenate(parts, axis=0)[:, 0:1]


def _attn_prompt_kernel(x_ref, mod_ref, nw_ref, wqkv_ref, qnw_ref, knw_ref, cos_ref, sin_ref,
                        sink_ref, wout_ref, xo_ref, ko_ref, vo_ref,
                        z_scr, qm_scr, k_scr, v_scr, o_scr):
    t = pl.program_id(1)
    n_qb = TM // WINDOW

    @pl.when(t == 0)
    def _():
        k_scr[0:WINDOW, :] = jnp.zeros((WINDOW, KV_W), BF16)
        v_scr[0:WINDOW, :] = jnp.zeros((WINDOW, KV_W), BF16)

    x = x_ref[0]
    bi = pl.program_id(0)
    h = _norm_mod(x, nw_ref[...], _mod_row(mod_ref, 1, bi), _mod_row(mod_ref, 0, bi)).astype(BF16)
    z_scr[...] = _dot(h, wqkv_ref[...])
    cos = cos_ref[...]
    sin = sin_ref[...]
    hmats = _head_lane_mats()
    qmat = _q_lane_mat()
    half0 = lax.broadcasted_iota(jnp.int32, (TM, LANES), 1) < HEAD_DIM
    qscale = HEAD_DIM ** -0.5
    for jb in range(Q_W // LANES):
        y = _q_norm_rope(z_scr[:, jb * LANES:(jb + 1) * LANES], qnw_ref[...], cos, sin, qmat) * qscale
        qm_scr[2 * jb] = jnp.where(half0, y, 0.0).astype(BF16)
        qm_scr[2 * jb + 1] = jnp.where(half0, 0.0, y).astype(BF16)
    for p in range(KV_W // LANES):
        kf = _qk_norm_rope(z_scr[:, Q_W + p * LANES:Q_W + (p + 1) * LANES], knw_ref[...], cos, sin, *hmats)
        ko_ref[0, :, p * LANES:(p + 1) * LANES] = kf[TM - WINDOW:TM]
        k_scr[WINDOW:WINDOW + TM, p * LANES:(p + 1) * LANES] = kf.astype(BF16)
    vf = z_scr[:, Q_W + KV_W:Q_W + 2 * KV_W]
    vo_ref[0] = vf[TM - WINDOW:TM]
    v_scr[WINDOW:WINDOW + TM, :] = vf.astype(BF16)

    r = lax.broadcasted_iota(jnp.int32, (8 * WINDOW, 2 * WINDOW), 0) % WINDOW
    c = lax.broadcasted_iota(jnp.int32, (8 * WINDOW, 2 * WINDOW), 1)
    valid = ((c < WINDOW) & (c > r)) | ((c >= WINDOW) & ((c - WINDOW) <= r))
    first_lim = jnp.where(t == 0, WINDOW, 0)
    half0q = lax.broadcasted_iota(jnp.int32, (4 * WINDOW, LANES), 1) < HEAD_DIM
    ones_kv = jnp.ones((2 * WINDOW, LANES), BF16)
    for qb in range(n_qb):
        rows = slice(qb * WINDOW, (qb + 1) * WINDOW)
        krows = slice(qb * WINDOW, (qb + 2) * WINDOW)
        vmask = (valid & (c >= first_lim)) if qb == 0 else valid
        for p in range(KV_W // LANES):
            kb = k_scr[krows, p * LANES:(p + 1) * LANES]
            vb = v_scr[krows, p * LANES:(p + 1) * LANES]
            qs = jnp.concatenate([qm_scr[2 * (4 * p + i) + e, rows, :] for e in range(2) for i in range(4)],
                                 axis=0)
            s = jnp.where(vmask, _dot_nt(qs, kb), -jnp.inf)
            sk = _sink_rows(sink_ref[p], WINDOW)
            mx = jnp.maximum(jnp.max(s, axis=-1, keepdims=True), sk)
            pr = jnp.exp(s - jnp.concatenate([mx, mx], axis=1))
            o2 = _dot(pr.astype(BF16), jnp.concatenate([vb, ones_kv], axis=1))
            den = o2[:, LANES:] + jnp.exp(sk - mx)
            o = o2[:, :LANES] * (1.0 / den)
            merged = jnp.where(half0q, o[0:4 * WINDOW], o[4 * WINDOW:])
            for i in range(4):
                o_scr[rows, (4 * p + i) * LANES:(4 * p + i + 1) * LANES] = (
                    merged[i * WINDOW:(i + 1) * WINDOW].astype(BF16))

    y = _dot(o_scr[...], wout_ref[...])
    xo_ref[0] = x + _mod_row(mod_ref, 2, bi) * y
    k_scr[0:WINDOW, :] = k_scr[TM:TM + WINDOW, :]
    v_scr[0:WINDOW, :] = v_scr[TM:TM + WINDOW, :]


def _attn_prompt_call(x, mod, n_seq_rows, l, nw, wqkv, qnw, knw, cos, sin, sink, wout):
    B, S, _ = x.shape
    nt = S // TM
    lj = l // 2
    layer3 = lambda b, t: (lj, 0, 0)
    return pl.pallas_call(
        _attn_prompt_kernel,
        grid=(B, nt),
        in_specs=[
            pl.BlockSpec((1, TM, D_MODEL), lambda b, t: (b, t, 0)),
            pl.BlockSpec((None, N_MOD, SUBLANES, D_MODEL), lambda b, t: (l, 0, n_seq_rows // SUBLANES, 0)),
            pl.BlockSpec((None, 1, D_MODEL), lambda b, t: (l, 0, 0)),
            pl.BlockSpec((None, D_MODEL, Q_W + 2 * KV_W), layer3),
            pl.BlockSpec((None, 1, LANES), layer3),
            pl.BlockSpec((None, 1, LANES), layer3),
            pl.BlockSpec((TM, LANES), lambda b, t: (t, 0)),
            pl.BlockSpec((TM, LANES), lambda b, t: (t, 0)),
            pl.BlockSpec((None, 2, SUBLANES, LANES), lambda b, t: (lj, 0, 0, 0)),
            pl.BlockSpec((None, Q_W, D_MODEL), layer3),
        ],
        out_specs=[
            pl.BlockSpec((1, TM, D_MODEL), lambda b, t: (b, t, 0)),
            pl.BlockSpec((1, WINDOW, KV_W), lambda b, t: (b, 0, 0)),
            pl.BlockSpec((1, WINDOW, KV_W), lambda b, t: (b, 0, 0)),
        ],
        out_shape=[
            jax.ShapeDtypeStruct((B, S, D_MODEL), F32),
            jax.ShapeDtypeStruct((B, WINDOW, KV_W), F32),
            jax.ShapeDtypeStruct((B, WINDOW, KV_W), F32),
        ],
        scratch_shapes=[
            pltpu.VMEM((TM, Q_W + 2 * KV_W), F32),
            pltpu.VMEM((2 * Q_W // LANES, TM, LANES), BF16),
            pltpu.VMEM((TM + WINDOW, KV_W), BF16),
            pltpu.VMEM((TM + WINDOW, KV_W), BF16),
            pltpu.VMEM((TM, Q_W), BF16),
        ],
        compiler_params=_cparams(2),
        name="attn_prompt",
    )(x, mod, nw, wqkv, qnw, knw, cos, sin, sink, wout)


def _attn_sample_kernel(x_ref, mod_ref, nw_ref, wqkv_ref, qnw_ref, knw_ref, cos_ref, sin_ref,
                        sink_ref, wout_ref, kc_ref, vc_ref,
                        xo_ref, kco_ref, vco_ref,
                        h_scr, g1_scr, z_scr, qm_scr, kn_scr, o_scr):
    n_tok = x_ref.shape[0]
    seq_len = n_tok // SEQ_BLK
    win = kc_ref.shape[1]

    def modulate(b, carry):
        rows = pl.ds(pl.multiple_of(b * seq_len, seq_len), seq_len)
        h_scr[rows, :] = _norm_mod(x_ref[rows, :], nw_ref[...], _mod_row(mod_ref, 1, b), _mod_row(mod_ref, 0, b))
        g1_scr[rows, :] = jnp.broadcast_to(_mod_row(mod_ref, 2, b), (seq_len, D_MODEL))
        return carry

    lax.fori_loop(0, SEQ_BLK, modulate, 0, unroll=8)
    z_scr[...] = _dot(h_scr[...].astype(BF16), wqkv_ref[...])
    cos = cos_ref[...]
    sin = sin_ref[...]
    hmats = _head_lane_mats()
    half0 =lax.broadcasted_iota(jnp.int32, (n_tok, LANES), 1) < HEAD_DIM
    qscale = HEAD_DIM ** -0.5
    for jb in range(Q_W // LANES):
        y = _qk_norm_rope(z_scr[:, jb * LANES:(jb + 1) * LANES], qnw_ref[...], cos, sin, *hmats) * qscale
        qm_scr[2 * jb] = jnp.where(half0, y, 0.0)
        qm_scr[2 * jb + 1] = jnp.where(half0, 0.0, y)
    for p in range(KV_W // LANES):
        kn_scr[:, p * LANES:(p + 1) * LANES] = _qk_norm_rope(
            z_scr[:, Q_W + p * LANES:Q_W + (p + 1) * LANES], knw_ref[...], cos, sin, *hmats)

    n_q = 8 * seq_len
    tq = lax.broadcasted_iota(jnp.int32, (SEQ_BLK, n_q, 2 * win), 1) % seq_len
    cc = lax.broadcasted_iota(jnp.int32, (SEQ_BLK, n_q, 2 * win), 2)
    valid = ((cc < win) & (cc > tq)) | ((cc >= 2 * win - seq_len) & ((cc - (2 * win - seq_len)) <= tq))
    half0q = lax.broadcasted_iota(jnp.int32, (SEQ_BLK, n_q // 2, LANES), 2) < HEAD_DIM
    ones_kv = jnp.ones((SEQ_BLK, 2 * win, LANES), BF16)

    kc = kc_ref[...]
    vc = vc_ref[...]
    knew = jnp.concatenate([kc[:, seq_len:], kn_scr[...].reshape(SEQ_BLK, seq_len, KV_W)], axis=1)
    vnew = jnp.concatenate(
        [vc[:, seq_len:], z_scr[:, Q_W + KV_W:Q_W + 2 * KV_W].reshape(SEQ_BLK, seq_len, KV_W)], axis=1)
    kco_ref[...] = knew
    vco_ref[...] = vnew
    for p in range(KV_W // LANES):
        lanes = slice(p * LANES, (p + 1) * LANES)
        qs = jnp.concatenate([qm_scr[2 * (4 * p + i) + e].reshape(SEQ_BLK, seq_len, LANES)
                              for e in range(2) for i in range(4)], axis=1).astype(BF16)
        kk = jnp.concatenate([kc[:, :, lanes], knew[:, :, lanes]], axis=1).astype(BF16)
        vv = jnp.concatenate([vc[:, :, lanes], vnew[:, :, lanes]], axis=1).astype(BF16)
        s = jnp.einsum("bqd,bkd->bqk", qs, kk, preferred_element_type=F32)
        s = jnp.where(valid, s, -jnp.inf)
        sk = _sink_rows(sink_ref[p], seq_len)[None]
        mx = jnp.maximum(jnp.max(s, axis=-1, keepdims=True), sk)
        pr = jnp.exp(s - jnp.concatenate([mx, mx], axis=-1))
        o2 = jnp.einsum("bqk,bkd->bqd", pr.astype(BF16), jnp.concatenate([vv, ones_kv], axis=-1),
                        preferred_element_type=F32)
        o = o2[:, :, :LANES] * (1.0 / (o2[:, :, LANES:] + jnp.exp(sk - mx)))
        merged = jnp.where(half0q, o[:, 0:n_q // 2], o[:, n_q // 2:])
        for i in range(4):
            o_scr[:, (4 * p + i) * LANES:(4 * p + i + 1) * LANES] = (
                merged[:, i * seq_len:(i + 1) * seq_len].reshape(n_tok, LANES))

    y = _dot(o_scr[...].astype(BF16), wout_ref[...])
    xo_ref[...] = x_ref[...] + g1_scr[...] * y


def _attn_sample_call(x, mod, l, nw, wqkv, qnw, knw, cos, sin, sink, wout, kc, vc):
    n_rows = x.shape[0]
    _, n_seq, win, _ = kc.shape
    lj = l // 2
    seq_len = n_rows // n_seq
    blk = SEQ_BLK * seq_len
    nb = n_seq // SEQ_BLK
    const2 = lambda i: (0, 0)
    return pl.pallas_call(
        _attn_sample_kernel,
        input_output_aliases={10: 1, 11: 2},
        grid=(nb,),
        in_specs=[
            pl.BlockSpec((blk, D_MODEL), lambda i: (i, 0)),
            pl.BlockSpec((None, N_MOD, SEQ_BLK, D_MODEL), lambda i: (l, 0, i, 0)),
            pl.BlockSpec((None, 1, D_MODEL), lambda i: (l, 0, 0)),
            pl.BlockSpec((None, D_MODEL, Q_W + 2 * KV_W), lambda i: (lj, 0, 0)),
            pl.BlockSpec((None, 1, LANES), lambda i: (lj, 0, 0)),
            pl.BlockSpec((None, 1, LANES), lambda i: (lj, 0, 0)),
            pl.BlockSpec((blk, LANES), const2),
            pl.BlockSpec((blk, LANES), const2),
            pl.BlockSpec((None, 2, SUBLANES, LANES), lambda i: (lj, 0, 0, 0)),
            pl.BlockSpec((None, Q_W, D_MODEL), lambda i: (lj, 0, 0)),
            pl.BlockSpec((None, SEQ_BLK, win, KV_W), lambda i: (lj, i, 0, 0)),
            pl.BlockSpec((None, SEQ_BLK, win, KV_W), lambda i: (lj, i, 0, 0)),
        ],
        out_specs=[
            pl.BlockSpec((blk, D_MODEL), lambda i: (i, 0)),
            pl.BlockSpec((None, SEQ_BLK, win, KV_W), lambda i: (lj, i, 0, 0)),
            pl.BlockSpec((None, SEQ_BLK, win, KV_W), lambda i: (lj, i, 0, 0)),
        ],
        out_shape=[
            jax.ShapeDtypeStruct((n_rows, D_MODEL), F32),
            jax.ShapeDtypeStruct(kc.shape, F32),
            jax.ShapeDtypeStruct(vc.shape, F32),
        ],
        scratch_shapes=[
            pltpu.VMEM((blk, D_MODEL), F32),
            pltpu.VMEM((blk, D_MODEL), F32),
            pltpu.VMEM((blk, Q_W + 2 * KV_W), F32),
            pltpu.VMEM((2 * Q_W // LANES, blk, LANES), F32),
            pltpu.VMEM((blk, KV_W), F32),
            pltpu.VMEM((blk, Q_W), F32),
        ],
        compiler_params=_cparams(1),
        name="attn_sample",
    )(x, mod, nw, wqkv, qnw, knw, cos, sin, sink, wout, kc, vc)


def _rope_tables(pos):
    half = HEAD_DIM // 2
    inv = ROPE_THETA ** (-jnp.arange(half, dtype=F32) / half)
    ang = pos.astype(F32)[:, None] * inv[None, :]
    cos = jnp.cos(ang)
    sin = jnp.sin(ang)
    return jnp.tile(cos, (1, 4)), jnp.concatenate([-sin, sin, -sin, sin], axis=1)


def _rope_tables_range(n):
    half = HEAD_DIM // 2
    inv = ROPE_THETA ** (-jnp.arange(half, dtype=F32) / half)
    a_hi = (jnp.arange(n // WINDOW, dtype=jnp.int32) * WINDOW).astype(F32)[:, None] * inv[None, :]
    a_lo = jnp.arange(WINDOW, dtype=jnp.int32).astype(F32)[:, None] * inv[None, :]
    ch, sh = jnp.cos(a_hi)[:, None, :], jnp.sin(a_hi)[:, None, :]
    cl, sl = jnp.cos(a_lo)[None], jnp.sin(a_lo)[None]
    cos = (ch * cl - sh * sl).reshape(n, half)
    sin = (sh * cl + ch * sl).reshape(n, half)
    return jnp.tile(cos, (1, 4)), jnp.concatenate([-sin, sin, -sin, sin], axis=1)


def _prep_attn(w_qkv, q_norm, k_norm, sink, w_out):
    n_l = w_qkv.shape[0]
    perm = np.asarray(HEAD_PERM)
    wq = w_qkv[:, :, :Q_W].reshape(n_l, D_MODEL, ATT_HEADS, HEAD_DIM)[:, :, perm].reshape(n_l, D_MODEL, Q_W)
    wqkv = jnp.concatenate([wq, w_qkv[:, :, Q_W:]], axis=2).astype(BF16)
    wout = w_out.reshape(n_l, ATT_HEADS, HEAD_DIM, D_MODEL)[:, perm].reshape(n_l, Q_W, D_MODEL).astype(BF16)
    qnw = jnp.tile(q_norm, (1, 2))[:, None, :]
    knw = jnp.tile(k_norm, (1, 2))[:, None, :]
    idx = np.asarray([[perm[2 * (4 * p + i) + e] for e in range(2) for i in range(4)] for p in range(2)])
    sink_arr = jnp.broadcast_to(sink[:, idx][..., None], (n_l, 2, SUBLANES, LANES)).astype(F32)
    return wqkv, qnw, knw, sink_arr, wout


def kernel(x_prompt, x_sample, c_prompt, c_sample, state_mlstm_C, state_mlstm_n, state_mlstm_m, state_sconv, cache_win_k, cache_win_v, state_ffn_conv, norm1, norm2, w_ada, b_ada, a_w_in, a_b_if, a_out_norm, a_conv_w, a_w_out, c_w_qkv, c_q_norm, c_k_norm, c_sink, c_w_out, f_w_up, f_conv_w, f_w_down):
    B, S, _ = x_prompt.shape
    NS, SL, _ = x_sample.shape
    assert S % TM == 0 and NS % SEQ_BLK == 0 and SEQ_BLK * SL == CHUNK and SL == SUBLANES

    assert B <= SUBLANES
    c_all = jnp.concatenate([c_sample, c_prompt, jnp.zeros((SUBLANES - B, D_MODEL), F32)], axis=0)
    mod = _ada_call(c_all, w_ada, b_ada)

    win_all, wkg_all = _prep_win_call(jnp.swapaxes(a_w_in, 1, 2))
    b_i, b_f = a_b_if[:, :ML_HEADS], a_b_if[:, ML_HEADS:]
    lane_pad = jnp.zeros((a_b_if.shape[0], LANES - 2 * ML_HEADS), F32)
    bifc_all = jnp.stack([jnp.concatenate([b_i, b_i, lane_pad], axis=1),
                          jnp.concatenate([b_f, b_f, lane_pad], axis=1)], axis=1)
    bifr_all = jnp.broadcast_to(jnp.concatenate([b_i, b_i, b_f, b_f], axis=1)[:, :, None],
                                (a_b_if.shape[0], 2 * SUBLANES, LANES))
    wout_a_all = _cast_call(a_w_out)
    wup_all = _cast_call(f_w_up)
    wdn_all = _cast_call(f_w_down)
    onw_all = a_out_norm.reshape(-1, 1, ML_WIDTH)
    norm1_r = norm1.reshape(DEPTH, 1, D_MODEL)
    norm2_r = norm2.reshape(DEPTH, 1, D_MODEL)
    win_buf = cache_win_k.shape[2]
    kc_all = cache_win_k.reshape(-1, NS, win_buf, KV_W)
    vc_all = cache_win_v.reshape(-1, NS, win_buf, KV_W)

    cos_p, sin_p = _rope_tables_range(S)
    cos_s, sin_s = _rope_tables(PAST_LEN + jnp.arange(SL, dtype=jnp.int32))
    cos_s = jnp.tile(cos_s, (SEQ_BLK, 1))
    sin_s = jnp.tile(sin_s, (SEQ_BLK, 1))

    n_tok = jnp.repeat(state_mlstm_n.transpose(0, 2, 1, 3), SL, axis=2)
    n_tok_t = jnp.swapaxes(n_tok, 2, 3)
    m_rep = jnp.repeat(state_mlstm_m, SL, axis=1)
    m_col = jnp.concatenate([m_rep, m_rep, jnp.zeros(m_rep.shape[:2] + (LANES - 2 * ML_HEADS,), F32)], axis=2)
    m_row = jnp.swapaxes(jnp.concatenate([m_rep, m_rep], axis=2), 1, 2)
    sc_pad = jnp.pad(state_sconv, ((0, 0), (0, 0), (0, SL - 2), (0, 0))).reshape(-1, NS * SL, SC_WIDTH)

    wqkv, qnw, knw, sink, wout = _prep_attn(c_w_qkv, c_q_norm, c_k_norm, c_sink, c_w_out)

    xp = x_prompt
    xs = x_sample.reshape(NS * SL, D_MODEL)
    p_C, p_n, p_m, p_sc, p_wk, p_wv, p_ffn = [], [], [], [], [], [], []
    s_n, s_m, s_sc = [], [], []
    s_C = s_ffn = None
    s_wk, s_wv = kc_all, vc_all

    for l in range(DEPTH):
        if l % 2 == 0:
            i = l // 2
            even_w = (norm1_r, win_all, wkg_all, bifc_all, bifr_all, onw_all, a_conv_w, wout_a_all)
            xp, co, no, mo, sco = _mix_even_prompt_call(xp, mod, NS, l, *even_w)
            p_C.append(co)
            p_n.append(no[:, :, :, 0])
            p_m.append(mo[:, :ML_HEADS, 0])
            p_sc.append(sco[:, SUBLANES - 2:, :])

            xs, s_C, no, mo, sco = _mix_even_sample_call(xs, mod, l, *even_w, state_mlstm_C, n_tok,
                                                         n_tok_t, m_col, m_row, sc_pad, s_C)
            s_n.append(no.transpose(1, 0, 2))
            s_m.append(mo[:ML_HEADS, ::SL].T)
            s_sc.append(sco.reshape(NS, SL, SC_WIDTH)[:, :2])
        else:
            xp, ko, vo = _attn_prompt_call(xp, mod, NS, l, norm1_r, wqkv, qnw, knw, cos_p, sin_p, sink, wout)
            p_wk.append(ko.reshape(B, WINDOW, KV_HEADS, HEAD_DIM))
            p_wv.append(vo.reshape(B, WINDOW, KV_HEADS, HEAD_DIM))
            xs, s_wk, s_wv = _attn_sample_call(xs, mod, l, norm1_r, wqkv, qnw, knw, cos_s, sin_s, sink, wout,
                                               s_wk, s_wv)

        xp, st = _ffn_prompt_call(xp, mod, NS, l, norm2_r, wup_all, f_conv_w, wdn_all)
        p_ffn.append(st[:, SUBLANES - 2:, :])
        xs, s_ffn = _ffn_sample_call(xs, mod, l, norm2_r, state_ffn_conv, wup_all, f_conv_w, wdn_all, s_ffn)

    kv_shape = (-1, NS, win_buf, KV_HEADS, HEAD_DIM)
    return (xp, xs.reshape(NS, SL, D_MODEL),
            jnp.stack(p_C), jnp.stack(p_n), jnp.stack(p_m), jnp.stack(p_sc),
            jnp.stack(p_wk), jnp.stack(p_wv), jnp.stack(p_ffn),
            s_C, jnp.stack(s_n), jnp.stack(s_m), jnp.stack(s_sc),
            s_wk.reshape(kv_shape), s_wv.reshape(kv_shape), s_ffn)
```

```python
import jax
import jax.numpy as jnp
import numpy as np
from jax import lax
from jax.experimental import pallas as pl
from jax.experimental.pallas import tpu as pltpu

F32 = jnp.float32
BF16 = jnp.bfloat16

D_MODEL = 1024
DEPTH = 4
PAST_LEN = 8192
ML_HEADS = 4
ML_DK = 128
ML_DV = 128
ML_WIDTH = ML_HEADS * ML_DV
SC_WIDTH = D_MODEL // 2
ATT_HEADS = 16
KV_HEADS = 4
HEAD_DIM = 64
WINDOW = 128
ROPE_THETA = 10000.0
D_FF = 2816
EPS = 1e-6

LANES = 128
SUBLANES = 8
VMEM_LIMIT = 56 * 1024 * 1024

TM = 1024
TM_FFN = 1024
CHUNK = 128
SEQ_BLK = 16
FC = 256
NCH = D_FF // FC
FS_GATE = 2
FC_S = D_FF // FS_GATE
Q_COL = 0
V_COL = Q_COL + ML_WIDTH
O_COL = V_COL + ML_WIDTH
G1_COL = O_COL + ML_WIDTH
G2_COL = G1_COL + LANES
B_COL = G2_COL + LANES
C_COL = B_COL + SC_WIDTH
X_COL = C_COL + SC_WIDTH
IN_W = X_COL + SC_WIDTH
KG_ROWS = ML_WIDTH + 16
SRC_K = ML_WIDTH
SRC_V = 2 * ML_WIDTH
SRC_G = 4 * ML_WIDTH
SRC_B = SRC_G + 2 * ML_HEADS
Q_W = ATT_HEADS * HEAD_DIM
KV_W = KV_HEADS * HEAD_DIM
HEAD_PERM = (0, 4, 1, 5, 2, 6, 3, 7, 8, 12, 9, 13, 10, 14, 11, 15)


def _cparams(n_axes):
    return pltpu.CompilerParams(dimension_semantics=("arbitrary",) * n_axes,
                                vmem_limit_bytes=VMEM_LIMIT)


def _stacked_call(kernel, n_in, out_idx, prev, slab, slab_spec, **kw):
    specs = list(kw.pop("in_specs"))
    out_specs = list(kw.pop("out_specs"))
    n_slabs = kw["out_shape"][out_idx].shape[0]
    if prev is None:
        out_specs[out_idx] = slab_spec(n_slabs, 0)

        def body(*refs):
            refs = list(refs)
            whole = refs[n_in + out_idx]
            for s in range(n_slabs):
                if s != slab:
                    whole[s] = jnp.zeros(whole.shape[1:], whole.dtype)
            refs[n_in + out_idx] = whole.at[slab]
            return kernel(*refs)

        return pl.pallas_call(body, in_specs=specs, out_specs=out_specs, **kw)

    out_specs[out_idx] = slab_spec(None, slab)

    def body(*refs):
        return kernel(*refs[:n_in], *refs[n_in + 1:])

    call = pl.pallas_call(body, in_specs=specs + [pl.BlockSpec(memory_space=pl.ANY)], out_specs=out_specs,
                          input_output_aliases={n_in: out_idx}, **kw)
    return lambda *args: call(*args, prev)


def _dot(a, b):
    return jnp.dot(a, b, preferred_element_type=F32)


def _dot_nt(a, b):
    return lax.dot_general(a, b, (((1,), (1,)), ((), ())), preferred_element_type=F32)


def _dot_exact01(m, a):
    a1 = a.astype(BF16)
    r1 = a - a1.astype(F32)
    a2 = r1.astype(BF16)
    a3 = (r1 - a2.astype(F32)).astype(BF16)
    return _dot(m, a1) + _dot(m, a2) + _dot(m, a3)


def _norm_mod(x, nw, sc, sh):
    ms = jnp.mean(x * x, axis=-1, keepdims=True)
    return (x * lax.rsqrt(ms + EPS) * nw) * (1.0 + sc) + sh


def _sigmoid(x):
    return 1.0 / (1.0 + jnp.exp(-x))


def _log_sigmoid(x):
    return jnp.minimum(x, 0.0) - jnp.log(1.0 + jnp.exp(-jnp.abs(x)))


N_MOD = 6


ADA_KINDS = 3


def _ada_kernel(c_ref, w_ref, b_ref, o_ref):
    c = c_ref[...]
    s = (c * _sigmoid(c)).astype(BF16)
    for k in range(ADA_KINDS):
        cols = slice(k * D_MODEL, (k + 1) * D_MODEL)
        o_ref[k] = _dot(s, w_ref[:, cols].astype(BF16)) + b_ref[:, cols]


def _ada_call(c_all, w_ada, b_ada):
    rows = c_all.shape[0]
    return pl.pallas_call(
        _ada_kernel,
        grid=(DEPTH, N_MOD // ADA_KINDS),
        in_specs=[
            pl.BlockSpec((rows, D_MODEL), lambda l, k: (0, 0)),
            pl.BlockSpec((None, D_MODEL, ADA_KINDS * D_MODEL), lambda l, k: (l, 0, k)),
            pl.BlockSpec((None, 1, ADA_KINDS * D_MODEL), lambda l, k: (l, 0, k)),
        ],
        out_specs=pl.BlockSpec((None, ADA_KINDS, rows, D_MODEL), lambda l, k: (l, k, 0, 0)),
        out_shape=jax.ShapeDtypeStruct((DEPTH, N_MOD, rows, D_MODEL), F32),
        compiler_params=_cparams(2),
        name="adaln_mod",
    )(c_all, w_ada, b_ada.reshape(DEPTH, 1, N_MOD * D_MODEL))


CAST_ROWS = 256


def _cast_kernel(w_ref, o_ref):
    o_ref[...] = w_ref[...].astype(BF16)


CAST_BLOCK_BYTES = 6 * 1024 * 1024


def _cast_call(w):
    n_l, rows, cols = w.shape
    fits = [r for r in range(2 * SUBLANES, rows + 1, 2 * SUBLANES)
            if rows % r == 0 and r * cols * 4 <= CAST_BLOCK_BYTES]
    tr = max(fits) if fits else rows
    return pl.pallas_call(
        _cast_kernel,
        grid=(n_l, rows // tr),
        in_specs=[pl.BlockSpec((None, tr, cols), lambda l, r: (l, r, 0))],
        out_specs=pl.BlockSpec((None, tr, cols), lambda l, r: (l, r, 0)),
        out_shape=jax.ShapeDtypeStruct(w.shape, BF16),
        compiler_params=_cparams(2),
        name="cast_bf16",
    )(w)


def _prep_win_kernel(wt_ref, o_ref, kg_ref):
    def put(dst_col, src_row, n):
        for c in range(0, n, LANES):
            o_ref[:, dst_col + c:dst_col + c + LANES] = wt_ref[src_row + c:src_row + c + LANES, :].T.astype(BF16)

    put(Q_COL, 0, ML_WIDTH)
    put(V_COL, SRC_V, 2 * ML_WIDTH)
    put(B_COL, SRC_B, 3 * SC_WIDTH)
    kg_ref[0:ML_WIDTH, :] = wt_ref[SRC_K:SRC_V, :].astype(BF16)

    g8 = wt_ref[SRC_G:SRC_G + SUBLANES, :]
    g16 = jnp.concatenate([g8, g8], axis=0)
    row = lax.broadcasted_iota(jnp.int32, g16.shape, 0)
    mid = (row >= ML_HEADS) & (row < 3 * ML_HEADS)
    kg_ref[ML_WIDTH:KG_ROWS, :] = jnp.where(mid, pltpu.roll(g16, ML_HEADS, 0), g16).astype(BF16)

    gt = wt_ref[SRC_G:SRC_G + LANES, :].T
    ig = gt[:, 0:ML_HEADS]
    fg = gt[:, ML_HEADS:2 * ML_HEADS]
    pad = jnp.zeros((gt.shape[0], LANES - 2 * ML_HEADS), F32)
    o_ref[:, G1_COL:G2_COL] = jnp.concatenate([ig, ig, pad], axis=1).astype(BF16)
    o_ref[:, G2_COL:B_COL] = jnp.concatenate([fg, fg, pad], axis=1).astype(BF16)


def _prep_win_call(a_w_in_t):
    n_l, in_a, _ = a_w_in_t.shape
    return pl.pallas_call(
        _prep_win_kernel,
        grid=(n_l,),
        in_specs=[pl.BlockSpec((None, in_a, D_MODEL), lambda l: (l, 0, 0))],
        out_specs=[pl.BlockSpec((None, D_MODEL, IN_W), lambda l: (l, 0, 0)),
                   pl.BlockSpec((None, KG_ROWS, D_MODEL), lambda l: (l, 0, 0))],
        out_shape=[jax.ShapeDtypeStruct((n_l, D_MODEL, IN_W), BF16),
                   jax.ShapeDtypeStruct((n_l, KG_ROWS, D_MODEL), BF16)],
        compiler_params=_cparams(1),
        name="prep_w_in",
    )(a_w_in_t)


def _mod_row(mod_ref, kind, b):
    return mod_ref[kind, pl.ds(b, 1), :]


def _ffn_prompt_kernel(x_ref, mod_ref, nw_ref, wup_ref, cw_ref, wdn_ref,
                       xo_ref, st_ref, h_scr, act_scr, carry_scr):
    b = pl.program_id(0)
    t = pl.program_id(1)

    @pl.when(t == 0)
    def _():
        carry_scr[...] = jnp.zeros_like(carry_scr)

    x = x_ref[0]
    h_scr[...] = _norm_mod(x, nw_ref[...], _mod_row(mod_ref, 4, b), _mod_row(mod_ref, 3, b)).astype(BF16)
    for j in range(NCH):
        ys = []
        for col in (j * FC, D_FF + j * FC):
            cols = slice(col, col + FC)
            u = _dot(h_scr[...], wup_ref[:, cols])
            ys.append(_conv3_rows(u, carry_scr[:, cols], cw_ref[:, cols]))
            carry_scr[:, cols] = u[TM_FFN - SUBLANES:TM_FFN]
        g = ys[0]
        act_scr[:, j * FC:(j + 1) * FC] = (g * _sigmoid(g) * ys[1]).astype(BF16)
    y = _dot(act_scr[...], wdn_ref[...])
    xo_ref[0] = x + _mod_row(mod_ref, 5, b) * y
    st_ref[0] = carry_scr[...]


def _ffn_prompt_call(x, mod, n_seq_rows, l, nw, wup, cw, wdn):
    B, S, _ = x.shape
    assert S % TM_FFN == 0
    nt = S // TM_FFN
    once = pl.Buffered(1)
    return pl.pallas_call(
        _ffn_prompt_kernel,
        grid=(B, nt),
        in_specs=[
            pl.BlockSpec((1, TM_FFN, D_MODEL), lambda b, t: (b, t, 0)),
            pl.BlockSpec((None, N_MOD, SUBLANES, D_MODEL), lambda b, t: (l, 0, n_seq_rows // SUBLANES, 0)),
            pl.BlockSpec((None, 1, D_MODEL), lambda b, t: (l, 0, 0)),
            pl.BlockSpec((None, D_MODEL, 2 * D_FF), lambda b, t: (l, 0, 0), pipeline_mode=once),
            pl.BlockSpec((None, 3, 2 * D_FF), lambda b, t: (l, 0, 0)),
            pl.BlockSpec((None, D_FF, D_MODEL), lambda b, t: (l, 0, 0), pipeline_mode=once),
        ],
        out_specs=[
            pl.BlockSpec((1, TM_FFN, D_MODEL), lambda b, t: (b, t, 0)),
            pl.BlockSpec((1, SUBLANES, 2 * D_FF), lambda b, t: (b, 0, 0)),
        ],
        out_shape=[
            jax.ShapeDtypeStruct((B, S, D_MODEL), F32),
            jax.ShapeDtypeStruct((B, SUBLANES, 2 * D_FF), F32),
        ],
        scratch_shapes=[
            pltpu.VMEM((TM_FFN, D_MODEL), BF16),
            pltpu.VMEM((TM_FFN, D_FF), BF16),
            pltpu.VMEM((SUBLANES, 2 * D_FF), F32),
        ],
        compiler_params=_cparams(2),
        name="ffn_prompt",
    )(x, mod, nw, wup, cw, wdn)


def _ffn_sample_kernel(x_ref, mod_ref, nw_ref, s_ref, w_ref, c_ref, wdn_ref,
                       xo_ref, so_ref, h_scr, hb_scr, g2_scr, acc_scr, yg_scr):
    j = pl.program_id(1)
    n_seq = s_ref.shape[0]
    n_rows = x_ref.shape[0]
    n_t = n_rows // n_seq

    @pl.when(j == 0)
    def _():
        def modulate(b, carry):
            rows = pl.ds(pl.multiple_of(b * n_t, n_t), n_t)
            h_scr[rows, :] = _norm_mod(x_ref[rows, :], nw_ref[...], _mod_row(mod_ref, 4, b), _mod_row(mod_ref, 3, b))
            g2_scr[rows, :] = jnp.broadcast_to(_mod_row(mod_ref, 5, b), (n_t, D_MODEL))
            return carry

        lax.fori_loop(0, n_seq, modulate, 0, unroll=8)
        hb_scr[...] = h_scr[...].astype(BF16)
        acc_scr[...] = jnp.zeros_like(acc_scr)

    sub = lax.broadcasted_iota(jnp.int32, (n_seq, n_t, FC_S), 1)
    u3 = _dot(hb_scr[...], w_ref[...]).reshape(n_seq, n_t, FC_S)
    cw = c_ref[...]
    p0 = jnp.broadcast_to(s_ref[:, 0:1, :], (n_seq, n_t, FC_S))
    p1 = jnp.broadcast_to(s_ref[:, 1:2, :], (n_seq, n_t, FC_S))
    s1 = jnp.where(sub < 1, p1, pltpu.roll(u3, 1, 1))
    s2 = jnp.where(sub < 1, p0, jnp.where(sub < 2, p1, pltpu.roll(u3, 2, 1)))
    y = (s2 * cw[0:1] + s1 * cw[1:2] + u3 * cw[2:3]).reshape(n_rows, FC_S)
    so_ref[...] = pltpu.roll(u3, 2, 1)[:, 0:2, :]

    @pl.when(j < FS_GATE)
    def _():
        yg_scr[j] = y

    @pl.when(j >= FS_GATE)
    def _():
        g = yg_scr[j - FS_GATE]
        acc_scr[...] += _dot((g * _sigmoid(g) * y).astype(BF16), wdn_ref[...])

    @pl.when(j == 2 * FS_GATE - 1)
    def _():
        xo_ref[...] = x_ref[...] + g2_scr[...] * acc_scr[...]


def _ffn_sample_call(x, mod, l, nw, st, wup, cw, wdn, so_prev):
    n_seq = st.shape[1] // 2
    n_rows = x.shape[0] // 2
    return _stacked_call(
        _ffn_sample_kernel, 7, 1, so_prev, l,
        lambda lead, idx: pl.BlockSpec((lead, n_seq, 2, FC_S), lambda hf, j: (idx, hf, 0, j)),
        grid=(2, 2 * FS_GATE),
        in_specs=[
            pl.BlockSpec((n_rows, D_MODEL), lambda hf, j: (hf, 0)),
            pl.BlockSpec((None, N_MOD, n_seq, D_MODEL), lambda hf, j: (l, 0, hf, 0)),
            pl.BlockSpec((None, 1, D_MODEL), lambda hf, j: (l, 0, 0)),
            pl.BlockSpec((None, n_seq, 2, FC_S), lambda hf, j: (l, hf, 0, j)),
            pl.BlockSpec((None, D_MODEL, FC_S), lambda hf, j: (l, 0, j)),
            pl.BlockSpec((None, 3, FC_S), lambda hf, j: (l, 0, j)),
            pl.BlockSpec((None, FC_S, D_MODEL), lambda hf, j: (l, jnp.maximum(j - FS_GATE, 0), 0)),
        ],
        out_specs=[
            pl.BlockSpec((n_rows, D_MODEL), lambda hf, j: (hf, 0)),
            None,
        ],
        out_shape=[
            jax.ShapeDtypeStruct(x.shape, F32),
            jax.ShapeDtypeStruct((DEPTH, st.shape[1], 2, 2 * D_FF), F32),
        ],
        scratch_shapes=[
            pltpu.VMEM((n_rows, D_MODEL), F32),
            pltpu.VMEM((n_rows, D_MODEL), BF16),
            pltpu.VMEM((n_rows, D_MODEL), F32),
            pltpu.VMEM((n_rows, D_MODEL), F32),
            pltpu.VMEM((FS_GATE, n_rows, FC_S), F32),
        ],
        compiler_params=_cparams(2),
        name="ffn_sample",
    )(x, mod, nw, st, wup, cw, wdn)


def _chunk_consts(seq_len):
    r = lax.broadcasted_iota(jnp.int32, (CHUNK, CHUNK), 0)
    c = lax.broadcasted_iota(jnp.int32, (CHUNK, CHUNK), 1)
    if seq_len >= CHUNK:
        same = r >= 0
    else:
        same = (r // seq_len) == (c // seq_len)
    mask = same & (c <= r)
    lmat = mask.astype(BF16)
    lmat_t = (same & (r <= c)).astype(BF16)
    return mask, lmat, lmat_t, same.astype(BF16)


def _dot_exact01_r(a, m):
    a1 = a.astype(BF16)
    r1 = a - a1.astype(F32)
    a2 = r1.astype(BF16)
    a3 = (r1 - a2.astype(F32)).astype(BF16)
    return _dot(a1, m) + _dot(a2, m) + _dot(a3, m)


def _pick_cols_t(a, sel):
    a1 = a.astype(BF16)
    r1 = a - a1.astype(F32)
    a2 = r1.astype(BF16)
    a3 = (r1 - a2.astype(F32)).astype(BF16)
    return _dot_nt(sel, a1) + _dot_nt(sel, a2) + _dot_nt(sel, a3)


def _seq_max_lanes(x, seq_len):
    n = x.shape[1]
    pos = lax.broadcasted_iota(jnp.int32, x.shape, 1)
    d = 1
    while d < seq_len:
        partner = jnp.where((pos & d) == 0, pltpu.roll(x, n - d, 1), pltpu.roll(x, d, 1))
        x = jnp.maximum(x, partner)
        d *= 2
    return x


def _seq_prefix_max_rows(x, seq_len):
    pos = lax.broadcasted_iota(jnp.int32, x.shape, 0) & (seq_len - 1)
    d = 1
    while d < seq_len:
        x = jnp.where(pos >= d, jnp.maximum(x, pltpu.roll(x, d, 0)), x)
        d *= 2
    return x


def _seq_last_row(x, seq_len):
    n, w = x.shape
    if seq_len >= n:
        return jnp.broadcast_to(x[n - 1:n], x.shape)
    x3 = x.reshape(n // seq_len, seq_len, w)
    return jnp.broadcast_to(x3[:, seq_len - 1:seq_len, :], x3.shape).reshape(n, w)


def _gates_rows_pre(gt, bias_r, lmat_t, tot, seq_len):
    ig = gt[0:SUBLANES] + bias_r[0:SUBLANES]
    lf = _log_sigmoid(gt[SUBLANES:] + bias_r[SUBLANES:])
    b = _dot_exact01_r(lf, lmat_t)
    bl = _dot_exact01_r(lf, tot)
    v = ig - b
    return bl, v, _seq_max_lanes(v, seq_len)


def _gates_rows_post(pre, mp_r):
    bl, v, vm = pre
    mn = bl + jnp.maximum(mp_r, vm)
    return jnp.exp(bl + v - mn), jnp.exp(bl + mp_r - mn), mn


def _gates_rows(gt, bias_r, mp_r, lmat_t, tot, seq_len):
    pre = _gates_rows_pre(gt, bias_r, lmat_t, tot, seq_len)
    return (pre[1],) + _gates_rows_post(pre, mp_r)


def _gates_cols_pre(g1, g2, bias_c, lmat, seq_len):
    lane = lax.broadcasted_iota(jnp.int32, g1.shape, 1)
    ig = g1 + bias_c[0:1]
    lf = jnp.where(lane < 2 * ML_HEADS, _log_sigmoid(g2 + bias_c[1:2]), 0.0)
    b = _dot_exact01(lmat, lf)
    return b, _seq_prefix_max_rows(ig - b, seq_len)


def _gates_cols_post(b, cm, mp_c):
    g = b + mp_c
    mt = jnp.maximum(b + cm, g)
    return b - mt, jnp.exp(g - mt), jnp.exp(-mt)


def _gates_cols(g1, g2, bias_c, mp_c, lmat, seq_len):
    b, cm = _gates_cols_pre(g1, g2, bias_c, lmat, seq_len)
    return (b, cm) + _gates_cols_post(b, cm, mp_c)


def _outer_sum_lhs(u):
    lane = lax.broadcasted_iota(jnp.int32, u.shape, 1)
    hi = u.astype(BF16).astype(F32)
    lo = u - hi
    return jnp.where(lane < ML_HEADS, hi, jnp.where(lane < 2 * ML_HEADS, lo,
                     jnp.where(lane < 4 * ML_HEADS, 1.0, 0.0))).astype(BF16)


def _outer_sum_rhs(v_r, hd):
    row = lax.broadcasted_iota(jnp.int32, v_r.shape, 0)
    hi = v_r.astype(BF16).astype(F32)
    lo = v_r - hi
    pick = (row == hd) | (row == ML_HEADS + hd)
    top = jnp.where(pick, 1.0, 0.0)
    bot = jnp.where(row == hd, hi, jnp.where(row == ML_HEADS + hd, lo, 0.0))
    r16 = jnp.concatenate([top, bot], axis=0).astype(BF16)
    return jnp.concatenate([r16, jnp.zeros((LANES - 2 * SUBLANES, v_r.shape[1]), BF16)], axis=0)


def _decayed_scores(q, kt, lhsc, v_r, hd, mask):
    e = _dot(lhsc, _outer_sum_rhs(v_r, hd))
    return _dot(q.astype(BF16), kt.astype(BF16)) * jnp.where(mask, jnp.exp(e), 0.0)


def _conv3_rows(cx, prev8, cw):
    n = cx.shape[1]
    row = lax.broadcasted_iota(jnp.int32, (SUBLANES, n), 0)
    s1 = pltpu.roll(cx, 1, 0)
    s2 = pltpu.roll(cx, 2, 0)
    f1 = jnp.where(row < 1, pltpu.roll(prev8, 1, 0), s1[0:SUBLANES])
    f2 = jnp.where(row < 2, pltpu.roll(prev8, 2, 0), s2[0:SUBLANES])
    s1 = jnp.concatenate([f1, s1[SUBLANES:]], axis=0)
    s2 = jnp.concatenate([f2, s2[SUBLANES:]], axis=0)
    return s2 * cw[0:1] + s1 * cw[1:2] + cx * cw[2:3]


def _mlstm_out_norm(hm, zo, onw):
    ms = jnp.mean(hm * hm, axis=-1, keepdims=True)
    return hm * lax.rsqrt(ms + EPS) * onw * _sigmoid(zo)


def _even_tail(z_scr, hm_scr, cat_scr, onw_ref):
    for hd in range(ML_HEADS):
        col = slice(hd * ML_DV, (hd + 1) * ML_DV)
        zo = z_scr[:, O_COL + hd * ML_DV:O_COL + (hd + 1) * ML_DV]
        cat_scr[:, col] = _mlstm_out_norm(hm_scr[:, col], zo, onw_ref[:, col]).astype(BF16)


def _mix_even_prompt_kernel(x_ref, mod_ref, nw_ref, win_ref, wkg_ref, bifc_ref, bifr_ref, onw_ref, cw_ref,
                            wout_ref, xo_ref, co_ref, no_ref, mo_ref, sco_ref,
                            z_scr, zt_scr, hm_scr, cat_scr, cn_scr, mrow_scr, mlane_scr, cc_scr):
    t = pl.program_id(1)

    @pl.when(t == 0)
    def _():
        cn_scr[...] = jnp.zeros_like(cn_scr)
        mrow_scr[...] = jnp.zeros_like(mrow_scr)
        mlane_scr[...] = jnp.zeros_like(mlane_scr)
        cc_scr[...] = jnp.zeros_like(cc_scr)

    x = x_ref[0]
    bi = pl.program_id(0)
    h = _norm_mod(x, nw_ref[...], _mod_row(mod_ref, 1, bi), _mod_row(mod_ref, 0, bi)).astype(BF16)
    z_scr[...] = _dot(h, win_ref[...])
    zt_scr[...] = _dot_nt(wkg_ref[...], h)

    mask, lmat, lmat_t, tot = _chunk_consts(CHUNK)
    scale = ML_DK ** -0.5
    ones_v = jnp.ones((CHUNK, ML_DV), BF16)

    n_ch = TM // CHUNK
    chunk_rows = [slice(c * CHUNK, (c + 1) * CHUNK) for c in range(n_ch)]
    pre_r = [_gates_rows_pre(zt_scr[ML_WIDTH:KG_ROWS, rows], bifr_ref[...], lmat_t, tot, CHUNK)
             for rows in chunk_rows]
    pre_c = [_gates_cols_pre(z_scr[rows, G1_COL:G1_COL + LANES], z_scr[rows, G2_COL:G2_COL + LANES],
                             bifc_ref[...], lmat, CHUNK) for rows in chunk_rows]
    mp_r = mrow_scr[...]
    mp_c = mlane_scr[0:1, :]
    post_r, post_c = [], []
    for c in range(n_ch):
        post_r.append(_gates_rows_post(pre_r[c], mp_r))
        mp_r = post_r[c][2]
        b, cm = pre_c[c]
        post_c.append(_gates_cols_post(b, cm, mp_c))
        mp_c = b[CHUNK - 1:CHUNK] + jnp.maximum(mp_c, cm[CHUNK - 1:CHUNK])
    mrow_scr[...] = mp_r
    mlane_scr[...] = jnp.broadcast_to(mp_c, (SUBLANES, LANES))

    intra, upd, wgq = {}, {}, {}
    for c, rows in enumerate(chunk_rows):
        u, wg, _ = post_c[c]
        lhsc = _outer_sum_lhs(u)
        for hd in range(ML_HEADS):
            q = z_scr[rows, Q_COL + hd * ML_DK:Q_COL + (hd + 1) * ML_DK]
            v = z_scr[rows, V_COL + hd * ML_DV:V_COL + (hd + 1) * ML_DV]
            kt = zt_scr[hd * ML_DK:(hd + 1) * ML_DK, rows] * scale
            s = _decayed_scores(q, kt, lhsc, pre_r[c][1], hd, mask)
            v1 = jnp.concatenate([v.astype(BF16), ones_v], axis=1)
            intra[c, hd] = _dot(s.astype(BF16), v1)
            upd[c, hd] = _dot((kt * post_r[c][0][hd:hd + 1, :]).astype(BF16), v1)
            wgq[c, hd] = (q * wg[:, hd:hd + 1]).astype(BF16)

    for hd in range(ML_HEADS):
        cn = cn_scr[hd]
        for c, rows in enumerate(chunk_rows):
            out = intra[c, hd] + _dot(wgq[c, hd], cn.astype(BF16))
            r = 1.0 / jnp.maximum(jnp.abs(out[:, ML_DV:]), post_c[c][2][:, hd:hd + 1])
            hm_scr[rows, hd * ML_DV:(hd + 1) * ML_DV] = out[:, :ML_DV] * r
            wc = post_r[c][1][hd:hd + 1, :]
            cn = jnp.concatenate([wc, wc], axis=1) * cn + upd[c, hd]
        cn_scr[hd] = cn

    _even_tail(z_scr, hm_scr, cat_scr, onw_ref)
    cx = z_scr[:, C_COL:C_COL + SC_WIDTH] * z_scr[:, X_COL:X_COL + SC_WIDTH]
    u = _conv3_rows(cx, cc_scr[...], cw_ref[...])
    cc_scr[...] = cx[TM - SUBLANES:TM]
    cat_scr[:, ML_WIDTH:] = (z_scr[:, B_COL:B_COL + SC_WIDTH] * u).astype(BF16)

    y = _dot(cat_scr[...], wout_ref[...])
    xo_ref[0] = x + _mod_row(mod_ref, 2, bi) * y
    for hd in range(ML_HEADS):
        co_ref[0, hd] = cn_scr[hd, :, 0:ML_DV]
        no_ref[0, hd] = cn_scr[hd, :, ML_DV:]
    mo_ref[0] = mrow_scr[...]
    sco_ref[0] = cc_scr[...]


def _even_weight_specs(i, idx):
    once = pl.Buffered(1)
    return [
        pl.BlockSpec((None, D_MODEL, IN_W), idx, pipeline_mode=once),
        pl.BlockSpec((None, KG_ROWS, D_MODEL), idx, pipeline_mode=once),
        pl.BlockSpec((None, 2, LANES), idx),
        pl.BlockSpec((None, 2 * SUBLANES, LANES), idx),
        pl.BlockSpec((None, 1, ML_WIDTH), idx),
        pl.BlockSpec((None, 3, SC_WIDTH), idx),
        pl.BlockSpec((None, ML_WIDTH + SC_WIDTH, D_MODEL), idx, pipeline_mode=once),
    ]


def _mix_even_prompt_call(x, mod, n_seq_rows, l, nw, win, wkg, bifc, bifr, onw, cw, wout):
    B, S, _ = x.shape
    nt = S // TM
    i = l // 2
    return pl.pallas_call(
        _mix_even_prompt_kernel,
        grid=(B, nt),
        in_specs=[
            pl.BlockSpec((1, TM, D_MODEL), lambda b, t: (b, t, 0)),
            pl.BlockSpec((None, N_MOD, SUBLANES, D_MODEL), lambda b, t: (l, 0, n_seq_rows // SUBLANES, 0)),
            pl.BlockSpec((None, 1, D_MODEL), lambda b, t: (l, 0, 0)),
        ] + _even_weight_specs(i, lambda b, t: (i, 0, 0)),
        out_specs=[
            pl.BlockSpec((1, TM, D_MODEL), lambda b, t: (b, t, 0)),
            pl.BlockSpec((1, ML_HEADS, ML_DK, ML_DV), lambda b, t: (b, 0, 0, 0)),
            pl.BlockSpec((1, ML_HEADS, ML_DK, LANES), lambda b, t: (b, 0, 0, 0)),
            pl.BlockSpec((1, SUBLANES, LANES), lambda b, t: (b, 0, 0)),
            pl.BlockSpec((1, SUBLANES, SC_WIDTH), lambda b, t: (b, 0, 0)),
        ],
        out_shape=[
            jax.ShapeDtypeStruct((B, S, D_MODEL), F32),
            jax.ShapeDtypeStruct((B, ML_HEADS, ML_DK, ML_DV), F32),
            jax.ShapeDtypeStruct((B, ML_HEADS, ML_DK, LANES), F32),
            jax.ShapeDtypeStruct((B, SUBLANES, LANES), F32),
            jax.ShapeDtypeStruct((B, SUBLANES, SC_WIDTH), F32),
        ],
        scratch_shapes=[
            pltpu.VMEM((TM, IN_W), F32),
            pltpu.VMEM((KG_ROWS, TM), F32),
            pltpu.VMEM((TM, ML_WIDTH), F32),
            pltpu.VMEM((TM, ML_WIDTH + SC_WIDTH), BF16),
            pltpu.VMEM((ML_HEADS, ML_DK, ML_DV + LANES), F32),
            pltpu.VMEM((SUBLANES, LANES), F32),
            pltpu.VMEM((SUBLANES, LANES), F32),
            pltpu.VMEM((SUBLANES, SC_WIDTH), F32),
        ],
        compiler_params=_cparams(2),
        name="mix_even_prompt",
    )(x, mod, nw, win, wkg, bifc, bifr, onw, cw, wout)


def _mix_even_sample_kernel(x_ref, mod_ref, nw_ref, win_ref, wkg_ref, bifc_ref, bifr_ref, onw_ref, cw_ref,
                            wout_ref, c_ref, nt_ref, ntt_ref, mcol_ref, mrow_ref, sc_ref,
                            xo_ref, co_ref, no_ref, mo_ref, sco_ref,
                            h_scr, z_scr, zt_scr, hm_scr, cat_scr, g1_scr,
                            intra_scr, dpart_scr, pbe_scr, wgq_scr, kwt_scr, vb_scr, wcb_scr, inter_scr):
    n_tok = x_ref.shape[0]
    seq_len = n_tok // SEQ_BLK

    def modulate(b, carry):
        rows = pl.ds(pl.multiple_of(b * seq_len, seq_len), seq_len)
        h_scr[rows, :] = _norm_mod(x_ref[rows, :], nw_ref[...], _mod_row(mod_ref, 1, b), _mod_row(mod_ref, 0, b))
        g1_scr[rows, :] = jnp.broadcast_to(_mod_row(mod_ref, 2, b), (seq_len, D_MODEL))
        return carry

    lax.fori_loop(0, SEQ_BLK, modulate, 0, unroll=8)
    hb = h_scr[...].astype(BF16)
    z_scr[...] = _dot(hb, win_ref[...])
    zt_scr[...] = _dot_nt(wkg_ref[...], hb)

    mask, lmat, lmat_t, tot = _chunk_consts(seq_len)
    scale = ML_DK ** -0.5
    ones_v = jnp.ones((CHUNK, ML_DV), BF16)
    mp_r = mrow_ref[...]
    v_r, ws_r, wc_r, mn_r = _gates_rows(zt_scr[ML_WIDTH:KG_ROWS, :], bifr_ref[...], mp_r, lmat_t, tot, seq_len)
    mo_ref[...] = mn_r
    mp_c = mcol_ref[...]
    b, cm, u, wg, pbe = _gates_cols(z_scr[:, G1_COL:G1_COL + LANES], z_scr[:, G2_COL:G2_COL + LANES],
                                    bifc_ref[...], mp_c, lmat, seq_len)
    wc_c = jnp.exp(mp_c - jnp.maximum(mp_c, _seq_last_row(cm, seq_len)))
    lhsc = _outer_sum_lhs(u)
    first_tok = (lax.broadcasted_iota(jnp.int32, (SEQ_BLK, n_tok), 0) * seq_len
                 == lax.broadcasted_iota(jnp.int32, (SEQ_BLK, n_tok), 1)).astype(BF16)

    for hd in range(ML_HEADS):
        q = z_scr[:, Q_COL + hd * ML_DK:Q_COL + (hd + 1) * ML_DK]
        v = z_scr[:, V_COL + hd * ML_DV:V_COL + (hd + 1) * ML_DV]
        kt = zt_scr[hd * ML_DK:(hd + 1) * ML_DK, :] * scale
        s = _decayed_scores(q, kt, lhsc, v_r, hd, mask)
        vb = v.astype(BF16)
        out = _dot(s.astype(BF16), jnp.concatenate([vb, ones_v], axis=1))
        wg_h = wg[:, hd:hd + 1]
        qn = jnp.sum(q * nt_ref[hd], axis=-1, keepdims=True)
        intra_scr[hd] = out[:, :ML_DV]
        dpart_scr[hd] = out[:, ML_DV:] + wg_h * qn
        pbe_scr[hd] = jnp.broadcast_to(pbe[:, hd:hd + 1], (n_tok, LANES))
        wgq_scr[hd] = q * wg_h
        kwt = kt * ws_r[hd:hd + 1, :]
        kwt_scr[hd] = kwt
        vb_scr[hd] = vb
        wcb_scr[hd] = jnp.broadcast_to(wc_c[:, hd:hd + 1], (n_tok, LANES))
        n_new_t = wc_r[hd:hd + 1, :] * ntt_ref[hd] + _dot_exact01_r(kwt, tot)
        no_ref[hd] = _pick_cols_t(n_new_t, first_tok)

    lane_i = lax.broadcasted_iota(jnp.int32, (ML_DK, n_tok), 1)

    def per_seq(bq, carry):
        r0 = pl.multiple_of(bq * seq_len, seq_len)
        rows = pl.ds(r0, seq_len)
        sel = (lane_i >= r0) & (lane_i < r0 + seq_len)
        for hd in range(ML_HEADS):
            c_prev = c_ref[bq, hd]
            inter_scr[hd, rows, :] = _dot(wgq_scr[hd, rows, :].astype(BF16), c_prev.astype(BF16))
            kw_b = jnp.where(sel, kwt_scr[hd], 0.0).astype(BF16)
            co_ref[bq, hd] = wcb_scr[hd, pl.ds(r0, 1), :] * c_prev + _dot(kw_b, vb_scr[hd])
        return carry

    lax.fori_loop(0, SEQ_BLK, per_seq, 0, unroll=8)

    for hd in range(ML_HEADS):
        num = inter_scr[hd] + intra_scr[hd]
        hm_scr[:, hd * ML_DV:(hd + 1) * ML_DV] = num * (
            1.0 / jnp.maximum(jnp.abs(dpart_scr[hd]), pbe_scr[hd]))

    _even_tail(z_scr, hm_scr, cat_scr, onw_ref)
    cx = z_scr[:, C_COL:C_COL + SC_WIDTH] * z_scr[:, X_COL:X_COL + SC_WIDTH]
    sub = lax.broadcasted_iota(jnp.int32, (n_tok, SC_WIDTH), 0) % seq_len
    p1 = sc_ref[...]
    s1 = jnp.where(sub < 1, pltpu.roll(p1, n_tok - 1, 0), pltpu.roll(cx, 1, 0))
    s2 = jnp.where(sub < 2, p1, pltpu.roll(cx, 2, 0))
    cw = cw_ref[...]
    u = s2 * cw[0:1] + s1 * cw[1:2] + cx * cw[2:3]
    sco_ref[...] = pltpu.roll(cx, n_tok - (seq_len - 2), 0)
    cat_scr[:, ML_WIDTH:] = (z_scr[:, B_COL:B_COL + SC_WIDTH] * u).astype(BF16)

    y = _dot(cat_scr[...], wout_ref[...])
    xo_ref[...] = x_ref[...] + g1_scr[...] * y


def _mix_even_sample_call(x, mod, l, nw, win, wkg, bifc, bifr, onw, cw, wout, c0, n_tok, n_tok_t, m_col, m_row,
                          sc_pad, co_prev):
    n_rows = x.shape[0]
    n_seq = c0.shape[1]
    seq_len = n_rows // n_seq
    blk = SEQ_BLK * seq_len
    nb = n_seq // SEQ_BLK
    li = l // 2
    head_blk = (ML_HEADS, blk, LANES)
    return _stacked_call(
        _mix_even_sample_kernel, 16, 1, co_prev, li,
        lambda lead, idx: pl.BlockSpec((lead, SEQ_BLK, ML_HEADS, ML_DK, ML_DV), lambda i: (idx, i, 0, 0, 0)),
        grid=(nb,),
        in_specs=[
            pl.BlockSpec((blk, D_MODEL), lambda i: (i, 0)),
            pl.BlockSpec((None, N_MOD, SEQ_BLK, D_MODEL), lambda i: (l, 0, i, 0)),
            pl.BlockSpec((None, 1, D_MODEL), lambda i: (l, 0, 0)),
        ] + _even_weight_specs(li, lambda i: (li, 0, 0)) + [
            pl.BlockSpec((None, SEQ_BLK, ML_HEADS, ML_DK, ML_DV), lambda i: (li, i, 0, 0, 0)),
            pl.BlockSpec((None, ML_HEADS, blk, ML_DK), lambda i: (li, 0, i, 0)),
            pl.BlockSpec((None, ML_HEADS, ML_DK, blk), lambda i: (li, 0, 0, i)),
            pl.BlockSpec((None, blk, LANES), lambda i: (li, i, 0)),
            pl.BlockSpec((None, SUBLANES, blk), lambda i: (li, 0, i)),
            pl.BlockSpec((None, blk, SC_WIDTH), lambda i: (li, i, 0)),
        ],
        out_specs=[
            pl.BlockSpec((blk, D_MODEL), lambda i: (i, 0)),
            None,
            pl.BlockSpec((ML_HEADS, SEQ_BLK, ML_DK), lambda i: (0, i, 0)),
            pl.BlockSpec((SUBLANES, blk), lambda i: (0, i)),
            pl.BlockSpec((blk, SC_WIDTH), lambda i: (i, 0)),
        ],
        out_shape=[
            jax.ShapeDtypeStruct((n_rows, D_MODEL), F32),
            jax.ShapeDtypeStruct(c0.shape, F32),
            jax.ShapeDtypeStruct((ML_HEADS, n_seq, ML_DK), F32),
            jax.ShapeDtypeStruct((SUBLANES, n_rows), F32),
            jax.ShapeDtypeStruct((n_rows, SC_WIDTH), F32),
        ],
        scratch_shapes=[
            pltpu.VMEM((blk, D_MODEL), F32),
            pltpu.VMEM((blk, IN_W), F32),
            pltpu.VMEM((KG_ROWS, blk), F32),
            pltpu.VMEM((blk, ML_WIDTH), F32),
            pltpu.VMEM((blk, ML_WIDTH + SC_WIDTH), BF16),
            pltpu.VMEM((blk, D_MODEL), F32),
            pltpu.VMEM(head_blk, F32),
            pltpu.VMEM(head_blk, F32),
            pltpu.VMEM(head_blk, F32),
            pltpu.VMEM(head_blk, F32),
            pltpu.VMEM((ML_HEADS, ML_DK, blk), F32),
            pltpu.VMEM(head_blk, BF16),
            pltpu.VMEM(head_blk, F32),
            pltpu.VMEM(head_blk, F32),
        ],
        compiler_params=_cparams(1),
        name="mix_even_sample",
    )(x, mod, nw, win, wkg, bifc, bifr, onw, cw, wout, c0, n_tok, n_tok_t, m_col, m_row, sc_pad)


def _split2(x):
    hi = x.astype(BF16)
    lo = (x - hi.astype(F32)).astype(BF16)
    return jnp.concatenate([hi, lo], axis=1)


def _head_lane_mats():
    r = lax.broadcasted_iota(jnp.int32, (2 * LANES, LANES), 0) % LANES
    c = lax.broadcasted_iota(jnp.int32, (2 * LANES, LANES), 1)
    hsum = ((r // HEAD_DIM) == (c // HEAD_DIM)).astype(BF16)
    half = HEAD_DIM // 2
    src = jnp.where((c % HEAD_DIM) < half, c + half, c - half)
    return hsum, (r == src).astype(BF16)


def _q_lane_mat():
    r = lax.broadcasted_iota(jnp.int32, (2 * LANES, 2 * LANES), 0)
    c = lax.broadcasted_iota(jnp.int32, (2 * LANES, 2 * LANES), 1)
    half = HEAD_DIM // 2
    src = jnp.where((c % HEAD_DIM) < half, c + half, c - half)
    top = (r < LANES) & (c < LANES) & ((r // HEAD_DIM) == (c // HEAD_DIM))
    return (top | ((r >= LANES) & (c >= LANES) & (r == src))).astype(BF16)


def _q_norm_rope(xb, gw, cos, sin, qmat):
    zg = xb * gw
    out = _dot(jnp.concatenate([(xb * xb).astype(BF16), zg.astype(BF16)], axis=1), qmat)
    ms = out[:, :LANES] * (1.0 / HEAD_DIM)
    return lax.rsqrt(ms + EPS) * (zg * cos + out[:, LANES:] * sin)


def _qk_norm_rope(xb, gw, cos, sin, hsum, rot_mat):
    ms = _dot(_split2(xb * xb), hsum) * (1.0 / HEAD_DIM)
    zg = xb * gw
    rot = _dot(_split2(zg), rot_mat)
    return lax.rsqrt(ms + EPS) * (zg * cos + rot * sin)


def _sink_rows(sink8, reps):
    return jnp.concatenate(
        [jnp.broadcast_to(sink8[r:r + 1, :], (reps, LANES)) for r in range(SUBLANES)], axis=0)


def _attn_prompt_kernel(x_ref, mod_ref, nw_ref, wqkv_ref, qnw_ref, knw_ref, cos_ref, sin_ref,
                        sink_ref, wout_ref, xo_ref, ko_ref, vo_ref,
                        z_scr, qm_scr, k_scr, v_scr, o_scr):
    t = pl.program_id(1)
    n_qb = TM // WINDOW

    @pl.when(t == 0)
    def _():
        k_scr[0:WINDOW, :] = jnp.zeros((WINDOW, KV_W), BF16)
        v_scr[0:WINDOW, :] = jnp.zeros((WINDOW, KV_W), BF16)

    x = x_ref[0]
    bi = pl.program_id(0)
    h = _norm_mod(x, nw_ref[...], _mod_row(mod_ref, 1, bi), _mod_row(mod_ref, 0, bi)).astype(BF16)
    z_scr[...] = _dot(h, wqkv_ref[...])
    cos = cos_ref[...]
    sin = sin_ref[...]
    hmats = _head_lane_mats()
    qmat = _q_lane_mat()
    half0 = lax.broadcasted_iota(jnp.int32, (TM, LANES), 1) < HEAD_DIM
    qscale = HEAD_DIM ** -0.5
    for jb in range(Q_W // LANES):
        y = _q_norm_rope(z_scr[:, jb * LANES:(jb + 1) * LANES], qnw_ref[...], cos, sin, qmat) * qscale
        qm_scr[2 * jb] = jnp.where(half0, y, 0.0).astype(BF16)
        qm_scr[2 * jb + 1] = jnp.where(half0, 0.0, y).astype(BF16)
    for p in range(KV_W // LANES):
        kf = _qk_norm_rope(z_scr[:, Q_W + p * LANES:Q_W + (p + 1) * LANES], knw_ref[...], cos, sin, *hmats)
        ko_ref[0, :, p * LANES:(p + 1) * LANES] = kf[TM - WINDOW:TM]
        k_scr[WINDOW:WINDOW + TM, p * LANES:(p + 1) * LANES] = kf.astype(BF16)
    vf = z_scr[:, Q_W + KV_W:Q_W + 2 * KV_W]
    vo_ref[0] = vf[TM - WINDOW:TM]
    v_scr[WINDOW:WINDOW + TM, :] = vf.astype(BF16)

    r = lax.broadcasted_iota(jnp.int32, (8 * WINDOW, 2 * WINDOW), 0) % WINDOW
    c = lax.broadcasted_iota(jnp.int32, (8 * WINDOW, 2 * WINDOW), 1)
    valid = ((c < WINDOW) & (c > r)) | ((c >= WINDOW) & ((c - WINDOW) <= r))
    first_lim = jnp.where(t == 0, WINDOW, 0)
    half0q = lax.broadcasted_iota(jnp.int32, (4 * WINDOW, LANES), 1) < HEAD_DIM
    ones_kv = jnp.ones((2 * WINDOW, LANES), BF16)
    for qb in range(n_qb):
        rows = slice(qb * WINDOW, (qb + 1) * WINDOW)
        krows = slice(qb * WINDOW, (qb + 2) * WINDOW)
        vmask = (valid & (c >= first_lim)) if qb == 0 else valid
        for p in range(KV_W // LANES):
            kb = k_scr[krows, p * LANES:(p + 1) * LANES]
            vb = v_scr[krows, p * LANES:(p + 1) * LANES]
            qs = jnp.concatenate([qm_scr[2 * (4 * p + i) + e, rows, :] for e in range(2) for i in range(4)],
                                 axis=0)
            s = jnp.where(vmask, _dot_nt(qs, kb), -jnp.inf)
            sk = _sink_rows(sink_ref[p], WINDOW)
            mx = jnp.maximum(jnp.max(s, axis=-1, keepdims=True), sk)
            pr = jnp.exp(s - jnp.concatenate([mx, mx], axis=1))
            o2 = _dot(pr.astype(BF16), jnp.concatenate([vb, ones_kv], axis=1))
            den = o2[:, LANES:] + jnp.exp(sk - mx)
            o = o2[:, :LANES] * (1.0 / den)
            merged = jnp.where(half0q, o[0:4 * WINDOW], o[4 * WINDOW:])
            for i in range(4):
                o_scr[rows, (4 * p + i) * LANES:(4 * p + i + 1) * LANES] = (
                    merged[i * WINDOW:(i + 1) * WINDOW].astype(BF16))

    y = _dot(o_scr[...], wout_ref[...])
    xo_ref[0] = x + _mod_row(mod_ref, 2, bi) * y
    k_scr[0:WINDOW, :] = k_scr[TM:TM + WINDOW, :]
    v_scr[0:WINDOW, :] = v_scr[TM:TM + WINDOW, :]


def _attn_prompt_call(x, mod, n_seq_rows, l, nw, wqkv, qnw, knw, cos, sin, sink, wout):
    B, S, _ = x.shape
    nt = S // TM
    lj = l // 2
    layer3 = lambda b, t: (lj, 0, 0)
    return pl.pallas_call(
        _attn_prompt_kernel,
        grid=(B, nt),
        in_specs=[
            pl.BlockSpec((1, TM, D_MODEL), lambda b, t: (b, t, 0)),
            pl.BlockSpec((None, N_MOD, SUBLANES, D_MODEL), lambda b, t: (l, 0, n_seq_rows // SUBLANES, 0)),
            pl.BlockSpec((None, 1, D_MODEL), lambda b, t: (l, 0, 0)),
            pl.BlockSpec((None, D_MODEL, Q_W + 2 * KV_W), layer3),
            pl.BlockSpec((None, 1, LANES), layer3),
            pl.BlockSpec((None, 1, LANES), layer3),
            pl.BlockSpec((TM, LANES), lambda b, t: (t, 0)),
            pl.BlockSpec((TM, LANES), lambda b, t: (t, 0)),
            pl.BlockSpec((None, 2, SUBLANES, LANES), lambda b, t: (lj, 0, 0, 0)),
            pl.BlockSpec((None, Q_W, D_MODEL), layer3),
        ],
        out_specs=[
            pl.BlockSpec((1, TM, D_MODEL), lambda b, t: (b, t, 0)),
            pl.BlockSpec((1, WINDOW, KV_W), lambda b, t: (b, 0, 0)),
            pl.BlockSpec((1, WINDOW, KV_W), lambda b, t: (b, 0, 0)),
        ],
        out_shape=[
            jax.ShapeDtypeStruct((B, S, D_MODEL), F32),
            jax.ShapeDtypeStruct((B, WINDOW, KV_W), F32),
            jax.ShapeDtypeStruct((B, WINDOW, KV_W), F32),
        ],
        scratch_shapes=[
            pltpu.VMEM((TM, Q_W + 2 * KV_W), F32),
            pltpu.VMEM((2 * Q_W // LANES, TM, LANES), BF16),
            pltpu.VMEM((TM + WINDOW, KV_W), BF16),
            pltpu.VMEM((TM + WINDOW, KV_W), BF16),
            pltpu.VMEM((TM, Q_W), BF16),
        ],
        compiler_params=_cparams(2),
        name="attn_prompt",
    )(x, mod, nw, wqkv, qnw, knw, cos, sin, sink, wout)


def _attn_sample_kernel(x_ref, mod_ref, nw_ref, wqkv_ref, qnw_ref, knw_ref, cos_ref, sin_ref,
                        sink_ref, wout_ref, kc_ref, vc_ref,
                        xo_ref, kco_ref, vco_ref,
                        h_scr, g1_scr, z_scr, qm_scr, kn_scr, o_scr):
    n_tok = x_ref.shape[0]
    seq_len = n_tok // SEQ_BLK
    win = kc_ref.shape[1]

    def modulate(b, carry):
        rows = pl.ds(pl.multiple_of(b * seq_len, seq_len), seq_len)
        h_scr[rows, :] = _norm_mod(x_ref[rows, :], nw_ref[...], _mod_row(mod_ref, 1, b), _mod_row(mod_ref, 0, b))
        g1_scr[rows, :] = jnp.broadcast_to(_mod_row(mod_ref, 2, b), (seq_len, D_MODEL))
        return carry

    lax.fori_loop(0, SEQ_BLK, modulate, 0, unroll=8)
    z_scr[...] = _dot(h_scr[...].astype(BF16), wqkv_ref[...])
    cos = cos_ref[...]
    sin = sin_ref[...]
    hmats = _head_lane_mats()
    half0 =lax.broadcasted_iota(jnp.int32, (n_tok, LANES), 1) < HEAD_DIM
    qscale = HEAD_DIM ** -0.5
    for jb in range(Q_W // LANES):
        y = _qk_norm_rope(z_scr[:, jb * LANES:(jb + 1) * LANES], qnw_ref[...], cos, sin, *hmats) * qscale
        qm_scr[2 * jb] = jnp.where(half0, y, 0.0)
        qm_scr[2 * jb + 1] = jnp.where(half0, 0.0, y)
    for p in range(KV_W // LANES):
        kn_scr[:, p * LANES:(p + 1) * LANES] = _qk_norm_rope(
            z_scr[:, Q_W + p * LANES:Q_W + (p + 1) * LANES], knw_ref[...], cos, sin, *hmats)

    n_q = 8 * seq_len
    tq = lax.broadcasted_iota(jnp.int32, (SEQ_BLK, n_q, 2 * win), 1) % seq_len
    cc = lax.broadcasted_iota(jnp.int32, (SEQ_BLK, n_q, 2 * win), 2)
    valid = ((cc < win) & (cc > tq)) | ((cc >= 2 * win - seq_len) & ((cc - (2 * win - seq_len)) <= tq))
    half0q = lax.broadcasted_iota(jnp.int32, (SEQ_BLK, n_q // 2, LANES), 2) < HEAD_DIM
    ones_kv = jnp.ones((SEQ_BLK, 2 * win, LANES), BF16)

    kc = kc_ref[...]
    vc = vc_ref[...]
    knew = jnp.concatenate([kc[:, seq_len:], kn_scr[...].reshape(SEQ_BLK, seq_len, KV_W)], axis=1)
    vnew = jnp.concatenate(
        [vc[:, seq_len:], z_scr[:, Q_W + KV_W:Q_W + 2 * KV_W].reshape(SEQ_BLK, seq_len, KV_W)], axis=1)
    kco_ref[...] = knew
    vco_ref[...] = vnew
    for p in range(KV_W // LANES):
        lanes = slice(p * LANES, (p + 1) * LANES)
        qs = jnp.concatenate([qm_scr[2 * (4 * p + i) + e].reshape(SEQ_BLK, seq_len, LANES)
                              for e in range(2) for i in range(4)], axis=1).astype(BF16)
        kk = jnp.concatenate([kc[:, :, lanes], knew[:, :, lanes]], axis=1).astype(BF16)
        vv = jnp.concatenate([vc[:, :, lanes], vnew[:, :, lanes]], axis=1).astype(BF16)
        s = jnp.einsum("bqd,bkd->bqk", qs, kk, preferred_element_type=F32)
        s = jnp.where(valid, s, -jnp.inf)
        sk = _sink_rows(sink_ref[p], seq_len)[None]
        mx = jnp.maximum(jnp.max(s, axis=-1, keepdims=True), sk)
        pr = jnp.exp(s - jnp.concatenate([mx, mx], axis=-1))
        o2 = jnp.einsum("bqk,bkd->bqd", pr.astype(BF16), jnp.concatenate([vv, ones_kv], axis=-1),
                        preferred_element_type=F32)
        o = o2[:, :, :LANES] * (1.0 / (o2[:, :, LANES:] + jnp.exp(sk - mx)))
        merged = jnp.where(half0q, o[:, 0:n_q // 2], o[:, n_q // 2:])
        for i in range(4):
            o_scr[:, (4 * p + i) * LANES:(4 * p + i + 1) * LANES] = (
                merged[:, i * seq_len:(i + 1) * seq_len].reshape(n_tok, LANES))

    y = _dot(o_scr[...].astype(BF16), wout_ref[...])
    xo_ref[...] = x_ref[...] + g1_scr[...] * y


def _attn_sample_call(x, mod, l, nw, wqkv, qnw, knw, cos, sin, sink, wout, kc, vc):
    n_rows = x.shape[0]
    _, n_seq, win, _ = kc.shape
    lj = l // 2
    seq_len = n_rows // n_seq
    blk = SEQ_BLK * seq_len
    nb = n_seq // SEQ_BLK
    const2 = lambda i: (0, 0)
    return pl.pallas_call(
        _attn_sample_kernel,
        input_output_aliases={10: 1, 11: 2},
        grid=(nb,),
        in_specs=[
            pl.BlockSpec((blk, D_MODEL), lambda i: (i, 0)),
            pl.BlockSpec((None, N_MOD, SEQ_BLK, D_MODEL), lambda i: (l, 0, i, 0)),
            pl.BlockSpec((None, 1, D_MODEL), lambda i: (l, 0, 0)),
            pl.BlockSpec((None, D_MODEL, Q_W + 2 * KV_W), lambda i: (lj, 0, 0)),
            pl.BlockSpec((None, 1, LANES), lambda i: (lj, 0, 0)),
            pl.BlockSpec((None, 1, LANES), lambda i: (lj, 0, 0)),
            pl.BlockSpec((blk, LANES), const2),
            pl.BlockSpec((blk, LANES), const2),
            pl.BlockSpec((None, 2, SUBLANES, LANES), lambda i: (lj, 0, 0, 0)),
            pl.BlockSpec((None, Q_W, D_MODEL), lambda i: (lj, 0, 0)),
            pl.BlockSpec((None, SEQ_BLK, win, KV_W), lambda i: (lj, i, 0, 0)),
            pl.BlockSpec((None, SEQ_BLK, win, KV_W), lambda i: (lj, i, 0, 0)),
        ],
        out_specs=[
            pl.BlockSpec((blk, D_MODEL), lambda i: (i, 0)),
            pl.BlockSpec((None, SEQ_BLK, win, KV_W), lambda i: (lj, i, 0, 0)),
            pl.BlockSpec((None, SEQ_BLK, win, KV_W), lambda i: (lj, i, 0, 0)),
        ],
        out_shape=[
            jax.ShapeDtypeStruct((n_rows, D_MODEL), F32),
            jax.ShapeDtypeStruct(kc.shape, F32),
            jax.ShapeDtypeStruct(vc.shape, F32),
        ],
        scratch_shapes=[
            pltpu.VMEM((blk, D_MODEL), F32),
            pltpu.VMEM((blk, D_MODEL), F32),
            pltpu.VMEM((blk, Q_W + 2 * KV_W), F32),
            pltpu.VMEM((2 * Q_W // LANES, blk, LANES), F32),
            pltpu.VMEM((blk, KV_W), F32),
            pltpu.VMEM((blk, Q_W), F32),
        ],
        compiler_params=_cparams(1),
        name="attn_sample",
    )(x, mod, nw, wqkv, qnw, knw, cos, sin, sink, wout, kc, vc)


def _rope_tables(pos):
    half = HEAD_DIM // 2
    inv = ROPE_THETA ** (-jnp.arange(half, dtype=F32) / half)
    ang = pos.astype(F32)[:, None] * inv[None, :]
    cos = jnp.cos(ang)
    sin = jnp.sin(ang)
    return jnp.tile(cos, (1, 4)), jnp.concatenate([-sin, sin, -sin, sin], axis=1)


def _rope_tables_range(n):
    half = HEAD_DIM // 2
    inv = jnp.tile(ROPE_THETA ** (-jnp.arange(half, dtype=F32) / half), LANES // half)
    sign = jnp.where((jnp.arange(LANES) % HEAD_DIM) < half, -1.0, 1.0).astype(F32)
    a_hi = (jnp.arange(n // WINDOW, dtype=jnp.int32) * WINDOW).astype(F32)[:, None] * inv[None, :]
    a_lo = jnp.arange(WINDOW, dtype=jnp.int32).astype(F32)[:, None] * inv[None, :]
    ch, sh = jnp.cos(a_hi)[:, None, :], jnp.sin(a_hi)[:, None, :]
    cl, sl = jnp.cos(a_lo)[None], jnp.sin(a_lo)[None]
    cos = (ch * cl - sh * sl).reshape(n, LANES)
    sin = ((sh * cl + ch * sl) * sign).reshape(n, LANES)
    return cos, sin


def _prep_attn(w_qkv, q_norm, k_norm, sink, w_out):
    n_l = w_qkv.shape[0]
    perm = np.asarray(HEAD_PERM)
    wq = w_qkv[:, :, :Q_W].reshape(n_l, D_MODEL, ATT_HEADS, HEAD_DIM)[:, :, perm].reshape(n_l, D_MODEL, Q_W)
    wqkv = jnp.concatenate([wq, w_qkv[:, :, Q_W:]], axis=2).astype(BF16)
    wout = w_out.reshape(n_l, ATT_HEADS, HEAD_DIM, D_MODEL)[:, perm].reshape(n_l, Q_W, D_MODEL).astype(BF16)
    qnw = jnp.tile(q_norm, (1, 2))[:, None, :]
    knw = jnp.tile(k_norm, (1, 2))[:, None, :]
    idx = np.asarray([[perm[2 * (4 * p + i) + e] for e in range(2) for i in range(4)] for p in range(2)])
    sink_arr = jnp.broadcast_to(sink[:, idx][..., None], (n_l, 2, SUBLANES, LANES)).astype(F32)
    return wqkv, qnw, knw, sink_arr, wout


def kernel(x_prompt, x_sample, c_prompt, c_sample, state_mlstm_C, state_mlstm_n, state_mlstm_m, state_sconv, cache_win_k, cache_win_v, state_ffn_conv, norm1, norm2, w_ada, b_ada, a_w_in, a_b_if, a_out_norm, a_conv_w, a_w_out, c_w_qkv, c_q_norm, c_k_norm, c_sink, c_w_out, f_w_up, f_conv_w, f_w_down):
    B, S, _ = x_prompt.shape
    NS, SL, _ = x_sample.shape
    assert S % TM == 0 and NS % SEQ_BLK == 0 and SEQ_BLK * SL == CHUNK and SL == SUBLANES

    assert B <= SUBLANES
    c_all = jnp.concatenate([c_sample, c_prompt, jnp.zeros((SUBLANES - B, D_MODEL), F32)], axis=0)
    mod = _ada_call(c_all, w_ada, b_ada)

    win_all, wkg_all = _prep_win_call(jnp.swapaxes(a_w_in, 1, 2))
    b_i, b_f = a_b_if[:, :ML_HEADS], a_b_if[:, ML_HEADS:]
    lane_pad = jnp.zeros((a_b_if.shape[0], LANES - 2 * ML_HEADS), F32)
    bifc_all = jnp.stack([jnp.concatenate([b_i, b_i, lane_pad], axis=1),
                          jnp.concatenate([b_f, b_f, lane_pad], axis=1)], axis=1)
    bifr_all = jnp.broadcast_to(jnp.concatenate([b_i, b_i, b_f, b_f], axis=1)[:, :, None],
                                (a_b_if.shape[0], 2 * SUBLANES, LANES))
    wout_a_all = _cast_call(a_w_out)
    wup_all = _cast_call(f_w_up)
    wdn_all = _cast_call(f_w_down)
    onw_all = a_out_norm.reshape(-1, 1, ML_WIDTH)
    norm1_r = norm1.reshape(DEPTH, 1, D_MODEL)
    norm2_r = norm2.reshape(DEPTH, 1, D_MODEL)
    win_buf = cache_win_k.shape[2]
    kc_all = cache_win_k.reshape(-1, NS, win_buf, KV_W)
    vc_all = cache_win_v.reshape(-1, NS, win_buf, KV_W)

    cos_p, sin_p = _rope_tables_range(S)
    cos_s, sin_s = _rope_tables(PAST_LEN + jnp.arange(SL, dtype=jnp.int32))
    cos_s = jnp.tile(cos_s, (SEQ_BLK, 1))
    sin_s = jnp.tile(sin_s, (SEQ_BLK, 1))

    n_tok = jnp.repeat(state_mlstm_n.transpose(0, 2, 1, 3), SL, axis=2)
    n_tok_t = jnp.swapaxes(n_tok, 2, 3)
    m_rep = jnp.repeat(state_mlstm_m, SL, axis=1)
    m_col = jnp.concatenate([m_rep, m_rep, jnp.zeros(m_rep.shape[:2] + (LANES - 2 * ML_HEADS,), F32)], axis=2)
    m_row = jnp.swapaxes(jnp.concatenate([m_rep, m_rep], axis=2), 1, 2)
    sc_pad = jnp.pad(state_sconv, ((0, 0), (0, 0), (0, SL - 2), (0, 0))).reshape(-1, NS * SL, SC_WIDTH)

    wqkv, qnw, knw, sink, wout = _prep_attn(c_w_qkv, c_q_norm, c_k_norm, c_sink, c_w_out)

    xp = x_prompt
    xs = x_sample.reshape(NS * SL, D_MODEL)
    p_C, p_n, p_m, p_sc, p_wk, p_wv, p_ffn = [], [], [], [], [], [], []
    s_n, s_m, s_sc = [], [], []
    s_C = s_ffn = None
    s_wk, s_wv = kc_all, vc_all

    for l in range(DEPTH):
        if l % 2 == 0:
            i = l // 2
            even_w = (norm1_r, win_all, wkg_all, bifc_all, bifr_all, onw_all, a_conv_w, wout_a_all)
            xp, co, no, mo, sco = _mix_even_prompt_call(xp, mod, NS, l, *even_w)
            p_C.append(co)
            p_n.append(no[:, :, :, 0])
            p_m.append(mo[:, :ML_HEADS, 0])
            p_sc.append(sco[:, SUBLANES - 2:, :])

            xs, s_C, no, mo, sco = _mix_even_sample_call(xs, mod, l, *even_w, state_mlstm_C, n_tok,
                                                         n_tok_t, m_col, m_row, sc_pad, s_C)
            s_n.append(no.transpose(1, 0, 2))
            s_m.append(mo[:ML_HEADS, ::SL].T)
            s_sc.append(sco.reshape(NS, SL, SC_WIDTH)[:, :2])
        else:
            xp, ko, vo = _attn_prompt_call(xp, mod, NS, l, norm1_r, wqkv, qnw, knw, cos_p, sin_p, sink, wout)
            p_wk.append(ko.reshape(B, WINDOW, KV_HEADS, HEAD_DIM))
            p_wv.append(vo.reshape(B, WINDOW, KV_HEADS, HEAD_DIM))
            xs, s_wk, s_wv = _attn_sample_call(xs, mod, l, norm1_r, wqkv, qnw, knw, cos_s, sin_s, sink, wout,
                                               s_wk, s_wv)

        xp, st = _ffn_prompt_call(xp, mod, NS, l, norm2_r, wup_all, f_conv_w, wdn_all)
        p_ffn.append(st[:, SUBLANES - 2:, :])
        xs, s_ffn = _ffn_sample_call(xs, mod, l, norm2_r, state_ffn_conv, wup_all, f_conv_w, wdn_all, s_ffn)

    kv_shape = (-1, NS, win_buf, KV_HEADS, HEAD_DIM)
    return (xp, xs.reshape(NS, SL, D_MODEL),
            jnp.stack(p_C), jnp.stack(p_n), jnp.stack(p_m), jnp.stack(p_sc),
            jnp.stack(p_wk), jnp.stack(p_wv), jnp.stack(p_ffn),
            s_C, jnp.stack(s_n), jnp.stack(s_m), jnp.stack(s_sc),
            s_wk.reshape(kv_shape), s_wv.reshape(kv_shape), s_ffn)
```

```python
import jax
import jax.numpy as jnp
import numpy as np
from jax import lax
from jax.experimental import pallas as pl
from jax.experimental.pallas import tpu as pltpu

F32 = jnp.float32
BF16 = jnp.bfloat16

D_MODEL = 1024
DEPTH = 4
PAST_LEN = 8192
ML_HEADS = 4
ML_DK = 128
ML_DV = 128
ML_WIDTH = ML_HEADS * ML_DV
SC_WIDTH = D_MODEL // 2
ATT_HEADS = 16
KV_HEADS = 4
HEAD_DIM = 64
WINDOW = 128
ROPE_THETA = 10000.0
D_FF = 2816
EPS = 1e-6

LANES = 128
SUBLANES = 8
VMEM_LIMIT = 56 * 1024 * 1024

TM = 1024
TM_FFN = 1024
CHUNK = 128
SEQ_BLK = 16
FC = 256
NCH = D_FF // FC
FS_GATE = 2
FC_S = D_FF // FS_GATE
Q_COL = 0
V_COL = Q_COL + ML_WIDTH
O_COL = V_COL + ML_WIDTH
G1_COL = O_COL + ML_WIDTH
G2_COL = G1_COL + LANES
B_COL = G2_COL + LANES
C_COL = B_COL + SC_WIDTH
X_COL = C_COL + SC_WIDTH
IN_W = X_COL + SC_WIDTH
KG_ROWS = ML_WIDTH + 16
SRC_K = ML_WIDTH
SRC_V = 2 * ML_WIDTH
SRC_G = 4 * ML_WIDTH
SRC_B = SRC_G + 2 * ML_HEADS
Q_W = ATT_HEADS * HEAD_DIM
KV_W = KV_HEADS * HEAD_DIM
HEAD_PERM = (0, 4, 1, 5, 2, 6, 3, 7, 8, 12, 9, 13, 10, 14, 11, 15)


def _cparams(n_axes):
    return pltpu.CompilerParams(dimension_semantics=("arbitrary",) * n_axes,
                                vmem_limit_bytes=VMEM_LIMIT)


def _stacked_call(kernel, n_in, out_idx, prev, slab, slab_spec, **kw):
    specs = list(kw.pop("in_specs"))
    out_specs = list(kw.pop("out_specs"))
    n_slabs = kw["out_shape"][out_idx].shape[0]
    if prev is None:
        out_specs[out_idx] = slab_spec(n_slabs, 0)

        def body(*refs):
            refs = list(refs)
            whole = refs[n_in + out_idx]
            for s in range(n_slabs):
                if s != slab:
                    whole[s] = jnp.zeros(whole.shape[1:], whole.dtype)
            refs[n_in + out_idx] = whole.at[slab]
            return kernel(*refs)

        return pl.pallas_call(body, in_specs=specs, out_specs=out_specs, **kw)

    out_specs[out_idx] = slab_spec(None, slab)

    def body(*refs):
        return kernel(*refs[:n_in], *refs[n_in + 1:])

    call = pl.pallas_call(body, in_specs=specs + [pl.BlockSpec(memory_space=pl.ANY)], out_specs=out_specs,
                          input_output_aliases={n_in: out_idx}, **kw)
    return lambda *args: call(*args, prev)


def _dot(a, b):
    return jnp.dot(a, b, preferred_element_type=F32)


def _dot_nt(a, b):
    return lax.dot_general(a, b, (((1,), (1,)), ((), ())), preferred_element_type=F32)


def _dot_exact01(m, a):
    a1 = a.astype(BF16)
    r1 = a - a1.astype(F32)
    a2 = r1.astype(BF16)
    a3 = (r1 - a2.astype(F32)).astype(BF16)
    return _dot(m, a1) + _dot(m, a2) + _dot(m, a3)


def _norm_mod(x, nw, sc, sh):
    ms = jnp.mean(x * x, axis=-1, keepdims=True)
    return (x * lax.rsqrt(ms + EPS) * nw) * (1.0 + sc) + sh


def _sigmoid(x):
    return 1.0 / (1.0 + jnp.exp(-x))


def _log_sigmoid(x):
    return jnp.minimum(x, 0.0) - jnp.log(1.0 + jnp.exp(-jnp.abs(x)))


N_MOD = 6


ADA_KINDS = 3


ADA_BUFS = 3


def _ada_kernel(c_ref, b_ref, w_hbm, o_ref, wbuf, sem):
    c = c_ref[...]
    s = (c * _sigmoid(c)).astype(BF16)
    n = DEPTH * N_MOD

    def copy(i):
        l, k = divmod(i, N_MOD)
        return pltpu.make_async_copy(w_hbm.at[l, :, pl.ds(k * D_MODEL, D_MODEL)],
                                     wbuf.at[i % ADA_BUFS], sem.at[i % ADA_BUFS])

    for i in range(ADA_BUFS):
        copy(i).start()
    for i in range(n):
        l, k = divmod(i, N_MOD)
        copy(i).wait()
        o_ref[l, k] = (_dot(s, wbuf[i % ADA_BUFS].astype(BF16))
                       + b_ref[l, :, k * D_MODEL:(k + 1) * D_MODEL])
        if i + ADA_BUFS < n:
            copy(i + ADA_BUFS).start()


def _ada_call(c_all, w_ada, b_ada):
    rows = c_all.shape[0]
    return pl.pallas_call(
        _ada_kernel,
        in_specs=[
            pl.BlockSpec(memory_space=pltpu.VMEM),
            pl.BlockSpec(memory_space=pltpu.VMEM),
            pl.BlockSpec(memory_space=pl.ANY),
        ],
        out_specs=pl.BlockSpec(memory_space=pltpu.VMEM),
        out_shape=jax.ShapeDtypeStruct((DEPTH, N_MOD, rows, D_MODEL), F32),
        scratch_shapes=[pltpu.VMEM((ADA_BUFS, D_MODEL, D_MODEL), F32), pltpu.SemaphoreType.DMA((ADA_BUFS,))],
        compiler_params=pltpu.CompilerParams(vmem_limit_bytes=VMEM_LIMIT),
        name="adaln_mod",
    )(c_all, b_ada.reshape(DEPTH, 1, N_MOD * D_MODEL), w_ada)


CAST_ROWS = 256


def _cast_kernel(w_ref, o_ref):
    o_ref[...] = w_ref[...].astype(BF16)


CAST_BLOCK_BYTES = 6 * 1024 * 1024


def _cast_call(w):
    n_l, rows, cols = w.shape
    fits = [r for r in range(2 * SUBLANES, rows + 1, 2 * SUBLANES)
            if rows % r == 0 and r * cols * 4 <= CAST_BLOCK_BYTES]
    tr = max(fits) if fits else rows
    return pl.pallas_call(
        _cast_kernel,
        grid=(n_l, rows // tr),
        in_specs=[pl.BlockSpec((None, tr, cols), lambda l, r: (l, r, 0))],
        out_specs=pl.BlockSpec((None, tr, cols), lambda l, r: (l, r, 0)),
        out_shape=jax.ShapeDtypeStruct(w.shape, BF16),
        compiler_params=_cparams(2),
        name="cast_bf16",
    )(w)


def _prep_win_kernel(wt_ref, o_ref, kg_ref):
    def put(dst_col, src_row, n):
        for c in range(0, n, LANES):
            o_ref[:, dst_col + c:dst_col + c + LANES] = wt_ref[src_row + c:src_row + c + LANES, :].T.astype(BF16)

    put(Q_COL, 0, ML_WIDTH)
    put(V_COL, SRC_V, 2 * ML_WIDTH)
    put(B_COL, SRC_B, 3 * SC_WIDTH)
    kg_ref[0:ML_WIDTH, :] = wt_ref[SRC_K:SRC_V, :].astype(BF16)

    g8 = wt_ref[SRC_G:SRC_G + SUBLANES, :]
    g16 = jnp.concatenate([g8, g8], axis=0)
    row = lax.broadcasted_iota(jnp.int32, g16.shape, 0)
    mid = (row >= ML_HEADS) & (row < 3 * ML_HEADS)
    kg_ref[ML_WIDTH:KG_ROWS, :] = jnp.where(mid, pltpu.roll(g16, ML_HEADS, 0), g16).astype(BF16)

    gt = wt_ref[SRC_G:SRC_G + LANES, :].T
    ig = gt[:, 0:ML_HEADS]
    fg = gt[:, ML_HEADS:2 * ML_HEADS]
    pad = jnp.zeros((gt.shape[0], LANES - 2 * ML_HEADS), F32)
    o_ref[:, G1_COL:G2_COL] = jnp.concatenate([ig, ig, pad], axis=1).astype(BF16)
    o_ref[:, G2_COL:B_COL] = jnp.concatenate([fg, fg, pad], axis=1).astype(BF16)


def _prep_win_call(a_w_in_t):
    n_l, in_a, _ = a_w_in_t.shape
    return pl.pallas_call(
        _prep_win_kernel,
        grid=(n_l,),
        in_specs=[pl.BlockSpec((None, in_a, D_MODEL), lambda l: (l, 0, 0))],
        out_specs=[pl.BlockSpec((None, D_MODEL, IN_W), lambda l: (l, 0, 0)),
                   pl.BlockSpec((None, KG_ROWS, D_MODEL), lambda l: (l, 0, 0))],
        out_shape=[jax.ShapeDtypeStruct((n_l, D_MODEL, IN_W), BF16),
                   jax.ShapeDtypeStruct((n_l, KG_ROWS, D_MODEL), BF16)],
        compiler_params=_cparams(1),
        name="prep_w_in",
    )(a_w_in_t)


def _mod_row(mod_ref, kind, b):
    return mod_ref[kind, pl.ds(b, 1), :]


def _ffn_prompt_kernel(x_ref, mod_ref, nw_ref, wup_ref, cw_ref, wdn_ref,
                       xo_ref, st_ref, h_scr, act_scr, carry_scr):
    b = pl.program_id(0)
    t = pl.program_id(1)

    @pl.when(t == 0)
    def _():
        carry_scr[...] = jnp.zeros_like(carry_scr)

    x = x_ref[0]
    h_scr[...] = _norm_mod(x, nw_ref[...], _mod_row(mod_ref, 4, b), _mod_row(mod_ref, 3, b)).astype(BF16)
    for j in range(NCH):
        ys = []
        for col in (j * FC, D_FF + j * FC):
            cols = slice(col, col + FC)
            u = _dot(h_scr[...], wup_ref[:, cols])
            ys.append(_conv3_rows(u, carry_scr[:, cols], cw_ref[:, cols]))
            carry_scr[:, cols] = u[TM_FFN - SUBLANES:TM_FFN]
        g = ys[0]
        act_scr[:, j * FC:(j + 1) * FC] = (g * _sigmoid(g) * ys[1]).astype(BF16)
    y = _dot(act_scr[...], wdn_ref[...])
    xo_ref[0] = x + _mod_row(mod_ref, 5, b) * y
    st_ref[0] = carry_scr[...]


def _ffn_prompt_call(x, mod, n_seq_rows, l, nw, wup, cw, wdn):
    B, S, _ = x.shape
    assert S % TM_FFN == 0
    nt = S // TM_FFN
    once = pl.Buffered(1)
    return pl.pallas_call(
        _ffn_prompt_kernel,
        grid=(B, nt),
        in_specs=[
            pl.BlockSpec((1, TM_FFN, D_MODEL), lambda b, t: (b, t, 0)),
            pl.BlockSpec((None, N_MOD, SUBLANES, D_MODEL), lambda b, t: (l, 0, n_seq_rows // SUBLANES, 0)),
            pl.BlockSpec((None, 1, D_MODEL), lambda b, t: (l, 0, 0)),
            pl.BlockSpec((None, D_MODEL, 2 * D_FF), lambda b, t: (l, 0, 0), pipeline_mode=once),
            pl.BlockSpec((None, 3, 2 * D_FF), lambda b, t: (l, 0, 0)),
            pl.BlockSpec((None, D_FF, D_MODEL), lambda b, t: (l, 0, 0), pipeline_mode=once),
        ],
        out_specs=[
            pl.BlockSpec((1, TM_FFN, D_MODEL), lambda b, t: (b, t, 0)),
            pl.BlockSpec((1, SUBLANES, 2 * D_FF), lambda b, t: (b, 0, 0)),
        ],
        out_shape=[
            jax.ShapeDtypeStruct((B, S, D_MODEL), F32),
            jax.ShapeDtypeStruct((B, SUBLANES, 2 * D_FF), F32),
        ],
        scratch_shapes=[
            pltpu.VMEM((TM_FFN, D_MODEL), BF16),
            pltpu.VMEM((TM_FFN, D_FF), BF16),
            pltpu.VMEM((SUBLANES, 2 * D_FF), F32),
        ],
        compiler_params=_cparams(2),
        name="ffn_prompt",
    )(x, mod, nw, wup, cw, wdn)


def _ffn_sample_kernel(x_ref, mod_ref, nw_ref, s_ref, w_ref, c_ref, wdn_ref,
                       xo_ref, so_ref, h_scr, hb_scr, g2_scr, acc_scr, yg_scr):
    j = pl.program_id(1)
    n_seq = s_ref.shape[0]
    n_rows = x_ref.shape[0]
    n_t = n_rows // n_seq

    @pl.when(j == 0)
    def _():
        def modulate(b, carry):
            rows = pl.ds(pl.multiple_of(b * n_t, n_t), n_t)
            h_scr[rows, :] = _norm_mod(x_ref[rows, :], nw_ref[...], _mod_row(mod_ref, 4, b), _mod_row(mod_ref, 3, b))
            g2_scr[rows, :] = jnp.broadcast_to(_mod_row(mod_ref, 5, b), (n_t, D_MODEL))
            return carry

        lax.fori_loop(0, n_seq, modulate, 0, unroll=8)
        hb_scr[...] = h_scr[...].astype(BF16)
        acc_scr[...] = jnp.zeros_like(acc_scr)

    sub = lax.broadcasted_iota(jnp.int32, (n_seq, n_t, FC_S), 1)
    u3 = _dot(hb_scr[...], w_ref[...]).reshape(n_seq, n_t, FC_S)
    cw = c_ref[...]
    p0 = jnp.broadcast_to(s_ref[:, 0:1, :], (n_seq, n_t, FC_S))
    p1 = jnp.broadcast_to(s_ref[:, 1:2, :], (n_seq, n_t, FC_S))
    s1 = jnp.where(sub < 1, p1, pltpu.roll(u3, 1, 1))
    s2 = jnp.where(sub < 1, p0, jnp.where(sub < 2, p1, pltpu.roll(u3, 2, 1)))
    y = (s2 * cw[0:1] + s1 * cw[1:2] + u3 * cw[2:3]).reshape(n_rows, FC_S)
    so_ref[...] = pltpu.roll(u3, 2, 1)[:, 0:2, :]

    @pl.when(j < FS_GATE)
    def _():
        yg_scr[j] = y

    @pl.when(j >= FS_GATE)
    def _():
        g = yg_scr[j - FS_GATE]
        acc_scr[...] += _dot((g * _sigmoid(g) * y).astype(BF16), wdn_ref[...])

    @pl.when(j == 2 * FS_GATE - 1)
    def _():
        xo_ref[...] = x_ref[...] + g2_scr[...] * acc_scr[...]


def _ffn_sample_call(x, mod, l, nw, st, wup, cw, wdn, so_prev):
    n_seq = st.shape[1] // 2
    n_rows = x.shape[0] // 2
    return _stacked_call(
        _ffn_sample_kernel, 7, 1, so_prev, l,
        lambda lead, idx: pl.BlockSpec((lead, n_seq, 2, FC_S), lambda hf, j: (idx, hf, 0, j)),
        grid=(2, 2 * FS_GATE),
        in_specs=[
            pl.BlockSpec((n_rows, D_MODEL), lambda hf, j: (hf, 0)),
            pl.BlockSpec((None, N_MOD, n_seq, D_MODEL), lambda hf, j: (l, 0, hf, 0)),
            pl.BlockSpec((None, 1, D_MODEL), lambda hf, j: (l, 0, 0)),
            pl.BlockSpec((None, n_seq, 2, FC_S), lambda hf, j: (l, hf, 0, j)),
            pl.BlockSpec((None, D_MODEL, FC_S), lambda hf, j: (l, 0, j)),
            pl.BlockSpec((None, 3, FC_S), lambda hf, j: (l, 0, j)),
            pl.BlockSpec((None, FC_S, D_MODEL), lambda hf, j: (l, jnp.maximum(j - FS_GATE, 0), 0)),
        ],
        out_specs=[
            pl.BlockSpec((n_rows, D_MODEL), lambda hf, j: (hf, 0)),
            None,
        ],
        out_shape=[
            jax.ShapeDtypeStruct(x.shape, F32),
            jax.ShapeDtypeStruct((DEPTH, st.shape[1], 2, 2 * D_FF), F32),
        ],
        scratch_shapes=[
            pltpu.VMEM((n_rows, D_MODEL), F32),
            pltpu.VMEM((n_rows, D_MODEL), BF16),
            pltpu.VMEM((n_rows, D_MODEL), F32),
            pltpu.VMEM((n_rows, D_MODEL), F32),
            pltpu.VMEM((FS_GATE, n_rows, FC_S), F32),
        ],
        compiler_params=_cparams(2),
        name="ffn_sample",
    )(x, mod, nw, st, wup, cw, wdn)


def _chunk_consts(seq_len):
    r = lax.broadcasted_iota(jnp.int32, (CHUNK, CHUNK), 0)
    c = lax.broadcasted_iota(jnp.int32, (CHUNK, CHUNK), 1)
    if seq_len >= CHUNK:
        same = r >= 0
    else:
        same = (r // seq_len) == (c // seq_len)
    mask = same & (c <= r)
    lmat = mask.astype(BF16)
    lmat_t = (same & (r <= c)).astype(BF16)
    return mask, lmat, lmat_t, same.astype(BF16)


def _dot_exact01_r(a, m):
    a1 = a.astype(BF16)
    r1 = a - a1.astype(F32)
    a2 = r1.astype(BF16)
    a3 = (r1 - a2.astype(F32)).astype(BF16)
    return _dot(a1, m) + _dot(a2, m) + _dot(a3, m)


def _pick_cols_t(a, sel):
    a1 = a.astype(BF16)
    r1 = a - a1.astype(F32)
    a2 = r1.astype(BF16)
    a3 = (r1 - a2.astype(F32)).astype(BF16)
    return _dot_nt(sel, a1) + _dot_nt(sel, a2) + _dot_nt(sel, a3)


def _seq_max_lanes(x, seq_len):
    n = x.shape[1]
    pos = lax.broadcasted_iota(jnp.int32, x.shape, 1)
    d = 1
    while d < seq_len:
        partner = jnp.where((pos & d) == 0, pltpu.roll(x, n - d, 1), pltpu.roll(x, d, 1))
        x = jnp.maximum(x, partner)
        d *= 2
    return x


def _seq_prefix_max_rows(x, seq_len):
    pos = lax.broadcasted_iota(jnp.int32, x.shape, 0) & (seq_len - 1)
    d = 1
    while d < seq_len:
        x = jnp.where(pos >= d, jnp.maximum(x, pltpu.roll(x, d, 0)), x)
        d *= 2
    return x


def _seq_last_row(x, seq_len):
    n, w = x.shape
    if seq_len >= n:
        return jnp.broadcast_to(x[n - 1:n], x.shape)
    x3 = x.reshape(n // seq_len, seq_len, w)
    return jnp.broadcast_to(x3[:, seq_len - 1:seq_len, :], x3.shape).reshape(n, w)


def _gates_rows_pre(gt, bias_r, lmat_t, tot, seq_len):
    ig = gt[0:SUBLANES] + bias_r[0:SUBLANES]
    lf = _log_sigmoid(gt[SUBLANES:] + bias_r[SUBLANES:])
    b = _dot_exact01_r(lf, lmat_t)
    bl = _dot_exact01_r(lf, tot)
    v = ig - b
    return bl, v, _seq_max_lanes(v, seq_len)


def _gates_rows_post(pre, mp_r):
    bl, v, vm = pre
    mn = bl + jnp.maximum(mp_r, vm)
    return jnp.exp(bl + v - mn), jnp.exp(bl + mp_r - mn), mn


def _gates_rows(gt, bias_r, mp_r, lmat_t, tot, seq_len):
    pre = _gates_rows_pre(gt, bias_r, lmat_t, tot, seq_len)
    return (pre[1],) + _gates_rows_post(pre, mp_r)


def _gates_cols_pre(g1, g2, bias_c, lmat, seq_len):
    lane = lax.broadcasted_iota(jnp.int32, g1.shape, 1)
    ig = g1 + bias_c[0:1]
    lf = jnp.where(lane < 2 * ML_HEADS, _log_sigmoid(g2 + bias_c[1:2]), 0.0)
    b = _dot_exact01(lmat, lf)
    return b, _seq_prefix_max_rows(ig - b, seq_len)


def _gates_cols_post(b, cm, mp_c):
    g = b + mp_c
    mt = jnp.maximum(b + cm, g)
    return b - mt, jnp.exp(g - mt), jnp.exp(-mt)


def _gates_cols(g1, g2, bias_c, mp_c, lmat, seq_len):
    b, cm = _gates_cols_pre(g1, g2, bias_c, lmat, seq_len)
    return (b, cm) + _gates_cols_post(b, cm, mp_c)


def _outer_sum_lhs(u):
    lane = lax.broadcasted_iota(jnp.int32, u.shape, 1)
    hi = u.astype(BF16).astype(F32)
    lo = u - hi
    return jnp.where(lane < ML_HEADS, hi, jnp.where(lane < 2 * ML_HEADS, lo,
                     jnp.where(lane < 4 * ML_HEADS, 1.0, 0.0))).astype(BF16)


def _outer_sum_rhs(v_r, hd):
    row = lax.broadcasted_iota(jnp.int32, v_r.shape, 0)
    hi = v_r.astype(BF16).astype(F32)
    lo = v_r - hi
    pick = (row == hd) | (row == ML_HEADS + hd)
    top = jnp.where(pick, 1.0, 0.0)
    bot = jnp.where(row == hd, hi, jnp.where(row == ML_HEADS + hd, lo, 0.0))
    r16 = jnp.concatenate([top, bot], axis=0).astype(BF16)
    return jnp.concatenate([r16, jnp.zeros((LANES - 2 * SUBLANES, v_r.shape[1]), BF16)], axis=0)


def _decayed_scores(q, kt, lhsc, v_r, hd, mask):
    e = _dot(lhsc, _outer_sum_rhs(v_r, hd))
    return _dot(q.astype(BF16), kt.astype(BF16)) * jnp.where(mask, jnp.exp(e), 0.0)


def _conv3_rows(cx, prev8, cw):
    n = cx.shape[1]
    row = lax.broadcasted_iota(jnp.int32, (SUBLANES, n), 0)
    s1 = pltpu.roll(cx, 1, 0)
    s2 = pltpu.roll(cx, 2, 0)
    f1 = jnp.where(row < 1, pltpu.roll(prev8, 1, 0), s1[0:SUBLANES])
    f2 = jnp.where(row < 2, pltpu.roll(prev8, 2, 0), s2[0:SUBLANES])
    s1 = jnp.concatenate([f1, s1[SUBLANES:]], axis=0)
    s2 = jnp.concatenate([f2, s2[SUBLANES:]], axis=0)
    return s2 * cw[0:1] + s1 * cw[1:2] + cx * cw[2:3]


def _mlstm_out_norm(hm, zo, onw):
    ms = jnp.mean(hm * hm, axis=-1, keepdims=True)
    return hm * lax.rsqrt(ms + EPS) * onw * _sigmoid(zo)


def _even_tail(z_scr, hm_scr, cat_scr, onw_ref):
    for hd in range(ML_HEADS):
        col = slice(hd * ML_DV, (hd + 1) * ML_DV)
        zo = z_scr[:, O_COL + hd * ML_DV:O_COL + (hd + 1) * ML_DV]
        cat_scr[:, col] = _mlstm_out_norm(hm_scr[:, col], zo, onw_ref[:, col]).astype(BF16)


def _mix_even_prompt_kernel(x_ref, mod_ref, nw_ref, win_ref, wkg_ref, bifc_ref, bifr_ref, onw_ref, cw_ref,
                            wout_ref, xo_ref, co_ref, no_ref, mo_ref, sco_ref,
                            z_scr, zt_scr, hm_scr, cat_scr, cn_scr, mrow_scr, mlane_scr, cc_scr):
    t = pl.program_id(1)

    @pl.when(t == 0)
    def _():
        cn_scr[...] = jnp.zeros_like(cn_scr)
        mrow_scr[...] = jnp.zeros_like(mrow_scr)
        mlane_scr[...] = jnp.zeros_like(mlane_scr)
        cc_scr[...] = jnp.zeros_like(cc_scr)

    x = x_ref[0]
    bi = pl.program_id(0)
    h = _norm_mod(x, nw_ref[...], _mod_row(mod_ref, 1, bi), _mod_row(mod_ref, 0, bi)).astype(BF16)
    z_scr[...] = _dot(h, win_ref[...])
    zt_scr[...] = _dot_nt(wkg_ref[...], h)

    mask, lmat, lmat_t, tot = _chunk_consts(CHUNK)
    scale = ML_DK ** -0.5
    ones_v = jnp.ones((CHUNK, ML_DV), BF16)

    n_ch = TM // CHUNK
    chunk_rows = [slice(c * CHUNK, (c + 1) * CHUNK) for c in range(n_ch)]
    pre_r = [_gates_rows_pre(zt_scr[ML_WIDTH:KG_ROWS, rows], bifr_ref[...], lmat_t, tot, CHUNK)
             for rows in chunk_rows]
    pre_c = [_gates_cols_pre(z_scr[rows, G1_COL:G1_COL + LANES], z_scr[rows, G2_COL:G2_COL + LANES],
                             bifc_ref[...], lmat, CHUNK) for rows in chunk_rows]
    mp_r = mrow_scr[...]
    mp_c = mlane_scr[0:1, :]
    post_r, post_c = [], []
    for c in range(n_ch):
        post_r.append(_gates_rows_post(pre_r[c], mp_r))
        mp_r = post_r[c][2]
        b, cm = pre_c[c]
        post_c.append(_gates_cols_post(b, cm, mp_c))
        mp_c = b[CHUNK - 1:CHUNK] + jnp.maximum(mp_c, cm[CHUNK - 1:CHUNK])
    mrow_scr[...] = mp_r
    mlane_scr[...] = jnp.broadcast_to(mp_c, (SUBLANES, LANES))

    intra, upd, wgq = {}, {}, {}
    for c, rows in enumerate(chunk_rows):
        u, wg, _ = post_c[c]
        lhsc = _outer_sum_lhs(u)
        for hd in range(ML_HEADS):
            q = z_scr[rows, Q_COL + hd * ML_DK:Q_COL + (hd + 1) * ML_DK]
            v = z_scr[rows, V_COL + hd * ML_DV:V_COL + (hd + 1) * ML_DV]
            kt = zt_scr[hd * ML_DK:(hd + 1) * ML_DK, rows] * scale
            s = _decayed_scores(q, kt, lhsc, pre_r[c][1], hd, mask)
            v1 = jnp.concatenate([v.astype(BF16), ones_v], axis=1)
            intra[c, hd] = _dot(s.astype(BF16), v1)
            upd[c, hd] = _dot((kt * post_r[c][0][hd:hd + 1, :]).astype(BF16), v1)
            wgq[c, hd] = (q * wg[:, hd:hd + 1]).astype(BF16)

    for hd in range(ML_HEADS):
        cn = cn_scr[hd]
        for c, rows in enumerate(chunk_rows):
            out = intra[c, hd] + _dot(wgq[c, hd], cn.astype(BF16))
            r = 1.0 / jnp.maximum(jnp.abs(out[:, ML_DV:]), post_c[c][2][:, hd:hd + 1])
            hm_scr[rows, hd * ML_DV:(hd + 1) * ML_DV] = out[:, :ML_DV] * r
            wc = post_r[c][1][hd:hd + 1, :]
            cn = jnp.concatenate([wc, wc], axis=1) * cn + upd[c, hd]
        cn_scr[hd] = cn

    _even_tail(z_scr, hm_scr, cat_scr, onw_ref)
    cx = z_scr[:, C_COL:C_COL + SC_WIDTH] * z_scr[:, X_COL:X_COL + SC_WIDTH]
    u = _conv3_rows(cx, cc_scr[...], cw_ref[...])
    cc_scr[...] = cx[TM - SUBLANES:TM]
    cat_scr[:, ML_WIDTH:] = (z_scr[:, B_COL:B_COL + SC_WIDTH] * u).astype(BF16)

    y = _dot(cat_scr[...], wout_ref[...])
    xo_ref[0] = x + _mod_row(mod_ref, 2, bi) * y
    for hd in range(ML_HEADS):
        co_ref[0, hd] = cn_scr[hd, :, 0:ML_DV]
        no_ref[0, hd] = cn_scr[hd, :, ML_DV:]
    mo_ref[0] = mrow_scr[...]
    sco_ref[0] = cc_scr[...]


def _even_weight_specs(i, idx):
    once = pl.Buffered(1)
    return [
        pl.BlockSpec((None, D_MODEL, IN_W), idx, pipeline_mode=once),
        pl.BlockSpec((None, KG_ROWS, D_MODEL), idx, pipeline_mode=once),
        pl.BlockSpec((None, 2, LANES), idx),
        pl.BlockSpec((None, 2 * SUBLANES, LANES), idx),
        pl.BlockSpec((None, 1, ML_WIDTH), idx),
        pl.BlockSpec((None, 3, SC_WIDTH), idx),
        pl.BlockSpec((None, ML_WIDTH + SC_WIDTH, D_MODEL), idx, pipeline_mode=once),
    ]


def _mix_even_prompt_call(x, mod, n_seq_rows, l, nw, win, wkg, bifc, bifr, onw, cw, wout):
    B, S, _ = x.shape
    nt = S // TM
    i = l // 2
    return pl.pallas_call(
        _mix_even_prompt_kernel,
        grid=(B, nt),
        in_specs=[
            pl.BlockSpec((1, TM, D_MODEL), lambda b, t: (b, t, 0)),
            pl.BlockSpec((None, N_MOD, SUBLANES, D_MODEL), lambda b, t: (l, 0, n_seq_rows // SUBLANES, 0)),
            pl.BlockSpec((None, 1, D_MODEL), lambda b, t: (l, 0, 0)),
        ] + _even_weight_specs(i, lambda b, t: (i, 0, 0)),
        out_specs=[
            pl.BlockSpec((1, TM, D_MODEL), lambda b, t: (b, t, 0)),
            pl.BlockSpec((1, ML_HEADS, ML_DK, ML_DV), lambda b, t: (b, 0, 0, 0)),
            pl.BlockSpec((1, ML_HEADS, ML_DK, LANES), lambda b, t: (b, 0, 0, 0)),
            pl.BlockSpec((1, SUBLANES, LANES), lambda b, t: (b, 0, 0)),
            pl.BlockSpec((1, SUBLANES, SC_WIDTH), lambda b, t: (b, 0, 0)),
        ],
        out_shape=[
            jax.ShapeDtypeStruct((B, S, D_MODEL), F32),
            jax.ShapeDtypeStruct((B, ML_HEADS, ML_DK, ML_DV), F32),
            jax.ShapeDtypeStruct((B, ML_HEADS, ML_DK, LANES), F32),
            jax.ShapeDtypeStruct((B, SUBLANES, LANES), F32),
            jax.ShapeDtypeStruct((B, SUBLANES, SC_WIDTH), F32),
        ],
        scratch_shapes=[
            pltpu.VMEM((TM, IN_W), F32),
            pltpu.VMEM((KG_ROWS, TM), F32),
            pltpu.VMEM((TM, ML_WIDTH), F32),
            pltpu.VMEM((TM, ML_WIDTH + SC_WIDTH), BF16),
            pltpu.VMEM((ML_HEADS, ML_DK, ML_DV + LANES), F32),
            pltpu.VMEM((SUBLANES, LANES), F32),
            pltpu.VMEM((SUBLANES, LANES), F32),
            pltpu.VMEM((SUBLANES, SC_WIDTH), F32),
        ],
        compiler_params=_cparams(2),
        name="mix_even_prompt",
    )(x, mod, nw, win, wkg, bifc, bifr, onw, cw, wout)


def _mix_even_sample_kernel(x_ref, mod_ref, nw_ref, win_ref, wkg_ref, bifc_ref, bifr_ref, onw_ref, cw_ref,
                            wout_ref, c_ref, nt_ref, ntt_ref, mcol_ref, mrow_ref, sc_ref,
                            xo_ref, co_ref, no_ref, mo_ref, sco_ref,
                            h_scr, z_scr, zt_scr, hm_scr, cat_scr, g1_scr,
                            intra_scr, dpart_scr, pbe_scr, wgq_scr, kwt_scr, vb_scr, wcb_scr, inter_scr):
    n_tok = x_ref.shape[0]
    seq_len = n_tok // SEQ_BLK

    def modulate(b, carry):
        rows = pl.ds(pl.multiple_of(b * seq_len, seq_len), seq_len)
        h_scr[rows, :] = _norm_mod(x_ref[rows, :], nw_ref[...], _mod_row(mod_ref, 1, b), _mod_row(mod_ref, 0, b))
        g1_scr[rows, :] = jnp.broadcast_to(_mod_row(mod_ref, 2, b), (seq_len, D_MODEL))
        return carry

    lax.fori_loop(0, SEQ_BLK, modulate, 0, unroll=8)
    hb = h_scr[...].astype(BF16)
    z_scr[...] = _dot(hb, win_ref[...])
    zt_scr[...] = _dot_nt(wkg_ref[...], hb)

    mask, lmat, lmat_t, tot = _chunk_consts(seq_len)
    scale = ML_DK ** -0.5
    ones_v = jnp.ones((CHUNK, ML_DV), BF16)
    mp_r = mrow_ref[...]
    v_r, ws_r, wc_r, mn_r = _gates_rows(zt_scr[ML_WIDTH:KG_ROWS, :], bifr_ref[...], mp_r, lmat_t, tot, seq_len)
    mo_ref[...] = mn_r
    mp_c = mcol_ref[...]
    b, cm, u, wg, pbe = _gates_cols(z_scr[:, G1_COL:G1_COL + LANES], z_scr[:, G2_COL:G2_COL + LANES],
                                    bifc_ref[...], mp_c, lmat, seq_len)
    wc_c = jnp.exp(mp_c - jnp.maximum(mp_c, _seq_last_row(cm, seq_len)))
    lhsc = _outer_sum_lhs(u)
    first_tok = (lax.broadcasted_iota(jnp.int32, (SEQ_BLK, n_tok), 0) * seq_len
                 == lax.broadcasted_iota(jnp.int32, (SEQ_BLK, n_tok), 1)).astype(BF16)

    for hd in range(ML_HEADS):
        q = z_scr[:, Q_COL + hd * ML_DK:Q_COL + (hd + 1) * ML_DK]
        v = z_scr[:, V_COL + hd * ML_DV:V_COL + (hd + 1) * ML_DV]
        kt = zt_scr[hd * ML_DK:(hd + 1) * ML_DK, :] * scale
        s = _decayed_scores(q, kt, lhsc, v_r, hd, mask)
        vb = v.astype(BF16)
        out = _dot(s.astype(BF16), jnp.concatenate([vb, ones_v], axis=1))
        wg_h = wg[:, hd:hd + 1]
        qn = jnp.sum(q * nt_ref[hd], axis=-1, keepdims=True)
        intra_scr[hd] = out[:, :ML_DV]
        dpart_scr[hd] = out[:, ML_DV:] + wg_h * qn
        pbe_scr[hd] = jnp.broadcast_to(pbe[:, hd:hd + 1], (n_tok, LANES))
        wgq_scr[hd] = q * wg_h
        kwt = kt * ws_r[hd:hd + 1, :]
        kwt_scr[hd] = kwt
        vb_scr[hd] = vb
        wcb_scr[hd] = jnp.broadcast_to(wc_c[:, hd:hd + 1], (n_tok, LANES))
        n_new_t = wc_r[hd:hd + 1, :] * ntt_ref[hd] + _dot_exact01_r(kwt, tot)
        no_ref[hd] = _pick_cols_t(n_new_t, first_tok)

    lane_i = lax.broadcasted_iota(jnp.int32, (ML_DK, n_tok), 1)

    def per_seq(bq, carry):
        r0 = pl.multiple_of(bq * seq_len, seq_len)
        rows = pl.ds(r0, seq_len)
        sel = (lane_i >= r0) & (lane_i < r0 + seq_len)
        for hd in range(ML_HEADS):
            c_prev = c_ref[bq, hd]
            inter_scr[hd, rows, :] = _dot(wgq_scr[hd, rows, :].astype(BF16), c_prev.astype(BF16))
            kw_b = jnp.where(sel, kwt_scr[hd], 0.0).astype(BF16)
            co_ref[bq, hd] = wcb_scr[hd, pl.ds(r0, 1), :] * c_prev + _dot(kw_b, vb_scr[hd])
        return carry

    lax.fori_loop(0, SEQ_BLK, per_seq, 0, unroll=8)

    for hd in range(ML_HEADS):
        num = inter_scr[hd] + intra_scr[hd]
        hm_scr[:, hd * ML_DV:(hd + 1) * ML_DV] = num * (
            1.0 / jnp.maximum(jnp.abs(dpart_scr[hd]), pbe_scr[hd]))

    _even_tail(z_scr, hm_scr, cat_scr, onw_ref)
    cx = z_scr[:, C_COL:C_COL + SC_WIDTH] * z_scr[:, X_COL:X_COL + SC_WIDTH]
    sub = lax.broadcasted_iota(jnp.int32, (n_tok, SC_WIDTH), 0) % seq_len
    p1 = sc_ref[...]
    s1 = jnp.where(sub < 1, pltpu.roll(p1, n_tok - 1, 0), pltpu.roll(cx, 1, 0))
    s2 = jnp.where(sub < 2, p1, pltpu.roll(cx, 2, 0))
    cw = cw_ref[...]
    u = s2 * cw[0:1] + s1 * cw[1:2] + cx * cw[2:3]
    sco_ref[...] = pltpu.roll(cx, n_tok - (seq_len - 2), 0)
    cat_scr[:, ML_WIDTH:] = (z_scr[:, B_COL:B_COL + SC_WIDTH] * u).astype(BF16)

    y = _dot(cat_scr[...], wout_ref[...])
    xo_ref[...] = x_ref[...] + g1_scr[...] * y


def _mix_even_sample_call(x, mod, l, nw, win, wkg, bifc, bifr, onw, cw, wout, c0, n_tok, n_tok_t, m_col, m_row,
                          sc_pad, co_prev):
    n_rows = x.shape[0]
    n_seq = c0.shape[1]
    seq_len = n_rows // n_seq
    blk = SEQ_BLK * seq_len
    nb = n_seq // SEQ_BLK
    li = l // 2
    head_blk = (ML_HEADS, blk, LANES)
    return _stacked_call(
        _mix_even_sample_kernel, 16, 1, co_prev, li,
        lambda lead, idx: pl.BlockSpec((lead, SEQ_BLK, ML_HEADS, ML_DK, ML_DV), lambda i: (idx, i, 0, 0, 0)),
        grid=(nb,),
        in_specs=[
            pl.BlockSpec((blk, D_MODEL), lambda i: (i, 0)),
            pl.BlockSpec((None, N_MOD, SEQ_BLK, D_MODEL), lambda i: (l, 0, i, 0)),
            pl.BlockSpec((None, 1, D_MODEL), lambda i: (l, 0, 0)),
        ] + _even_weight_specs(li, lambda i: (li, 0, 0)) + [
            pl.BlockSpec((None, SEQ_BLK, ML_HEADS, ML_DK, ML_DV), lambda i: (li, i, 0, 0, 0)),
            pl.BlockSpec((None, ML_HEADS, blk, ML_DK), lambda i: (li, 0, i, 0)),
            pl.BlockSpec((None, ML_HEADS, ML_DK, blk), lambda i: (li, 0, 0, i)),
            pl.BlockSpec((None, blk, LANES), lambda i: (li, i, 0)),
            pl.BlockSpec((None, SUBLANES, blk), lambda i: (li, 0, i)),
            pl.BlockSpec((None, blk, SC_WIDTH), lambda i: (li, i, 0)),
        ],
        out_specs=[
            pl.BlockSpec((blk, D_MODEL), lambda i: (i, 0)),
            None,
            pl.BlockSpec((ML_HEADS, SEQ_BLK, ML_DK), lambda i: (0, i, 0)),
            pl.BlockSpec((SUBLANES, blk), lambda i: (0, i)),
            pl.BlockSpec((blk, SC_WIDTH), lambda i: (i, 0)),
        ],
        out_shape=[
            jax.ShapeDtypeStruct((n_rows, D_MODEL), F32),
            jax.ShapeDtypeStruct(c0.shape, F32),
            jax.ShapeDtypeStruct((ML_HEADS, n_seq, ML_DK), F32),
            jax.ShapeDtypeStruct((SUBLANES, n_rows), F32),
            jax.ShapeDtypeStruct((n_rows, SC_WIDTH), F32),
        ],
        scratch_shapes=[
            pltpu.VMEM((blk, D_MODEL), F32),
            pltpu.VMEM((blk, IN_W), F32),
            pltpu.VMEM((KG_ROWS, blk), F32),
            pltpu.VMEM((blk, ML_WIDTH), F32),
            pltpu.VMEM((blk, ML_WIDTH + SC_WIDTH), BF16),
            pltpu.VMEM((blk, D_MODEL), F32),
            pltpu.VMEM(head_blk, F32),
            pltpu.VMEM(head_blk, F32),
            pltpu.VMEM(head_blk, F32),
            pltpu.VMEM(head_blk, F32),
            pltpu.VMEM((ML_HEADS, ML_DK, blk), F32),
            pltpu.VMEM(head_blk, BF16),
            pltpu.VMEM(head_blk, F32),
            pltpu.VMEM(head_blk, F32),
        ],
        compiler_params=_cparams(1),
        name="mix_even_sample",
    )(x, mod, nw, win, wkg, bifc, bifr, onw, cw, wout, c0, n_tok, n_tok_t, m_col, m_row, sc_pad)


def _split2(x):
    hi = x.astype(BF16)
    lo = (x - hi.astype(F32)).astype(BF16)
    return jnp.concatenate([hi, lo], axis=1)


def _head_lane_mats():
    r = lax.broadcasted_iota(jnp.int32, (2 * LANES, LANES), 0) % LANES
    c = lax.broadcasted_iota(jnp.int32, (2 * LANES, LANES), 1)
    hsum = ((r // HEAD_DIM) == (c // HEAD_DIM)).astype(BF16)
    half = HEAD_DIM // 2
    src = jnp.where((c % HEAD_DIM) < half, c + half, c - half)
    return hsum, (r == src).astype(BF16)


def _q_lane_mat():
    r = lax.broadcasted_iota(jnp.int32, (2 * LANES, 2 * LANES), 0)
    c = lax.broadcasted_iota(jnp.int32, (2 * LANES, 2 * LANES), 1)
    half = HEAD_DIM // 2
    src = jnp.where((c % HEAD_DIM) < half, c + half, c - half)
    top = (r < LANES) & (c < LANES) & ((r // HEAD_DIM) == (c // HEAD_DIM))
    return (top | ((r >= LANES) & (c >= LANES) & (r == src))).astype(BF16)


def _q_norm_rope(xb, gw, cos, sin, qmat):
    zg = xb * gw
    out = _dot(jnp.concatenate([(xb * xb).astype(BF16), zg.astype(BF16)], axis=1), qmat)
    ms = out[:, :LANES] * (1.0 / HEAD_DIM)
    return lax.rsqrt(ms + EPS) * (zg * cos + out[:, LANES:] * sin)


def _qk_norm_rope(xb, gw, cos, sin, hsum, rot_mat):
    ms = _dot(_split2(xb * xb), hsum) * (1.0 / HEAD_DIM)
    zg = xb * gw
    rot = _dot(_split2(zg), rot_mat)
    return lax.rsqrt(ms + EPS) * (zg * cos + rot * sin)


def _sink_rows(sink8, reps):
    return jnp.concatenate(
        [jnp.broadcast_to(sink8[r:r + 1, :], (reps, LANES)) for r in range(SUBLANES)], axis=0)


def _attn_prompt_kernel(x_ref, mod_ref, nw_ref, wqkv_ref, qnw_ref, knw_ref, cos_ref, sin_ref,
                        sink_ref, wout_ref, xo_ref, ko_ref, vo_ref,
                        z_scr, qm_scr, k_scr, v_scr, o_scr):
    t = pl.program_id(1)
    n_qb = TM // WINDOW

    @pl.when(t == 0)
    def _():
        k_scr[0:WINDOW, :] = jnp.zeros((WINDOW, KV_W), BF16)
        v_scr[0:WINDOW, :] = jnp.zeros((WINDOW, KV_W), BF16)

    x = x_ref[0]
    bi = pl.program_id(0)
    h = _norm_mod(x, nw_ref[...], _mod_row(mod_ref, 1, bi), _mod_row(mod_ref, 0, bi)).astype(BF16)
    z_scr[...] = _dot(h, wqkv_ref[...])
    cos = cos_ref[...]
    sin = sin_ref[...]
    hmats = _head_lane_mats()
    qmat = _q_lane_mat()
    half0 = lax.broadcasted_iota(jnp.int32, (TM, LANES), 1) < HEAD_DIM
    qscale = HEAD_DIM ** -0.5
    for jb in range(Q_W // LANES):
        y = _q_norm_rope(z_scr[:, jb * LANES:(jb + 1) * LANES], qnw_ref[...], cos, sin, qmat) * qscale
        qm_scr[2 * jb] = jnp.where(half0, y, 0.0).astype(BF16)
        qm_scr[2 * jb + 1] = jnp.where(half0, 0.0, y).astype(BF16)
    for p in range(KV_W // LANES):
        kf = _qk_norm_rope(z_scr[:, Q_W + p * LANES:Q_W + (p + 1) * LANES], knw_ref[...], cos, sin, *hmats)
        ko_ref[0, :, p * LANES:(p + 1) * LANES] = kf[TM - WINDOW:TM]
        k_scr[WINDOW:WINDOW + TM, p * LANES:(p + 1) * LANES] = kf.astype(BF16)
    vf = z_scr[:, Q_W + KV_W:Q_W + 2 * KV_W]
    vo_ref[0] = vf[TM - WINDOW:TM]
    v_scr[WINDOW:WINDOW + TM, :] = vf.astype(BF16)

    r = lax.broadcasted_iota(jnp.int32, (8 * WINDOW, 2 * WINDOW), 0) % WINDOW
    c = lax.broadcasted_iota(jnp.int32, (8 * WINDOW, 2 * WINDOW), 1)
    valid = ((c < WINDOW) & (c > r)) | ((c >= WINDOW) & ((c - WINDOW) <= r))
    first_lim = jnp.where(t == 0, WINDOW, 0)
    half0q = lax.broadcasted_iota(jnp.int32, (4 * WINDOW, LANES), 1) < HEAD_DIM
    ones_kv = jnp.ones((2 * WINDOW, LANES), BF16)
    for qb in range(n_qb):
        rows = slice(qb * WINDOW, (qb + 1) * WINDOW)
        krows = slice(qb * WINDOW, (qb + 2) * WINDOW)
        vmask = (valid & (c >= first_lim)) if qb == 0 else valid
        for p in range(KV_W // LANES):
            kb = k_scr[krows, p * LANES:(p + 1) * LANES]
            vb = v_scr[krows, p * LANES:(p + 1) * LANES]
            qs = jnp.concatenate([qm_scr[2 * (4 * p + i) + e, rows, :] for e in range(2) for i in range(4)],
                                 axis=0)
            s = jnp.where(vmask, _dot_nt(qs, kb), -jnp.inf)
            sk = _sink_rows(sink_ref[p], WINDOW)
            mx = jnp.maximum(jnp.max(s, axis=-1, keepdims=True), sk)
            pr = jnp.exp(s - jnp.concatenate([mx, mx], axis=1))
            o2 = _dot(pr.astype(BF16), jnp.concatenate([vb, ones_kv], axis=1))
            den = o2[:, LANES:] + jnp.exp(sk - mx)
            o = o2[:, :LANES] * (1.0 / den)
            merged = jnp.where(half0q, o[0:4 * WINDOW], o[4 * WINDOW:])
            for i in range(4):
                o_scr[rows, (4 * p + i) * LANES:(4 * p + i + 1) * LANES] = (
                    merged[i * WINDOW:(i + 1) * WINDOW].astype(BF16))

    y = _dot(o_scr[...], wout_ref[...])
    xo_ref[0] = x + _mod_row(mod_ref, 2, bi) * y
    k_scr[0:WINDOW, :] = k_scr[TM:TM + WINDOW, :]
    v_scr[0:WINDOW, :] = v_scr[TM:TM + WINDOW, :]


def _attn_prompt_call(x, mod, n_seq_rows, l, nw, wqkv, qnw, knw, cos, sin, sink, wout):
    B, S, _ = x.shape
    nt = S // TM
    lj = l // 2
    layer3 = lambda b, t: (lj, 0, 0)
    return pl.pallas_call(
        _attn_prompt_kernel,
        grid=(B, nt),
        in_specs=[
            pl.BlockSpec((1, TM, D_MODEL), lambda b, t: (b, t, 0)),
            pl.BlockSpec((None, N_MOD, SUBLANES, D_MODEL), lambda b, t: (l, 0, n_seq_rows // SUBLANES, 0)),
            pl.BlockSpec((None, 1, D_MODEL), lambda b, t: (l, 0, 0)),
            pl.BlockSpec((None, D_MODEL, Q_W + 2 * KV_W), layer3),
            pl.BlockSpec((None, 1, LANES), layer3),
            pl.BlockSpec((None, 1, LANES), layer3),
            pl.BlockSpec((TM, LANES), lambda b, t: (t, 0)),
            pl.BlockSpec((TM, LANES), lambda b, t: (t, 0)),
            pl.BlockSpec((None, 2, SUBLANES, LANES), lambda b, t: (lj, 0, 0, 0)),
            pl.BlockSpec((None, Q_W, D_MODEL), layer3),
        ],
        out_specs=[
            pl.BlockSpec((1, TM, D_MODEL), lambda b, t: (b, t, 0)),
            pl.BlockSpec((1, WINDOW, KV_W), lambda b, t: (b, 0, 0)),
            pl.BlockSpec((1, WINDOW, KV_W), lambda b, t: (b, 0, 0)),
        ],
        out_shape=[
            jax.ShapeDtypeStruct((B, S, D_MODEL), F32),
            jax.ShapeDtypeStruct((B, WINDOW, KV_W), F32),
            jax.ShapeDtypeStruct((B, WINDOW, KV_W), F32),
        ],
        scratch_shapes=[
            pltpu.VMEM((TM, Q_W + 2 * KV_W), F32),
            pltpu.VMEM((2 * Q_W // LANES, TM, LANES), BF16),
            pltpu.VMEM((TM + WINDOW, KV_W), BF16),
            pltpu.VMEM((TM + WINDOW, KV_W), BF16),
            pltpu.VMEM((TM, Q_W), BF16),
        ],
        compiler_params=_cparams(2),
        name="attn_prompt",
    )(x, mod, nw, wqkv, qnw, knw, cos, sin, sink, wout)


def _attn_sample_kernel(x_ref, mod_ref, nw_ref, wqkv_ref, qnw_ref, knw_ref, cos_ref, sin_ref,
                        sink_ref, wout_ref, kc_ref, vc_ref,
                        xo_ref, kco_ref, vco_ref,
                        h_scr, g1_scr, z_scr, qm_scr, kn_scr, o_scr):
    n_tok = x_ref.shape[0]
    seq_len = n_tok // SEQ_BLK
    win = kc_ref.shape[1]

    def modulate(b, carry):
        rows = pl.ds(pl.multiple_of(b * seq_len, seq_len), seq_len)
        h_scr[rows, :] = _norm_mod(x_ref[rows, :], nw_ref[...], _mod_row(mod_ref, 1, b), _mod_row(mod_ref, 0, b))
        g1_scr[rows, :] = jnp.broadcast_to(_mod_row(mod_ref, 2, b), (seq_len, D_MODEL))
        return carry

    lax.fori_loop(0, SEQ_BLK, modulate, 0, unroll=8)
    z_scr[...] = _dot(h_scr[...].astype(BF16), wqkv_ref[...])
    cos = cos_ref[...]
    sin = sin_ref[...]
    hmats = _head_lane_mats()
    half0 =lax.broadcasted_iota(jnp.int32, (n_tok, LANES), 1) < HEAD_DIM
    qscale = HEAD_DIM ** -0.5
    for jb in range(Q_W // LANES):
        y = _qk_norm_rope(z_scr[:, jb * LANES:(jb + 1) * LANES], qnw_ref[...], cos, sin, *hmats) * qscale
        qm_scr[2 * jb] = jnp.where(half0, y, 0.0)
        qm_scr[2 * jb + 1] = jnp.where(half0, 0.0, y)
    for p in range(KV_W // LANES):
        kn_scr[:, p * LANES:(p + 1) * LANES] = _qk_norm_rope(
            z_scr[:, Q_W + p * LANES:Q_W + (p + 1) * LANES], knw_ref[...], cos, sin, *hmats)

    n_q = 8 * seq_len
    tq = lax.broadcasted_iota(jnp.int32, (SEQ_BLK, n_q, 2 * win), 1) % seq_len
    cc = lax.broadcasted_iota(jnp.int32, (SEQ_BLK, n_q, 2 * win), 2)
    valid = ((cc < win) & (cc > tq)) | ((cc >= 2 * win - seq_len) & ((cc - (2 * win - seq_len)) <= tq))
    half0q = lax.broadcasted_iota(jnp.int32, (SEQ_BLK, n_q // 2, LANES), 2) < HEAD_DIM
    ones_kv = jnp.ones((SEQ_BLK, 2 * win, LANES), BF16)

    kc = kc_ref[...]
    vc = vc_ref[...]
    knew = jnp.concatenate([kc[:, seq_len:], kn_scr[...].reshape(SEQ_BLK, seq_len, KV_W)], axis=1)
    vnew = jnp.concatenate(
        [vc[:, seq_len:], z_scr[:, Q_W + KV_W:Q_W + 2 * KV_W].reshape(SEQ_BLK, seq_len, KV_W)], axis=1)
    kco_ref[...] = knew
    vco_ref[...] = vnew
    for p in range(KV_W // LANES):
        lanes = slice(p * LANES, (p + 1) * LANES)
        qs = jnp.concatenate([qm_scr[2 * (4 * p + i) + e].reshape(SEQ_BLK, seq_len, LANES)
                              for e in range(2) for i in range(4)], axis=1).astype(BF16)
        kk = jnp.concatenate([kc[:, :, lanes], knew[:, :, lanes]], axis=1).astype(BF16)
        vv = jnp.concatenate([vc[:, :, lanes], vnew[:, :, lanes]], axis=1).astype(BF16)
        s = jnp.einsum("bqd,bkd->bqk", qs, kk, preferred_element_type=F32)
        s = jnp.where(valid, s, -jnp.inf)
        sk = _sink_rows(sink_ref[p], seq_len)[None]
        mx = jnp.maximum(jnp.max(s, axis=-1, keepdims=True), sk)
        pr = jnp.exp(s - jnp.concatenate([mx, mx], axis=-1))
        o2 = jnp.einsum("bqk,bkd->bqd", pr.astype(BF16), jnp.concatenate([vv, ones_kv], axis=-1),
                        preferred_element_type=F32)
        o = o2[:, :, :LANES] * (1.0 / (o2[:, :, LANES:] + jnp.exp(sk - mx)))
        merged = jnp.where(half0q, o[:, 0:n_q // 2], o[:, n_q // 2:])
        for i in range(4):
            o_scr[:, (4 * p + i) * LANES:(4 * p + i + 1) * LANES] = (
                merged[:, i * seq_len:(i + 1) * seq_len].reshape(n_tok, LANES))

    y = _dot(o_scr[...].astype(BF16), wout_ref[...])
    xo_ref[...] = x_ref[...] + g1_scr[...] * y


def _attn_sample_call(x, mod, l, nw, wqkv, qnw, knw, cos, sin, sink, wout, kc, vc):
    n_rows = x.shape[0]
    _, n_seq, win, _ = kc.shape
    lj = l // 2
    seq_len = n_rows // n_seq
    blk = SEQ_BLK * seq_len
    nb = n_seq // SEQ_BLK
    const2 = lambda i: (0, 0)
    return pl.pallas_call(
        _attn_sample_kernel,
        input_output_aliases={10: 1, 11: 2},
        grid=(nb,),
        in_specs=[
            pl.BlockSpec((blk, D_MODEL), lambda i: (i, 0)),
            pl.BlockSpec((None, N_MOD, SEQ_BLK, D_MODEL), lambda i: (l, 0, i, 0)),
            pl.BlockSpec((None, 1, D_MODEL), lambda i: (l, 0, 0)),
            pl.BlockSpec((None, D_MODEL, Q_W + 2 * KV_W), lambda i: (lj, 0, 0)),
            pl.BlockSpec((None, 1, LANES), lambda i: (lj, 0, 0)),
            pl.BlockSpec((None, 1, LANES), lambda i: (lj, 0, 0)),
            pl.BlockSpec((blk, LANES), const2),
            pl.BlockSpec((blk, LANES), const2),
            pl.BlockSpec((None, 2, SUBLANES, LANES), lambda i: (lj, 0, 0, 0)),
            pl.BlockSpec((None, Q_W, D_MODEL), lambda i: (lj, 0, 0)),
            pl.BlockSpec((None, SEQ_BLK, win, KV_W), lambda i: (lj, i, 0, 0)),
            pl.BlockSpec((None, SEQ_BLK, win, KV_W), lambda i: (lj, i, 0, 0)),
        ],
        out_specs=[
            pl.BlockSpec((blk, D_MODEL), lambda i: (i, 0)),
            pl.BlockSpec((None, SEQ_BLK, win, KV_W), lambda i: (lj, i, 0, 0)),
            pl.BlockSpec((None, SEQ_BLK, win, KV_W), lambda i: (lj, i, 0, 0)),
        ],
        out_shape=[
            jax.ShapeDtypeStruct((n_rows, D_MODEL), F32),
            jax.ShapeDtypeStruct(kc.shape, F32),
            jax.ShapeDtypeStruct(vc.shape, F32),
        ],
        scratch_shapes=[
            pltpu.VMEM((blk, D_MODEL), F32),
            pltpu.VMEM((blk, D_MODEL), F32),
            pltpu.VMEM((blk, Q_W + 2 * KV_W), F32),
            pltpu.VMEM((2 * Q_W // LANES, blk, LANES), F32),
            pltpu.VMEM((blk, KV_W), F32),
            pltpu.VMEM((blk, Q_W), F32),
        ],
        compiler_params=_cparams(1),
        name="attn_sample",
    )(x, mod, nw, wqkv, qnw, knw, cos, sin, sink, wout, kc, vc)


def _rope_tables(pos):
    half = HEAD_DIM // 2
    inv = ROPE_THETA ** (-jnp.arange(half, dtype=F32) / half)
    ang = pos.astype(F32)[:, None] * inv[None, :]
    cos = jnp.cos(ang)
    sin = jnp.sin(ang)
    return jnp.tile(cos, (1, 4)), jnp.concatenate([-sin, sin, -sin, sin], axis=1)


def _rope_tables_range(n):
    half = HEAD_DIM // 2
    inv = jnp.tile(ROPE_THETA ** (-jnp.arange(half, dtype=F32) / half), LANES // half)
    sign = jnp.where((jnp.arange(LANES) % HEAD_DIM) < half, -1.0, 1.0).astype(F32)
    a_hi = (jnp.arange(n // WINDOW, dtype=jnp.int32) * WINDOW).astype(F32)[:, None] * inv[None, :]
    a_lo = jnp.arange(WINDOW, dtype=jnp.int32).astype(F32)[:, None] * inv[None, :]
    ch, sh = jnp.cos(a_hi)[:, None, :], jnp.sin(a_hi)[:, None, :]
    cl, sl = jnp.cos(a_lo)[None], jnp.sin(a_lo)[None]
    cos = (ch * cl - sh * sl).reshape(n, LANES)
    sin = ((sh * cl + ch * sl) * sign).reshape(n, LANES)
    return cos, sin


def _prep_attn(w_qkv, q_norm, k_norm, sink, w_out):
    n_l = w_qkv.shape[0]
    perm = np.asarray(HEAD_PERM)
    wq = w_qkv[:, :, :Q_W].reshape(n_l, D_MODEL, ATT_HEADS, HEAD_DIM)[:, :, perm].reshape(n_l, D_MODEL, Q_W)
    wqkv = jnp.concatenate([wq, w_qkv[:, :, Q_W:]], axis=2).astype(BF16)
    wout = w_out.reshape(n_l, ATT_HEADS, HEAD_DIM, D_MODEL)[:, perm].reshape(n_l, Q_W, D_MODEL).astype(BF16)
    qnw = jnp.tile(q_norm, (1, 2))[:, None, :]
    knw = jnp.tile(k_norm, (1, 2))[:, None, :]
    idx = np.asarray([[perm[2 * (4 * p + i) + e] for e in range(2) for i in range(4)] for p in range(2)])
    sink_arr = jnp.broadcast_to(sink[:, idx][..., None], (n_l, 2, SUBLANES, LANES)).astype(F32)
    return wqkv, qnw, knw, sink_arr, wout


def kernel(x_prompt, x_sample, c_prompt, c_sample, state_mlstm_C, state_mlstm_n, state_mlstm_m, state_sconv, cache_win_k, cache_win_v, state_ffn_conv, norm1, norm2, w_ada, b_ada, a_w_in, a_b_if, a_out_norm, a_conv_w, a_w_out, c_w_qkv, c_q_norm, c_k_norm, c_sink, c_w_out, f_w_up, f_conv_w, f_w_down):
    B, S, _ = x_prompt.shape
    NS, SL, _ = x_sample.shape
    assert S % TM == 0 and NS % SEQ_BLK == 0 and SEQ_BLK * SL == CHUNK and SL == SUBLANES

    assert B <= SUBLANES
    c_all = jnp.concatenate([c_sample, c_prompt, jnp.zeros((SUBLANES - B, D_MODEL), F32)], axis=0)
    mod = _ada_call(c_all, w_ada, b_ada)

    win_all, wkg_all = _prep_win_call(jnp.swapaxes(a_w_in, 1, 2))
    b_i, b_f = a_b_if[:, :ML_HEADS], a_b_if[:, ML_HEADS:]
    lane_pad = jnp.zeros((a_b_if.shape[0], LANES - 2 * ML_HEADS), F32)
    bifc_all = jnp.stack([jnp.concatenate([b_i, b_i, lane_pad], axis=1),
                          jnp.concatenate([b_f, b_f, lane_pad], axis=1)], axis=1)
    bifr_all = jnp.broadcast_to(jnp.concatenate([b_i, b_i, b_f, b_f], axis=1)[:, :, None],
                                (a_b_if.shape[0], 2 * SUBLANES, LANES))
    wout_a_all = _cast_call(a_w_out)
    wup_all = _cast_call(f_w_up)
    wdn_all = _cast_call(f_w_down)
    onw_all = a_out_norm.reshape(-1, 1, ML_WIDTH)
    norm1_r = norm1.reshape(DEPTH, 1, D_MODEL)
    norm2_r = norm2.reshape(DEPTH, 1, D_MODEL)
    win_buf = cache_win_k.shape[2]
    kc_all = cache_win_k.reshape(-1, NS, win_buf, KV_W)
    vc_all = cache_win_v.reshape(-1, NS, win_buf, KV_W)

    cos_p, sin_p = _rope_tables_range(S)
    cos_s, sin_s = _rope_tables(PAST_LEN + jnp.arange(SL, dtype=jnp.int32))
    cos_s = jnp.tile(cos_s, (SEQ_BLK, 1))
    sin_s = jnp.tile(sin_s, (SEQ_BLK, 1))

    n_tok = jnp.repeat(state_mlstm_n.transpose(0, 2, 1, 3), SL, axis=2)
    n_tok_t = jnp.swapaxes(n_tok, 2, 3)
    m_rep = jnp.repeat(state_mlstm_m, SL, axis=1)
    m_col = jnp.concatenate([m_rep, m_rep, jnp.zeros(m_rep.shape[:2] + (LANES - 2 * ML_HEADS,), F32)], axis=2)
    m_row = jnp.swapaxes(jnp.concatenate([m_rep, m_rep], axis=2), 1, 2)
    sc_pad = jnp.pad(state_sconv, ((0, 0), (0, 0), (0, SL - 2), (0, 0))).reshape(-1, NS * SL, SC_WIDTH)

    wqkv, qnw, knw, sink, wout = _prep_attn(c_w_qkv, c_q_norm, c_k_norm, c_sink, c_w_out)

    xp = x_prompt
    xs = x_sample.reshape(NS * SL, D_MODEL)
    p_C, p_n, p_m, p_sc, p_wk, p_wv, p_ffn = [], [], [], [], [], [], []
    s_n, s_m, s_sc = [], [], []
    s_C = s_ffn = None
    s_wk, s_wv = kc_all, vc_all

    for l in range(DEPTH):
        if l % 2 == 0:
            i = l // 2
            even_w = (norm1_r, win_all, wkg_all, bifc_all, bifr_all, onw_all, a_conv_w, wout_a_all)
            xp, co, no, mo, sco = _mix_even_prompt_call(xp, mod, NS, l, *even_w)
            p_C.append(co)
            p_n.append(no[:, :, :, 0])
            p_m.append(mo[:, :ML_HEADS, 0])
            p_sc.append(sco[:, SUBLANES - 2:, :])

            xs, s_C, no, mo, sco = _mix_even_sample_call(xs, mod, l, *even_w, state_mlstm_C, n_tok,
                                                         n_tok_t, m_col, m_row, sc_pad, s_C)
            s_n.append(no.transpose(1, 0, 2))
            s_m.append(mo[:ML_HEADS, ::SL].T)
            s_sc.append(sco.reshape(NS, SL, SC_WIDTH)[:, :2])
        else:
            xp, ko, vo = _attn_prompt_call(xp, mod, NS, l, norm1_r, wqkv, qnw, knw, cos_p, sin_p, sink, wout)
            p_wk.append(ko.reshape(B, WINDOW, KV_HEADS, HEAD_DIM))
            p_wv.append(vo.reshape(B, WINDOW, KV_HEADS, HEAD_DIM))
            xs, s_wk, s_wv = _attn_sample_call(xs, mod, l, norm1_r, wqkv, qnw, knw, cos_s, sin_s, sink, wout,
                                               s_wk, s_wv)

        xp, st = _ffn_prompt_call(xp, mod, NS, l, norm2_r, wup_all, f_conv_w, wdn_all)
        p_ffn.append(st[:, SUBLANES - 2:, :])
        xs, s_ffn = _ffn_sample_call(xs, mod, l, norm2_r, state_ffn_conv, wup_all, f_conv_w, wdn_all, s_ffn)

    kv_shape = (-1, NS, win_buf, KV_HEADS, HEAD_DIM)
    return (xp, xs.reshape(NS, SL, D_MODEL),
            jnp.stack(p_C), jnp.stack(p_n), jnp.stack(p_m), jnp.stack(p_sc),
            jnp.stack(p_wk), jnp.stack(p_wv), jnp.stack(p_ffn),
            s_C, jnp.stack(s_n), jnp.stack(s_m), jnp.stack(s_sc),
            s_wk.reshape(kv_shape), s_wv.reshape(kv_shape), s_ffn)
```
